```python
import jax
import jax.numpy as jnp
from jax import lax
import numpy as np

D_MODEL = 2048
BATCH = 4
SEQ = 2048
DEPTH = 2

CHUNK = 64
GROUP_WIDTH = D_MODEL // 4
MLSTM_HEADS = 4
MLSTM_HEAD_DIM = GROUP_WIDTH // MLSTM_HEADS
MLSTM_CONV = 4
FORGET_BIAS = 3.0
POOL_WINDOWS = (2, 4, 8, 16)
POOL_GROUPS = len(POOL_WINDOWS)
POOL_GROUP_DIM = GROUP_WIDTH // POOL_GROUPS
SB_HEADS = 4
SB_HEAD_DIM = GROUP_WIDTH // SB_HEADS
SB_BLOCK = 128
CONV_WIDTH = 31
IN_SPLITS = (2 * GROUP_WIDTH, GROUP_WIDTH, GROUP_WIDTH, 2 * MLSTM_HEADS,
             GROUP_WIDTH, 3 * GROUP_WIDTH, 2 * GROUP_WIDTH)
IN_COLS = sum(IN_SPLITS)
D_FF = 5632
N_EXPERTS = 8
TOP_K = 2
D_FF_EXPERT = D_FF // TOP_K
N_DENSE = (DEPTH + 1) // 2
N_MOE = DEPTH // 2
ALPHA = (2.0 * DEPTH) ** 0.25
BETA = (8.0 * DEPTH) ** -0.25
EPS = 1e-5

kernel_name = "hybrid_streaming_encoder_mlstm_pool_stickbreak_conformer_moe"


def _layer_norm(x, w, b):
    xf = x.astype(jnp.float32)
    mu = jnp.mean(xf, axis=-1, keepdims=True)
    var = jnp.mean(jnp.square(xf - mu), axis=-1, keepdims=True)
    return ((xf - mu) * lax.rsqrt(var + EPS)).astype(x.dtype) * w + b


def _rms_norm(x, w):
    xf = x.astype(jnp.float32)
    return (xf * lax.rsqrt(jnp.mean(jnp.square(xf), axis=-1, keepdims=True) + EPS)).astype(x.dtype) * w


def _causal_depthwise_conv(x, w, b):
    k = w.shape[0]
    y = lax.conv_general_dilated(x, w[:, None, :], window_strides=(1,), padding=((k - 1, 0),),
                                 dimension_numbers=("NWC", "WIO", "NWC"),
                                 feature_group_count=x.shape[-1])
    return y + b


def _mlstm_chunkwise(q, k, v, i_pre, f_pre):
    B_, H, S, d = q.shape
    nc = S // CHUNK
    f32 = jnp.float32

    def chunks(t):
        return jnp.moveaxis(t.reshape(B_, H, nc, CHUNK, *t.shape[3:]), 2, 0)

    qc = chunks(q.astype(f32) * (d ** -0.5))
    kc = chunks(k.astype(f32))
    vc = chunks(v.astype(f32))
    ic = chunks(i_pre.astype(f32))
    bc = jnp.cumsum(chunks(jax.nn.log_sigmoid(f_pre.astype(f32))), axis=-1)
    causal = jnp.tril(jnp.ones((CHUNK, CHUNK), dtype=bool))

    def step(carry, xs):
        C, n, m = carry
        q_, k_, v_, i_, b_ = xs
        g = b_[..., -1]
        log_d = jnp.where(causal, b_[..., :, None] - b_[..., None, :] + i_[..., None, :], -jnp.inf)
        m_inter = b_ + m[..., None]
        m_t = jnp.maximum(m_inter, jnp.max(log_d, axis=-1))
        w = jnp.exp(log_d - m_t[..., None]) * jnp.einsum("bhtd,bhsd->bhts", q_, k_)
        inter = jnp.exp(m_inter - m_t)
        num = inter[..., None] * jnp.einsum("bhtd,bhde->bhte", q_, C) + jnp.einsum("bhts,bhse->bhte", w, v_)
        den = inter * jnp.einsum("bhtd,bhd->bht", q_, n) + jnp.sum(w, axis=-1)
        h = num / jnp.maximum(jnp.abs(den), jnp.exp(-m_t))[..., None]
        a = g[..., None] - b_ + i_
        m_new = jnp.maximum(g + m, jnp.max(a, axis=-1))
        decay = jnp.exp(g + m - m_new)
        kw = k_ * jnp.exp(a - m_new[..., None])[..., None]
        C = decay[..., None, None] * C + jnp.einsum("bhsd,bhse->bhde", kw, v_)
        n = decay[..., None] * n + jnp.sum(kw, axis=-2)
        return (C, n, m_new), h

    init = (jnp.zeros((B_, H, d, d), f32), jnp.zeros((B_, H, d), f32), jnp.zeros((B_, H), f32))
    _, hc = lax.scan(step, init, (qc, kc, vc, ic, bc))
    return jnp.moveaxis(hc, 0, 2).reshape(B_, H, S, d)


def _pool_mixer(u, pool_w, pool_scale):
    B_, S, _ = u.shape
    uf = u.astype(jnp.float32)
    cs = jnp.pad(jnp.cumsum(uf, axis=1), ((0, 0), (1, 0), (0, 0)))
    outs = []
    for g, win in enumerate(POOL_WINDOWS):
        sl = slice(g * POOL_GROUP_DIM, (g + 1) * POOL_GROUP_DIM)
        csg = cs[..., sl]
        lower = jnp.pad(csg[:, :S + 1 - win], ((0, 0), (win - 1, 0), (0, 0)))
        cnt = jnp.minimum(jnp.arange(1, S + 1), win).astype(jnp.float32)[None, :, None]
        outs.append((csg[:, 1:] - lower) / cnt - uf[..., sl])
    y = jnp.stack(outs, axis=2).astype(u.dtype)
    y = jnp.einsum("bsgc,gce->bsge", y, pool_w).reshape(B_, S, GROUP_WIDTH)
    return y * pool_scale


def _stick_breaking(q, k, v):
    B_, S, H, d = q.shape
    nb = S // SB_BLOCK
    kh = k.transpose(0, 2, 1, 3)
    vh = v.transpose(0, 2, 1, 3)
    qb = q.reshape(B_, nb, SB_BLOCK, H, d).transpose(1, 0, 3, 2, 4)
    key_pos = jnp.arange(S)

    def one_block(args):
        q_blk, blk = args
        z = jnp.einsum("bhqd,bhsd->bhqs", q_blk, kh).astype(jnp.float32) * (d ** -0.5)
        q_pos = blk * SB_BLOCK + jnp.arange(SB_BLOCK)
        strict = key_pos[None, :] < q_pos[:, None]
        log_keep = jnp.where(strict, jax.nn.log_sigmoid(-z), 0.0)
        log_after = lax.cumsum(log_keep, axis=3, reverse=True) - log_keep
        a = jnp.where(strict, jnp.exp(jax.nn.log_sigmoid(z) + log_after), 0.0)
        return jnp.einsum("bhqs,bhsd->bhqd", a.astype(v.dtype), vh)

    out = lax.map(one_block, (qb, jnp.arange(nb)))
    return out.transpose(1, 0, 3, 2, 4).reshape(B_, S, H, d)


def _conformer_conv(u, dw_w, dw_b, ln_w, ln_b, pw_w, pw_b):
    a, gte = jnp.split(u, 2, axis=-1)
    h = a * jax.nn.sigmoid(gte)
    h = _causal_depthwise_conv(h, dw_w, dw_b)
    h = jax.nn.silu(_layer_norm(h, ln_w, ln_b))
    return h @ pw_w + pw_b


def _heads(t, n_heads):
    B_, S, _ = t.shape
    return t.reshape(B_, S, n_heads, -1).transpose(0, 2, 1, 3)


def _hybrid_mixer(u, w_in, mlstm_conv_w, mlstm_conv_b, mlstm_gate_b, mlstm_norm_w,
                  pool_w, pool_scale, conv_dw_w, conv_dw_b, conv_ln_w, conv_ln_b,
                  conv_pw_w, conv_pw_b, group_norm_w, w_out):
    B_, S, _ = u.shape
    proj = u @ w_in
    bounds = np.cumsum(IN_SPLITS)[:-1].tolist()
    qk_pre, v_m, o_pre, if_pre, pool_in, sb_qkv, conv_in = jnp.split(proj, bounds, axis=-1)

    qk = jax.nn.silu(_causal_depthwise_conv(qk_pre, mlstm_conv_w, mlstm_conv_b))
    q_m, k_m = jnp.split(qk, 2, axis=-1)
    gates = (if_pre + mlstm_gate_b).transpose(0, 2, 1)
    h_m = _mlstm_chunkwise(_heads(q_m, MLSTM_HEADS), _heads(k_m, MLSTM_HEADS), _heads(v_m, MLSTM_HEADS),
                           gates[:, :MLSTM_HEADS], gates[:, MLSTM_HEADS:])
    h_m = h_m.transpose(0, 2, 1, 3)
    mu = jnp.mean(h_m, axis=-1, keepdims=True)
    var = jnp.mean(jnp.square(h_m - mu), axis=-1, keepdims=True)
    h_m = ((h_m - mu) * lax.rsqrt(var + EPS)).reshape(B_, S, GROUP_WIDTH).astype(u.dtype) * mlstm_norm_w
    y_a = jax.nn.sigmoid(o_pre) * h_m

    y_b = _pool_mixer(pool_in, pool_w, pool_scale)

    q_s, k_s, v_s = jnp.split(sb_qkv, 3, axis=-1)
    rs = lambda t: t.reshape(B_, S, SB_HEADS, SB_HEAD_DIM)
    y_c = _stick_breaking(rs(q_s), rs(k_s), rs(v_s)).reshape(B_, S, GROUP_WIDTH)

    y_d = _conformer_conv(conv_in, conv_dw_w, conv_dw_b, conv_ln_w, conv_ln_b, conv_pw_w, conv_pw_b)

    gn_b, gn_c, gn_d = jnp.split(group_norm_w, 3, axis=-1)
    y = jnp.concatenate([y_a, _rms_norm(y_b, gn_b), _rms_norm(y_c, gn_c), _rms_norm(y_d, gn_d)], axis=-1)
    return y @ w_out


def _swiglu(u, w_gate, w_up, w_down):
    return (jax.nn.silu(u @ w_gate) * (u @ w_up)) @ w_down


def _moe(u, router_w, router_b, w_gate, w_up, w_down):
    B_, S, D = u.shape
    t = u.reshape(B_ * S, D)
    logits = (t @ router_w).astype(jnp.float32) + router_b
    top_v, top_i = lax.top_k(logits, TOP_K)
    top_w = jax.nn.softmax(top_v, axis=-1)
    gates = jnp.sum(jax.nn.one_hot(top_i, N_EXPERTS, dtype=jnp.float32) * top_w[..., None], axis=1)
    y = jnp.zeros_like(t)
    for e in range(N_EXPERTS):
        y = y + gates[:, e:e + 1].astype(t.dtype) * _swiglu(t, w_gate[e], w_up[e], w_down[e])
    return y.reshape(B_, S, D)


def setup_inputs(seed: int = 0) -> dict:
    key = jax.random.key(seed)
    ks = list(jax.random.split(key, 32))

    def nrm(shape, std):
        return std * jax.random.normal(ks.pop(), shape, jnp.float32)

    L, D, GW = DEPTH, D_MODEL, GROUP_WIDTH
    gate_offset = jnp.concatenate([jnp.zeros((MLSTM_HEADS,), jnp.float32),
                                   jnp.full((MLSTM_HEADS,), FORGET_BIAS, jnp.float32)])
    return {
        "x": nrm((BATCH, SEQ, D), 1.0),
        "c": nrm((BATCH, D), 1.0),
        "w_in": nrm((L, D, IN_COLS), D ** -0.5),
        "mlstm_conv_w": nrm((L, MLSTM_CONV, 2 * GW), MLSTM_CONV ** -0.5),
        "mlstm_conv_b": nrm((L, 2 * GW), 0.01),
        "mlstm_gate_b": gate_offset + nrm((L, 2 * MLSTM_HEADS), 0.1),
        "mlstm_norm_w": 1.0 + nrm((L, GW), 0.02),
        "pool_w": nrm((L, POOL_GROUPS, POOL_GROUP_DIM, POOL_GROUP_DIM), POOL_GROUP_DIM ** -0.5),
        "pool_scale": 1.0 + nrm((L, GW), 0.1),
        "conv_dw_w": nrm((L, CONV_WIDTH, GW), CONV_WIDTH ** -0.5),
        "conv_dw_b": nrm((L, GW), 0.01),
        "conv_ln_w": 1.0 + nrm((L, GW), 0.02),
        "conv_ln_b": nrm((L, GW), 0.01),
        "conv_pw_w": nrm((L, GW, GW), GW ** -0.5),
        "conv_pw_b": nrm((L, GW), 0.01),
        "group_norm_w": 1.0 + nrm((L, 3 * GW), 0.02),
        "w_out": nrm((L, D, D), BETA * D ** -0.5),
        "ada_w": nrm((L, D, 6 * D), 0.1 * D ** -0.5),
        "ada_b": nrm((L, 6 * D), 0.01),
        "ln1_w": 1.0 + nrm((L, D), 0.02),
        "ln1_b": nrm((L, D), 0.01),
        "ln2_w": 1.0 + nrm((L, D), 0.02),
        "ln2_b": nrm((L, D), 0.01),
        "ffn_w_gate": nrm((N_DENSE, D, D_FF), D ** -0.5),
        "ffn_w_up": nrm((N_DENSE, D, D_FF), D ** -0.5),
        "ffn_w_down": nrm((N_DENSE, D_FF, D), BETA * D_FF ** -0.5),
        "moe_router_w": nrm((N_MOE, D, N_EXPERTS), D ** -0.5),
        "moe_router_b": nrm((N_MOE, N_EXPERTS), 0.01),
        "moe_w_gate": nrm((N_MOE, N_EXPERTS, D, D_FF_EXPERT), D ** -0.5),
        "moe_w_up": nrm((N_MOE, N_EXPERTS, D, D_FF_EXPERT), D ** -0.5),
        "moe_w_down": nrm((N_MOE, N_EXPERTS, D_FF_EXPERT, D), BETA * D_FF_EXPERT ** -0.5),
    }


def reference(x, c, w_in, mlstm_conv_w, mlstm_conv_b, mlstm_gate_b, mlstm_norm_w,
              pool_w, pool_scale, conv_dw_w, conv_dw_b, conv_ln_w, conv_ln_b, conv_pw_w, conv_pw_b,
              group_norm_w, w_out, ada_w, ada_b, ln1_w, ln1_b, ln2_w, ln2_b,
              ffn_w_gate, ffn_w_up, ffn_w_down, moe_router_w, moe_router_b,
              moe_w_gate, moe_w_up, moe_w_down):
    c_act = jax.nn.silu(c)
    for l in range(DEPTH):
        ada = c_act @ ada_w[l] + ada_b[l]
        sh1, sc1, g1, sh2, sc2, g2 = jnp.split(ada[:, None, :], 6, axis=-1)
        u = x * (1.0 + sc1) + sh1
        y = _hybrid_mixer(u, w_in[l], mlstm_conv_w[l], mlstm_conv_b[l], mlstm_gate_b[l], mlstm_norm_w[l],
                          pool_w[l], pool_scale[l], conv_dw_w[l], conv_dw_b[l], conv_ln_w[l], conv_ln_b[l],
                          conv_pw_w[l], conv_pw_b[l], group_norm_w[l], w_out[l])
        x = _layer_norm(ALPHA * x + (1.0 + g1) * y, ln1_w[l], ln1_b[l])
        u = x * (1.0 + sc2) + sh2
        j = l // 2
        if l % 2 == 0:
            y = _swiglu(u, ffn_w_gate[j], ffn_w_up[j], ffn_w_down[j])
        else:
            y = _moe(u, moe_router_w[j], moe_router_b[j], moe_w_gate[j], moe_w_up[j], moe_w_down[j])
        x = _layer_norm(ALPHA * x + (1.0 + g2) * y, ln2_w[l], ln2_b[l])
    return x
```

```python
import functools

import jax
import jax.numpy as jnp
from jax import lax
from jax.experimental import pallas as pl
from jax.experimental.pallas import tpu as pltpu

F32 = jnp.float32
BF16 = jnp.bfloat16

DEPTH = 2
CHUNK = 64
GROUP_WIDTH = 512
N_HEADS = 4
HEAD_DIM = 128
MLSTM_CONV = 4
POOL_WINDOWS = (2, 4, 8, 16)
CONV_WIDTH = 31
N_EXPERTS = 8
ALPHA = (2.0 * DEPTH) ** 0.25
EPS = 1e-5
LANES = 128
VMEM_LIMIT = 56 * 1024 * 1024


def _params(sem, vmem=VMEM_LIMIT):
    return pltpu.CompilerParams(dimension_semantics=sem, vmem_limit_bytes=vmem)


def _silu(x):
    return x * (1.0 / (1.0 + jnp.exp(-x)))


def _sigmoid(x):
    return 1.0 / (1.0 + jnp.exp(-x))


def _neg_softplus(x):
    return -(jnp.maximum(x, 0.0) + jnp.log1p(jnp.exp(-jnp.abs(x))))


def _layer_norm_rows(r, w, b):
    mu = jnp.mean(r, axis=-1, keepdims=True)
    d = r - mu
    var = jnp.mean(d * d, axis=-1, keepdims=True)
    return d * lax.rsqrt(var + EPS) * w + b


def _rms_rows(y, w):
    return y * lax.rsqrt(jnp.mean(y * y, axis=-1, keepdims=True) + EPS) * w


def _bdot(a, b):
    return jnp.dot(a.astype(BF16), b.astype(BF16), preferred_element_type=F32)


def _ada_kernel(c_ref, w_ref, b_ref, out_ref):
    c = c_ref[...]
    out_ref[...] = jnp.dot(_silu(c), w_ref[...], preferred_element_type=F32,
                           precision=lax.Precision.HIGHEST) + b_ref[...]


def _ada_all(c, ada_w, ada_b):
    L, D, D6 = ada_w.shape
    Bn = c.shape[0]
    tn = 1024
    return pl.pallas_call(
        _ada_kernel,
        out_shape=jax.ShapeDtypeStruct((L, Bn, D6), F32),
        grid=(L, D6 // tn),
        in_specs=[pl.BlockSpec((Bn, D), lambda l, n: (0, 0)),
                  pl.BlockSpec((None, D, tn), lambda l, n: (l, 0, n)),
                  pl.BlockSpec((None, 1, tn), lambda l, n: (l, 0, n))],
        out_specs=pl.BlockSpec((None, Bn, tn), lambda l, n: (l, 0, n)),
        compiler_params=_params(("parallel", "parallel")),
        name="ada_mod",
    )(c, ada_w, ada_b.reshape(L, 1, D6))


def _inproj_kernel(x_ref, sc_ref, sh_ref, w_ref, wg_ref, proj_ref, gates_ref, u_ref):
    @pl.when(pl.program_id(1) == 0)
    def _():
        u = (x_ref[...] * (1.0 + sc_ref[...]) + sh_ref[...]).astype(BF16)
        u_ref[...] = u
        gates_ref[...] = jnp.dot(u, wg_ref[...], preferred_element_type=F32)

    proj_ref[...] = jnp.dot(u_ref[...], w_ref[...], preferred_element_type=F32)


def _in_proj(x2d, sc, sh, w_main, w_gate, seq):
    N, D = x2d.shape
    NC = w_main.shape[1]
    tm, tn = 512, 1280
    tpb = seq // tm
    return pl.pallas_call(
        _inproj_kernel,
        out_shape=(jax.ShapeDtypeStruct((N, NC), F32), jax.ShapeDtypeStruct((N, LANES), F32)),
        grid=(N // tm, NC // tn),
        in_specs=[pl.BlockSpec((tm, D), lambda m, n: (m, 0)),
                  pl.BlockSpec((None, 1, D), lambda m, n: (m // tpb, 0, 0)),
                  pl.BlockSpec((None, 1, D), lambda m, n: (m // tpb, 0, 0)),
                  pl.BlockSpec((D, tn), lambda m, n: (0, n)),
                  pl.BlockSpec((D, LANES), lambda m, n: (0, 0))],
        out_specs=(pl.BlockSpec((tm, tn), lambda m, n: (m, n)),
                   pl.BlockSpec((tm, LANES), lambda m, n: (m, 0))),
        scratch_shapes=[pltpu.VMEM((tm, D), BF16)],
        compiler_params=_params(("parallel", "arbitrary")),
        name="in_proj",
    )(x2d, sc, sh, w_main, w_gate)


def _cumsum_rows(x):
    n = x.shape[0]
    row = lax.broadcasted_iota(jnp.int32, x.shape, 0)
    k = 1
    while k < n:
        x = x + jnp.where(row >= k, pltpu.roll(x, k, axis=0), 0.0)
        k *= 2
    return x


def _mlstm_kernel(qk_ref, v_ref, o_ref, g_ref, cw_ref, cb_ref, gb_ref, nw_ref, out_ref,
                  c_state, n_state, m_state, tail_ref):
    T = qk_ref.shape[0]
    GW = GROUP_WIDTH

    @pl.when(pl.program_id(1) == 0)
    def _():
        c_state[...] = jnp.zeros_like(c_state)
        n_state[...] = jnp.zeros_like(n_state)
        m_state[...] = jnp.zeros_like(m_state)
        tail_ref[...] = jnp.zeros_like(tail_ref)

    xx = jnp.concatenate([tail_ref[...], qk_ref[...]], axis=0)
    tail_ref[...] = qk_ref[T - 8:T, :]
    conv = cb_ref[...]
    for j in range(MLSTM_CONV):
        off = 8 - (MLSTM_CONV - 1) + j
        conv = conv + cw_ref[j:j + 1, :] * xx[off:off + T, :]
    qk = _silu(conv)

    gates = g_ref[...] + gb_ref[...]
    logf = _neg_softplus(-gates)
    causal = (lax.broadcasted_iota(jnp.int32, (CHUNK, CHUNK), 1)
              <= lax.broadcasted_iota(jnp.int32, (CHUNK, CHUNK), 0))

    for c in range(T // CHUNK):
        rows = slice(c * CHUNK, (c + 1) * CHUNK)
        gi = gates[rows, :]
        bcum = _cumsum_rows(logf[rows, :])
        gi_t = gi.T
        bcum_t = bcum.T
        for h in range(N_HEADS):
            cols = slice(h * HEAD_DIM, (h + 1) * HEAD_DIM)
            q = qk[rows, cols] * (HEAD_DIM ** -0.5)
            k = qk[rows, GW + h * HEAD_DIM:GW + (h + 1) * HEAD_DIM]
            v = v_ref[rows, cols]
            b_col = bcum[:, N_HEADS + h:N_HEADS + h + 1]
            b_row = bcum_t[N_HEADS + h:N_HEADS + h + 1, :]
            i_col = gi[:, h:h + 1]
            i_row = gi_t[h:h + 1, :]
            g_tot = b_col[CHUNK - 1:CHUNK, :]
            m_prev = m_state[h:h + 1, 0:1]
            c_prev = c_state[h]
            n_prev = n_state[h:h + 1, :]

            log_d = jnp.where(causal, b_col - b_row + i_row, -jnp.inf)
            m_inter = b_col + m_prev
            m_t = jnp.maximum(m_inter, jnp.max(log_d, axis=-1, keepdims=True))
            s = lax.dot_general(q.astype(BF16), k.astype(BF16), (((1,), (1,)), ((), ())),
                                preferred_element_type=F32)
            w = jnp.exp(log_d - m_t) * s
            inter = jnp.exp(m_inter - m_t)
            num = inter * _bdot(q, c_prev) + _bdot(w, v)
            den = inter * jnp.sum(q * n_prev, axis=-1, keepdims=True) + jnp.sum(w, axis=-1, keepdims=True)
            hcur = num / jnp.maximum(jnp.abs(den), jnp.exp(-m_t))

            a_col = g_tot - b_col + i_col
            m_new = jnp.maximum(g_tot + m_prev, jnp.max(a_col, axis=0, keepdims=True))
            decay = jnp.exp(g_tot + m_prev - m_new)
            kw = k * jnp.exp(a_col - m_new)
            c_state[h] = decay * c_prev + _bdot(kw.T, v)
            n_state[h:h + 1, :] = decay * n_prev + jnp.sum(kw, axis=0, keepdims=True)
            m_state[h:h + 1, :] = jnp.broadcast_to(m_new, (1, LANES))

            mu = jnp.mean(hcur, axis=-1, keepdims=True)
            dlt = hcur - mu
            var = jnp.mean(dlt * dlt, axis=-1, keepdims=True)
            hn = dlt * lax.rsqrt(var + EPS) * nw_ref[:, cols]
            out_ref[rows, cols] = (_sigmoid(o_ref[rows, cols]) * hn).astype(out_ref.dtype)


def _mlstm_mixer(proj, gates, conv_w, conv_b, gate_b, norm_w, B, S):
    T = 256
    GW = GROUP_WIDTH
    gb = jnp.zeros((1, LANES), F32).at[0, :2 * N_HEADS].set(gate_b)
    return pl.pallas_call(
        _mlstm_kernel,
        out_shape=jax.ShapeDtypeStruct((B, S, GW), BF16),
        grid=(B, S // T),
        in_specs=[pl.BlockSpec((None, T, 2 * GW), lambda b, t: (b, t, 0)),
                  pl.BlockSpec((None, T, GW), lambda b, t: (b, t, 2)),
                  pl.BlockSpec((None, T, GW), lambda b, t: (b, t, 3)),
                  pl.BlockSpec((None, T, LANES), lambda b, t: (b, t, 0)),
                  pl.BlockSpec((MLSTM_CONV, 2 * GW), lambda b, t: (0, 0)),
                  pl.BlockSpec((1, 2 * GW), lambda b, t: (0, 0)),
                  pl.BlockSpec((1, LANES), lambda b, t: (0, 0)),
                  pl.BlockSpec((1, GW), lambda b, t: (0, 0))],
        out_specs=pl.BlockSpec((None, T, GW), lambda b, t: (b, t, 0)),
        scratch_shapes=[pltpu.VMEM((N_HEADS, HEAD_DIM, HEAD_DIM), F32),
                        pltpu.VMEM((8, LANES), F32),
                        pltpu.VMEM((8, LANES), F32),
                        pltpu.VMEM((8, 2 * GW), F32)],
        compiler_params=_params(("parallel", "arbitrary")),
        name="mlstm_mixer",
    )(proj, proj, proj, gates, conv_w, conv_b.reshape(1, -1), gb, norm_w.reshape(1, -1))


def _pool_kernel(x_ref, w_ref, scale_ref, gn_ref, out_ref, tail_ref):
    T = x_ref.shape[0]
    HALO = 16
    t_blk = pl.program_id(1)

    @pl.when(t_blk == 0)
    def _():
        tail_ref[...] = jnp.zeros_like(tail_ref)

    x = x_ref[...]
    xx = jnp.concatenate([tail_ref[...], x], axis=0)
    tail_ref[...] = x[T - HALO:T, :]
    pos = t_blk * T + lax.broadcasted_iota(jnp.int32, (T, 1), 0) + 1
    outs = []
    for g, win in enumerate(POOL_WINDOWS):
        cols = slice(g * LANES, (g + 1) * LANES)
        s = xx[:, cols]
        k = 1
        while k < win:
            s = s + pltpu.roll(s, k, axis=0)
            k *= 2
        cnt = jnp.minimum(pos, win).astype(F32)
        yg = s[HALO:, :] / cnt - x[:, cols]
        outs.append(_bdot(yg, w_ref[g]))
    y = jnp.concatenate(outs, axis=-1) * scale_ref[...]
    out_ref[...] = _rms_rows(y, gn_ref[...]).astype(out_ref.dtype)


def _pool_mixer(proj, pool_w, pool_scale, gn, B, S):
    T = 512
    GW = GROUP_WIDTH
    return pl.pallas_call(
        _pool_kernel,
        out_shape=jax.ShapeDtypeStruct((B, S, GW), BF16),
        grid=(B, S // T),
        in_specs=[pl.BlockSpec((None, T, GW), lambda b, t: (b, t, 4)),
                  pl.BlockSpec((len(POOL_WINDOWS), LANES, LANES), lambda b, t: (0, 0, 0)),
                  pl.BlockSpec((1, GW), lambda b, t: (0, 0)),
                  pl.BlockSpec((1, GW), lambda b, t: (0, 0))],
        out_specs=pl.BlockSpec((None, T, GW), lambda b, t: (b, t, 0)),
        scratch_shapes=[pltpu.VMEM((16, GW), F32)],
        compiler_params=_params(("parallel", "arbitrary")),
        name="pool_mixer",
    )(proj, pool_w.astype(BF16), pool_scale.reshape(1, -1), gn.reshape(1, -1))


def _sb_kernel(q_ref, k_ref, v_ref, out_ref):
    LQ = q_ref.shape[0]
    LK = LQ
    qi = pl.program_id(2)
    q = q_ref[...].astype(BF16)
    scale = HEAD_DIM ** -0.5
    tri = (lax.broadcasted_iota(jnp.int32, (LK, LK), 0)
           > lax.broadcasted_iota(jnp.int32, (LK, LK), 1)).astype(BF16)
    q_pos = qi * LQ + lax.broadcasted_iota(jnp.int32, (LQ, LK), 0)
    k_off = lax.broadcasted_iota(jnp.int32, (LQ, LK), 1)

    def body(it, carry):
        after, acc = carry
        kb = qi - it
        start = pl.multiple_of(kb * LK, LK)
        k_blk = k_ref[pl.ds(start, LK), :].astype(BF16)
        v_blk = v_ref[pl.ds(start, LK), :].astype(BF16)
        z = lax.dot_general(q, k_blk, (((1,), (1,)), ((), ())), preferred_element_type=F32) * scale
        strict = (kb * LK + k_off) < q_pos
        lk = jnp.where(strict, _neg_softplus(z), 0.0)
        hi = lk.astype(BF16)
        lo = (lk - hi.astype(F32)).astype(BF16)
        inner = (jnp.dot(hi, tri, preferred_element_type=F32)
                 + jnp.dot(lo, tri, preferred_element_type=F32))
        a = jnp.where(strict, jnp.exp(z + lk + inner + after), 0.0)
        acc = acc + jnp.dot(a.astype(BF16), v_blk, preferred_element_type=F32)
        after = after + jnp.sum(lk, axis=-1, keepdims=True)
        return after, acc

    init = (jnp.zeros((LQ, 1), F32), jnp.zeros((LQ, HEAD_DIM), F32))
    _, acc = lax.fori_loop(0, qi + 1, body, init)
    out_ref[...] = acc


def _sb_mixer(proj, B, S):
    LQ = 256
    GW = GROUP_WIDTH
    q0, k0, v0 = 5 * N_HEADS, 6 * N_HEADS, 7 * N_HEADS
    return pl.pallas_call(
        _sb_kernel,
        out_shape=jax.ShapeDtypeStruct((B, S, GW), F32),
        grid=(B, N_HEADS, S // LQ),
        in_specs=[pl.BlockSpec((None, LQ, HEAD_DIM), lambda b, h, i: (b, i, q0 + h)),
                  pl.BlockSpec((None, S, HEAD_DIM), lambda b, h, i: (b, 0, k0 + h)),
                  pl.BlockSpec((None, S, HEAD_DIM), lambda b, h, i: (b, 0, v0 + h))],
        out_specs=pl.BlockSpec((None, LQ, HEAD_DIM), lambda b, h, i: (b, i, h)),
        compiler_params=_params(("parallel", "parallel", "arbitrary")),
        name="sb_attention",
    )(proj, proj, proj)


def _conv_kernel(x_ref, dw_ref, dwb_ref, lnw_ref, lnb_ref, pw_ref, pwb_ref, gn_ref, out_ref,
                 hbuf, ybuf):
    T = x_ref.shape[0]
    GW = GROUP_WIDTH
    HALO = 32
    SUB = 32

    @pl.when(pl.program_id(1) == 0)
    def _():
        hbuf[0:HALO, :] = jnp.zeros((HALO, GW), F32)

    @pl.when(pl.program_id(1) > 0)
    def _():
        hbuf[0:HALO, :] = hbuf[T:T + HALO, :]

    hbuf[HALO:HALO + T, :] = x_ref[:, 0:GW] * _sigmoid(x_ref[:, GW:2 * GW])

    def sub(i, carry):
        r0 = pl.multiple_of(i * SUB, SUB)
        acc = jnp.broadcast_to(dwb_ref[...], (SUB, GW))
        win = hbuf[pl.ds(r0, SUB + HALO), :]
        for j in range(CONV_WIDTH):
            off = HALO - CONV_WIDTH + 1 + j
            acc = acc + dw_ref[j:j + 1, :] * win[off:off + SUB, :]
        ybuf[pl.ds(r0, SUB), :] = acc
        return carry

    lax.fori_loop(0, T // SUB, sub, 0)
    hn = _silu(_layer_norm_rows(ybuf[...], lnw_ref[...], lnb_ref[...]))
    y = _bdot(hn, pw_ref[...]) + pwb_ref[...]
    out_ref[...] = _rms_rows(y, gn_ref[...]).astype(out_ref.dtype)


def _conv_mixer(proj, dw_w, dw_b, ln_w, ln_b, pw_w, pw_b, gn, B, S):
    T = 256
    GW = GROUP_WIDTH
    dw_pad = jnp.zeros((32, GW), F32).at[:CONV_WIDTH].set(dw_w)
    r = lambda a: a.reshape(1, -1)
    return pl.pallas_call(
        _conv_kernel,
        out_shape=jax.ShapeDtypeStruct((B, S, GW), BF16),
        grid=(B, S // T),
        in_specs=[pl.BlockSpec((None, T, 2 * GW), lambda b, t: (b, t, 4)),
                  pl.BlockSpec((32, GW), lambda b, t: (0, 0)),
                  pl.BlockSpec((1, GW), lambda b, t: (0, 0)),
                  pl.BlockSpec((1, GW), lambda b, t: (0, 0)),
                  pl.BlockSpec((1, GW), lambda b, t: (0, 0)),
                  pl.BlockSpec((GW, GW), lambda b, t: (0, 0)),
                  pl.BlockSpec((1, GW), lambda b, t: (0, 0)),
                  pl.BlockSpec((1, GW), lambda b, t: (0, 0))],
        out_specs=pl.BlockSpec((None, T, GW), lambda b, t: (b, t, 0)),
        scratch_shapes=[pltpu.VMEM((T + 32, GW), F32), pltpu.VMEM((T, GW), F32)],
        compiler_params=_params(("parallel", "arbitrary")),
        name="conv_mixer",
    )(proj, dw_pad, r(dw_b), r(ln_w), r(ln_b), pw_w.astype(BF16), r(pw_b), r(gn))


def _outproj_kernel(with_router, ya_ref, yb_ref, yc_ref, yd_ref, gnc_ref, w_ref, x_ref, g1_ref,
                    lnw_ref, lnb_ref, sc_ref, sh_ref, *rest):
    GW = GROUP_WIDTH
    if with_router:
        rw_ref, rb_ref, x1_ref, u2_ref, gate_ref = rest
    else:
        x1_ref, u2_ref = rest
    yc = _rms_rows(yc_ref[...], gnc_ref[...]).astype(BF16)
    acc = jnp.dot(ya_ref[...], w_ref[0:GW, :], preferred_element_type=F32)
    acc = acc + jnp.dot(yb_ref[...], w_ref[GW:2 * GW, :], preferred_element_type=F32)
    acc = acc + jnp.dot(yc, w_ref[2 * GW:3 * GW, :], preferred_element_type=F32)
    acc = acc + jnp.dot(yd_ref[...], w_ref[3 * GW:4 * GW, :], preferred_element_type=F32)
    r = ALPHA * x_ref[...] + (1.0 + g1_ref[...]) * acc
    x1 = _layer_norm_rows(r, lnw_ref[...], lnb_ref[...])
    x1_ref[...] = x1
    u2 = x1 * (1.0 + sc_ref[...]) + sh_ref[...]
    u2_ref[...] = u2.astype(BF16)
    if with_router:
        logits = jnp.dot(u2, rw_ref[...], preferred_element_type=F32,
                         precision=lax.Precision.HIGHEST) + rb_ref[...]
        lane = lax.broadcasted_iota(jnp.int32, logits.shape, 1)
        lg = jnp.where(lane < N_EXPERTS, logits, -jnp.inf)
        m1 = jnp.max(lg, axis=-1, keepdims=True)
        i1 = jnp.min(jnp.where(lg == m1, lane, LANES), axis=-1, keepdims=True)
        lg2 = jnp.where(lane == i1, -jnp.inf, lg)
        m2 = jnp.max(lg2, axis=-1, keepdims=True)
        i2 = jnp.min(jnp.where(lg2 == m2, lane, LANES), axis=-1, keepdims=True)
        e2 = jnp.exp(m2 - m1)
        w1 = 1.0 / (1.0 + e2)
        w2 = e2 / (1.0 + e2)
        gate_ref[...] = jnp.where(lane == i1, w1, 0.0) + jnp.where(lane == i2, w2, 0.0)


def _out_proj(ya, yb, yc, yd, gnc, w_out, x2d, g1, ln_w, ln_b, sc2, sh2, seq, router=None):
    N, D = x2d.shape
    GW = GROUP_WIDTH
    tm = 256
    tpb = seq // tm
    row = lambda m: (m, 0)
    const = lambda m: (0, 0)
    perb = lambda m: (m // tpb, 0, 0)
    in_specs = [pl.BlockSpec((tm, GW), row), pl.BlockSpec((tm, GW), row),
                pl.BlockSpec((tm, GW), row), pl.BlockSpec((tm, GW), row),
                pl.BlockSpec((1, GW), const),
                pl.BlockSpec((D, D), const),
                pl.BlockSpec((tm, D), row),
                pl.BlockSpec((None, 1, D), perb),
                pl.BlockSpec((1, D), const), pl.BlockSpec((1, D), const),
                pl.BlockSpec((None, 1, D), perb), pl.BlockSpec((None, 1, D), perb)]
    args = [ya, yb, yc, yd, gnc.reshape(1, -1), w_out, x2d, g1, ln_w.reshape(1, -1),
            ln_b.reshape(1, -1), sc2, sh2]
    out_shape = [jax.ShapeDtypeStruct((N, D), F32), jax.ShapeDtypeStruct((N, D), BF16)]
    out_specs = [pl.BlockSpec((tm, D), row), pl.BlockSpec((tm, D), row)]
    if router is not None:
        rw, rb = router
        in_specs += [pl.BlockSpec((D, LANES), const), pl.BlockSpec((1, LANES), const)]
        args += [rw, rb]
        out_shape.append(jax.ShapeDtypeStruct((N, LANES), F32))
        out_specs.append(pl.BlockSpec((tm, LANES), row))
    return pl.pallas_call(
        functools.partial(_outproj_kernel, router is not None),
        out_shape=tuple(out_shape),
        grid=(N // tm,),
        in_specs=in_specs,
        out_specs=tuple(out_specs),
        compiler_params=_params(("parallel",)),
        name="out_proj_ln",
    )(*args)


def _ffn_kernel(with_gates, u_ref, wg_ref, wu_ref, wd_ref, *rest):
    if with_gates:
        gate_ref, x_ref, g2_ref, lnw_ref, lnb_ref, out_ref, acc_ref = rest
    else:
        x_ref, g2_ref, lnw_ref, lnb_ref, out_ref, acc_ref = rest
    e = pl.program_id(1)
    f = pl.program_id(2)

    @pl.when((e == 0) & (f == 0))
    def _():
        acc_ref[...] = jnp.zeros_like(acc_ref)

    u = u_ref[...]
    hg = jnp.dot(u, wg_ref[...], preferred_element_type=F32)
    hu = jnp.dot(u, wu_ref[...], preferred_element_type=F32)
    h = _silu(hg) * hu
    if with_gates:
        g = gate_ref[...]
        lane = lax.broadcasted_iota(jnp.int32, g.shape, 1)
        h = h * jnp.sum(jnp.where(lane == e, g, 0.0), axis=-1, keepdims=True)
    acc_ref[...] += jnp.dot(h.astype(BF16), wd_ref[...], preferred_element_type=F32)

    @pl.when((e == pl.num_programs(1) - 1) & (f == pl.num_programs(2) - 1))
    def _():
        r = ALPHA * x_ref[...] + (1.0 + g2_ref[...]) * acc_ref[...]
        out_ref[...] = _layer_norm_rows(r, lnw_ref[...], lnb_ref[...])


def _ffn(u2, wg, wu, wd, x1, g2, ln_w, ln_b, seq, tf, gates=None):
    N, D = u2.shape
    E, _, F = wg.shape
    tm = 512
    tpb = seq // tm
    row = lambda m, e, f: (m, 0)
    const = lambda m, e, f: (0, 0)
    perb = lambda m, e, f: (m // tpb, 0, 0)
    in_specs = [pl.BlockSpec((tm, D), row),
                pl.BlockSpec((None, D, tf), lambda m, e, f: (e, 0, f)),
                pl.BlockSpec((None, D, tf), lambda m, e, f: (e, 0, f)),
                pl.BlockSpec((None, tf, D), lambda m, e, f: (e, f, 0))]
    args = [u2, wg, wu, wd]
    if gates is not None:
        in_specs.append(pl.BlockSpec((tm, LANES), row))
        args.append(gates)
    in_specs += [pl.BlockSpec((tm, D), row), pl.BlockSpec((None, 1, D), perb),
                 pl.BlockSpec((1, D), const), pl.BlockSpec((1, D), const)]
    args += [x1, g2, ln_w.reshape(1, -1), ln_b.reshape(1, -1)]
    return pl.pallas_call(
        functools.partial(_ffn_kernel, gates is not None),
        out_shape=jax.ShapeDtypeStruct((N, D), F32),
        grid=(N // tm, E, F // tf),
        in_specs=in_specs,
        out_specs=pl.BlockSpec((tm, D), row),
        scratch_shapes=[pltpu.VMEM((tm, D), F32)],
        compiler_params=_params(("parallel", "arbitrary", "arbitrary")),
        name="ffn_ln",
    )(*args)


def _split_w_in(w):
    GW = GROUP_WIDTH
    g0 = 4 * GW
    g1 = g0 + 2 * N_HEADS
    main = jnp.concatenate([w[:, :g0], w[:, g1:]], axis=1).astype(BF16)
    gate = jnp.zeros((w.shape[0], LANES), BF16).at[:, :2 * N_HEADS].set(w[:, g0:g1].astype(BF16))
    return main, gate


def kernel(x, c, w_in, mlstm_conv_w, mlstm_conv_b, mlstm_gate_b, mlstm_norm_w, pool_w, pool_scale, conv_dw_w, conv_dw_b, conv_ln_w, conv_ln_b, conv_pw_w, conv_pw_b, group_norm_w, w_out, ada_w, ada_b, ln1_w, ln1_b, ln2_w, ln2_b, ffn_w_gate, ffn_w_up, ffn_w_down, moe_router_w, moe_router_b, moe_w_gate, moe_w_up, moe_w_down):
    B, S, D = x.shape
    GW = GROUP_WIDTH
    ada = _ada_all(c, ada_w, ada_b)
    x2d = x.reshape(B * S, D)
    for l in range(DEPTH):
        mod = [ada[l, :, i * D:(i + 1) * D].reshape(B, 1, D) for i in range(6)]
        sh1, sc1, g1, sh2, sc2, g2 = mod
        w_main, w_gate = _split_w_in(w_in[l])
        proj, gates = _in_proj(x2d, sc1, sh1, w_main, w_gate, S)
        proj = proj.reshape(B, S, -1)
        gates = gates.reshape(B, S, LANES)
        gn_b, gn_c, gn_d = (group_norm_w[l, i * GW:(i + 1) * GW] for i in range(3))
        ya = _mlstm_mixer(proj, gates, mlstm_conv_w[l], mlstm_conv_b[l], mlstm_gate_b[l],
                          mlstm_norm_w[l], B, S)
        yb = _pool_mixer(proj, pool_w[l], pool_scale[l], gn_b, B, S)
        yc = _sb_mixer(proj, B, S)
        yd = _conv_mixer(proj, conv_dw_w[l], conv_dw_b[l], conv_ln_w[l], conv_ln_b[l],
                         conv_pw_w[l], conv_pw_b[l], gn_d, B, S)
        flat = lambda t: t.reshape(B * S, GW)
        j = l // 2
        router = None
        if l % 2 == 1:
            rw = jnp.zeros((D, LANES), F32).at[:, :N_EXPERTS].set(moe_router_w[j])
            rb = jnp.zeros((1, LANES), F32).at[0, :N_EXPERTS].set(moe_router_b[j])
            router = (rw, rb)
        outs = _out_proj(flat(ya), flat(yb), flat(yc), flat(yd), gn_c, w_out[l].astype(BF16), x2d,
                         g1, ln1_w[l], ln1_b[l], sc2, sh2, S, router)
        if l % 2 == 0:
            x1, u2 = outs
            x2d = _ffn(u2, ffn_w_gate[j][None].astype(BF16), ffn_w_up[j][None].astype(BF16),
                       ffn_w_down[j][None].astype(BF16), x1, g2, ln2_w[l], ln2_b[l], S, 512)
        else:
            x1, u2, moe_gates = outs
            x2d = _ffn(u2, moe_w_gate[j].astype(BF16), moe_w_up[j].astype(BF16),
                       moe_w_down[j].astype(BF16), x1, g2, ln2_w[l], ln2_b[l], S, 256,
                       gates=moe_gates)
    return x2d.reshape(B, S, D)
```

```python
import functools

import jax
import jax.numpy as jnp
from jax import lax
from jax.experimental import pallas as pl
from jax.experimental.pallas import tpu as pltpu

F32 = jnp.float32
BF16 = jnp.bfloat16

DEPTH = 2
CHUNK = 64
GROUP_WIDTH = 512
N_HEADS = 4
HEAD_DIM = 128
MLSTM_CONV = 4
POOL_WINDOWS = (2, 4, 8, 16)
CONV_WIDTH = 31
N_EXPERTS = 8
ALPHA = (2.0 * DEPTH) ** 0.25
EPS = 1e-5
LANES = 128
VMEM_LIMIT = 56 * 1024 * 1024


def _params(sem, vmem=VMEM_LIMIT):
    return pltpu.CompilerParams(dimension_semantics=sem, vmem_limit_bytes=vmem)


def _silu(x):
    return x * (1.0 / (1.0 + jnp.exp(-x)))


def _sigmoid(x):
    return 1.0 / (1.0 + jnp.exp(-x))


def _neg_softplus(x):
    return -(jnp.maximum(x, 0.0) + jnp.log1p(jnp.exp(-jnp.abs(x))))


def _layer_norm_rows(r, w, b):
    mu = jnp.mean(r, axis=-1, keepdims=True)
    d = r - mu
    var = jnp.mean(d * d, axis=-1, keepdims=True)
    return d * lax.rsqrt(var + EPS) * w + b


def _rms_rows(y, w):
    return y * lax.rsqrt(jnp.mean(y * y, axis=-1, keepdims=True) + EPS) * w


def _bdot(a, b):
    return jnp.dot(a.astype(BF16), b.astype(BF16), preferred_element_type=F32)


def _ada_kernel(c_ref, w_ref, b_ref, out_ref):
    c = c_ref[...]
    out_ref[...] = jnp.dot(_silu(c), w_ref[...], preferred_element_type=F32,
                           precision=lax.Precision.HIGHEST) + b_ref[...]


def _ada_all(c, ada_w, ada_b):
    L, D, D6 = ada_w.shape
    Bn = c.shape[0]
    tn = 1024
    return pl.pallas_call(
        _ada_kernel,
        out_shape=jax.ShapeDtypeStruct((L, Bn, D6), F32),
        grid=(L, D6 // tn),
        in_specs=[pl.BlockSpec((Bn, D), lambda l, n: (0, 0)),
                  pl.BlockSpec((None, D, tn), lambda l, n: (l, 0, n)),
                  pl.BlockSpec((None, 1, tn), lambda l, n: (l, 0, n))],
        out_specs=pl.BlockSpec((None, Bn, tn), lambda l, n: (l, 0, n)),
        compiler_params=_params(("parallel", "parallel")),
        name="ada_mod",
    )(c, ada_w, ada_b.reshape(L, 1, D6))


def _inproj_kernel(x_ref, sc_ref, sh_ref, w_ref, wg_ref, proj_ref, gates_ref, u_ref):
    @pl.when(pl.program_id(1) == 0)
    def _():
        u = (x_ref[...] * (1.0 + sc_ref[...]) + sh_ref[...]).astype(BF16)
        u_ref[...] = u
        gates_ref[...] = jnp.dot(u, wg_ref[...], preferred_element_type=F32)

    proj_ref[...] = jnp.dot(u_ref[...], w_ref[...], preferred_element_type=F32)


def _in_proj(x2d, sc, sh, w_main, w_gate, seq):
    N, D = x2d.shape
    NC = w_main.shape[1]
    tm, tn = 512, 1280
    tpb = seq // tm
    return pl.pallas_call(
        _inproj_kernel,
        out_shape=(jax.ShapeDtypeStruct((N, NC), F32), jax.ShapeDtypeStruct((N, LANES), F32)),
        grid=(N // tm, NC // tn),
        in_specs=[pl.BlockSpec((tm, D), lambda m, n: (m, 0)),
                  pl.BlockSpec((None, 1, D), lambda m, n: (m // tpb, 0, 0)),
                  pl.BlockSpec((None, 1, D), lambda m, n: (m // tpb, 0, 0)),
                  pl.BlockSpec((D, tn), lambda m, n: (0, n)),
                  pl.BlockSpec((D, LANES), lambda m, n: (0, 0))],
        out_specs=(pl.BlockSpec((tm, tn), lambda m, n: (m, n)),
                   pl.BlockSpec((tm, LANES), lambda m, n: (m, 0))),
        scratch_shapes=[pltpu.VMEM((tm, D), BF16)],
        compiler_params=_params(("parallel", "arbitrary")),
        name="in_proj",
    )(x2d, sc, sh, w_main, w_gate)


def _cumsum_rows(x):
    n = x.shape[0]
    row = lax.broadcasted_iota(jnp.int32, x.shape, 0)
    k = 1
    while k < n:
        x = x + jnp.where(row >= k, pltpu.roll(x, k, axis=0), 0.0)
        k *= 2
    return x


def _mlstm_kernel(qk_ref, v_ref, o_ref, g_ref, cw_ref, cb_ref, gb_ref, nw_ref, out_ref,
                  c_state, n_state, m_state, tail_ref):
    T = qk_ref.shape[0]
    GW = GROUP_WIDTH

    @pl.when(pl.program_id(1) == 0)
    def _():
        c_state[...] = jnp.zeros_like(c_state)
        n_state[...] = jnp.zeros_like(n_state)
        m_state[...] = jnp.zeros_like(m_state)
        tail_ref[...] = jnp.zeros_like(tail_ref)

    xx = jnp.concatenate([tail_ref[...], qk_ref[...]], axis=0)
    tail_ref[...] = qk_ref[T - 8:T, :]
    conv = cb_ref[...]
    for j in range(MLSTM_CONV):
        off = 8 - (MLSTM_CONV - 1) + j
        conv = conv + cw_ref[j:j + 1, :] * xx[off:off + T, :]
    qk = _silu(conv)

    gates = g_ref[...] + gb_ref[...]
    logf = _neg_softplus(-gates)
    causal = (lax.broadcasted_iota(jnp.int32, (CHUNK, CHUNK), 1)
              <= lax.broadcasted_iota(jnp.int32, (CHUNK, CHUNK), 0))

    for c in range(T // CHUNK):
        rows = slice(c * CHUNK, (c + 1) * CHUNK)
        gi = gates[rows, :]
        bcum = _cumsum_rows(logf[rows, :])
        gi_t = gi.T
        bcum_t = bcum.T
        for h in range(N_HEADS):
            cols = slice(h * HEAD_DIM, (h + 1) * HEAD_DIM)
            q = qk[rows, cols] * (HEAD_DIM ** -0.5)
            k = qk[rows, GW + h * HEAD_DIM:GW + (h + 1) * HEAD_DIM]
            v = v_ref[rows, cols]
            b_col = bcum[:, N_HEADS + h:N_HEADS + h + 1]
            b_row = bcum_t[N_HEADS + h:N_HEADS + h + 1, :]
            i_col = gi[:, h:h + 1]
            i_row = gi_t[h:h + 1, :]
            g_tot = b_col[CHUNK - 1:CHUNK, :]
            m_prev = m_state[h:h + 1, 0:1]
            c_prev = c_state[h]
            n_prev = n_state[h:h + 1, :]

            log_d = jnp.where(causal, b_col - b_row + i_row, -jnp.inf)
            m_inter = b_col + m_prev
            m_t = jnp.maximum(m_inter, jnp.max(log_d, axis=-1, keepdims=True))
            s = lax.dot_general(q.astype(BF16), k.astype(BF16), (((1,), (1,)), ((), ())),
                                preferred_element_type=F32)
            w = jnp.exp(log_d - m_t) * s
            inter = jnp.exp(m_inter - m_t)
            num = inter * _bdot(q, c_prev) + _bdot(w, v)
            den = inter * jnp.sum(q * n_prev, axis=-1, keepdims=True) + jnp.sum(w, axis=-1, keepdims=True)
            hcur = num / jnp.maximum(jnp.abs(den), jnp.exp(-m_t))

            a_col = g_tot - b_col + i_col
            m_new = jnp.maximum(g_tot + m_prev, jnp.max(a_col, axis=0, keepdims=True))
            decay = jnp.exp(g_tot + m_prev - m_new)
            kw = k * jnp.exp(a_col - m_new)
            c_state[h] = decay * c_prev + _bdot(kw.T, v)
            n_state[h:h + 1, :] = decay * n_prev + jnp.sum(kw, axis=0, keepdims=True)
            m_state[h:h + 1, :] = jnp.broadcast_to(m_new, (1, LANES))

            mu = jnp.mean(hcur, axis=-1, keepdims=True)
            dlt = hcur - mu
            var = jnp.mean(dlt * dlt, axis=-1, keepdims=True)
            hn = dlt * lax.rsqrt(var + EPS) * nw_ref[:, cols]
            out_ref[rows, cols] = (_sigmoid(o_ref[rows, cols]) * hn).astype(out_ref.dtype)


def _mlstm_mixer(proj, gates, conv_w, conv_b, gate_b, norm_w, B, S):
    T = 256
    GW = GROUP_WIDTH
    gb = jnp.zeros((1, LANES), F32).at[0, :2 * N_HEADS].set(gate_b)
    return pl.pallas_call(
        _mlstm_kernel,
        out_shape=jax.ShapeDtypeStruct((B, S, GW), BF16),
        grid=(B, S // T),
        in_specs=[pl.BlockSpec((None, T, 2 * GW), lambda b, t: (b, t, 0)),
                  pl.BlockSpec((None, T, GW), lambda b, t: (b, t, 2)),
                  pl.BlockSpec((None, T, GW), lambda b, t: (b, t, 3)),
                  pl.BlockSpec((None, T, LANES), lambda b, t: (b, t, 0)),
                  pl.BlockSpec((MLSTM_CONV, 2 * GW), lambda b, t: (0, 0)),
                  pl.BlockSpec((1, 2 * GW), lambda b, t: (0, 0)),
                  pl.BlockSpec((1, LANES), lambda b, t: (0, 0)),
                  pl.BlockSpec((1, GW), lambda b, t: (0, 0))],
        out_specs=pl.BlockSpec((None, T, GW), lambda b, t: (b, t, 0)),
        scratch_shapes=[pltpu.VMEM((N_HEADS, HEAD_DIM, HEAD_DIM), F32),
                        pltpu.VMEM((8, LANES), F32),
                        pltpu.VMEM((8, LANES), F32),
                        pltpu.VMEM((8, 2 * GW), F32)],
        compiler_params=_params(("parallel", "arbitrary")),
        name="mlstm_mixer",
    )(proj, proj, proj, gates, conv_w, conv_b.reshape(1, -1), gb, norm_w.reshape(1, -1))


def _pool_kernel(x_ref, w_ref, scale_ref, gn_ref, out_ref, tail_ref):
    T = x_ref.shape[0]
    HALO = 16
    t_blk = pl.program_id(1)

    @pl.when(t_blk == 0)
    def _():
        tail_ref[...] = jnp.zeros_like(tail_ref)

    x = x_ref[...]
    xx = jnp.concatenate([tail_ref[...], x], axis=0)
    tail_ref[...] = x[T - HALO:T, :]
    pos = t_blk * T + lax.broadcasted_iota(jnp.int32, (T, 1), 0) + 1
    outs = []
    for g, win in enumerate(POOL_WINDOWS):
        cols = slice(g * LANES, (g + 1) * LANES)
        s = xx[:, cols]
        k = 1
        while k < win:
            s = s + pltpu.roll(s, k, axis=0)
            k *= 2
        cnt = jnp.minimum(pos, win).astype(F32)
        yg = s[HALO:, :] / cnt - x[:, cols]
        outs.append(_bdot(yg, w_ref[g]))
    y = jnp.concatenate(outs, axis=-1) * scale_ref[...]
    out_ref[...] = _rms_rows(y, gn_ref[...]).astype(out_ref.dtype)


def _pool_mixer(proj, pool_w, pool_scale, gn, B, S):
    T = 512
    GW = GROUP_WIDTH
    return pl.pallas_call(
        _pool_kernel,
        out_shape=jax.ShapeDtypeStruct((B, S, GW), BF16),
        grid=(B, S // T),
        in_specs=[pl.BlockSpec((None, T, GW), lambda b, t: (b, t, 4)),
                  pl.BlockSpec((len(POOL_WINDOWS), LANES, LANES), lambda b, t: (0, 0, 0)),
                  pl.BlockSpec((1, GW), lambda b, t: (0, 0)),
                  pl.BlockSpec((1, GW), lambda b, t: (0, 0))],
        out_specs=pl.BlockSpec((None, T, GW), lambda b, t: (b, t, 0)),
        scratch_shapes=[pltpu.VMEM((16, GW), F32)],
        compiler_params=_params(("parallel", "arbitrary")),
        name="pool_mixer",
    )(proj, pool_w.astype(BF16), pool_scale.reshape(1, -1), gn.reshape(1, -1))


def _sb_kernel(q_ref, k_ref, v_ref, out_ref):
    LQ = q_ref.shape[0]
    LK = LQ
    qi = pl.program_id(2)
    q = q_ref[...].astype(BF16)
    scale = HEAD_DIM ** -0.5
    tri = (lax.broadcasted_iota(jnp.int32, (LK, LK), 0)
           > lax.broadcasted_iota(jnp.int32, (LK, LK), 1)).astype(BF16)
    q_pos = qi * LQ + lax.broadcasted_iota(jnp.int32, (LQ, LK), 0)
    k_off = lax.broadcasted_iota(jnp.int32, (LQ, LK), 1)

    def body(it, carry):
        after, acc = carry
        kb = qi - it
        start = pl.multiple_of(kb * LK, LK)
        k_blk = k_ref[pl.ds(start, LK), :].astype(BF16)
        v_blk = v_ref[pl.ds(start, LK), :].astype(BF16)
        z = lax.dot_general(q, k_blk, (((1,), (1,)), ((), ())), preferred_element_type=F32) * scale
        strict = (kb * LK + k_off) < q_pos
        lk = jnp.where(strict, _neg_softplus(z), 0.0)
        hi = lk.astype(BF16)
        lo = (lk - hi.astype(F32)).astype(BF16)
        inner = (jnp.dot(hi, tri, preferred_element_type=F32)
                 + jnp.dot(lo, tri, preferred_element_type=F32))
        a = jnp.where(strict, jnp.exp(z + lk + inner + after), 0.0)
        acc = acc + jnp.dot(a.astype(BF16), v_blk, preferred_element_type=F32)
        after = after + jnp.sum(lk, axis=-1, keepdims=True)
        return after, acc

    init = (jnp.zeros((LQ, 1), F32), jnp.zeros((LQ, HEAD_DIM), F32))
    _, acc = lax.fori_loop(0, qi + 1, body, init)
    out_ref[...] = acc


def _sb_mixer(proj, B, S):
    LQ = 256
    GW = GROUP_WIDTH
    q0, k0, v0 = 5 * N_HEADS, 6 * N_HEADS, 7 * N_HEADS
    return pl.pallas_call(
        _sb_kernel,
        out_shape=jax.ShapeDtypeStruct((B, S, GW), F32),
        grid=(B, N_HEADS, S // LQ),
        in_specs=[pl.BlockSpec((None, LQ, HEAD_DIM), lambda b, h, i: (b, i, q0 + h)),
                  pl.BlockSpec((None, S, HEAD_DIM), lambda b, h, i: (b, 0, k0 + h)),
                  pl.BlockSpec((None, S, HEAD_DIM), lambda b, h, i: (b, 0, v0 + h))],
        out_specs=pl.BlockSpec((None, LQ, HEAD_DIM), lambda b, h, i: (b, i, h)),
        compiler_params=_params(("parallel", "parallel", "arbitrary")),
        name="sb_attention",
    )(proj, proj, proj)


def _conv_kernel(x_ref, dw_ref, dwb_ref, lnw_ref, lnb_ref, pw_ref, pwb_ref, gn_ref, out_ref,
                 hbuf, ybuf):
    T = x_ref.shape[0]
    GW = GROUP_WIDTH
    HALO = 32
    SUB = 32

    @pl.when(pl.program_id(1) == 0)
    def _():
        hbuf[0:HALO, :] = jnp.zeros((HALO, GW), F32)

    @pl.when(pl.program_id(1) > 0)
    def _():
        hbuf[0:HALO, :] = hbuf[T:T + HALO, :]

    hbuf[HALO:HALO + T, :] = x_ref[:, 0:GW] * _sigmoid(x_ref[:, GW:2 * GW])

    def sub(i, carry):
        r0 = pl.multiple_of(i * SUB, SUB)
        acc = jnp.broadcast_to(dwb_ref[...], (SUB, GW))
        win = hbuf[pl.ds(r0, SUB + HALO), :]
        for j in range(CONV_WIDTH):
            off = HALO - CONV_WIDTH + 1 + j
            acc = acc + dw_ref[j:j + 1, :] * win[off:off + SUB, :]
        ybuf[pl.ds(r0, SUB), :] = acc
        return carry

    lax.fori_loop(0, T // SUB, sub, 0)
    hn = _silu(_layer_norm_rows(ybuf[...], lnw_ref[...], lnb_ref[...]))
    y = _bdot(hn, pw_ref[...]) + pwb_ref[...]
    out_ref[...] = _rms_rows(y, gn_ref[...]).astype(out_ref.dtype)


def _conv_mixer(proj, dw_w, dw_b, ln_w, ln_b, pw_w, pw_b, gn, B, S):
    T = 256
    GW = GROUP_WIDTH
    dw_pad = jnp.zeros((32, GW), F32).at[:CONV_WIDTH].set(dw_w)
    r = lambda a: a.reshape(1, -1)
    return pl.pallas_call(
        _conv_kernel,
        out_shape=jax.ShapeDtypeStruct((B, S, GW), BF16),
        grid=(B, S // T),
        in_specs=[pl.BlockSpec((None, T, 2 * GW), lambda b, t: (b, t, 4)),
                  pl.BlockSpec((32, GW), lambda b, t: (0, 0)),
                  pl.BlockSpec((1, GW), lambda b, t: (0, 0)),
                  pl.BlockSpec((1, GW), lambda b, t: (0, 0)),
                  pl.BlockSpec((1, GW), lambda b, t: (0, 0)),
                  pl.BlockSpec((GW, GW), lambda b, t: (0, 0)),
                  pl.BlockSpec((1, GW), lambda b, t: (0, 0)),
                  pl.BlockSpec((1, GW), lambda b, t: (0, 0))],
        out_specs=pl.BlockSpec((None, T, GW), lambda b, t: (b, t, 0)),
        scratch_shapes=[pltpu.VMEM((T + 32, GW), F32), pltpu.VMEM((T, GW), F32)],
        compiler_params=_params(("parallel", "arbitrary")),
        name="conv_mixer",
    )(proj, dw_pad, r(dw_b), r(ln_w), r(ln_b), pw_w.astype(BF16), r(pw_b), r(gn))


def _outproj_kernel(with_router, ya_ref, yb_ref, yc_ref, yd_ref, gnc_ref, w_ref, x_ref, g1_ref,
                    lnw_ref, lnb_ref, sc_ref, sh_ref, *rest):
    GW = GROUP_WIDTH
    if with_router:
        rw_ref, rb_ref, x1_ref, u2_ref, ids_ref, topw_ref = rest
    else:
        x1_ref, u2_ref = rest
    yc = _rms_rows(yc_ref[...], gnc_ref[...]).astype(BF16)
    acc = jnp.dot(ya_ref[...], w_ref[0:GW, :], preferred_element_type=F32)
    acc = acc + jnp.dot(yb_ref[...], w_ref[GW:2 * GW, :], preferred_element_type=F32)
    acc = acc + jnp.dot(yc, w_ref[2 * GW:3 * GW, :], preferred_element_type=F32)
    acc = acc + jnp.dot(yd_ref[...], w_ref[3 * GW:4 * GW, :], preferred_element_type=F32)
    r = ALPHA * x_ref[...] + (1.0 + g1_ref[...]) * acc
    x1 = _layer_norm_rows(r, lnw_ref[...], lnb_ref[...])
    x1_ref[...] = x1
    u2 = x1 * (1.0 + sc_ref[...]) + sh_ref[...]
    u2_ref[...] = u2.astype(u2_ref.dtype)
    if with_router:
        u_hi = u2.astype(BF16)
        u_lo = (u2 - u_hi.astype(F32)).astype(BF16)
        p = jnp.dot(u_hi, rw_ref[...], preferred_element_type=F32)
        logits = (p[:, :LANES] + p[:, LANES:]
                  + jnp.dot(u_lo, rw_ref[:, :LANES], preferred_element_type=F32) + rb_ref[...])
        lane = lax.broadcasted_iota(jnp.int32, logits.shape, 1)
        lg = jnp.where(lane < N_EXPERTS, logits, -jnp.inf)
        m1 = jnp.max(lg, axis=-1, keepdims=True)
        i1 = jnp.min(jnp.where(lg == m1, lane, LANES), axis=-1, keepdims=True)
        lg2 = jnp.where(lane == i1, -jnp.inf, lg)
        m2 = jnp.max(lg2, axis=-1, keepdims=True)
        i2 = jnp.min(jnp.where(lg2 == m2, lane, LANES), axis=-1, keepdims=True)
        e2 = jnp.exp(m2 - m1)
        w1 = 1.0 / (1.0 + e2)
        w2 = e2 / (1.0 + e2)
        ids_ref[...] = jnp.where(lane == 0, i1, jnp.where(lane == 1, i2, 0))
        topw_ref[...] = jnp.where(lane == 0, w1, jnp.where(lane == 1, w2, 0.0))


def _out_proj(ya, yb, yc, yd, gnc, w_out, x2d, g1, ln_w, ln_b, sc2, sh2, seq, router=None):
    N, D = x2d.shape
    GW = GROUP_WIDTH
    tm = 256
    tpb = seq // tm
    row = lambda m: (m, 0)
    const = lambda m: (0, 0)
    perb = lambda m: (m // tpb, 0, 0)
    in_specs = [pl.BlockSpec((tm, GW), row), pl.BlockSpec((tm, GW), row),
                pl.BlockSpec((tm, GW), row), pl.BlockSpec((tm, GW), row),
                pl.BlockSpec((1, GW), const),
                pl.BlockSpec((D, D), const),
                pl.BlockSpec((tm, D), row),
                pl.BlockSpec((None, 1, D), perb),
                pl.BlockSpec((1, D), const), pl.BlockSpec((1, D), const),
                pl.BlockSpec((None, 1, D), perb), pl.BlockSpec((None, 1, D), perb)]
    args = [ya, yb, yc, yd, gnc.reshape(1, -1), w_out, x2d, g1, ln_w.reshape(1, -1),
            ln_b.reshape(1, -1), sc2, sh2]
    out_shape = [jax.ShapeDtypeStruct((N, D), F32),
                 jax.ShapeDtypeStruct((N, D), BF16 if router is None else F32)]
    out_specs = [pl.BlockSpec((tm, D), row), pl.BlockSpec((tm, D), row)]
    if router is not None:
        rw, rb = router
        in_specs += [pl.BlockSpec((D, 2 * LANES), const), pl.BlockSpec((1, LANES), const)]
        args += [rw, rb]
        out_shape += [jax.ShapeDtypeStruct((N, LANES), jnp.int32), jax.ShapeDtypeStruct((N, LANES), F32)]
        out_specs += [pl.BlockSpec((tm, LANES), row), pl.BlockSpec((tm, LANES), row)]
    return pl.pallas_call(
        functools.partial(_outproj_kernel, router is not None),
        out_shape=tuple(out_shape),
        grid=(N // tm,),
        in_specs=in_specs,
        out_specs=tuple(out_specs),
        compiler_params=_params(("parallel",)),
        name="out_proj_ln",
    )(*args)


def _ffn_kernel(u_ref, wg_ref, wu_ref, wd_ref, x_ref, g2_ref, lnw_ref, lnb_ref, out_ref, acc_ref):
    f = pl.program_id(1)

    @pl.when(f == 0)
    def _():
        acc_ref[...] = jnp.zeros_like(acc_ref)

    u = u_ref[...]
    hg = jnp.dot(u, wg_ref[...], preferred_element_type=F32)
    hu = jnp.dot(u, wu_ref[...], preferred_element_type=F32)
    h = _silu(hg) * hu
    acc_ref[...] += jnp.dot(h.astype(BF16), wd_ref[...], preferred_element_type=F32)

    @pl.when(f == pl.num_programs(1) - 1)
    def _():
        r = ALPHA * x_ref[...] + (1.0 + g2_ref[...]) * acc_ref[...]
        out_ref[...] = _layer_norm_rows(r, lnw_ref[...], lnb_ref[...])


def _ffn(u2, wg, wu, wd, x1, g2, ln_w, ln_b, seq):
    N, D = u2.shape
    F = wg.shape[1]
    tm, tf = 512, 512
    tpb = seq // tm
    row = lambda m, f: (m, 0)
    const = lambda m, f: (0, 0)
    return pl.pallas_call(
        _ffn_kernel,
        out_shape=jax.ShapeDtypeStruct((N, D), F32),
        grid=(N // tm, F // tf),
        in_specs=[pl.BlockSpec((tm, D), row),
                  pl.BlockSpec((D, tf), lambda m, f: (0, f)),
                  pl.BlockSpec((D, tf), lambda m, f: (0, f)),
                  pl.BlockSpec((tf, D), lambda m, f: (f, 0)),
                  pl.BlockSpec((tm, D), row),
                  pl.BlockSpec((None, 1, D), lambda m, f: (m // tpb, 0, 0)),
                  pl.BlockSpec((1, D), const), pl.BlockSpec((1, D), const)],
        out_specs=pl.BlockSpec((tm, D), row),
        scratch_shapes=[pltpu.VMEM((tm, D), F32)],
        compiler_params=_params(("parallel", "arbitrary")),
        name="ffn_ln",
    )(u2, wg, wu, wd, x1, g2, ln_w.reshape(1, -1), ln_b.reshape(1, -1))


MOE_TM = 512


def _row_copy(src, src_row, dst, dst_row, sem):
    return pltpu.make_async_copy(src.at[pl.ds(src_row, 1), :], dst.at[pl.ds(dst_row, 1), :], sem)


def _dispatch_kernel(pos_ref, u_ref, zeros_ref, out_ref, sem):
    del zeros_ref
    tm = u_ref.shape[0]
    base = pl.program_id(0) * tm

    def start(r, carry):
        for k in range(2):
            _row_copy(u_ref, r, out_ref, pos_ref[2 * (base + r) + k], sem).start()
        return carry

    def wait(r, carry):
        for k in range(2):
            _row_copy(u_ref, 0, out_ref, 0, sem).wait()
        return carry

    lax.fori_loop(0, tm, start, 0)
    lax.fori_loop(0, tm, wait, 0)


def _dispatch(u2, pos, n_rows):
    N, D = u2.shape
    tm = 256
    return pl.pallas_call(
        _dispatch_kernel,
        out_shape=jax.ShapeDtypeStruct((n_rows, D), u2.dtype),
        grid_spec=pltpu.PrefetchScalarGridSpec(
            num_scalar_prefetch=1,
            grid=(N // tm,),
            in_specs=[pl.BlockSpec((tm, D), lambda m, pos: (m, 0)),
                      pl.BlockSpec(memory_space=pl.ANY)],
            out_specs=pl.BlockSpec(memory_space=pl.ANY),
            scratch_shapes=[pltpu.SemaphoreType.DMA(())]),
        input_output_aliases={2: 0},
        compiler_params=_params(("arbitrary",)),
        name="moe_dispatch",
    )(pos, u2, jnp.zeros((n_rows, D), u2.dtype))


def _moe_ffn_kernel(te_ref, nv_ref, u_ref, wg_ref, wu_ref, wd_ref, out_ref, ub_ref):
    t = pl.program_id(0)
    f = pl.program_id(1)

    @pl.when(t < nv_ref[0])
    def _():
        @pl.when(f == 0)
        def _():
            ub_ref[...] = u_ref[...].astype(BF16)
            out_ref[...] = jnp.zeros_like(out_ref)

        u = ub_ref[...]
        hg = jnp.dot(u, wg_ref[...], preferred_element_type=F32)
        hu = jnp.dot(u, wu_ref[...], preferred_element_type=F32)
        h = _silu(hg) * hu
        out_ref[...] += jnp.dot(h.astype(BF16), wd_ref[...], preferred_element_type=F32)

    @pl.when((t >= nv_ref[0]) & (f == 0))
    def _():
        out_ref[...] = jnp.zeros_like(out_ref)


def _moe_ffn(u_sorted, tile_expert, n_valid, wg, wu, wd):
    R, D = u_sorted.shape
    F = wg.shape[2]
    tm, tf = MOE_TM, 256
    nf = F // tf
    tile = lambda t, nv: jnp.minimum(t, nv[0] - 1)
    fidx = lambda t, f, nv: jnp.where(t < nv[0], f, nf - 1)
    return pl.pallas_call(
        _moe_ffn_kernel,
        out_shape=jax.ShapeDtypeStruct((R, D), F32),
        grid_spec=pltpu.PrefetchScalarGridSpec(
            num_scalar_prefetch=2,
            grid=(R // tm, nf),
            in_specs=[pl.BlockSpec((tm, D), lambda t, f, te, nv: (tile(t, nv), 0)),
                      pl.BlockSpec((None, D, tf), lambda t, f, te, nv: (te[t], 0, fidx(t, f, nv))),
                      pl.BlockSpec((None, D, tf), lambda t, f, te, nv: (te[t], 0, fidx(t, f, nv))),
                      pl.BlockSpec((None, tf, D), lambda t, f, te, nv: (te[t], fidx(t, f, nv), 0))],
            out_specs=pl.BlockSpec((tm, D), lambda t, f, te, nv: (t, 0)),
            scratch_shapes=[pltpu.VMEM((tm, D), BF16)]),
        compiler_params=_params(("arbitrary", "arbitrary")),
        name="moe_ffn",
    )(tile_expert, n_valid, u_sorted, wg, wu, wd)


def _combine_kernel(pos_ref, y_ref, topw_ref, x_ref, g2_ref, lnw_ref, lnb_ref, out_ref, ybuf, sem):
    tm = x_ref.shape[0]
    base = pl.program_id(0) * tm

    def start(r, carry):
        for k in range(2):
            _row_copy(y_ref, pos_ref[2 * (base + r) + k], ybuf.at[k], r, sem).start()
        return carry

    def wait(r, carry):
        for k in range(2):
            _row_copy(y_ref, 0, ybuf.at[k], 0, sem).wait()
        return carry

    lax.fori_loop(0, tm, start, 0)
    lax.fori_loop(0, tm, wait, 0)
    tw = topw_ref[...]
    y = tw[:, 0:1] * ybuf[0] + tw[:, 1:2] * ybuf[1]
    r = ALPHA * x_ref[...] + (1.0 + g2_ref[...]) * y
    out_ref[...] = _layer_norm_rows(r, lnw_ref[...], lnb_ref[...])


def _combine(y_sorted, pos, topw, x1, g2, ln_w, ln_b, seq):
    N, D = x1.shape
    tm = 256
    tpb = seq // tm
    row = lambda m, pos: (m, 0)
    const = lambda m, pos: (0, 0)
    return pl.pallas_call(
        _combine_kernel,
        out_shape=jax.ShapeDtypeStruct((N, D), F32),
        grid_spec=pltpu.PrefetchScalarGridSpec(
            num_scalar_prefetch=1,
            grid=(N // tm,),
            in_specs=[pl.BlockSpec(memory_space=pl.ANY),
                      pl.BlockSpec((tm, LANES), row),
                      pl.BlockSpec((tm, D), row),
                      pl.BlockSpec((None, 1, D), lambda m, pos: (m // tpb, 0, 0)),
                      pl.BlockSpec((1, D), const), pl.BlockSpec((1, D), const)],
            out_specs=pl.BlockSpec((tm, D), row),
            scratch_shapes=[pltpu.VMEM((2, tm, D), F32), pltpu.SemaphoreType.DMA(())]),
        compiler_params=_params(("arbitrary",)),
        name="moe_combine_ln",
    )(pos, y_sorted, topw, x1, g2, ln_w.reshape(1, -1), ln_b.reshape(1, -1))


def _routing_plan(ids, n_tiles):
    e_flat = ids[:, :2].reshape(-1)
    onehot = (e_flat[:, None] == jnp.arange(N_EXPERTS, dtype=jnp.int32)[None, :]).astype(jnp.int32)
    csum = jnp.cumsum(onehot, axis=0)
    rank = jnp.sum((csum - onehot) * onehot, axis=1)
    counts = csum[-1]
    padded = ((counts + MOE_TM - 1) // MOE_TM) * MOE_TM
    ends = jnp.cumsum(padded)
    offs = ends - padded
    pos = (jnp.sum(onehot * offs[None, :], axis=1) + rank).astype(jnp.int32)
    tile_start = jnp.arange(n_tiles, dtype=jnp.int32) * MOE_TM
    n_valid = (ends[-1] // MOE_TM).astype(jnp.int32)
    tile_start = jnp.minimum(tile_start, (n_valid - 1) * MOE_TM)
    tile_expert = jnp.sum((tile_start[:, None] >= ends[None, :]).astype(jnp.int32), axis=1).astype(jnp.int32)
    return pos, tile_expert, n_valid.reshape(1)


def _moe(u2, ids, topw, wg, wu, wd, x1, g2, ln_w, ln_b, seq):
    N = u2.shape[0]
    n_tiles = 2 * N // MOE_TM + N_EXPERTS
    pos, tile_expert, n_valid = _routing_plan(ids, n_tiles)
    u_sorted = _dispatch(u2, pos, n_tiles * MOE_TM)
    y_sorted = _moe_ffn(u_sorted, tile_expert, n_valid, wg, wu, wd)
    return _combine(y_sorted, pos, topw, x1, g2, ln_w, ln_b, seq)


def _split_w_in(w):
    GW = GROUP_WIDTH
    g0 = 4 * GW
    g1 = g0 + 2 * N_HEADS
    main = jnp.concatenate([w[:, :g0], w[:, g1:]], axis=1).astype(BF16)
    gate = jnp.zeros((w.shape[0], LANES), BF16).at[:, :2 * N_HEADS].set(w[:, g0:g1].astype(BF16))
    return main, gate


def kernel(x, c, w_in, mlstm_conv_w, mlstm_conv_b, mlstm_gate_b, mlstm_norm_w, pool_w, pool_scale, conv_dw_w, conv_dw_b, conv_ln_w, conv_ln_b, conv_pw_w, conv_pw_b, group_norm_w, w_out, ada_w, ada_b, ln1_w, ln1_b, ln2_w, ln2_b, ffn_w_gate, ffn_w_up, ffn_w_down, moe_router_w, moe_router_b, moe_w_gate, moe_w_up, moe_w_down):
    B, S, D = x.shape
    GW = GROUP_WIDTH
    ada = _ada_all(c, ada_w, ada_b)
    x2d = x.reshape(B * S, D)
    for l in range(DEPTH):
        mod = [ada[l, :, i * D:(i + 1) * D].reshape(B, 1, D) for i in range(6)]
        sh1, sc1, g1, sh2, sc2, g2 = mod
        w_main, w_gate = _split_w_in(w_in[l])
        proj, gates = _in_proj(x2d, sc1, sh1, w_main, w_gate, S)
        proj = proj.reshape(B, S, -1)
        gates = gates.reshape(B, S, LANES)
        gn_b, gn_c, gn_d = (group_norm_w[l, i * GW:(i + 1) * GW] for i in range(3))
        ya = _mlstm_mixer(proj, gates, mlstm_conv_w[l], mlstm_conv_b[l], mlstm_gate_b[l],
                          mlstm_norm_w[l], B, S)
        yb = _pool_mixer(proj, pool_w[l], pool_scale[l], gn_b, B, S)
        yc = _sb_mixer(proj, B, S)
        yd = _conv_mixer(proj, conv_dw_w[l], conv_dw_b[l], conv_ln_w[l], conv_ln_b[l],
                         conv_pw_w[l], conv_pw_b[l], gn_d, B, S)
        flat = lambda t: t.reshape(B * S, GW)
        j = l // 2
        router = None
        if l % 2 == 1:
            rw = jnp.zeros((D, LANES), F32).at[:, :N_EXPERTS].set(moe_router_w[j])
            rw_hi = rw.astype(BF16)
            rw_lo = (rw - rw_hi.astype(F32)).astype(BF16)
            rb = jnp.zeros((1, LANES), F32).at[0, :N_EXPERTS].set(moe_router_b[j])
            router = (jnp.concatenate([rw_hi, rw_lo], axis=1), rb)
        outs = _out_proj(flat(ya), flat(yb), flat(yc), flat(yd), gn_c, w_out[l].astype(BF16), x2d,
                         g1, ln1_w[l], ln1_b[l], sc2, sh2, S, router)
        if l % 2 == 0:
            x1, u2 = outs
            x2d = _ffn(u2, ffn_w_gate[j].astype(BF16), ffn_w_up[j].astype(BF16),
                       ffn_w_down[j].astype(BF16), x1, g2, ln2_w[l], ln2_b[l], S)
        else:
            x1, u2, ids, topw = outs
            x2d = _moe(u2, ids, topw, moe_w_gate[j].astype(BF16), moe_w_up[j].astype(BF16),
                       moe_w_down[j].astype(BF16), x1, g2, ln2_w[l], ln2_b[l], S)
    return x2d.reshape(B, S, D)
```

```python
import functools

import jax
import jax.numpy as jnp
from jax import lax
from jax.experimental import pallas as pl
from jax.experimental.pallas import tpu as pltpu

F32 = jnp.float32
BF16 = jnp.bfloat16

DEPTH = 2
CHUNK = 256
GROUP_WIDTH = 512
N_HEADS = 4
HEAD_DIM = 128
MLSTM_CONV = 4
POOL_WINDOWS = (2, 4, 8, 16)
CONV_WIDTH = 31
SB_KEY_BLOCK = 256
N_EXPERTS = 8
ALPHA = (2.0 * DEPTH) ** 0.25
EPS = 1e-5
LOG2E = 1.4426950408889634
LANES = 128
VMEM_LIMIT = 56 * 1024 * 1024


def _params(sem, vmem=VMEM_LIMIT):
    return pltpu.CompilerParams(dimension_semantics=sem, vmem_limit_bytes=vmem)


def _silu(x):
    return x * (1.0 / (1.0 + jnp.exp(-x)))


def _sigmoid(x):
    return 1.0 / (1.0 + jnp.exp(-x))


def _neg_softplus(x):
    return -(jnp.maximum(x, 0.0) + jnp.log(1.0 + jnp.exp(-jnp.abs(x))))


def _layer_norm_rows(r, w, b):
    mu = jnp.mean(r, axis=-1, keepdims=True)
    d = r - mu
    var = jnp.mean(d * d, axis=-1, keepdims=True)
    return d * lax.rsqrt(var + EPS) * w + b


def _rms_rows(y, w):
    return y * lax.rsqrt(jnp.mean(y * y, axis=-1, keepdims=True) + EPS) * w


def _bdot(a, b):
    return jnp.dot(a.astype(BF16), b.astype(BF16), preferred_element_type=F32)


def _ada_kernel(c_ref, w_ref, b_ref, out_ref):
    c = c_ref[...]
    out_ref[...] = jnp.dot(_silu(c), w_ref[...], preferred_element_type=F32,
                           precision=lax.Precision.HIGHEST) + b_ref[...]


def _ada_all(c, ada_w, ada_b):
    L, D, D6 = ada_w.shape
    Bn = c.shape[0]
    tn = 1024
    return pl.pallas_call(
        _ada_kernel,
        out_shape=jax.ShapeDtypeStruct((L, Bn, D6), F32),
        grid=(L, D6 // tn),
        in_specs=[pl.BlockSpec((Bn, D), lambda l, n: (0, 0)),
                  pl.BlockSpec((None, D, tn), lambda l, n: (l, 0, n)),
                  pl.BlockSpec((None, 1, tn), lambda l, n: (l, 0, n))],
        out_specs=pl.BlockSpec((None, Bn, tn), lambda l, n: (l, 0, n)),
        compiler_params=_params(("parallel", "parallel")),
        name="ada_mod",
    )(c, ada_w, ada_b.reshape(L, 1, D6))


def _inproj_kernel(x_ref, sc_ref, sh_ref, w_ref, wg_ref, proj_ref, gates_ref, u_ref):
    @pl.when(pl.program_id(1) == 0)
    def _():
        u = (x_ref[...] * (1.0 + sc_ref[...]) + sh_ref[...]).astype(BF16)
        u_ref[...] = u
        gates_ref[...] = jnp.dot(u, wg_ref[...], preferred_element_type=F32)

    proj_ref[...] = jnp.dot(u_ref[...], w_ref[...], preferred_element_type=F32)


def _in_proj(x2d, sc, sh, w_main, w_gate, seq):
    N, D = x2d.shape
    NC = w_main.shape[1]
    tm, tn = 1024, 1280
    tpb = seq // tm
    return pl.pallas_call(
        _inproj_kernel,
        out_shape=(jax.ShapeDtypeStruct((N, NC), F32), jax.ShapeDtypeStruct((N, LANES), F32)),
        grid=(N // tm, NC // tn),
        in_specs=[pl.BlockSpec((tm, D), lambda m, n: (m, 0)),
                  pl.BlockSpec((None, 1, D), lambda m, n: (m // tpb, 0, 0)),
                  pl.BlockSpec((None, 1, D), lambda m, n: (m // tpb, 0, 0)),
                  pl.BlockSpec((D, tn), lambda m, n: (0, n)),
                  pl.BlockSpec((D, LANES), lambda m, n: (0, 0))],
        out_specs=(pl.BlockSpec((tm, tn), lambda m, n: (m, n)),
                   pl.BlockSpec((tm, LANES), lambda m, n: (m, 0))),
        scratch_shapes=[pltpu.VMEM((tm, D), BF16)],
        compiler_params=_params(("parallel", "arbitrary")),
        name="in_proj",
    )(x2d, sc, sh, w_main, w_gate)


def _cumsum_rows(x):
    n = x.shape[0]
    row = lax.broadcasted_iota(jnp.int32, x.shape, 0)
    k = 1
    while k < n:
        x = x + jnp.where(row >= k, pltpu.roll(x, k, axis=0), 0.0)
        k *= 2
    return x


def _mlstm_kernel(qk_ref, v_ref, o_ref, g_ref, cw_ref, cb_ref, gb_ref, nw_ref, out_ref,
                  c_state, n_state, m_state, tail_ref):
    T = qk_ref.shape[0]
    GW = GROUP_WIDTH

    @pl.when(pl.program_id(1) == 0)
    def _():
        c_state[...] = jnp.zeros_like(c_state)
        n_state[...] = jnp.zeros_like(n_state)
        m_state[...] = jnp.zeros_like(m_state)
        tail_ref[...] = jnp.zeros_like(tail_ref)

    xx = jnp.concatenate([tail_ref[...], qk_ref[...]], axis=0)
    tail_ref[...] = qk_ref[T - 8:T, :]
    conv = cb_ref[...]
    for j in range(MLSTM_CONV):
        off = 8 - (MLSTM_CONV - 1) + j
        conv = conv + cw_ref[j:j + 1, :] * xx[off:off + T, :]
    qk = _silu(conv)

    gates = g_ref[...] + gb_ref[...]
    logf = _neg_softplus(-gates)
    causal = (lax.broadcasted_iota(jnp.int32, (CHUNK, CHUNK), 1)
              <= lax.broadcasted_iota(jnp.int32, (CHUNK, CHUNK), 0))

    for c in range(T // CHUNK):
        rows = slice(c * CHUNK, (c + 1) * CHUNK)
        gi = gates[rows, :]
        bcum = _cumsum_rows(logf[rows, :])
        gi_t = gi.T
        bcum_t = bcum.T
        for h in range(N_HEADS):
            cols = slice(h * HEAD_DIM, (h + 1) * HEAD_DIM)
            q = qk[rows, cols] * (HEAD_DIM ** -0.5)
            k = qk[rows, GW + h * HEAD_DIM:GW + (h + 1) * HEAD_DIM]
            v = v_ref[rows, cols]
            b_col = bcum[:, N_HEADS + h:N_HEADS + h + 1]
            b_row = bcum_t[N_HEADS + h:N_HEADS + h + 1, :]
            i_col = gi[:, h:h + 1]
            i_row = gi_t[h:h + 1, :]
            g_tot = b_col[CHUNK - 1:CHUNK, :]
            m_prev = m_state[h:h + 1, 0:1]
            c_prev = c_state[h]
            n_prev = n_state[h:h + 1, :]

            log_d = jnp.where(causal, b_col - b_row + i_row, -jnp.inf)
            m_inter = b_col + m_prev
            m_t = jnp.maximum(m_inter, jnp.max(log_d, axis=-1, keepdims=True))
            s = lax.dot_general(q.astype(BF16), k.astype(BF16), (((1,), (1,)), ((), ())),
                                preferred_element_type=F32)
            w = jnp.exp(log_d - m_t) * s
            inter = jnp.exp(m_inter - m_t)
            num = inter * _bdot(q, c_prev) + _bdot(w, v)
            den = inter * jnp.sum(q * n_prev, axis=-1, keepdims=True) + jnp.sum(w, axis=-1, keepdims=True)
            hcur = num / jnp.maximum(jnp.abs(den), jnp.exp(-m_t))

            a_col = g_tot - b_col + i_col
            m_new = jnp.maximum(g_tot + m_prev, jnp.max(a_col, axis=0, keepdims=True))
            decay = jnp.exp(g_tot + m_prev - m_new)
            kw = k * jnp.exp(a_col - m_new)
            c_state[h] = decay * c_prev + _bdot(kw.T, v)
            n_state[h:h + 1, :] = decay * n_prev + jnp.sum(kw, axis=0, keepdims=True)
            m_state[h:h + 1, :] = jnp.broadcast_to(m_new, (1, LANES))

            mu = jnp.mean(hcur, axis=-1, keepdims=True)
            dlt = hcur - mu
            var = jnp.mean(dlt * dlt, axis=-1, keepdims=True)
            hn = dlt * lax.rsqrt(var + EPS) * nw_ref[:, cols]
            out_ref[rows, cols] = (_sigmoid(o_ref[rows, cols]) * hn).astype(out_ref.dtype)


def _mlstm_mixer(proj, gates, conv_w, conv_b, gate_b, norm_w, B, S):
    T = 256
    GW = GROUP_WIDTH
    gb = jnp.zeros((1, LANES), F32).at[0, :2 * N_HEADS].set(gate_b)
    return pl.pallas_call(
        _mlstm_kernel,
        out_shape=jax.ShapeDtypeStruct((B, S, GW), BF16),
        grid=(B, S // T),
        in_specs=[pl.BlockSpec((None, T, 2 * GW), lambda b, t: (b, t, 0)),
                  pl.BlockSpec((None, T, GW), lambda b, t: (b, t, 2)),
                  pl.BlockSpec((None, T, GW), lambda b, t: (b, t, 3)),
                  pl.BlockSpec((None, T, LANES), lambda b, t: (b, t, 0)),
                  pl.BlockSpec((MLSTM_CONV, 2 * GW), lambda b, t: (0, 0)),
                  pl.BlockSpec((1, 2 * GW), lambda b, t: (0, 0)),
                  pl.BlockSpec((1, LANES), lambda b, t: (0, 0)),
                  pl.BlockSpec((1, GW), lambda b, t: (0, 0))],
        out_specs=pl.BlockSpec((None, T, GW), lambda b, t: (b, t, 0)),
        scratch_shapes=[pltpu.VMEM((N_HEADS, HEAD_DIM, HEAD_DIM), F32),
                        pltpu.VMEM((8, LANES), F32),
                        pltpu.VMEM((8, LANES), F32),
                        pltpu.VMEM((8, 2 * GW), F32)],
        compiler_params=_params(("parallel", "arbitrary")),
        name="mlstm_mixer",
    )(proj, proj, proj, gates, conv_w, conv_b.reshape(1, -1), gb, norm_w.reshape(1, -1))


def _pool_kernel(x_ref, w_ref, scale_ref, gn_ref, out_ref, tail_ref):
    T = x_ref.shape[0]
    HALO = 16
    t_blk = pl.program_id(1)

    @pl.when(t_blk == 0)
    def _():
        tail_ref[...] = jnp.zeros_like(tail_ref)

    x = x_ref[...]
    xx = jnp.concatenate([tail_ref[...], x], axis=0)
    tail_ref[...] = x[T - HALO:T, :]
    pos = t_blk * T + lax.broadcasted_iota(jnp.int32, (T, 1), 0) + 1
    outs = []
    for g, win in enumerate(POOL_WINDOWS):
        cols = slice(g * LANES, (g + 1) * LANES)
        s = xx[:, cols]
        k = 1
        while k < win:
            s = s + pltpu.roll(s, k, axis=0)
            k *= 2
        cnt = jnp.minimum(pos, win).astype(F32)
        yg = s[HALO:, :] / cnt - x[:, cols]
        outs.append(_bdot(yg, w_ref[g]))
    y = jnp.concatenate(outs, axis=-1) * scale_ref[...]
    out_ref[...] = _rms_rows(y, gn_ref[...]).astype(out_ref.dtype)


def _pool_mixer(proj, pool_w, pool_scale, gn, B, S):
    T = 512
    GW = GROUP_WIDTH
    return pl.pallas_call(
        _pool_kernel,
        out_shape=jax.ShapeDtypeStruct((B, S, GW), BF16),
        grid=(B, S // T),
        in_specs=[pl.BlockSpec((None, T, GW), lambda b, t: (b, t, 4)),
                  pl.BlockSpec((len(POOL_WINDOWS), LANES, LANES), lambda b, t: (0, 0, 0)),
                  pl.BlockSpec((1, GW), lambda b, t: (0, 0)),
                  pl.BlockSpec((1, GW), lambda b, t: (0, 0))],
        out_specs=pl.BlockSpec((None, T, GW), lambda b, t: (b, t, 0)),
        scratch_shapes=[pltpu.VMEM((16, GW), F32)],
        compiler_params=_params(("parallel", "arbitrary")),
        name="pool_mixer",
    )(proj, pool_w.astype(BF16), pool_scale.reshape(1, -1), gn.reshape(1, -1))


def _sb_kernel(q_ref, k_ref, v_ref, out_ref, kb_ref, vb_ref):
    LQ = q_ref.shape[0]
    LK = SB_KEY_BLOCK
    ratio = LQ // LK
    qi = pl.program_id(1)

    @pl.when(qi == 0)
    def _():
        kb_ref[...] = k_ref[...].astype(BF16)
        vb_ref[...] = v_ref[...].astype(BF16)

    scale2 = HEAD_DIM ** -0.5 * LOG2E
    tri = (lax.broadcasted_iota(jnp.int32, (LK, LK), 0)
           > lax.broadcasted_iota(jnp.int32, (LK, LK), 1)).astype(BF16)
    row = lax.broadcasted_iota(jnp.int32, (LQ, LK), 0)
    col = lax.broadcasted_iota(jnp.int32, (LQ, LK), 1)
    heads = [slice(h * HEAD_DIM, (h + 1) * HEAD_DIM) for h in range(N_HEADS)]
    qs = [(q_ref[:, hs] * scale2).astype(BF16) for hs in heads]

    def sweep(kb, carry, strict):
        start = pl.multiple_of(kb * LK, LK)
        zs, lks = [], []
        for h, hs in enumerate(heads):
            k_blk = kb_ref[pl.ds(start, LK), hs]
            z = lax.dot_general(qs[h], k_blk, (((1,), (1,)), ((), ())), preferred_element_type=F32)
            lk = -(jnp.maximum(z, 0.0) + jnp.log2(1.0 + jnp.exp2(-jnp.abs(z))))
            if strict is not None:
                lk = jnp.where(strict, lk, 0.0)
            zs.append(z)
            lks.append(lk)
        inner = jnp.dot(jnp.concatenate([lk.astype(BF16) for lk in lks], axis=0), tri,
                        preferred_element_type=F32)
        new = []
        for h, hs in enumerate(heads):
            after, acc = carry[h]
            v_blk = vb_ref[pl.ds(start, LK), hs]
            a = jnp.exp2(zs[h] + lks[h] + inner[h * LQ:(h + 1) * LQ, :] + after)
            if strict is not None:
                a = jnp.where(strict, a, 0.0)
            acc = acc + jnp.dot(a.astype(BF16), v_blk, preferred_element_type=F32)
            after = after + jnp.sum(lks[h], axis=-1, keepdims=True)
            new.append((after, acc))
        return tuple(new)

    init = tuple((jnp.zeros((LQ, 1), F32), jnp.zeros((LQ, HEAD_DIM), F32)) for _ in heads)
    carry = init
    for j in reversed(range(ratio)):
        carry = sweep(ratio * qi + j, carry, (j * LK + col) < row)
    carry = lax.fori_loop(0, ratio * qi, lambda it, c: sweep(ratio * qi - 1 - it, c, None), carry)
    for h, hs in enumerate(heads):
        out_ref[:, hs] = carry[h][1]


def _sb_mixer(proj, B, S):
    LQ = 512
    GW = GROUP_WIDTH
    return pl.pallas_call(
        _sb_kernel,
        out_shape=jax.ShapeDtypeStruct((B, S, GW), F32),
        grid=(B, S // LQ),
        in_specs=[pl.BlockSpec((None, LQ, GW), lambda b, i: (b, i, 5)),
                  pl.BlockSpec((None, S, GW), lambda b, i: (b, 0, 6)),
                  pl.BlockSpec((None, S, GW), lambda b, i: (b, 0, 7))],
        out_specs=pl.BlockSpec((None, LQ, GW), lambda b, i: (b, i, 0)),
        scratch_shapes=[pltpu.VMEM((S, GW), BF16), pltpu.VMEM((S, GW), BF16)],
        compiler_params=_params(("parallel", "arbitrary")),
        name="sb_attention",
    )(proj, proj, proj)


def _conv_kernel(x_ref, dw_ref, dwb_ref, lnw_ref, lnb_ref, pw_ref, pwb_ref, gn_ref, out_ref,
                 hbuf, ybuf):
    T = x_ref.shape[0]
    GW = GROUP_WIDTH
    HALO = 32
    SUB = 32

    @pl.when(pl.program_id(1) == 0)
    def _():
        hbuf[0:HALO, :] = jnp.zeros((HALO, GW), F32)

    @pl.when(pl.program_id(1) > 0)
    def _():
        hbuf[0:HALO, :] = hbuf[T:T + HALO, :]

    hbuf[HALO:HALO + T, :] = x_ref[:, 0:GW] * _sigmoid(x_ref[:, GW:2 * GW])

    def sub(i, carry):
        r0 = pl.multiple_of(i * SUB, SUB)
        acc = jnp.broadcast_to(dwb_ref[...], (SUB, GW))
        win = hbuf[pl.ds(r0, SUB + HALO), :]
        for j in range(CONV_WIDTH):
            off = HALO - CONV_WIDTH + 1 + j
            acc = acc + dw_ref[j:j + 1, :] * win[off:off + SUB, :]
        ybuf[pl.ds(r0, SUB), :] = acc
        return carry

    lax.fori_loop(0, T // SUB, sub, 0)
    hn = _silu(_layer_norm_rows(ybuf[...], lnw_ref[...], lnb_ref[...]))
    y = _bdot(hn, pw_ref[...]) + pwb_ref[...]
    out_ref[...] = _rms_rows(y, gn_ref[...]).astype(out_ref.dtype)


def _conv_mixer(proj, dw_w, dw_b, ln_w, ln_b, pw_w, pw_b, gn, B, S):
    T = 256
    GW = GROUP_WIDTH
    dw_pad = jnp.zeros((32, GW), F32).at[:CONV_WIDTH].set(dw_w)
    r = lambda a: a.reshape(1, -1)
    return pl.pallas_call(
        _conv_kernel,
        out_shape=jax.ShapeDtypeStruct((B, S, GW), BF16),
        grid=(B, S // T),
        in_specs=[pl.BlockSpec((None, T, 2 * GW), lambda b, t: (b, t, 4)),
                  pl.BlockSpec((32, GW), lambda b, t: (0, 0)),
                  pl.BlockSpec((1, GW), lambda b, t: (0, 0)),
                  pl.BlockSpec((1, GW), lambda b, t: (0, 0)),
                  pl.BlockSpec((1, GW), lambda b, t: (0, 0)),
                  pl.BlockSpec((GW, GW), lambda b, t: (0, 0)),
                  pl.BlockSpec((1, GW), lambda b, t: (0, 0)),
                  pl.BlockSpec((1, GW), lambda b, t: (0, 0))],
        out_specs=pl.BlockSpec((None, T, GW), lambda b, t: (b, t, 0)),
        scratch_shapes=[pltpu.VMEM((T + 32, GW), F32), pltpu.VMEM((T, GW), F32)],
        compiler_params=_params(("parallel", "arbitrary")),
        name="conv_mixer",
    )(proj, dw_pad, r(dw_b), r(ln_w), r(ln_b), pw_w.astype(BF16), r(pw_b), r(gn))


def _outproj_kernel(with_router, ya_ref, yb_ref, yc_ref, yd_ref, gnc_ref, w_ref, x_ref, g1_ref,
                    lnw_ref, lnb_ref, sc_ref, sh_ref, *rest):
    GW = GROUP_WIDTH
    if with_router:
        rw_ref, rb_ref, x1_ref, u2_ref, ids_ref, topw_ref = rest
    else:
        x1_ref, u2_ref = rest
    yc = _rms_rows(yc_ref[...], gnc_ref[...]).astype(BF16)
    acc = jnp.dot(ya_ref[...], w_ref[0:GW, :], preferred_element_type=F32)
    acc = acc + jnp.dot(yb_ref[...], w_ref[GW:2 * GW, :], preferred_element_type=F32)
    acc = acc + jnp.dot(yc, w_ref[2 * GW:3 * GW, :], preferred_element_type=F32)
    acc = acc + jnp.dot(yd_ref[...], w_ref[3 * GW:4 * GW, :], preferred_element_type=F32)
    r = ALPHA * x_ref[...] + (1.0 + g1_ref[...]) * acc
    x1 = _layer_norm_rows(r, lnw_ref[...], lnb_ref[...])
    x1_ref[...] = x1
    u2 = x1 * (1.0 + sc_ref[...]) + sh_ref[...]
    u2_ref[...] = u2.astype(u2_ref.dtype)
    if with_router:
        u_hi = u2.astype(BF16)
        u_lo = (u2 - u_hi.astype(F32)).astype(BF16)
        p = jnp.dot(u_hi, rw_ref[...], preferred_element_type=F32)
        logits = (p[:, :LANES] + p[:, LANES:]
                  + jnp.dot(u_lo, rw_ref[:, :LANES], preferred_element_type=F32) + rb_ref[...])
        lane = lax.broadcasted_iota(jnp.int32, logits.shape, 1)
        lg = jnp.where(lane < N_EXPERTS, logits, -jnp.inf)
        m1 = jnp.max(lg, axis=-1, keepdims=True)
        i1 = jnp.min(jnp.where(lg == m1, lane, LANES), axis=-1, keepdims=True)
        lg2 = jnp.where(lane == i1, -jnp.inf, lg)
        m2 = jnp.max(lg2, axis=-1, keepdims=True)
        i2 = jnp.min(jnp.where(lg2 == m2, lane, LANES), axis=-1, keepdims=True)
        e2 = jnp.exp(m2 - m1)
        w1 = 1.0 / (1.0 + e2)
        w2 = e2 / (1.0 + e2)
        ids_ref[...] = jnp.where(lane == 0, i1, jnp.where(lane == 1, i2, 0))
        topw_ref[...] = jnp.where(lane == 0, w1, jnp.where(lane == 1, w2, 0.0))


def _out_proj(ya, yb, yc, yd, gnc, w_out, x2d, g1, ln_w, ln_b, sc2, sh2, seq, router=None):
    N, D = x2d.shape
    GW = GROUP_WIDTH
    tm = 256
    tpb = seq // tm
    row = lambda m: (m, 0)
    const = lambda m: (0, 0)
    perb = lambda m: (m // tpb, 0, 0)
    in_specs = [pl.BlockSpec((tm, GW), row), pl.BlockSpec((tm, GW), row),
                pl.BlockSpec((tm, GW), row), pl.BlockSpec((tm, GW), row),
                pl.BlockSpec((1, GW), const),
                pl.BlockSpec((D, D), const),
                pl.BlockSpec((tm, D), row),
                pl.BlockSpec((None, 1, D), perb),
                pl.BlockSpec((1, D), const), pl.BlockSpec((1, D), const),
                pl.BlockSpec((None, 1, D), perb), pl.BlockSpec((None, 1, D), perb)]
    args = [ya, yb, yc, yd, gnc.reshape(1, -1), w_out, x2d, g1, ln_w.reshape(1, -1),
            ln_b.reshape(1, -1), sc2, sh2]
    out_shape = [jax.ShapeDtypeStruct((N, D), F32),
                 jax.ShapeDtypeStruct((N, D), BF16 if router is None else F32)]
    out_specs = [pl.BlockSpec((tm, D), row), pl.BlockSpec((tm, D), row)]
    if router is not None:
        rw, rb = router
        in_specs += [pl.BlockSpec((D, 2 * LANES), const), pl.BlockSpec((1, LANES), const)]
        args += [rw, rb]
        out_shape += [jax.ShapeDtypeStruct((N, LANES), jnp.int32), jax.ShapeDtypeStruct((N, LANES), F32)]
        out_specs += [pl.BlockSpec((tm, LANES), row), pl.BlockSpec((tm, LANES), row)]
    return pl.pallas_call(
        functools.partial(_outproj_kernel, router is not None),
        out_shape=tuple(out_shape),
        grid=(N // tm,),
        in_specs=in_specs,
        out_specs=tuple(out_specs),
        compiler_params=_params(("parallel",)),
        name="out_proj_ln",
    )(*args)


def _ffn_kernel(u_ref, wg_ref, wu_ref, wd_ref, x_ref, g2_ref, lnw_ref, lnb_ref, out_ref, acc_ref):
    f = pl.program_id(1)

    @pl.when(f == 0)
    def _():
        acc_ref[...] = jnp.zeros_like(acc_ref)

    u = u_ref[...]
    hg = jnp.dot(u, wg_ref[...], preferred_element_type=F32)
    hu = jnp.dot(u, wu_ref[...], preferred_element_type=F32)
    h = _silu(hg) * hu
    acc_ref[...] += jnp.dot(h.astype(BF16), wd_ref[...], preferred_element_type=F32)

    @pl.when(f == pl.num_programs(1) - 1)
    def _():
        r = ALPHA * x_ref[...] + (1.0 + g2_ref[...]) * acc_ref[...]
        out_ref[...] = _layer_norm_rows(r, lnw_ref[...], lnb_ref[...])


def _ffn(u2, wg, wu, wd, x1, g2, ln_w, ln_b, seq):
    N, D = u2.shape
    F = wg.shape[1]
    tm, tf = 512, 512
    tpb = seq // tm
    row = lambda m, f: (m, 0)
    const = lambda m, f: (0, 0)
    return pl.pallas_call(
        _ffn_kernel,
        out_shape=jax.ShapeDtypeStruct((N, D), F32),
        grid=(N // tm, F // tf),
        in_specs=[pl.BlockSpec((tm, D), row),
                  pl.BlockSpec((D, tf), lambda m, f: (0, f)),
                  pl.BlockSpec((D, tf), lambda m, f: (0, f)),
                  pl.BlockSpec((tf, D), lambda m, f: (f, 0)),
                  pl.BlockSpec((tm, D), row),
                  pl.BlockSpec((None, 1, D), lambda m, f: (m // tpb, 0, 0)),
                  pl.BlockSpec((1, D), const), pl.BlockSpec((1, D), const)],
        out_specs=pl.BlockSpec((tm, D), row),
        scratch_shapes=[pltpu.VMEM((tm, D), F32)],
        compiler_params=_params(("parallel", "arbitrary")),
        name="ffn_ln",
    )(u2, wg, wu, wd, x1, g2, ln_w.reshape(1, -1), ln_b.reshape(1, -1))


MOE_TM = 512


def _row_copy(src, src_row, dst, dst_row, sem):
    return pltpu.make_async_copy(src.at[pl.ds(src_row, 1), :], dst.at[pl.ds(dst_row, 1), :], sem)


def _dispatch_kernel(pos_ref, u_ref, zeros_ref, out_ref, sem):
    del zeros_ref
    tm = u_ref.shape[0]
    base = pl.program_id(0) * tm

    def start(r, carry):
        for k in range(2):
            _row_copy(u_ref, r, out_ref, pos_ref[2 * (base + r) + k], sem).start(priority=k)
        return carry

    def wait(r, carry):
        for k in range(2):
            _row_copy(u_ref, 0, out_ref, 0, sem).wait()
        return carry

    lax.fori_loop(0, tm, start, 0, unroll=8)
    lax.fori_loop(0, tm, wait, 0, unroll=8)


def _dispatch(u2, pos, n_rows):
    N, D = u2.shape
    tm = 256
    return pl.pallas_call(
        _dispatch_kernel,
        out_shape=jax.ShapeDtypeStruct((n_rows, D), u2.dtype),
        grid_spec=pltpu.PrefetchScalarGridSpec(
            num_scalar_prefetch=1,
            grid=(N // tm,),
            in_specs=[pl.BlockSpec((tm, D), lambda m, pos: (m, 0)),
                      pl.BlockSpec(memory_space=pl.ANY)],
            out_specs=pl.BlockSpec(memory_space=pl.ANY),
            scratch_shapes=[pltpu.SemaphoreType.DMA(())]),
        input_output_aliases={2: 0},
        compiler_params=_params(("arbitrary",)),
        name="moe_dispatch",
    )(pos, u2, jnp.zeros((n_rows, D), u2.dtype))


def _moe_ffn_kernel(te_ref, nv_ref, u_ref, wg_ref, wu_ref, wd_ref, out_ref):
    t = pl.program_id(0)
    f = pl.program_id(1)
    tf = wg_ref.shape[1]
    half = (tf // LANES // 2) * LANES

    @pl.when(t < nv_ref[0])
    def _():
        u = u_ref[...].astype(BF16)
        y = None
        for c0, c1 in ((0, half), (half, tf)):
            hg = jnp.dot(u, wg_ref[:, c0:c1], preferred_element_type=F32)
            hu = jnp.dot(u, wu_ref[:, c0:c1], preferred_element_type=F32)
            h = (_silu(hg) * hu).astype(BF16)
            part = jnp.dot(h, wd_ref[c0:c1, :], preferred_element_type=F32)
            y = part if y is None else y + part

        @pl.when(f == 0)
        def _():
            out_ref[...] = y

        @pl.when(f > 0)
        def _():
            out_ref[...] += y

    @pl.when((t >= nv_ref[0]) & (f == 0))
    def _():
        out_ref[...] = jnp.zeros_like(out_ref)


def _moe_ffn(u_sorted, tile_expert, n_valid, wg, wu, wd):
    R, D = u_sorted.shape
    F = wg.shape[2]
    tm, tf = MOE_TM, F // 2
    nf = F // tf
    tile = lambda t, nv: jnp.minimum(t, nv[0] - 1)
    fidx = lambda t, f, nv: jnp.where(t < nv[0], f, nf - 1)
    return pl.pallas_call(
        _moe_ffn_kernel,
        out_shape=jax.ShapeDtypeStruct((R, D), F32),
        grid_spec=pltpu.PrefetchScalarGridSpec(
            num_scalar_prefetch=2,
            grid=(R // tm, nf),
            in_specs=[pl.BlockSpec((tm, D), lambda t, f, te, nv: (tile(t, nv), 0)),
                      pl.BlockSpec((None, D, tf), lambda t, f, te, nv: (te[t], 0, fidx(t, f, nv))),
                      pl.BlockSpec((None, D, tf), lambda t, f, te, nv: (te[t], 0, fidx(t, f, nv))),
                      pl.BlockSpec((None, tf, D), lambda t, f, te, nv: (te[t], fidx(t, f, nv), 0))],
            out_specs=pl.BlockSpec((tm, D), lambda t, f, te, nv: (t, 0))),
        compiler_params=_params(("arbitrary", "arbitrary"), vmem=60 * 1024 * 1024),
        name="moe_ffn",
    )(tile_expert, n_valid, u_sorted, wg, wu, wd)


def _combine_kernel(pos_ref, y_ref, topw_ref, x_ref, g2_ref, lnw_ref, lnb_ref, out_ref, ybuf, sem):
    tm = x_ref.shape[0]
    base = pl.program_id(0) * tm

    def start(r, carry):
        for k in range(2):
            _row_copy(y_ref, pos_ref[2 * (base + r) + k], ybuf.at[k], r, sem).start(priority=k)
        return carry

    def wait(r, carry):
        for k in range(2):
            _row_copy(y_ref, 0, ybuf.at[k], 0, sem).wait()
        return carry

    lax.fori_loop(0, tm, start, 0, unroll=8)
    lax.fori_loop(0, tm, wait, 0, unroll=8)
    tw = topw_ref[...]
    y = tw[:, 0:1] * ybuf[0] + tw[:, 1:2] * ybuf[1]
    r = ALPHA * x_ref[...] + (1.0 + g2_ref[...]) * y
    out_ref[...] = _layer_norm_rows(r, lnw_ref[...], lnb_ref[...])


def _combine(y_sorted, pos, topw, x1, g2, ln_w, ln_b, seq):
    N, D = x1.shape
    tm = 256
    tpb = seq // tm
    row = lambda m, pos: (m, 0)
    const = lambda m, pos: (0, 0)
    return pl.pallas_call(
        _combine_kernel,
        out_shape=jax.ShapeDtypeStruct((N, D), F32),
        grid_spec=pltpu.PrefetchScalarGridSpec(
            num_scalar_prefetch=1,
            grid=(N // tm,),
            in_specs=[pl.BlockSpec(memory_space=pl.ANY),
                      pl.BlockSpec((tm, LANES), row),
                      pl.BlockSpec((tm, D), row),
                      pl.BlockSpec((None, 1, D), lambda m, pos: (m // tpb, 0, 0)),
                      pl.BlockSpec((1, D), const), pl.BlockSpec((1, D), const)],
            out_specs=pl.BlockSpec((tm, D), row),
            scratch_shapes=[pltpu.VMEM((2, tm, D), F32), pltpu.SemaphoreType.DMA(())]),
        compiler_params=_params(("arbitrary",)),
        name="moe_combine_ln",
    )(pos, y_sorted, topw, x1, g2, ln_w.reshape(1, -1), ln_b.reshape(1, -1))


def _routing_plan(ids, n_tiles):
    e_flat = ids[:, :2].reshape(-1)
    onehot = (e_flat[:, None] == jnp.arange(N_EXPERTS, dtype=jnp.int32)[None, :]).astype(jnp.int32)
    csum = jnp.cumsum(onehot, axis=0)
    rank = jnp.sum((csum - onehot) * onehot, axis=1)
    counts = csum[-1]
    padded = ((counts + MOE_TM - 1) // MOE_TM) * MOE_TM
    ends = jnp.cumsum(padded)
    offs = ends - padded
    pos = (jnp.sum(onehot * offs[None, :], axis=1) + rank).astype(jnp.int32)
    tile_start = jnp.arange(n_tiles, dtype=jnp.int32) * MOE_TM
    n_valid = (ends[-1] // MOE_TM).astype(jnp.int32)
    tile_start = jnp.minimum(tile_start, (n_valid - 1) * MOE_TM)
    tile_expert = jnp.sum((tile_start[:, None] >= ends[None, :]).astype(jnp.int32), axis=1).astype(jnp.int32)
    return pos, tile_expert, n_valid.reshape(1)


def _moe(u2, ids, topw, wg, wu, wd, x1, g2, ln_w, ln_b, seq):
    N = u2.shape[0]
    n_tiles = 2 * N // MOE_TM + N_EXPERTS
    pos, tile_expert, n_valid = _routing_plan(ids, n_tiles)
    u_sorted = _dispatch(u2, pos, n_tiles * MOE_TM)
    y_sorted = _moe_ffn(u_sorted, tile_expert, n_valid, wg, wu, wd)
    return _combine(y_sorted, pos, topw, x1, g2, ln_w, ln_b, seq)


def _split_w_in(w):
    GW = GROUP_WIDTH
    g0 = 4 * GW
    g1 = g0 + 2 * N_HEADS
    main = jnp.concatenate([w[:, :g0], w[:, g1:]], axis=1).astype(BF16)
    gate = jnp.zeros((w.shape[0], LANES), BF16).at[:, :2 * N_HEADS].set(w[:, g0:g1].astype(BF16))
    return main, gate


def kernel(x, c, w_in, mlstm_conv_w, mlstm_conv_b, mlstm_gate_b, mlstm_norm_w, pool_w, pool_scale, conv_dw_w, conv_dw_b, conv_ln_w, conv_ln_b, conv_pw_w, conv_pw_b, group_norm_w, w_out, ada_w, ada_b, ln1_w, ln1_b, ln2_w, ln2_b, ffn_w_gate, ffn_w_up, ffn_w_down, moe_router_w, moe_router_b, moe_w_gate, moe_w_up, moe_w_down):
    B, S, D = x.shape
    GW = GROUP_WIDTH
    ada = _ada_all(c, ada_w, ada_b)
    x2d = x.reshape(B * S, D)
    for l in range(DEPTH):
        mod = [ada[l, :, i * D:(i + 1) * D].reshape(B, 1, D) for i in range(6)]
        sh1, sc1, g1, sh2, sc2, g2 = mod
        w_main, w_gate = _split_w_in(w_in[l])
        proj, gates = _in_proj(x2d, sc1, sh1, w_main, w_gate, S)
        proj = proj.reshape(B, S, -1)
        gates = gates.reshape(B, S, LANES)
        gn_b, gn_c, gn_d = (group_norm_w[l, i * GW:(i + 1) * GW] for i in range(3))
        ya = _mlstm_mixer(proj, gates, mlstm_conv_w[l], mlstm_conv_b[l], mlstm_gate_b[l],
                          mlstm_norm_w[l], B, S)
        yb = _pool_mixer(proj, pool_w[l], pool_scale[l], gn_b, B, S)
        yc = _sb_mixer(proj, B, S)
        yd = _conv_mixer(proj, conv_dw_w[l], conv_dw_b[l], conv_ln_w[l], conv_ln_b[l],
                         conv_pw_w[l], conv_pw_b[l], gn_d, B, S)
        flat = lambda t: t.reshape(B * S, GW)
        j = l // 2
        router = None
        if l % 2 == 1:
            rw = jnp.zeros((D, LANES), F32).at[:, :N_EXPERTS].set(moe_router_w[j])
            rw_hi = rw.astype(BF16)
            rw_lo = (rw - rw_hi.astype(F32)).astype(BF16)
            rb = jnp.zeros((1, LANES), F32).at[0, :N_EXPERTS].set(moe_router_b[j])
            router = (jnp.concatenate([rw_hi, rw_lo], axis=1), rb)
        outs = _out_proj(flat(ya), flat(yb), flat(yc), flat(yd), gn_c, w_out[l].astype(BF16), x2d,
                         g1, ln1_w[l], ln1_b[l], sc2, sh2, S, router)
        if l % 2 == 0:
            x1, u2 = outs
            x2d = _ffn(u2, ffn_w_gate[j].astype(BF16), ffn_w_up[j].astype(BF16),
                       ffn_w_down[j].astype(BF16), x1, g2, ln2_w[l], ln2_b[l], S)
        else:
            x1, u2, ids, topw = outs
            x2d = _moe(u2, ids, topw, moe_w_gate[j].astype(BF16), moe_w_up[j].astype(BF16),
                       moe_w_down[j].astype(BF16), x1, g2, ln2_w[l], ln2_b[l], S)
    return x2d.reshape(B, S, D)
```

```python
import functools

import jax
import jax.numpy as jnp
from jax import lax
from jax.experimental import pallas as pl
from jax.experimental.pallas import tpu as pltpu

F32 = jnp.float32
BF16 = jnp.bfloat16

DEPTH = 2
CHUNK = 256
GROUP_WIDTH = 512
N_HEADS = 4
HEAD_DIM = 128
MLSTM_CONV = 4
POOL_WINDOWS = (2, 4, 8, 16)
CONV_WIDTH = 31
SB_KEY_BLOCK = 256
N_EXPERTS = 8
ALPHA = (2.0 * DEPTH) ** 0.25
EPS = 1e-5
LOG2E = 1.4426950408889634
LANES = 128
VMEM_LIMIT = 56 * 1024 * 1024


def _params(sem, vmem=VMEM_LIMIT):
    return pltpu.CompilerParams(dimension_semantics=sem, vmem_limit_bytes=vmem)


def _silu(x):
    return x * (1.0 / (1.0 + jnp.exp(-x)))


def _sigmoid(x):
    return 1.0 / (1.0 + jnp.exp(-x))


def _neg_softplus(x):
    return -(jnp.maximum(x, 0.0) + jnp.log(1.0 + jnp.exp(-jnp.abs(x))))


def _layer_norm_rows(r, w, b):
    mu = jnp.mean(r, axis=-1, keepdims=True)
    d = r - mu
    var = jnp.mean(d * d, axis=-1, keepdims=True)
    return d * lax.rsqrt(var + EPS) * w + b


def _rms_rows(y, w):
    return y * lax.rsqrt(jnp.mean(y * y, axis=-1, keepdims=True) + EPS) * w


def _bdot(a, b):
    return jnp.dot(a.astype(BF16), b.astype(BF16), preferred_element_type=F32)


def _ada_kernel(c_ref, w_ref, b_ref, out_ref):
    c = c_ref[...]
    out_ref[...] = jnp.dot(_silu(c), w_ref[...], preferred_element_type=F32,
                           precision=lax.Precision.HIGHEST) + b_ref[...]


def _ada_all(c, ada_w, ada_b):
    L, D, D6 = ada_w.shape
    Bn = c.shape[0]
    tn = 1024
    return pl.pallas_call(
        _ada_kernel,
        out_shape=jax.ShapeDtypeStruct((L, Bn, D6), F32),
        grid=(L, D6 // tn),
        in_specs=[pl.BlockSpec((Bn, D), lambda l, n: (0, 0)),
                  pl.BlockSpec((None, D, tn), lambda l, n: (l, 0, n)),
                  pl.BlockSpec((None, 1, tn), lambda l, n: (l, 0, n))],
        out_specs=pl.BlockSpec((None, Bn, tn), lambda l, n: (l, 0, n)),
        compiler_params=_params(("parallel", "parallel")),
        name="ada_mod",
    )(c, ada_w, ada_b.reshape(L, 1, D6))


def _inproj_kernel(x_ref, sc_ref, sh_ref, w_ref, wg_ref, proj_ref, gates_ref, u_ref):
    @pl.when(pl.program_id(1) == 0)
    def _():
        u = (x_ref[...] * (1.0 + sc_ref[...]) + sh_ref[...]).astype(BF16)
        u_ref[...] = u
        gates_ref[...] = jnp.dot(u, wg_ref[...], preferred_element_type=F32)

    proj_ref[...] = jnp.dot(u_ref[...], w_ref[...], preferred_element_type=F32)


def _in_proj(x2d, sc, sh, w_main, w_gate, seq):
    N, D = x2d.shape
    NC = w_main.shape[1]
    tm, tn = 1024, 1280
    tpb = seq // tm
    return pl.pallas_call(
        _inproj_kernel,
        out_shape=(jax.ShapeDtypeStruct((N, NC), F32), jax.ShapeDtypeStruct((N, LANES), F32)),
        grid=(N // tm, NC // tn),
        in_specs=[pl.BlockSpec((tm, D), lambda m, n: (m, 0)),
                  pl.BlockSpec((None, 1, D), lambda m, n: (m // tpb, 0, 0)),
                  pl.BlockSpec((None, 1, D), lambda m, n: (m // tpb, 0, 0)),
                  pl.BlockSpec((D, tn), lambda m, n: (0, n)),
                  pl.BlockSpec((D, LANES), lambda m, n: (0, 0))],
        out_specs=(pl.BlockSpec((tm, tn), lambda m, n: (m, n)),
                   pl.BlockSpec((tm, LANES), lambda m, n: (m, 0))),
        scratch_shapes=[pltpu.VMEM((tm, D), BF16)],
        compiler_params=_params(("parallel", "arbitrary")),
        name="in_proj",
    )(x2d, sc, sh, w_main, w_gate)


def _cumsum_rows(x):
    n = x.shape[0]
    row = lax.broadcasted_iota(jnp.int32, x.shape, 0)
    k = 1
    while k < n:
        x = x + jnp.where(row >= k, pltpu.roll(x, k, axis=0), 0.0)
        k *= 2
    return x


def _mlstm_kernel(qk_ref, v_ref, o_ref, g_ref, cw_ref, cb_ref, gb_ref, nw_ref, out_ref,
                  c_state, n_state, m_state, tail_ref):
    T = qk_ref.shape[0]
    GW = GROUP_WIDTH

    @pl.when(pl.program_id(1) == 0)
    def _():
        c_state[...] = jnp.zeros_like(c_state)
        n_state[...] = jnp.zeros_like(n_state)
        m_state[...] = jnp.zeros_like(m_state)
        tail_ref[...] = jnp.zeros_like(tail_ref)

    xx = jnp.concatenate([tail_ref[...], qk_ref[...]], axis=0)
    tail_ref[...] = qk_ref[T - 8:T, :]
    conv = cb_ref[...]
    for j in range(MLSTM_CONV):
        off = 8 - (MLSTM_CONV - 1) + j
        conv = conv + cw_ref[j:j + 1, :] * xx[off:off + T, :]
    qk = _silu(conv)

    gates = g_ref[...] + gb_ref[...]
    logf = _neg_softplus(-gates)
    causal = (lax.broadcasted_iota(jnp.int32, (CHUNK, CHUNK), 1)
              <= lax.broadcasted_iota(jnp.int32, (CHUNK, CHUNK), 0))

    for c in range(T // CHUNK):
        rows = slice(c * CHUNK, (c + 1) * CHUNK)
        gi = gates[rows, :]
        bcum = _cumsum_rows(logf[rows, :])
        gi_t = gi.T
        bcum_t = bcum.T
        for h in range(N_HEADS):
            cols = slice(h * HEAD_DIM, (h + 1) * HEAD_DIM)
            q = qk[rows, cols] * (HEAD_DIM ** -0.5)
            k = qk[rows, GW + h * HEAD_DIM:GW + (h + 1) * HEAD_DIM]
            v = v_ref[rows, cols]
            b_col = bcum[:, N_HEADS + h:N_HEADS + h + 1]
            b_row = bcum_t[N_HEADS + h:N_HEADS + h + 1, :]
            i_col = gi[:, h:h + 1]
            i_row = gi_t[h:h + 1, :]
            g_tot = b_col[CHUNK - 1:CHUNK, :]
            m_prev = m_state[h:h + 1, 0:1]
            c_prev = c_state[h]
            n_prev = n_state[h:h + 1, :]

            log_d = jnp.where(causal, b_col - b_row + i_row, -jnp.inf)
            m_inter = b_col + m_prev
            m_t = jnp.maximum(m_inter, jnp.max(log_d, axis=-1, keepdims=True))
            s = lax.dot_general(q.astype(BF16), k.astype(BF16), (((1,), (1,)), ((), ())),
                                preferred_element_type=F32)
            w = jnp.exp(log_d - m_t) * s
            inter = jnp.exp(m_inter - m_t)
            num = inter * _bdot(q, c_prev) + _bdot(w, v)
            den = inter * jnp.sum(q * n_prev, axis=-1, keepdims=True) + jnp.sum(w, axis=-1, keepdims=True)
            hcur = num / jnp.maximum(jnp.abs(den), jnp.exp(-m_t))

            a_col = g_tot - b_col + i_col
            m_new = jnp.maximum(g_tot + m_prev, jnp.max(a_col, axis=0, keepdims=True))
            decay = jnp.exp(g_tot + m_prev - m_new)
            kw = k * jnp.exp(a_col - m_new)
            c_state[h] = decay * c_prev + _bdot(kw.T, v)
            n_state[h:h + 1, :] = decay * n_prev + jnp.sum(kw, axis=0, keepdims=True)
            m_state[h:h + 1, :] = jnp.broadcast_to(m_new, (1, LANES))

            mu = jnp.mean(hcur, axis=-1, keepdims=True)
            dlt = hcur - mu
            var = jnp.mean(dlt * dlt, axis=-1, keepdims=True)
            hn = dlt * lax.rsqrt(var + EPS) * nw_ref[:, cols]
            out_ref[rows, cols] = (_sigmoid(o_ref[rows, cols]) * hn).astype(out_ref.dtype)


def _mlstm_mixer(proj, gates, conv_w, conv_b, gate_b, norm_w, B, S):
    T = 256
    GW = GROUP_WIDTH
    gb = jnp.zeros((1, LANES), F32).at[0, :2 * N_HEADS].set(gate_b)
    return pl.pallas_call(
        _mlstm_kernel,
        out_shape=jax.ShapeDtypeStruct((B, S, GW), BF16),
        grid=(B, S // T),
        in_specs=[pl.BlockSpec((None, T, 2 * GW), lambda b, t: (b, t, 0)),
                  pl.BlockSpec((None, T, GW), lambda b, t: (b, t, 2)),
                  pl.BlockSpec((None, T, GW), lambda b, t: (b, t, 3)),
                  pl.BlockSpec((None, T, LANES), lambda b, t: (b, t, 0)),
                  pl.BlockSpec((MLSTM_CONV, 2 * GW), lambda b, t: (0, 0)),
                  pl.BlockSpec((1, 2 * GW), lambda b, t: (0, 0)),
                  pl.BlockSpec((1, LANES), lambda b, t: (0, 0)),
                  pl.BlockSpec((1, GW), lambda b, t: (0, 0))],
        out_specs=pl.BlockSpec((None, T, GW), lambda b, t: (b, t, 0)),
        scratch_shapes=[pltpu.VMEM((N_HEADS, HEAD_DIM, HEAD_DIM), F32),
                        pltpu.VMEM((8, LANES), F32),
                        pltpu.VMEM((8, LANES), F32),
                        pltpu.VMEM((8, 2 * GW), F32)],
        compiler_params=_params(("parallel", "arbitrary")),
        name="mlstm_mixer",
    )(proj, proj, proj, gates, conv_w, conv_b.reshape(1, -1), gb, norm_w.reshape(1, -1))


def _pool_kernel(x_ref, w_ref, scale_ref, gn_ref, out_ref, tail_ref):
    T = x_ref.shape[0]
    HALO = 16
    t_blk = pl.program_id(1)

    @pl.when(t_blk == 0)
    def _():
        tail_ref[...] = jnp.zeros_like(tail_ref)

    x = x_ref[...]
    xx = jnp.concatenate([tail_ref[...], x], axis=0)
    tail_ref[...] = x[T - HALO:T, :]
    pos = t_blk * T + lax.broadcasted_iota(jnp.int32, (T, 1), 0) + 1
    outs = []
    for g, win in enumerate(POOL_WINDOWS):
        cols = slice(g * LANES, (g + 1) * LANES)
        s = xx[:, cols]
        k = 1
        while k < win:
            s = s + pltpu.roll(s, k, axis=0)
            k *= 2
        cnt = jnp.minimum(pos, win).astype(F32)
        yg = s[HALO:, :] / cnt - x[:, cols]
        outs.append(_bdot(yg, w_ref[g]))
    y = jnp.concatenate(outs, axis=-1) * scale_ref[...]
    out_ref[...] = _rms_rows(y, gn_ref[...]).astype(out_ref.dtype)


def _pool_mixer(proj, pool_w, pool_scale, gn, B, S):
    T = 512
    GW = GROUP_WIDTH
    return pl.pallas_call(
        _pool_kernel,
        out_shape=jax.ShapeDtypeStruct((B, S, GW), BF16),
        grid=(B, S // T),
        in_specs=[pl.BlockSpec((None, T, GW), lambda b, t: (b, t, 4)),
                  pl.BlockSpec((len(POOL_WINDOWS), LANES, LANES), lambda b, t: (0, 0, 0)),
                  pl.BlockSpec((1, GW), lambda b, t: (0, 0)),
                  pl.BlockSpec((1, GW), lambda b, t: (0, 0))],
        out_specs=pl.BlockSpec((None, T, GW), lambda b, t: (b, t, 0)),
        scratch_shapes=[pltpu.VMEM((16, GW), F32)],
        compiler_params=_params(("parallel", "arbitrary")),
        name="pool_mixer",
    )(proj, pool_w.astype(BF16), pool_scale.reshape(1, -1), gn.reshape(1, -1))


def _sb_kernel(q_ref, k_ref, v_ref, out_ref, kb_ref, vb_ref):
    LQ = q_ref.shape[0]
    LK = SB_KEY_BLOCK
    ratio = LQ // LK
    qi = pl.program_id(1)

    @pl.when(qi == 0)
    def _():
        kb_ref[...] = k_ref[...].astype(BF16)
        vb_ref[...] = v_ref[...].astype(BF16)

    scale2 = HEAD_DIM ** -0.5 * LOG2E
    tri = (lax.broadcasted_iota(jnp.int32, (LK, LK), 0)
           > lax.broadcasted_iota(jnp.int32, (LK, LK), 1)).astype(BF16)
    row = lax.broadcasted_iota(jnp.int32, (LQ, LK), 0)
    col = lax.broadcasted_iota(jnp.int32, (LQ, LK), 1)
    heads = [slice(h * HEAD_DIM, (h + 1) * HEAD_DIM) for h in range(N_HEADS)]
    qs = [(q_ref[:, hs] * scale2).astype(BF16) for hs in heads]

    def sweep(kb, carry, strict):
        start = pl.multiple_of(kb * LK, LK)
        zs, lks = [], []
        for h, hs in enumerate(heads):
            k_blk = kb_ref[pl.ds(start, LK), hs]
            z = lax.dot_general(qs[h], k_blk, (((1,), (1,)), ((), ())), preferred_element_type=F32)
            lk = -(jnp.maximum(z, 0.0) + jnp.log2(1.0 + jnp.exp2(-jnp.abs(z))))
            if strict is not None:
                lk = jnp.where(strict, lk, 0.0)
            zs.append(z)
            lks.append(lk)
        inner = jnp.dot(jnp.concatenate([lk.astype(BF16) for lk in lks], axis=0), tri,
                        preferred_element_type=F32)
        new = []
        for h, hs in enumerate(heads):
            after, acc = carry[h]
            v_blk = vb_ref[pl.ds(start, LK), hs]
            a = jnp.exp2(zs[h] + lks[h] + inner[h * LQ:(h + 1) * LQ, :] + after)
            if strict is not None:
                a = jnp.where(strict, a, 0.0)
            acc = acc + jnp.dot(a.astype(BF16), v_blk, preferred_element_type=F32)
            after = after + jnp.sum(lks[h], axis=-1, keepdims=True)
            new.append((after, acc))
        return tuple(new)

    init = tuple((jnp.zeros((LQ, 1), F32), jnp.zeros((LQ, HEAD_DIM), F32)) for _ in heads)
    carry = init
    for j in reversed(range(ratio)):
        carry = sweep(ratio * qi + j, carry, (j * LK + col) < row)
    carry = lax.fori_loop(0, ratio * qi, lambda it, c: sweep(ratio * qi - 1 - it, c, None), carry)
    for h, hs in enumerate(heads):
        out_ref[:, hs] = carry[h][1]


def _sb_mixer(proj, B, S):
    LQ = 512
    GW = GROUP_WIDTH
    return pl.pallas_call(
        _sb_kernel,
        out_shape=jax.ShapeDtypeStruct((B, S, GW), F32),
        grid=(B, S // LQ),
        in_specs=[pl.BlockSpec((None, LQ, GW), lambda b, i: (b, i, 5)),
                  pl.BlockSpec((None, S, GW), lambda b, i: (b, 0, 6)),
                  pl.BlockSpec((None, S, GW), lambda b, i: (b, 0, 7))],
        out_specs=pl.BlockSpec((None, LQ, GW), lambda b, i: (b, i, 0)),
        scratch_shapes=[pltpu.VMEM((S, GW), BF16), pltpu.VMEM((S, GW), BF16)],
        compiler_params=_params(("parallel", "arbitrary")),
        name="sb_attention",
    )(proj, proj, proj)


def _conv_kernel(x_ref, dw_ref, dwb_ref, lnw_ref, lnb_ref, pw_ref, pwb_ref, gn_ref, out_ref,
                 hbuf, ybuf):
    T = x_ref.shape[0]
    GW = GROUP_WIDTH
    HALO = 32
    SUB = 32

    @pl.when(pl.program_id(1) == 0)
    def _():
        hbuf[0:HALO, :] = jnp.zeros((HALO, GW), F32)

    @pl.when(pl.program_id(1) > 0)
    def _():
        hbuf[0:HALO, :] = hbuf[T:T + HALO, :]

    hbuf[HALO:HALO + T, :] = x_ref[:, 0:GW] * _sigmoid(x_ref[:, GW:2 * GW])

    def sub(i, carry):
        r0 = pl.multiple_of(i * SUB, SUB)
        acc = jnp.broadcast_to(dwb_ref[...], (SUB, GW))
        win = hbuf[pl.ds(r0, SUB + HALO), :]
        for j in range(CONV_WIDTH):
            off = HALO - CONV_WIDTH + 1 + j
            acc = acc + dw_ref[j:j + 1, :] * win[off:off + SUB, :]
        ybuf[pl.ds(r0, SUB), :] = acc
        return carry

    lax.fori_loop(0, T // SUB, sub, 0)
    hn = _silu(_layer_norm_rows(ybuf[...], lnw_ref[...], lnb_ref[...]))
    y = _bdot(hn, pw_ref[...]) + pwb_ref[...]
    out_ref[...] = _rms_rows(y, gn_ref[...]).astype(out_ref.dtype)


def _conv_mixer(proj, dw_w, dw_b, ln_w, ln_b, pw_w, pw_b, gn, B, S):
    T = 256
    GW = GROUP_WIDTH
    dw_pad = jnp.zeros((32, GW), F32).at[:CONV_WIDTH].set(dw_w)
    r = lambda a: a.reshape(1, -1)
    return pl.pallas_call(
        _conv_kernel,
        out_shape=jax.ShapeDtypeStruct((B, S, GW), BF16),
        grid=(B, S // T),
        in_specs=[pl.BlockSpec((None, T, 2 * GW), lambda b, t: (b, t, 4)),
                  pl.BlockSpec((32, GW), lambda b, t: (0, 0)),
                  pl.BlockSpec((1, GW), lambda b, t: (0, 0)),
                  pl.BlockSpec((1, GW), lambda b, t: (0, 0)),
                  pl.BlockSpec((1, GW), lambda b, t: (0, 0)),
                  pl.BlockSpec((GW, GW), lambda b, t: (0, 0)),
                  pl.BlockSpec((1, GW), lambda b, t: (0, 0)),
                  pl.BlockSpec((1, GW), lambda b, t: (0, 0))],
        out_specs=pl.BlockSpec((None, T, GW), lambda b, t: (b, t, 0)),
        scratch_shapes=[pltpu.VMEM((T + 32, GW), F32), pltpu.VMEM((T, GW), F32)],
        compiler_params=_params(("parallel", "arbitrary")),
        name="conv_mixer",
    )(proj, dw_pad, r(dw_b), r(ln_w), r(ln_b), pw_w.astype(BF16), r(pw_b), r(gn))


def _outproj_kernel(with_router, ya_ref, yb_ref, yc_ref, yd_ref, gnc_ref, w_ref, x_ref, g1_ref,
                    lnw_ref, lnb_ref, sc_ref, sh_ref, *rest):
    GW = GROUP_WIDTH
    if with_router:
        rw_ref, rb_ref, x1_ref, u2_ref, ids_ref, topw_ref = rest
    else:
        x1_ref, u2_ref = rest
    yc = _rms_rows(yc_ref[...], gnc_ref[...]).astype(BF16)
    acc = jnp.dot(ya_ref[...], w_ref[0:GW, :], preferred_element_type=F32)
    acc = acc + jnp.dot(yb_ref[...], w_ref[GW:2 * GW, :], preferred_element_type=F32)
    acc = acc + jnp.dot(yc, w_ref[2 * GW:3 * GW, :], preferred_element_type=F32)
    acc = acc + jnp.dot(yd_ref[...], w_ref[3 * GW:4 * GW, :], preferred_element_type=F32)
    r = ALPHA * x_ref[...] + (1.0 + g1_ref[...]) * acc
    x1 = _layer_norm_rows(r, lnw_ref[...], lnb_ref[...])
    x1_ref[...] = x1
    u2 = x1 * (1.0 + sc_ref[...]) + sh_ref[...]
    u2_ref[...] = u2.astype(u2_ref.dtype)
    if with_router:
        u_hi = u2.astype(BF16)
        u_lo = (u2 - u_hi.astype(F32)).astype(BF16)
        p = jnp.dot(u_hi, rw_ref[...], preferred_element_type=F32)
        logits = (p[:, :LANES] + p[:, LANES:]
                  + jnp.dot(u_lo, rw_ref[:, :LANES], preferred_element_type=F32) + rb_ref[...])
        lane = lax.broadcasted_iota(jnp.int32, logits.shape, 1)
        lg = jnp.where(lane < N_EXPERTS, logits, -jnp.inf)
        m1 = jnp.max(lg, axis=-1, keepdims=True)
        i1 = jnp.min(jnp.where(lg == m1, lane, LANES), axis=-1, keepdims=True)
        lg2 = jnp.where(lane == i1, -jnp.inf, lg)
        m2 = jnp.max(lg2, axis=-1, keepdims=True)
        i2 = jnp.min(jnp.where(lg2 == m2, lane, LANES), axis=-1, keepdims=True)
        e2 = jnp.exp(m2 - m1)
        w1 = 1.0 / (1.0 + e2)
        w2 = e2 / (1.0 + e2)
        ids_ref[...] = jnp.where(lane == 0, i1, jnp.where(lane == 1, i2, 0))
        topw_ref[...] = jnp.where(lane == 0, w1, jnp.where(lane == 1, w2, 0.0))


def _out_proj(ya, yb, yc, yd, gnc, w_out, x2d, g1, ln_w, ln_b, sc2, sh2, seq, router=None):
    N, D = x2d.shape
    GW = GROUP_WIDTH
    tm = 512
    tpb = seq // tm
    row = lambda m: (m, 0)
    const = lambda m: (0, 0)
    perb = lambda m: (m // tpb, 0, 0)
    in_specs = [pl.BlockSpec((tm, GW), row), pl.BlockSpec((tm, GW), row),
                pl.BlockSpec((tm, GW), row), pl.BlockSpec((tm, GW), row),
                pl.BlockSpec((1, GW), const),
                pl.BlockSpec((D, D), const),
                pl.BlockSpec((tm, D), row),
                pl.BlockSpec((None, 1, D), perb),
                pl.BlockSpec((1, D), const), pl.BlockSpec((1, D), const),
                pl.BlockSpec((None, 1, D), perb), pl.BlockSpec((None, 1, D), perb)]
    args = [ya, yb, yc, yd, gnc.reshape(1, -1), w_out, x2d, g1, ln_w.reshape(1, -1),
            ln_b.reshape(1, -1), sc2, sh2]
    out_shape = [jax.ShapeDtypeStruct((N, D), F32),
                 jax.ShapeDtypeStruct((N, D), BF16 if router is None else F32)]
    out_specs = [pl.BlockSpec((tm, D), row), pl.BlockSpec((tm, D), row)]
    if router is not None:
        rw, rb = router
        in_specs += [pl.BlockSpec((D, 2 * LANES), const), pl.BlockSpec((1, LANES), const)]
        args += [rw, rb]
        out_shape += [jax.ShapeDtypeStruct((N, LANES), jnp.int32), jax.ShapeDtypeStruct((N, LANES), F32)]
        out_specs += [pl.BlockSpec((tm, LANES), row), pl.BlockSpec((tm, LANES), row)]
    return pl.pallas_call(
        functools.partial(_outproj_kernel, router is not None),
        out_shape=tuple(out_shape),
        grid=(N // tm,),
        in_specs=in_specs,
        out_specs=tuple(out_specs),
        compiler_params=_params(("parallel",)),
        name="out_proj_ln",
    )(*args)


def _ffn_kernel(u_ref, wg_ref, wu_ref, wd_ref, x_ref, g2_ref, lnw_ref, lnb_ref, out_ref, acc_ref):
    f = pl.program_id(1)

    @pl.when(f == 0)
    def _():
        acc_ref[...] = jnp.zeros_like(acc_ref)

    u = u_ref[...]
    hg = jnp.dot(u, wg_ref[...], preferred_element_type=F32)
    hu = jnp.dot(u, wu_ref[...], preferred_element_type=F32)
    h = _silu(hg) * hu
    acc_ref[...] += jnp.dot(h.astype(BF16), wd_ref[...], preferred_element_type=F32)

    @pl.when(f == pl.num_programs(1) - 1)
    def _():
        r = ALPHA * x_ref[...] + (1.0 + g2_ref[...]) * acc_ref[...]
        out_ref[...] = _layer_norm_rows(r, lnw_ref[...], lnb_ref[...])


def _ffn(u2, wg, wu, wd, x1, g2, ln_w, ln_b, seq):
    N, D = u2.shape
    F = wg.shape[1]
    tm, tf = 512, 512
    tpb = seq // tm
    row = lambda m, f: (m, 0)
    const = lambda m, f: (0, 0)
    return pl.pallas_call(
        _ffn_kernel,
        out_shape=jax.ShapeDtypeStruct((N, D), F32),
        grid=(N // tm, F // tf),
        in_specs=[pl.BlockSpec((tm, D), row),
                  pl.BlockSpec((D, tf), lambda m, f: (0, f)),
                  pl.BlockSpec((D, tf), lambda m, f: (0, f)),
                  pl.BlockSpec((tf, D), lambda m, f: (f, 0)),
                  pl.BlockSpec((tm, D), row),
                  pl.BlockSpec((None, 1, D), lambda m, f: (m // tpb, 0, 0)),
                  pl.BlockSpec((1, D), const), pl.BlockSpec((1, D), const)],
        out_specs=pl.BlockSpec((tm, D), row),
        scratch_shapes=[pltpu.VMEM((tm, D), F32)],
        compiler_params=_params(("parallel", "arbitrary")),
        name="ffn_ln",
    )(u2, wg, wu, wd, x1, g2, ln_w.reshape(1, -1), ln_b.reshape(1, -1))


MOE_TM = 512


def _row_copy(src, src_row, dst, dst_row, sem):
    return pltpu.make_async_copy(src.at[pl.ds(src_row, 1), :], dst.at[pl.ds(dst_row, 1), :], sem)


def _start_rows(n_rows, make_copy):
    def body(i, carry):
        base = pl.multiple_of(i * 8, 8)
        for j in range(8):
            make_copy(base, j).start(priority=j % 2)
        return carry
    lax.fori_loop(0, n_rows // 8, body, 0)


def _wait_rows(n_rows, one_copy):
    def body(i, carry):
        one_copy.wait()
        return carry
    lax.fori_loop(0, n_rows, body, 0, unroll=8)


def _moe_ffn_kernel(te_ref, nv_ref, tok_ref, u_hbm, wg_ref, wu_ref, wd_ref, out_ref, xbuf, sem):
    t = pl.program_id(0)
    f = pl.program_id(1)
    tm = out_ref.shape[0]
    tf = wg_ref.shape[1]
    half = (tf // LANES // 2) * LANES
    slot = t % 2

    def gather(tile, buf):
        _start_rows(tm, lambda base, j: _row_copy(u_hbm, tok_ref[tile * tm + base + j], xbuf.at[buf],
                                                  base + j, sem.at[buf]))

    @pl.when((t == 0) & (f == 0))
    def _():
        gather(0, 0)

    @pl.when((f == 0) & (t + 1 < nv_ref[0]))
    def _():
        gather(t + 1, 1 - slot)

    @pl.when((f == 0) & (t < nv_ref[0]))
    def _():
        _wait_rows(tm, _row_copy(u_hbm, 0, xbuf.at[slot], 0, sem.at[slot]))

    @pl.when(t < nv_ref[0])
    def _():
        u = xbuf[slot].astype(BF16)
        y = None
        for c0, c1 in ((0, half), (half, tf)):
            hg = jnp.dot(u, wg_ref[:, c0:c1], preferred_element_type=F32)
            hu = jnp.dot(u, wu_ref[:, c0:c1], preferred_element_type=F32)
            h = (_silu(hg) * hu).astype(BF16)
            part = jnp.dot(h, wd_ref[c0:c1, :], preferred_element_type=F32)
            y = part if y is None else y + part

        @pl.when(f == 0)
        def _():
            out_ref[...] = y

        @pl.when(f > 0)
        def _():
            out_ref[...] += y

    @pl.when((t >= nv_ref[0]) & (f == 0))
    def _():
        out_ref[...] = jnp.zeros_like(out_ref)


def _moe_ffn(u2, slot_token, tile_expert, n_valid, wg, wu, wd):
    D = u2.shape[1]
    R = slot_token.shape[0]
    F = wg.shape[2]
    tm, tf = MOE_TM, F // 2
    nf = F // tf
    fidx = lambda t, f, nv: jnp.where(t < nv[0], f, nf - 1)
    return pl.pallas_call(
        _moe_ffn_kernel,
        out_shape=jax.ShapeDtypeStruct((R, D), F32),
        grid_spec=pltpu.PrefetchScalarGridSpec(
            num_scalar_prefetch=3,
            grid=(R // tm, nf),
            in_specs=[pl.BlockSpec(memory_space=pl.ANY),
                      pl.BlockSpec((None, D, tf), lambda t, f, te, nv, tok: (te[t], 0, fidx(t, f, nv))),
                      pl.BlockSpec((None, D, tf), lambda t, f, te, nv, tok: (te[t], 0, fidx(t, f, nv))),
                      pl.BlockSpec((None, tf, D), lambda t, f, te, nv, tok: (te[t], fidx(t, f, nv), 0))],
            out_specs=pl.BlockSpec((tm, D), lambda t, f, te, nv, tok: (t, 0)),
            scratch_shapes=[pltpu.VMEM((2, tm, D), F32), pltpu.SemaphoreType.DMA((2,))]),
        compiler_params=_params(("arbitrary", "arbitrary"), vmem=60 * 1024 * 1024),
        name="moe_ffn",
    )(tile_expert, n_valid, slot_token, u2, wg, wu, wd)


def _combine_kernel(pos_ref, y_ref, topw_ref, x_ref, g2_ref, lnw_ref, lnb_ref, out_ref, ybuf, sem):
    tm = x_ref.shape[0]
    m = pl.program_id(0)
    slot = m % 2

    def gather(step, buf):
        for k in range(2):
            _start_rows(tm, lambda base, j: _row_copy(y_ref, pos_ref[2 * (step * tm + base + j) + k],
                                                      ybuf.at[buf, k], base + j, sem.at[buf]))

    @pl.when(m == 0)
    def _():
        gather(0, 0)

    @pl.when(m + 1 < pl.num_programs(0))
    def _():
        gather(m + 1, 1 - slot)

    _wait_rows(2 * tm, _row_copy(y_ref, 0, ybuf.at[slot, 0], 0, sem.at[slot]))
    tw = topw_ref[...]
    y = tw[:, 0:1] * ybuf[slot, 0] + tw[:, 1:2] * ybuf[slot, 1]
    r = ALPHA * x_ref[...] + (1.0 + g2_ref[...]) * y
    out_ref[...] = _layer_norm_rows(r, lnw_ref[...], lnb_ref[...])


def _combine(y_sorted, pos, topw, x1, g2, ln_w, ln_b, seq):
    N, D = x1.shape
    tm = 256
    tpb = seq // tm
    row = lambda m, pos: (m, 0)
    const = lambda m, pos: (0, 0)
    return pl.pallas_call(
        _combine_kernel,
        out_shape=jax.ShapeDtypeStruct((N, D), F32),
        grid_spec=pltpu.PrefetchScalarGridSpec(
            num_scalar_prefetch=1,
            grid=(N // tm,),
            in_specs=[pl.BlockSpec(memory_space=pl.ANY),
                      pl.BlockSpec((tm, LANES), row),
                      pl.BlockSpec((tm, D), row),
                      pl.BlockSpec((None, 1, D), lambda m, pos: (m // tpb, 0, 0)),
                      pl.BlockSpec((1, D), const), pl.BlockSpec((1, D), const)],
            out_specs=pl.BlockSpec((tm, D), row),
            scratch_shapes=[pltpu.VMEM((2, 2, tm, D), F32), pltpu.SemaphoreType.DMA((2,))]),
        compiler_params=_params(("arbitrary",)),
        name="moe_combine_ln",
    )(pos, y_sorted, topw, x1, g2, ln_w.reshape(1, -1), ln_b.reshape(1, -1))


def _routing_plan(ids, n_tiles):
    e_flat = ids[:, :2].reshape(-1)
    onehot = (e_flat[:, None] == jnp.arange(N_EXPERTS, dtype=jnp.int32)[None, :]).astype(jnp.int32)
    csum = jnp.cumsum(onehot, axis=0)
    rank = jnp.sum((csum - onehot) * onehot, axis=1)
    counts = csum[-1]
    padded = ((counts + MOE_TM - 1) // MOE_TM) * MOE_TM
    ends = jnp.cumsum(padded)
    offs = ends - padded
    pos = (jnp.sum(onehot * offs[None, :], axis=1) + rank).astype(jnp.int32)
    tile_start = jnp.arange(n_tiles, dtype=jnp.int32) * MOE_TM
    n_valid = (ends[-1] // MOE_TM).astype(jnp.int32)
    tile_start = jnp.minimum(tile_start, (n_valid - 1) * MOE_TM)
    tile_expert = jnp.sum((tile_start[:, None] >= ends[None, :]).astype(jnp.int32), axis=1).astype(jnp.int32)
    slot_token = jnp.zeros((n_tiles * MOE_TM,), jnp.int32).at[pos].set(
        jnp.arange(pos.shape[0], dtype=jnp.int32) // 2, unique_indices=True)
    return pos, slot_token, tile_expert, n_valid.reshape(1)


def _moe(u2, ids, topw, wg, wu, wd, x1, g2, ln_w, ln_b, seq):
    N = u2.shape[0]
    n_tiles = 2 * N // MOE_TM + N_EXPERTS
    pos, slot_token, tile_expert, n_valid = _routing_plan(ids, n_tiles)
    y_sorted = _moe_ffn(u2, slot_token, tile_expert, n_valid, wg, wu, wd)
    return _combine(y_sorted, pos, topw, x1, g2, ln_w, ln_b, seq)


def _split_w_in(w):
    GW = GROUP_WIDTH
    g0 = 4 * GW
    g1 = g0 + 2 * N_HEADS
    main = jnp.concatenate([w[:, :g0], w[:, g1:]], axis=1).astype(BF16)
    gate = jnp.zeros((w.shape[0], LANES), BF16).at[:, :2 * N_HEADS].set(w[:, g0:g1].astype(BF16))
    return main, gate


def kernel(x, c, w_in, mlstm_conv_w, mlstm_conv_b, mlstm_gate_b, mlstm_norm_w, pool_w, pool_scale, conv_dw_w, conv_dw_b, conv_ln_w, conv_ln_b, conv_pw_w, conv_pw_b, group_norm_w, w_out, ada_w, ada_b, ln1_w, ln1_b, ln2_w, ln2_b, ffn_w_gate, ffn_w_up, ffn_w_down, moe_router_w, moe_router_b, moe_w_gate, moe_w_up, moe_w_down):
    B, S, D = x.shape
    GW = GROUP_WIDTH
    ada = _ada_all(c, ada_w, ada_b)
    x2d = x.reshape(B * S, D)
    for l in range(DEPTH):
        mod = [ada[l, :, i * D:(i + 1) * D].reshape(B, 1, D) for i in range(6)]
        sh1, sc1, g1, sh2, sc2, g2 = mod
        w_main, w_gate = _split_w_in(w_in[l])
        proj, gates = _in_proj(x2d, sc1, sh1, w_main, w_gate, S)
        proj = proj.reshape(B, S, -1)
        gates = gates.reshape(B, S, LANES)
        gn_b, gn_c, gn_d = (group_norm_w[l, i * GW:(i + 1) * GW] for i in range(3))
        ya = _mlstm_mixer(proj, gates, mlstm_conv_w[l], mlstm_conv_b[l], mlstm_gate_b[l],
                          mlstm_norm_w[l], B, S)
        yb = _pool_mixer(proj, pool_w[l], pool_scale[l], gn_b, B, S)
        yc = _sb_mixer(proj, B, S)
        yd = _conv_mixer(proj, conv_dw_w[l], conv_dw_b[l], conv_ln_w[l], conv_ln_b[l],
                         conv_pw_w[l], conv_pw_b[l], gn_d, B, S)
        flat = lambda t: t.reshape(B * S, GW)
        j = l // 2
        router = None
        if l % 2 == 1:
            rw = jnp.zeros((D, LANES), F32).at[:, :N_EXPERTS].set(moe_router_w[j])
            rw_hi = rw.astype(BF16)
            rw_lo = (rw - rw_hi.astype(F32)).astype(BF16)
            rb = jnp.zeros((1, LANES), F32).at[0, :N_EXPERTS].set(moe_router_b[j])
            router = (jnp.concatenate([rw_hi, rw_lo], axis=1), rb)
        outs = _out_proj(flat(ya), flat(yb), flat(yc), flat(yd), gn_c, w_out[l].astype(BF16), x2d,
                         g1, ln1_w[l], ln1_b[l], sc2, sh2, S, router)
        if l % 2 == 0:
            x1, u2 = outs
            x2d = _ffn(u2, ffn_w_gate[j].astype(BF16), ffn_w_up[j].astype(BF16),
                       ffn_w_down[j].astype(BF16), x1, g2, ln2_w[l], ln2_b[l], S)
        else:
            x1, u2, ids, topw = outs
            x2d = _moe(u2, ids, topw, moe_w_gate[j].astype(BF16), moe_w_up[j].astype(BF16),
                       moe_w_down[j].astype(BF16), x1, g2, ln2_w[l], ln2_b[l], S)
    return x2d.reshape(B, S, D)
```

```python
import functools

import jax
import jax.numpy as jnp
from jax import lax
from jax.experimental import pallas as pl
from jax.experimental.pallas import tpu as pltpu

F32 = jnp.float32
BF16 = jnp.bfloat16

DEPTH = 2
CHUNK = 256
GROUP_WIDTH = 512
N_HEADS = 4
HEAD_DIM = 128
MLSTM_CONV = 4
POOL_WINDOWS = (2, 4, 8, 16)
CONV_WIDTH = 31
SB_KEY_BLOCK = 256
N_EXPERTS = 8
ALPHA = (2.0 * DEPTH) ** 0.25
EPS = 1e-5
LOG2E = 1.4426950408889634
LANES = 128
VMEM_LIMIT = 56 * 1024 * 1024


def _params(sem, vmem=VMEM_LIMIT):
    return pltpu.CompilerParams(dimension_semantics=sem, vmem_limit_bytes=vmem)


def _silu(x):
    return x * (1.0 / (1.0 + jnp.exp(-x)))


def _sigmoid(x):
    return 1.0 / (1.0 + jnp.exp(-x))


def _neg_softplus(x):
    return -(jnp.maximum(x, 0.0) + jnp.log(1.0 + jnp.exp(-jnp.abs(x))))


def _layer_norm_rows(r, w, b):
    mu = jnp.mean(r, axis=-1, keepdims=True)
    d = r - mu
    var = jnp.mean(d * d, axis=-1, keepdims=True)
    return d * lax.rsqrt(var + EPS) * w + b


def _rms_rows(y, w):
    return y * lax.rsqrt(jnp.mean(y * y, axis=-1, keepdims=True) + EPS) * w


def _bdot(a, b):
    return jnp.dot(a.astype(BF16), b.astype(BF16), preferred_element_type=F32)


def _ada_kernel(c_ref, w_ref, b_ref, out_ref):
    out_ref[...] = _bdot(_silu(c_ref[...]), w_ref[...]) + b_ref[...]


def _ada_all(c, ada_w, ada_b):
    L, D, D6 = ada_w.shape
    Bn = c.shape[0]
    tn = 1024
    return pl.pallas_call(
        _ada_kernel,
        out_shape=jax.ShapeDtypeStruct((L, Bn, D6), F32),
        grid=(L, D6 // tn),
        in_specs=[pl.BlockSpec((Bn, D), lambda l, n: (0, 0)),
                  pl.BlockSpec((None, D, tn), lambda l, n: (l, 0, n)),
                  pl.BlockSpec((None, 1, tn), lambda l, n: (l, 0, n))],
        out_specs=pl.BlockSpec((None, Bn, tn), lambda l, n: (l, 0, n)),
        compiler_params=_params(("parallel", "parallel")),
        name="ada_mod",
    )(c, ada_w, ada_b.reshape(L, 1, D6))


def _inproj_kernel(x_ref, sc_ref, sh_ref, w_ref, wg_ref, proj_ref, gates_ref, u_ref):
    @pl.when(pl.program_id(1) == 0)
    def _():
        u = (x_ref[...] * (1.0 + sc_ref[...]) + sh_ref[...]).astype(BF16)
        u_ref[...] = u
        gates_ref[...] = jnp.dot(u, wg_ref[...], preferred_element_type=F32)

    proj_ref[...] = jnp.dot(u_ref[...], w_ref[...], preferred_element_type=F32)


def _in_proj(x2d, sc, sh, w_main, w_gate, seq):
    N, D = x2d.shape
    NC = w_main.shape[1]
    tm, tn = 1024, 1280
    tpb = seq // tm
    return pl.pallas_call(
        _inproj_kernel,
        out_shape=(jax.ShapeDtypeStruct((N, NC), F32), jax.ShapeDtypeStruct((N, LANES), F32)),
        grid=(N // tm, NC // tn),
        in_specs=[pl.BlockSpec((tm, D), lambda m, n: (m, 0)),
                  pl.BlockSpec((None, 1, D), lambda m, n: (m // tpb, 0, 0)),
                  pl.BlockSpec((None, 1, D), lambda m, n: (m // tpb, 0, 0)),
                  pl.BlockSpec((D, tn), lambda m, n: (0, n)),
                  pl.BlockSpec((D, LANES), lambda m, n: (0, 0))],
        out_specs=(pl.BlockSpec((tm, tn), lambda m, n: (m, n)),
                   pl.BlockSpec((tm, LANES), lambda m, n: (m, 0))),
        scratch_shapes=[pltpu.VMEM((tm, D), BF16)],
        compiler_params=_params(("parallel", "arbitrary")),
        name="in_proj",
    )(x2d, sc, sh, w_main, w_gate)


def _cumsum_rows(x):
    n = x.shape[0]
    row = lax.broadcasted_iota(jnp.int32, x.shape, 0)
    k = 1
    while k < n:
        x = x + jnp.where(row >= k, pltpu.roll(x, k, axis=0), 0.0)
        k *= 2
    return x


def _mlstm_kernel(qk_ref, v_ref, o_ref, g_ref, cw_ref, cb_ref, gb_ref, nw_ref, out_ref,
                  c_state, n_state, m_state, tail_ref):
    T = qk_ref.shape[0]
    GW = GROUP_WIDTH

    @pl.when(pl.program_id(1) == 0)
    def _():
        c_state[...] = jnp.zeros_like(c_state)
        n_state[...] = jnp.zeros_like(n_state)
        m_state[...] = jnp.zeros_like(m_state)
        tail_ref[...] = jnp.zeros_like(tail_ref)

    xx = jnp.concatenate([tail_ref[...], qk_ref[...]], axis=0)
    tail_ref[...] = qk_ref[T - 8:T, :]
    conv = cb_ref[...]
    for j in range(MLSTM_CONV):
        off = 8 - (MLSTM_CONV - 1) + j
        conv = conv + cw_ref[j:j + 1, :] * xx[off:off + T, :]
    qk = _silu(conv)

    gates = g_ref[...] + gb_ref[...]
    logf = _neg_softplus(-gates)
    causal = (lax.broadcasted_iota(jnp.int32, (CHUNK, CHUNK), 1)
              <= lax.broadcasted_iota(jnp.int32, (CHUNK, CHUNK), 0))

    for c in range(T // CHUNK):
        rows = slice(c * CHUNK, (c + 1) * CHUNK)
        gi = gates[rows, :]
        bcum = _cumsum_rows(logf[rows, :])
        gi_t = gi.T
        bcum_t = bcum.T
        for h in range(N_HEADS):
            cols = slice(h * HEAD_DIM, (h + 1) * HEAD_DIM)
            q = qk[rows, cols] * (HEAD_DIM ** -0.5)
            k = qk[rows, GW + h * HEAD_DIM:GW + (h + 1) * HEAD_DIM]
            v = v_ref[rows, cols]
            b_col = bcum[:, N_HEADS + h:N_HEADS + h + 1]
            b_row = bcum_t[N_HEADS + h:N_HEADS + h + 1, :]
            i_col = gi[:, h:h + 1]
            i_row = gi_t[h:h + 1, :]
            g_tot = b_col[CHUNK - 1:CHUNK, :]
            m_prev = m_state[h:h + 1, 0:1]
            c_prev = c_state[h]
            n_prev = n_state[h:h + 1, :]

            log_d = jnp.where(causal, b_col - b_row + i_row, -jnp.inf)
            m_inter = b_col + m_prev
            m_t = jnp.maximum(m_inter, jnp.max(log_d, axis=-1, keepdims=True))
            s = lax.dot_general(q.astype(BF16), k.astype(BF16), (((1,), (1,)), ((), ())),
                                preferred_element_type=F32)
            w = jnp.exp(log_d - m_t) * s
            inter = jnp.exp(m_inter - m_t)
            num = inter * _bdot(q, c_prev) + _bdot(w, v)
            den = inter * jnp.sum(q * n_prev, axis=-1, keepdims=True) + jnp.sum(w, axis=-1, keepdims=True)
            hcur = num / jnp.maximum(jnp.abs(den), jnp.exp(-m_t))

            a_col = g_tot - b_col + i_col
            m_new = jnp.maximum(g_tot + m_prev, jnp.max(a_col, axis=0, keepdims=True))
            decay = jnp.exp(g_tot + m_prev - m_new)
            kw = k * jnp.exp(a_col - m_new)
            c_state[h] = decay * c_prev + _bdot(kw.T, v)
            n_state[h:h + 1, :] = decay * n_prev + jnp.sum(kw, axis=0, keepdims=True)
            m_state[h:h + 1, :] = jnp.broadcast_to(m_new, (1, LANES))

            mu = jnp.mean(hcur, axis=-1, keepdims=True)
            dlt = hcur - mu
            var = jnp.mean(dlt * dlt, axis=-1, keepdims=True)
            hn = dlt * lax.rsqrt(var + EPS) * nw_ref[:, cols]
            out_ref[rows, cols] = (_sigmoid(o_ref[rows, cols]) * hn).astype(out_ref.dtype)


def _mlstm_mixer(proj, gates, conv_w, conv_b, gate_b, norm_w, B, S):
    T = 256
    GW = GROUP_WIDTH
    gb = jnp.zeros((1, LANES), F32).at[0, :2 * N_HEADS].set(gate_b)
    return pl.pallas_call(
        _mlstm_kernel,
        out_shape=jax.ShapeDtypeStruct((B, S, GW), BF16),
        grid=(B, S // T),
        in_specs=[pl.BlockSpec((None, T, 2 * GW), lambda b, t: (b, t, 0)),
                  pl.BlockSpec((None, T, GW), lambda b, t: (b, t, 2)),
                  pl.BlockSpec((None, T, GW), lambda b, t: (b, t, 3)),
                  pl.BlockSpec((None, T, LANES), lambda b, t: (b, t, 0)),
                  pl.BlockSpec((MLSTM_CONV, 2 * GW), lambda b, t: (0, 0)),
                  pl.BlockSpec((1, 2 * GW), lambda b, t: (0, 0)),
                  pl.BlockSpec((1, LANES), lambda b, t: (0, 0)),
                  pl.BlockSpec((1, GW), lambda b, t: (0, 0))],
        out_specs=pl.BlockSpec((None, T, GW), lambda b, t: (b, t, 0)),
        scratch_shapes=[pltpu.VMEM((N_HEADS, HEAD_DIM, HEAD_DIM), F32),
                        pltpu.VMEM((8, LANES), F32),
                        pltpu.VMEM((8, LANES), F32),
                        pltpu.VMEM((8, 2 * GW), F32)],
        compiler_params=_params(("parallel", "arbitrary")),
        name="mlstm_mixer",
    )(proj, proj, proj, gates, conv_w, conv_b.reshape(1, -1), gb, norm_w.reshape(1, -1))


def _pool_kernel(x_ref, w_ref, scale_ref, gn_ref, out_ref, tail_ref):
    T = x_ref.shape[0]
    HALO = 16
    t_blk = pl.program_id(1)

    @pl.when(t_blk == 0)
    def _():
        tail_ref[...] = jnp.zeros_like(tail_ref)

    x = x_ref[...]
    xx = jnp.concatenate([tail_ref[...], x], axis=0)
    tail_ref[...] = x[T - HALO:T, :]
    pos = t_blk * T + lax.broadcasted_iota(jnp.int32, (T, 1), 0) + 1
    outs = []
    for g, win in enumerate(POOL_WINDOWS):
        cols = slice(g * LANES, (g + 1) * LANES)
        s = xx[:, cols]
        k = 1
        while k < win:
            s = s + pltpu.roll(s, k, axis=0)
            k *= 2
        cnt = jnp.minimum(pos, win).astype(F32)
        yg = s[HALO:, :] / cnt - x[:, cols]
        outs.append(_bdot(yg, w_ref[g]))
    y = jnp.concatenate(outs, axis=-1) * scale_ref[...]
    out_ref[...] = _rms_rows(y, gn_ref[...]).astype(out_ref.dtype)


def _pool_mixer(proj, pool_w, pool_scale, gn, B, S):
    T = 512
    GW = GROUP_WIDTH
    return pl.pallas_call(
        _pool_kernel,
        out_shape=jax.ShapeDtypeStruct((B, S, GW), BF16),
        grid=(B, S // T),
        in_specs=[pl.BlockSpec((None, T, GW), lambda b, t: (b, t, 4)),
                  pl.BlockSpec((len(POOL_WINDOWS), LANES, LANES), lambda b, t: (0, 0, 0)),
                  pl.BlockSpec((1, GW), lambda b, t: (0, 0)),
                  pl.BlockSpec((1, GW), lambda b, t: (0, 0))],
        out_specs=pl.BlockSpec((None, T, GW), lambda b, t: (b, t, 0)),
        scratch_shapes=[pltpu.VMEM((16, GW), F32)],
        compiler_params=_params(("parallel", "arbitrary")),
        name="pool_mixer",
    )(proj, pool_w.astype(BF16), pool_scale.reshape(1, -1), gn.reshape(1, -1))


def _sb_kernel(q_ref, k_ref, v_ref, out_ref, kb_ref, vb_ref):
    LQ = q_ref.shape[0]
    LK = SB_KEY_BLOCK
    ratio = LQ // LK
    qi = pl.program_id(1)

    @pl.when(qi == 0)
    def _():
        kb_ref[...] = k_ref[...].astype(BF16)
        vb_ref[...] = v_ref[...].astype(BF16)

    scale2 = HEAD_DIM ** -0.5 * LOG2E
    tri = (lax.broadcasted_iota(jnp.int32, (LK, LK), 0)
           > lax.broadcasted_iota(jnp.int32, (LK, LK), 1)).astype(BF16)
    row = lax.broadcasted_iota(jnp.int32, (LQ, LK), 0)
    col = lax.broadcasted_iota(jnp.int32, (LQ, LK), 1)
    heads = [slice(h * HEAD_DIM, (h + 1) * HEAD_DIM) for h in range(N_HEADS)]
    qs = [(q_ref[:, hs] * scale2).astype(BF16) for hs in heads]

    def sweep(kb, carry, strict):
        start = pl.multiple_of(kb * LK, LK)
        zs, lks = [], []
        for h, hs in enumerate(heads):
            k_blk = kb_ref[pl.ds(start, LK), hs]
            z = lax.dot_general(qs[h], k_blk, (((1,), (1,)), ((), ())), preferred_element_type=F32)
            lk = -(jnp.maximum(z, 0.0) + jnp.log2(1.0 + jnp.exp2(-jnp.abs(z))))
            if strict is not None:
                lk = jnp.where(strict, lk, 0.0)
            zs.append(z)
            lks.append(lk)
        inner = jnp.dot(jnp.concatenate([lk.astype(BF16) for lk in lks], axis=0), tri,
                        preferred_element_type=F32)
        new = []
        for h, hs in enumerate(heads):
            after, acc = carry[h]
            v_blk = vb_ref[pl.ds(start, LK), hs]
            a = jnp.exp2(zs[h] + lks[h] + inner[h * LQ:(h + 1) * LQ, :] + after)
            if strict is not None:
                a = jnp.where(strict, a, 0.0)
            acc = acc + jnp.dot(a.astype(BF16), v_blk, preferred_element_type=F32)
            after = after + jnp.sum(lks[h], axis=-1, keepdims=True)
            new.append((after, acc))
        return tuple(new)

    init = tuple((jnp.zeros((LQ, 1), F32), jnp.zeros((LQ, HEAD_DIM), F32)) for _ in heads)
    carry = init
    for j in reversed(range(ratio)):
        carry = sweep(ratio * qi + j, carry, (j * LK + col) < row)
    carry = lax.fori_loop(0, ratio * qi, lambda it, c: sweep(ratio * qi - 1 - it, c, None), carry)
    for h, hs in enumerate(heads):
        out_ref[:, hs] = carry[h][1]


def _sb_mixer(proj, B, S):
    LQ = 512
    GW = GROUP_WIDTH
    return pl.pallas_call(
        _sb_kernel,
        out_shape=jax.ShapeDtypeStruct((B, S, GW), F32),
        grid=(B, S // LQ),
        in_specs=[pl.BlockSpec((None, LQ, GW), lambda b, i: (b, i, 5)),
                  pl.BlockSpec((None, S, GW), lambda b, i: (b, 0, 6)),
                  pl.BlockSpec((None, S, GW), lambda b, i: (b, 0, 7))],
        out_specs=pl.BlockSpec((None, LQ, GW), lambda b, i: (b, i, 0)),
        scratch_shapes=[pltpu.VMEM((S, GW), BF16), pltpu.VMEM((S, GW), BF16)],
        compiler_params=_params(("parallel", "arbitrary")),
        name="sb_attention",
    )(proj, proj, proj)


def _conv_kernel(x_ref, dw_ref, dwb_ref, lnw_ref, lnb_ref, pw_ref, pwb_ref, gn_ref, out_ref,
                 hbuf, ybuf):
    T = x_ref.shape[0]
    GW = GROUP_WIDTH
    HALO = 32
    SUB = 32

    @pl.when(pl.program_id(1) == 0)
    def _():
        hbuf[0:HALO, :] = jnp.zeros((HALO, GW), F32)

    @pl.when(pl.program_id(1) > 0)
    def _():
        hbuf[0:HALO, :] = hbuf[T:T + HALO, :]

    hbuf[HALO:HALO + T, :] = x_ref[:, 0:GW] * _sigmoid(x_ref[:, GW:2 * GW])

    def sub(i, carry):
        r0 = pl.multiple_of(i * SUB, SUB)
        acc = jnp.broadcast_to(dwb_ref[...], (SUB, GW))
        win = hbuf[pl.ds(r0, SUB + HALO), :]
        for j in range(CONV_WIDTH):
            off = HALO - CONV_WIDTH + 1 + j
            acc = acc + dw_ref[j:j + 1, :] * win[off:off + SUB, :]
        ybuf[pl.ds(r0, SUB), :] = acc
        return carry

    lax.fori_loop(0, T // SUB, sub, 0)
    hn = _silu(_layer_norm_rows(ybuf[...], lnw_ref[...], lnb_ref[...]))
    y = _bdot(hn, pw_ref[...]) + pwb_ref[...]
    out_ref[...] = _rms_rows(y, gn_ref[...]).astype(out_ref.dtype)


def _conv_mixer(proj, dw_w, dw_b, ln_w, ln_b, pw_w, pw_b, gn, B, S):
    T = 256
    GW = GROUP_WIDTH
    dw_pad = jnp.zeros((32, GW), F32).at[:CONV_WIDTH].set(dw_w)
    r = lambda a: a.reshape(1, -1)
    return pl.pallas_call(
        _conv_kernel,
        out_shape=jax.ShapeDtypeStruct((B, S, GW), BF16),
        grid=(B, S // T),
        in_specs=[pl.BlockSpec((None, T, 2 * GW), lambda b, t: (b, t, 4)),
                  pl.BlockSpec((32, GW), lambda b, t: (0, 0)),
                  pl.BlockSpec((1, GW), lambda b, t: (0, 0)),
                  pl.BlockSpec((1, GW), lambda b, t: (0, 0)),
                  pl.BlockSpec((1, GW), lambda b, t: (0, 0)),
                  pl.BlockSpec((GW, GW), lambda b, t: (0, 0)),
                  pl.BlockSpec((1, GW), lambda b, t: (0, 0)),
                  pl.BlockSpec((1, GW), lambda b, t: (0, 0))],
        out_specs=pl.BlockSpec((None, T, GW), lambda b, t: (b, t, 0)),
        scratch_shapes=[pltpu.VMEM((T + 32, GW), F32), pltpu.VMEM((T, GW), F32)],
        compiler_params=_params(("parallel", "arbitrary")),
        name="conv_mixer",
    )(proj, dw_pad, r(dw_b), r(ln_w), r(ln_b), pw_w.astype(BF16), r(pw_b), r(gn))


def _outproj_kernel(with_router, ya_ref, yb_ref, yc_ref, yd_ref, gnc_ref, w_ref, x_ref, g1_ref,
                    lnw_ref, lnb_ref, sc_ref, sh_ref, *rest):
    GW = GROUP_WIDTH
    if with_router:
        rw_ref, rb_ref, x1_ref, u2_ref, ids_ref, topw_ref = rest
    else:
        x1_ref, u2_ref = rest
    yc = _rms_rows(yc_ref[...], gnc_ref[...]).astype(BF16)
    acc = jnp.dot(ya_ref[...], w_ref[0:GW, :], preferred_element_type=F32)
    acc = acc + jnp.dot(yb_ref[...], w_ref[GW:2 * GW, :], preferred_element_type=F32)
    acc = acc + jnp.dot(yc, w_ref[2 * GW:3 * GW, :], preferred_element_type=F32)
    acc = acc + jnp.dot(yd_ref[...], w_ref[3 * GW:4 * GW, :], preferred_element_type=F32)
    r = ALPHA * x_ref[...] + (1.0 + g1_ref[...]) * acc
    x1 = _layer_norm_rows(r, lnw_ref[...], lnb_ref[...])
    x1_ref[...] = x1
    u2 = x1 * (1.0 + sc_ref[...]) + sh_ref[...]
    u2_ref[...] = u2.astype(u2_ref.dtype)
    if with_router:
        u_hi = u2.astype(BF16)
        u_lo = (u2 - u_hi.astype(F32)).astype(BF16)
        p = jnp.dot(u_hi, rw_ref[...], preferred_element_type=F32)
        logits = (p[:, :LANES] + p[:, LANES:]
                  + jnp.dot(u_lo, rw_ref[:, :LANES], preferred_element_type=F32) + rb_ref[...])
        lane = lax.broadcasted_iota(jnp.int32, logits.shape, 1)
        lg = jnp.where(lane < N_EXPERTS, logits, -jnp.inf)
        m1 = jnp.max(lg, axis=-1, keepdims=True)
        i1 = jnp.min(jnp.where(lg == m1, lane, LANES), axis=-1, keepdims=True)
        lg2 = jnp.where(lane == i1, -jnp.inf, lg)
        m2 = jnp.max(lg2, axis=-1, keepdims=True)
        i2 = jnp.min(jnp.where(lg2 == m2, lane, LANES), axis=-1, keepdims=True)
        e2 = jnp.exp(m2 - m1)
        w1 = 1.0 / (1.0 + e2)
        w2 = e2 / (1.0 + e2)
        ids_ref[...] = jnp.where(lane == 0, i1, jnp.where(lane == 1, i2, 0))
        topw_ref[...] = jnp.where(lane == 0, w1, jnp.where(lane == 1, w2, 0.0))


def _out_proj(ya, yb, yc, yd, gnc, w_out, x2d, g1, ln_w, ln_b, sc2, sh2, seq, router=None):
    N, D = x2d.shape
    GW = GROUP_WIDTH
    tm = 512
    tpb = seq // tm
    row = lambda m: (m, 0)
    const = lambda m: (0, 0)
    perb = lambda m: (m // tpb, 0, 0)
    in_specs = [pl.BlockSpec((tm, GW), row), pl.BlockSpec((tm, GW), row),
                pl.BlockSpec((tm, GW), row), pl.BlockSpec((tm, GW), row),
                pl.BlockSpec((1, GW), const),
                pl.BlockSpec((D, D), const),
                pl.BlockSpec((tm, D), row),
                pl.BlockSpec((None, 1, D), perb),
                pl.BlockSpec((1, D), const), pl.BlockSpec((1, D), const),
                pl.BlockSpec((None, 1, D), perb), pl.BlockSpec((None, 1, D), perb)]
    args = [ya, yb, yc, yd, gnc.reshape(1, -1), w_out, x2d, g1, ln_w.reshape(1, -1),
            ln_b.reshape(1, -1), sc2, sh2]
    out_shape = [jax.ShapeDtypeStruct((N, D), F32),
                 jax.ShapeDtypeStruct((N, D), BF16 if router is None else F32)]
    out_specs = [pl.BlockSpec((tm, D), row), pl.BlockSpec((tm, D), row)]
    if router is not None:
        rw, rb = router
        in_specs += [pl.BlockSpec((D, 2 * LANES), const), pl.BlockSpec((1, LANES), const)]
        args += [rw, rb]
        out_shape += [jax.ShapeDtypeStruct((N, LANES), jnp.int32), jax.ShapeDtypeStruct((N, LANES), F32)]
        out_specs += [pl.BlockSpec((tm, LANES), row), pl.BlockSpec((tm, LANES), row)]
    return pl.pallas_call(
        functools.partial(_outproj_kernel, router is not None),
        out_shape=tuple(out_shape),
        grid=(N // tm,),
        in_specs=in_specs,
        out_specs=tuple(out_specs),
        compiler_params=_params(("parallel",)),
        name="out_proj_ln",
    )(*args)


def _ffn_kernel(u_ref, wg_ref, wu_ref, wd_ref, x_ref, g2_ref, lnw_ref, lnb_ref, out_ref, acc_ref):
    f = pl.program_id(1)

    @pl.when(f == 0)
    def _():
        acc_ref[...] = jnp.zeros_like(acc_ref)

    u = u_ref[...]
    hg = jnp.dot(u, wg_ref[...], preferred_element_type=F32)
    hu = jnp.dot(u, wu_ref[...], preferred_element_type=F32)
    h = _silu(hg) * hu
    acc_ref[...] += jnp.dot(h.astype(BF16), wd_ref[...], preferred_element_type=F32)

    @pl.when(f == pl.num_programs(1) - 1)
    def _():
        r = ALPHA * x_ref[...] + (1.0 + g2_ref[...]) * acc_ref[...]
        out_ref[...] = _layer_norm_rows(r, lnw_ref[...], lnb_ref[...])


def _ffn(u2, wg, wu, wd, x1, g2, ln_w, ln_b, seq):
    N, D = u2.shape
    F = wg.shape[1]
    tm, tf = 512, 512
    tpb = seq // tm
    row = lambda m, f: (m, 0)
    const = lambda m, f: (0, 0)
    return pl.pallas_call(
        _ffn_kernel,
        out_shape=jax.ShapeDtypeStruct((N, D), F32),
        grid=(N // tm, F // tf),
        in_specs=[pl.BlockSpec((tm, D), row),
                  pl.BlockSpec((D, tf), lambda m, f: (0, f)),
                  pl.BlockSpec((D, tf), lambda m, f: (0, f)),
                  pl.BlockSpec((tf, D), lambda m, f: (f, 0)),
                  pl.BlockSpec((tm, D), row),
                  pl.BlockSpec((None, 1, D), lambda m, f: (m // tpb, 0, 0)),
                  pl.BlockSpec((1, D), const), pl.BlockSpec((1, D), const)],
        out_specs=pl.BlockSpec((tm, D), row),
        scratch_shapes=[pltpu.VMEM((tm, D), F32)],
        compiler_params=_params(("parallel", "arbitrary")),
        name="ffn_ln",
    )(u2, wg, wu, wd, x1, g2, ln_w.reshape(1, -1), ln_b.reshape(1, -1))


MOE_TM = 1024
MOE_SUB = 512


def _row_copy(src, src_row, dst, dst_row, sem):
    return pltpu.make_async_copy(src.at[pl.ds(src_row, 1), :], dst.at[pl.ds(dst_row, 1), :], sem)


def _start_rows(n_rows, make_copy):
    def body(i, carry):
        base = pl.multiple_of(i * 8, 8)
        for j in range(8):
            make_copy(base, j).start(priority=j % 2)
        return carry
    lax.fori_loop(0, n_rows // 8, body, 0)


def _wait_rows(n_rows, one_copy):
    def body(i, carry):
        one_copy.wait()
        return carry
    lax.fori_loop(0, n_rows, body, 0, unroll=8)


def _dispatch_kernel(pos_ref, u_ref, zeros_ref, out_ref, sem):
    del zeros_ref
    tm = u_ref.shape[0]
    first = pl.program_id(0) * tm
    for k in range(2):
        _start_rows(tm, lambda base, j: _row_copy(u_ref, base + j, out_ref,
                                                  pos_ref[2 * (first + base + j) + k], sem))
    _wait_rows(2 * tm, _row_copy(u_ref, 0, out_ref, 0, sem))


def _dispatch(u2, pos, n_rows):
    N, D = u2.shape
    tm = 256
    return pl.pallas_call(
        _dispatch_kernel,
        out_shape=jax.ShapeDtypeStruct((n_rows, D), u2.dtype),
        grid_spec=pltpu.PrefetchScalarGridSpec(
            num_scalar_prefetch=1,
            grid=(N // tm,),
            in_specs=[pl.BlockSpec((tm, D), lambda m, pos: (m, 0)),
                      pl.BlockSpec(memory_space=pl.ANY)],
            out_specs=pl.BlockSpec(memory_space=pl.ANY),
            scratch_shapes=[pltpu.SemaphoreType.DMA(())]),
        input_output_aliases={2: 0},
        compiler_params=_params(("arbitrary",)),
        name="moe_dispatch",
    )(pos, u2, jnp.zeros((n_rows, D), u2.dtype))


def _moe_ffn_kernel(te_ref, nv_ref, ns_ref, u_ref, wg_ref, wu_ref, wd_ref, out_ref,
                    ub_ref, wgb_ref, wub_ref, wdb_ref):
    t = pl.program_id(0)
    f = pl.program_id(1)

    @pl.when(t < nv_ref[0])
    def _():
        wgb_ref[...] = wg_ref[...].astype(BF16)
        wub_ref[...] = wu_ref[...].astype(BF16)
        wdb_ref[...] = wd_ref[...].astype(BF16)
        for s in range(MOE_TM // MOE_SUB):
            rows = slice(s * MOE_SUB, (s + 1) * MOE_SUB)

            @pl.when(s < ns_ref[t])
            def _():
                @pl.when(f == 0)
                def _():
                    ub_ref[rows, :] = u_ref[rows, :].astype(BF16)

                u = ub_ref[rows, :]
                hg = jnp.dot(u, wgb_ref[...], preferred_element_type=F32)
                hu = jnp.dot(u, wub_ref[...], preferred_element_type=F32)
                h = (_silu(hg) * hu).astype(BF16)
                y = jnp.dot(h, wdb_ref[...], preferred_element_type=F32)

                @pl.when(f == 0)
                def _():
                    out_ref[rows, :] = y

                @pl.when(f > 0)
                def _():
                    out_ref[rows, :] += y

            @pl.when((s >= ns_ref[t]) & (f == 0))
            def _():
                out_ref[rows, :] = jnp.zeros((MOE_SUB, out_ref.shape[1]), F32)

    @pl.when((t >= nv_ref[0]) & (f == 0))
    def _():
        out_ref[...] = jnp.zeros_like(out_ref)


def _moe_ffn(u_sorted, tile_expert, n_valid, n_sub, wg, wu, wd):
    R, D = u_sorted.shape
    F = wg.shape[2]
    tm, tf = MOE_TM, 256
    nf = F // tf
    tile = lambda t, nv: jnp.minimum(t, nv[0] - 1)
    fidx = lambda t, f, nv: jnp.where(t < nv[0], f, nf - 1)
    return pl.pallas_call(
        _moe_ffn_kernel,
        out_shape=jax.ShapeDtypeStruct((R, D), F32),
        grid_spec=pltpu.PrefetchScalarGridSpec(
            num_scalar_prefetch=3,
            grid=(R // tm, nf),
            in_specs=[pl.BlockSpec((tm, D), lambda t, f, te, nv, ns: (tile(t, nv), 0)),
                      pl.BlockSpec((None, D, tf), lambda t, f, te, nv, ns: (te[t], 0, fidx(t, f, nv))),
                      pl.BlockSpec((None, D, tf), lambda t, f, te, nv, ns: (te[t], 0, fidx(t, f, nv))),
                      pl.BlockSpec((None, tf, D), lambda t, f, te, nv, ns: (te[t], fidx(t, f, nv), 0))],
            out_specs=pl.BlockSpec((tm, D), lambda t, f, te, nv, ns: (t, 0)),
            scratch_shapes=[pltpu.VMEM((tm, D), BF16), pltpu.VMEM((D, tf), BF16),
                            pltpu.VMEM((D, tf), BF16), pltpu.VMEM((tf, D), BF16)]),
        compiler_params=_params(("arbitrary", "arbitrary")),
        name="moe_ffn",
    )(tile_expert, n_valid, n_sub, u_sorted, wg, wu, wd)


def _combine_kernel(pos_ref, y_ref, topw_ref, x_ref, g2_ref, lnw_ref, lnb_ref, out_ref, ybuf, sem):
    tm = x_ref.shape[0]
    m = pl.program_id(0)
    slot = m % 2

    def gather(step, buf):
        for k in range(2):
            _start_rows(tm, lambda base, j: _row_copy(y_ref, pos_ref[2 * (step * tm + base + j) + k],
                                                      ybuf.at[buf, k], base + j, sem.at[buf]))

    @pl.when(m == 0)
    def _():
        gather(0, 0)

    @pl.when(m + 1 < pl.num_programs(0))
    def _():
        gather(m + 1, 1 - slot)

    _wait_rows(2 * tm, _row_copy(y_ref, 0, ybuf.at[slot, 0], 0, sem.at[slot]))
    tw = topw_ref[...]
    y = tw[:, 0:1] * ybuf[slot, 0] + tw[:, 1:2] * ybuf[slot, 1]
    r = ALPHA * x_ref[...] + (1.0 + g2_ref[...]) * y
    out_ref[...] = _layer_norm_rows(r, lnw_ref[...], lnb_ref[...])


def _combine(y_sorted, pos, topw, x1, g2, ln_w, ln_b, seq):
    N, D = x1.shape
    tm = 256
    tpb = seq // tm
    row = lambda m, pos: (m, 0)
    const = lambda m, pos: (0, 0)
    return pl.pallas_call(
        _combine_kernel,
        out_shape=jax.ShapeDtypeStruct((N, D), F32),
        grid_spec=pltpu.PrefetchScalarGridSpec(
            num_scalar_prefetch=1,
            grid=(N // tm,),
            in_specs=[pl.BlockSpec(memory_space=pl.ANY),
                      pl.BlockSpec((tm, LANES), row),
                      pl.BlockSpec((tm, D), row),
                      pl.BlockSpec((None, 1, D), lambda m, pos: (m // tpb, 0, 0)),
                      pl.BlockSpec((1, D), const), pl.BlockSpec((1, D), const)],
            out_specs=pl.BlockSpec((tm, D), row),
            scratch_shapes=[pltpu.VMEM((2, 2, tm, D), F32), pltpu.SemaphoreType.DMA((2,))]),
        compiler_params=_params(("arbitrary",)),
        name="moe_combine_ln",
    )(pos, y_sorted, topw, x1, g2, ln_w.reshape(1, -1), ln_b.reshape(1, -1))


def _routing_plan(ids, n_tiles):
    e_flat = ids[:, :2].reshape(-1)
    onehot = (e_flat[:, None] == jnp.arange(N_EXPERTS, dtype=jnp.int32)[None, :]).astype(jnp.int32)
    csum = jnp.cumsum(onehot, axis=0)
    rank = jnp.sum((csum - onehot) * onehot, axis=1)
    counts = csum[-1]
    padded = ((counts + MOE_TM - 1) // MOE_TM) * MOE_TM
    ends = jnp.cumsum(padded)
    offs = ends - padded
    pos = (jnp.sum(onehot * offs[None, :], axis=1) + rank).astype(jnp.int32)
    tile_start = jnp.arange(n_tiles, dtype=jnp.int32) * MOE_TM
    n_valid = (ends[-1] // MOE_TM).astype(jnp.int32)
    tile_start = jnp.minimum(tile_start, (n_valid - 1) * MOE_TM)
    tile_expert = jnp.sum((tile_start[:, None] >= ends[None, :]).astype(jnp.int32), axis=1).astype(jnp.int32)
    used = jnp.clip((offs + counts)[tile_expert] - tile_start, 0, MOE_TM)
    n_sub = ((used + MOE_SUB - 1) // MOE_SUB).astype(jnp.int32)
    return pos, tile_expert, n_valid.reshape(1), n_sub


def _moe(u2, ids, topw, wg, wu, wd, x1, g2, ln_w, ln_b, seq):
    N = u2.shape[0]
    n_tiles = 2 * N // MOE_TM + N_EXPERTS
    pos, tile_expert, n_valid, n_sub = _routing_plan(ids, n_tiles)
    u_sorted = _dispatch(u2, pos, n_tiles * MOE_TM)
    y_sorted = _moe_ffn(u_sorted, tile_expert, n_valid, n_sub, wg, wu, wd)
    return _combine(y_sorted, pos, topw, x1, g2, ln_w, ln_b, seq)


def _split_w_in(w):
    GW = GROUP_WIDTH
    g0 = 4 * GW
    g1 = g0 + 2 * N_HEADS
    main = jnp.concatenate([w[:, :g0], w[:, g1:]], axis=1).astype(BF16)
    gate = jnp.zeros((w.shape[0], LANES), BF16).at[:, :2 * N_HEADS].set(w[:, g0:g1].astype(BF16))
    return main, gate


def kernel(x, c, w_in, mlstm_conv_w, mlstm_conv_b, mlstm_gate_b, mlstm_norm_w, pool_w, pool_scale, conv_dw_w, conv_dw_b, conv_ln_w, conv_ln_b, conv_pw_w, conv_pw_b, group_norm_w, w_out, ada_w, ada_b, ln1_w, ln1_b, ln2_w, ln2_b, ffn_w_gate, ffn_w_up, ffn_w_down, moe_router_w, moe_router_b, moe_w_gate, moe_w_up, moe_w_down):
    B, S, D = x.shape
    GW = GROUP_WIDTH
    ada = _ada_all(c, ada_w, ada_b)
    x2d = x.reshape(B * S, D)
    for l in range(DEPTH):
        mod = [ada[l, :, i * D:(i + 1) * D].reshape(B, 1, D) for i in range(6)]
        sh1, sc1, g1, sh2, sc2, g2 = mod
        w_main, w_gate = _split_w_in(w_in[l])
        proj, gates = _in_proj(x2d, sc1, sh1, w_main, w_gate, S)
        proj = proj.reshape(B, S, -1)
        gates = gates.reshape(B, S, LANES)
        gn_b, gn_c, gn_d = (group_norm_w[l, i * GW:(i + 1) * GW] for i in range(3))
        ya = _mlstm_mixer(proj, gates, mlstm_conv_w[l], mlstm_conv_b[l], mlstm_gate_b[l],
                          mlstm_norm_w[l], B, S)
        yb = _pool_mixer(proj, pool_w[l], pool_scale[l], gn_b, B, S)
        yc = _sb_mixer(proj, B, S)
        yd = _conv_mixer(proj, conv_dw_w[l], conv_dw_b[l], conv_ln_w[l], conv_ln_b[l],
                         conv_pw_w[l], conv_pw_b[l], gn_d, B, S)
        flat = lambda t: t.reshape(B * S, GW)
        j = l // 2
        router = None
        if l % 2 == 1:
            rw = jnp.zeros((D, LANES), F32).at[:, :N_EXPERTS].set(moe_router_w[j])
            rw_hi = rw.astype(BF16)
            rw_lo = (rw - rw_hi.astype(F32)).astype(BF16)
            rb = jnp.zeros((1, LANES), F32).at[0, :N_EXPERTS].set(moe_router_b[j])
            router = (jnp.concatenate([rw_hi, rw_lo], axis=1), rb)
        outs = _out_proj(flat(ya), flat(yb), flat(yc), flat(yd), gn_c, w_out[l].astype(BF16), x2d,
                         g1, ln1_w[l], ln1_b[l], sc2, sh2, S, router)
        if l % 2 == 0:
            x1, u2 = outs
            x2d = _ffn(u2, ffn_w_gate[j].astype(BF16), ffn_w_up[j].astype(BF16),
                       ffn_w_down[j].astype(BF16), x1, g2, ln2_w[l], ln2_b[l], S)
        else:
            x1, u2, ids, topw = outs
            x2d = _moe(u2, ids, topw, moe_w_gate[j], moe_w_up[j], moe_w_down[j], x1, g2,
                       ln2_w[l], ln2_b[l], S)
    return x2d.reshape(B, S, D)
```

```python
import functools

import jax
import jax.numpy as jnp
from jax import lax
from jax.experimental import pallas as pl
from jax.experimental.pallas import tpu as pltpu

F32 = jnp.float32
BF16 = jnp.bfloat16

DEPTH = 2
CHUNK = 256
GROUP_WIDTH = 512
N_HEADS = 4
HEAD_DIM = 128
MLSTM_CONV = 4
POOL_WINDOWS = (2, 4, 8, 16)
CONV_WIDTH = 31
SB_KEY_BLOCK = 256
N_EXPERTS = 8
ALPHA = (2.0 * DEPTH) ** 0.25
EPS = 1e-5
LOG2E = 1.4426950408889634
LANES = 128
VMEM_LIMIT = 56 * 1024 * 1024


def _params(sem, vmem=VMEM_LIMIT):
    return pltpu.CompilerParams(dimension_semantics=sem, vmem_limit_bytes=vmem)


def _silu(x):
    return x * (1.0 / (1.0 + jnp.exp(-x)))


def _sigmoid(x):
    return 1.0 / (1.0 + jnp.exp(-x))


def _neg_softplus(x):
    return -(jnp.maximum(x, 0.0) + jnp.log(1.0 + jnp.exp(-jnp.abs(x))))


def _layer_norm_rows(r, w, b):
    mu = jnp.mean(r, axis=-1, keepdims=True)
    d = r - mu
    var = jnp.mean(d * d, axis=-1, keepdims=True)
    return d * lax.rsqrt(var + EPS) * w + b


def _rms_rows(y, w):
    return y * lax.rsqrt(jnp.mean(y * y, axis=-1, keepdims=True) + EPS) * w


def _bdot(a, b):
    return jnp.dot(a.astype(BF16), b.astype(BF16), preferred_element_type=F32)


def _ada_kernel(c_ref, w_ref, b_ref, out_ref):
    out_ref[...] = _bdot(_silu(c_ref[...]), w_ref[...]) + b_ref[...]


def _ada_all(c, ada_w, ada_b):
    L, D, D6 = ada_w.shape
    Bn = c.shape[0]
    tn = 1024
    return pl.pallas_call(
        _ada_kernel,
        out_shape=jax.ShapeDtypeStruct((L, Bn, D6), F32),
        grid=(L, D6 // tn),
        in_specs=[pl.BlockSpec((Bn, D), lambda l, n: (0, 0)),
                  pl.BlockSpec((None, D, tn), lambda l, n: (l, 0, n)),
                  pl.BlockSpec((None, 1, tn), lambda l, n: (l, 0, n))],
        out_specs=pl.BlockSpec((None, Bn, tn), lambda l, n: (l, 0, n)),
        compiler_params=_params(("parallel", "parallel")),
        name="ada_mod",
    )(c, ada_w, ada_b.reshape(L, 1, D6))


def _inproj_kernel(x_ref, sc_ref, sh_ref, w_ref, wg_ref, proj_ref, gates_ref, u_ref):
    @pl.when(pl.program_id(1) == 0)
    def _():
        u = (x_ref[...] * (1.0 + sc_ref[...]) + sh_ref[...]).astype(BF16)
        u_ref[...] = u
        gates_ref[...] = jnp.dot(u, wg_ref[...], preferred_element_type=F32)

    proj_ref[...] = jnp.dot(u_ref[...], w_ref[...], preferred_element_type=F32)


def _in_proj(x2d, sc, sh, w_main, w_gate, seq):
    N, D = x2d.shape
    NC = w_main.shape[1]
    tm, tn = 1024, 1280
    tpb = seq // tm
    return pl.pallas_call(
        _inproj_kernel,
        out_shape=(jax.ShapeDtypeStruct((N, NC), F32), jax.ShapeDtypeStruct((N, LANES), F32)),
        grid=(N // tm, NC // tn),
        in_specs=[pl.BlockSpec((tm, D), lambda m, n: (m, 0)),
                  pl.BlockSpec((None, 1, D), lambda m, n: (m // tpb, 0, 0)),
                  pl.BlockSpec((None, 1, D), lambda m, n: (m // tpb, 0, 0)),
                  pl.BlockSpec((D, tn), lambda m, n: (0, n)),
                  pl.BlockSpec((D, LANES), lambda m, n: (0, 0))],
        out_specs=(pl.BlockSpec((tm, tn), lambda m, n: (m, n)),
                   pl.BlockSpec((tm, LANES), lambda m, n: (m, 0))),
        scratch_shapes=[pltpu.VMEM((tm, D), BF16)],
        compiler_params=_params(("parallel", "arbitrary")),
        name="in_proj",
    )(x2d, sc, sh, w_main, w_gate)


def _cumsum_rows(x):
    n = x.shape[0]
    row = lax.broadcasted_iota(jnp.int32, x.shape, 0)
    k = 1
    while k < n:
        x = x + jnp.where(row >= k, pltpu.roll(x, k, axis=0), 0.0)
        k *= 2
    return x


def _mlstm_kernel(qk_ref, v_ref, o_ref, g_ref, cw_ref, cb_ref, gb_ref, nw_ref, out_ref,
                  c_state, n_state, m_state, tail_ref):
    T = qk_ref.shape[0]
    GW = GROUP_WIDTH

    @pl.when(pl.program_id(1) == 0)
    def _():
        c_state[...] = jnp.zeros_like(c_state)
        n_state[...] = jnp.zeros_like(n_state)
        m_state[...] = jnp.zeros_like(m_state)
        tail_ref[...] = jnp.zeros_like(tail_ref)

    xx = jnp.concatenate([tail_ref[...], qk_ref[...]], axis=0)
    tail_ref[...] = qk_ref[T - 8:T, :]
    conv = cb_ref[...]
    for j in range(MLSTM_CONV):
        off = 8 - (MLSTM_CONV - 1) + j
        conv = conv + cw_ref[j:j + 1, :] * xx[off:off + T, :]
    qk = _silu(conv)

    gates = g_ref[...] + gb_ref[...]
    logf = _neg_softplus(-gates)
    causal = (lax.broadcasted_iota(jnp.int32, (CHUNK, CHUNK), 1)
              <= lax.broadcasted_iota(jnp.int32, (CHUNK, CHUNK), 0))

    for c in range(T // CHUNK):
        rows = slice(c * CHUNK, (c + 1) * CHUNK)
        gi = gates[rows, :]
        bcum = _cumsum_rows(logf[rows, :])
        gi_t = gi.T
        bcum_t = bcum.T
        for h in range(N_HEADS):
            cols = slice(h * HEAD_DIM, (h + 1) * HEAD_DIM)
            q = qk[rows, cols] * (HEAD_DIM ** -0.5)
            k = qk[rows, GW + h * HEAD_DIM:GW + (h + 1) * HEAD_DIM]
            v = v_ref[rows, cols]
            b_col = bcum[:, N_HEADS + h:N_HEADS + h + 1]
            b_row = bcum_t[N_HEADS + h:N_HEADS + h + 1, :]
            i_col = gi[:, h:h + 1]
            i_row = gi_t[h:h + 1, :]
            g_tot = b_col[CHUNK - 1:CHUNK, :]
            m_prev = m_state[h:h + 1, 0:1]
            c_prev = c_state[h]
            n_prev = n_state[h:h + 1, :]

            log_d = jnp.where(causal, b_col - b_row + i_row, -jnp.inf)
            m_inter = b_col + m_prev
            m_t = jnp.maximum(m_inter, jnp.max(log_d, axis=-1, keepdims=True))
            s = lax.dot_general(q.astype(BF16), k.astype(BF16), (((1,), (1,)), ((), ())),
                                preferred_element_type=F32)
            w = jnp.exp(log_d - m_t) * s
            inter = jnp.exp(m_inter - m_t)
            num = inter * _bdot(q, c_prev) + _bdot(w, v)
            den = inter * jnp.sum(q * n_prev, axis=-1, keepdims=True) + jnp.sum(w, axis=-1, keepdims=True)
            hcur = num / jnp.maximum(jnp.abs(den), jnp.exp(-m_t))

            a_col = g_tot - b_col + i_col
            m_new = jnp.maximum(g_tot + m_prev, jnp.max(a_col, axis=0, keepdims=True))
            decay = jnp.exp(g_tot + m_prev - m_new)
            kw = k * jnp.exp(a_col - m_new)
            c_state[h] = decay * c_prev + _bdot(kw.T, v)
            n_state[h:h + 1, :] = decay * n_prev + jnp.sum(kw, axis=0, keepdims=True)
            m_state[h:h + 1, :] = jnp.broadcast_to(m_new, (1, LANES))

            mu = jnp.mean(hcur, axis=-1, keepdims=True)
            dlt = hcur - mu
            var = jnp.mean(dlt * dlt, axis=-1, keepdims=True)
            hn = dlt * lax.rsqrt(var + EPS) * nw_ref[:, cols]
            out_ref[rows, cols] = (_sigmoid(o_ref[rows, cols]) * hn).astype(out_ref.dtype)


def _mlstm_mixer(proj, gates, conv_w, conv_b, gate_b, norm_w, B, S):
    T = 256
    GW = GROUP_WIDTH
    gb = jnp.zeros((1, LANES), F32).at[0, :2 * N_HEADS].set(gate_b)
    return pl.pallas_call(
        _mlstm_kernel,
        out_shape=jax.ShapeDtypeStruct((B, S, GW), BF16),
        grid=(B, S // T),
        in_specs=[pl.BlockSpec((None, T, 2 * GW), lambda b, t: (b, t, 0)),
                  pl.BlockSpec((None, T, GW), lambda b, t: (b, t, 2)),
                  pl.BlockSpec((None, T, GW), lambda b, t: (b, t, 3)),
                  pl.BlockSpec((None, T, LANES), lambda b, t: (b, t, 0)),
                  pl.BlockSpec((MLSTM_CONV, 2 * GW), lambda b, t: (0, 0)),
                  pl.BlockSpec((1, 2 * GW), lambda b, t: (0, 0)),
                  pl.BlockSpec((1, LANES), lambda b, t: (0, 0)),
                  pl.BlockSpec((1, GW), lambda b, t: (0, 0))],
        out_specs=pl.BlockSpec((None, T, GW), lambda b, t: (b, t, 0)),
        scratch_shapes=[pltpu.VMEM((N_HEADS, HEAD_DIM, HEAD_DIM), F32),
                        pltpu.VMEM((8, LANES), F32),
                        pltpu.VMEM((8, LANES), F32),
                        pltpu.VMEM((8, 2 * GW), F32)],
        compiler_params=_params(("parallel", "arbitrary")),
        name="mlstm_mixer",
    )(proj, proj, proj, gates, conv_w, conv_b.reshape(1, -1), gb, norm_w.reshape(1, -1))


def _pool_kernel(x_ref, w_ref, scale_ref, gn_ref, out_ref, tail_ref):
    T = x_ref.shape[0]
    HALO = 16
    t_blk = pl.program_id(1)

    @pl.when(t_blk == 0)
    def _():
        tail_ref[...] = jnp.zeros_like(tail_ref)

    x = x_ref[...]
    xx = jnp.concatenate([tail_ref[...], x], axis=0)
    tail_ref[...] = x[T - HALO:T, :]
    pos = t_blk * T + lax.broadcasted_iota(jnp.int32, (T, 1), 0) + 1
    outs = []
    for g, win in enumerate(POOL_WINDOWS):
        cols = slice(g * LANES, (g + 1) * LANES)
        s = xx[:, cols]
        k = 1
        while k < win:
            s = s + pltpu.roll(s, k, axis=0)
            k *= 2
        cnt = jnp.minimum(pos, win).astype(F32)
        yg = s[HALO:, :] / cnt - x[:, cols]
        outs.append(_bdot(yg, w_ref[g]))
    y = jnp.concatenate(outs, axis=-1) * scale_ref[...]
    out_ref[...] = _rms_rows(y, gn_ref[...]).astype(out_ref.dtype)


def _pool_mixer(proj, pool_w, pool_scale, gn, B, S):
    T = 512
    GW = GROUP_WIDTH
    return pl.pallas_call(
        _pool_kernel,
        out_shape=jax.ShapeDtypeStruct((B, S, GW), BF16),
        grid=(B, S // T),
        in_specs=[pl.BlockSpec((None, T, GW), lambda b, t: (b, t, 4)),
                  pl.BlockSpec((len(POOL_WINDOWS), LANES, LANES), lambda b, t: (0, 0, 0)),
                  pl.BlockSpec((1, GW), lambda b, t: (0, 0)),
                  pl.BlockSpec((1, GW), lambda b, t: (0, 0))],
        out_specs=pl.BlockSpec((None, T, GW), lambda b, t: (b, t, 0)),
        scratch_shapes=[pltpu.VMEM((16, GW), F32)],
        compiler_params=_params(("parallel", "arbitrary")),
        name="pool_mixer",
    )(proj, pool_w.astype(BF16), pool_scale.reshape(1, -1), gn.reshape(1, -1))


def _sb_kernel(q_ref, k_ref, v_ref, out_ref, kb_ref, vb_ref):
    LQ = q_ref.shape[0]
    LK = SB_KEY_BLOCK
    ratio = LQ // LK
    qi = pl.program_id(1)

    @pl.when(qi == 0)
    def _():
        kb_ref[...] = k_ref[...].astype(BF16)
        vb_ref[...] = v_ref[...].astype(BF16)

    scale2 = HEAD_DIM ** -0.5 * LOG2E
    tri = (lax.broadcasted_iota(jnp.int32, (LK, LK), 0)
           > lax.broadcasted_iota(jnp.int32, (LK, LK), 1)).astype(BF16)
    row = lax.broadcasted_iota(jnp.int32, (LQ, LK), 0)
    col = lax.broadcasted_iota(jnp.int32, (LQ, LK), 1)
    heads = [slice(h * HEAD_DIM, (h + 1) * HEAD_DIM) for h in range(N_HEADS)]
    qs = [(q_ref[:, hs] * scale2).astype(BF16) for hs in heads]

    def sweep(kb, carry, strict):
        start = pl.multiple_of(kb * LK, LK)
        zs, lks = [], []
        for h, hs in enumerate(heads):
            k_blk = kb_ref[pl.ds(start, LK), hs]
            z = lax.dot_general(qs[h], k_blk, (((1,), (1,)), ((), ())), preferred_element_type=F32)
            lk = -(jnp.maximum(z, 0.0) + jnp.log2(1.0 + jnp.exp2(-jnp.abs(z))))
            if strict is not None:
                lk = jnp.where(strict, lk, 0.0)
            zs.append(z)
            lks.append(lk)
        inner = jnp.dot(jnp.concatenate([lk.astype(BF16) for lk in lks], axis=0), tri,
                        preferred_element_type=F32)
        new = []
        for h, hs in enumerate(heads):
            after, acc = carry[h]
            v_blk = vb_ref[pl.ds(start, LK), hs]
            a = jnp.exp2(zs[h] + lks[h] + inner[h * LQ:(h + 1) * LQ, :] + after)
            if strict is not None:
                a = jnp.where(strict, a, 0.0)
            acc = acc + jnp.dot(a.astype(BF16), v_blk, preferred_element_type=F32)
            after = after + jnp.sum(lks[h], axis=-1, keepdims=True)
            new.append((after, acc))
        return tuple(new)

    init = tuple((jnp.zeros((LQ, 1), F32), jnp.zeros((LQ, HEAD_DIM), F32)) for _ in heads)
    carry = init
    for j in reversed(range(ratio)):
        carry = sweep(ratio * qi + j, carry, (j * LK + col) < row)
    carry = lax.fori_loop(0, ratio * qi, lambda it, c: sweep(ratio * qi - 1 - it, c, None), carry)
    for h, hs in enumerate(heads):
        out_ref[:, hs] = carry[h][1]


def _sb_mixer(proj, B, S):
    LQ = 512
    GW = GROUP_WIDTH
    return pl.pallas_call(
        _sb_kernel,
        out_shape=jax.ShapeDtypeStruct((B, S, GW), F32),
        grid=(B, S // LQ),
        in_specs=[pl.BlockSpec((None, LQ, GW), lambda b, i: (b, i, 5)),
                  pl.BlockSpec((None, S, GW), lambda b, i: (b, 0, 6)),
                  pl.BlockSpec((None, S, GW), lambda b, i: (b, 0, 7))],
        out_specs=pl.BlockSpec((None, LQ, GW), lambda b, i: (b, i, 0)),
        scratch_shapes=[pltpu.VMEM((S, GW), BF16), pltpu.VMEM((S, GW), BF16)],
        compiler_params=_params(("parallel", "arbitrary")),
        name="sb_attention",
    )(proj, proj, proj)


def _conv_kernel(x_ref, dw_ref, dwb_ref, lnw_ref, lnb_ref, pw_ref, pwb_ref, gn_ref, out_ref,
                 hbuf, ybuf):
    T = x_ref.shape[0]
    GW = GROUP_WIDTH
    HALO = 32
    SUB = 64

    @pl.when(pl.program_id(1) == 0)
    def _():
        hbuf[0:HALO, :] = jnp.zeros((HALO, GW), F32)

    @pl.when(pl.program_id(1) > 0)
    def _():
        hbuf[0:HALO, :] = hbuf[T:T + HALO, :]

    hbuf[HALO:HALO + T, :] = x_ref[:, 0:GW] * _sigmoid(x_ref[:, GW:2 * GW])

    def sub(i, carry):
        r0 = pl.multiple_of(i * SUB, SUB)
        acc = jnp.broadcast_to(dwb_ref[...], (SUB, GW))
        win = hbuf[pl.ds(r0, SUB + HALO), :]
        for res in range(8):
            shifted = win if res == 0 else pltpu.roll(win, SUB + HALO - res, axis=0)
            for j in range(CONV_WIDTH):
                off = HALO - CONV_WIDTH + 1 + j
                if off % 8 == res:
                    base = off - res
                    acc = acc + dw_ref[j:j + 1, :] * shifted[base:base + SUB, :]
        ybuf[pl.ds(r0, SUB), :] = acc
        return carry

    lax.fori_loop(0, T // SUB, sub, 0)
    hn = _silu(_layer_norm_rows(ybuf[...], lnw_ref[...], lnb_ref[...]))
    y = _bdot(hn, pw_ref[...]) + pwb_ref[...]
    out_ref[...] = _rms_rows(y, gn_ref[...]).astype(out_ref.dtype)


def _conv_mixer(proj, dw_w, dw_b, ln_w, ln_b, pw_w, pw_b, gn, B, S):
    T = 256
    GW = GROUP_WIDTH
    dw_pad = jnp.zeros((32, GW), F32).at[:CONV_WIDTH].set(dw_w)
    r = lambda a: a.reshape(1, -1)
    return pl.pallas_call(
        _conv_kernel,
        out_shape=jax.ShapeDtypeStruct((B, S, GW), BF16),
        grid=(B, S // T),
        in_specs=[pl.BlockSpec((None, T, 2 * GW), lambda b, t: (b, t, 4)),
                  pl.BlockSpec((32, GW), lambda b, t: (0, 0)),
                  pl.BlockSpec((1, GW), lambda b, t: (0, 0)),
                  pl.BlockSpec((1, GW), lambda b, t: (0, 0)),
                  pl.BlockSpec((1, GW), lambda b, t: (0, 0)),
                  pl.BlockSpec((GW, GW), lambda b, t: (0, 0)),
                  pl.BlockSpec((1, GW), lambda b, t: (0, 0)),
                  pl.BlockSpec((1, GW), lambda b, t: (0, 0))],
        out_specs=pl.BlockSpec((None, T, GW), lambda b, t: (b, t, 0)),
        scratch_shapes=[pltpu.VMEM((T + 32, GW), F32), pltpu.VMEM((T, GW), F32)],
        compiler_params=_params(("parallel", "arbitrary")),
        name="conv_mixer",
    )(proj, dw_pad, r(dw_b), r(ln_w), r(ln_b), pw_w.astype(BF16), r(pw_b), r(gn))


def _outproj_kernel(with_router, ya_ref, yb_ref, yc_ref, yd_ref, gnc_ref, w_ref, x_ref, g1_ref,
                    lnw_ref, lnb_ref, sc_ref, sh_ref, *rest):
    GW = GROUP_WIDTH
    if with_router:
        rw_ref, rb_ref, x1_ref, u2_ref, ids_ref, topw_ref = rest
    else:
        x1_ref, u2_ref = rest
    yc = _rms_rows(yc_ref[...], gnc_ref[...]).astype(BF16)
    acc = jnp.dot(ya_ref[...], w_ref[0:GW, :], preferred_element_type=F32)
    acc = acc + jnp.dot(yb_ref[...], w_ref[GW:2 * GW, :], preferred_element_type=F32)
    acc = acc + jnp.dot(yc, w_ref[2 * GW:3 * GW, :], preferred_element_type=F32)
    acc = acc + jnp.dot(yd_ref[...], w_ref[3 * GW:4 * GW, :], preferred_element_type=F32)
    r = ALPHA * x_ref[...] + (1.0 + g1_ref[...]) * acc
    x1 = _layer_norm_rows(r, lnw_ref[...], lnb_ref[...])
    x1_ref[...] = x1
    u2 = x1 * (1.0 + sc_ref[...]) + sh_ref[...]
    u2_ref[...] = u2.astype(u2_ref.dtype)
    if with_router:
        u_hi = u2.astype(BF16)
        u_lo = (u2 - u_hi.astype(F32)).astype(BF16)
        p = jnp.dot(u_hi, rw_ref[...], preferred_element_type=F32)
        logits = (p[:, :LANES] + p[:, LANES:]
                  + jnp.dot(u_lo, rw_ref[:, :LANES], preferred_element_type=F32) + rb_ref[...])
        lane = lax.broadcasted_iota(jnp.int32, logits.shape, 1)
        lg = jnp.where(lane < N_EXPERTS, logits, -jnp.inf)
        m1 = jnp.max(lg, axis=-1, keepdims=True)
        i1 = jnp.min(jnp.where(lg == m1, lane, LANES), axis=-1, keepdims=True)
        lg2 = jnp.where(lane == i1, -jnp.inf, lg)
        m2 = jnp.max(lg2, axis=-1, keepdims=True)
        i2 = jnp.min(jnp.where(lg2 == m2, lane, LANES), axis=-1, keepdims=True)
        e2 = jnp.exp(m2 - m1)
        w1 = 1.0 / (1.0 + e2)
        w2 = e2 / (1.0 + e2)
        ids_ref[...] = jnp.where(lane == 0, i1, jnp.where(lane == 1, i2, 0))
        topw_ref[...] = jnp.where(lane == 0, w1, jnp.where(lane == 1, w2, 0.0))


def _out_proj(ya, yb, yc, yd, gnc, w_out, x2d, g1, ln_w, ln_b, sc2, sh2, seq, router=None):
    N, D = x2d.shape
    GW = GROUP_WIDTH
    tm = 512
    tpb = seq // tm
    row = lambda m: (m, 0)
    const = lambda m: (0, 0)
    perb = lambda m: (m // tpb, 0, 0)
    in_specs = [pl.BlockSpec((tm, GW), row), pl.BlockSpec((tm, GW), row),
                pl.BlockSpec((tm, GW), row), pl.BlockSpec((tm, GW), row),
                pl.BlockSpec((1, GW), const),
                pl.BlockSpec((D, D), const),
                pl.BlockSpec((tm, D), row),
                pl.BlockSpec((None, 1, D), perb),
                pl.BlockSpec((1, D), const), pl.BlockSpec((1, D), const),
                pl.BlockSpec((None, 1, D), perb), pl.BlockSpec((None, 1, D), perb)]
    args = [ya, yb, yc, yd, gnc.reshape(1, -1), w_out, x2d, g1, ln_w.reshape(1, -1),
            ln_b.reshape(1, -1), sc2, sh2]
    out_shape = [jax.ShapeDtypeStruct((N, D), F32),
                 jax.ShapeDtypeStruct((N, D), BF16 if router is None else F32)]
    out_specs = [pl.BlockSpec((tm, D), row), pl.BlockSpec((tm, D), row)]
    if router is not None:
        rw, rb = router
        in_specs += [pl.BlockSpec((D, 2 * LANES), const), pl.BlockSpec((1, LANES), const)]
        args += [rw, rb]
        out_shape += [jax.ShapeDtypeStruct((N, LANES), jnp.int32), jax.ShapeDtypeStruct((N, LANES), F32)]
        out_specs += [pl.BlockSpec((tm, LANES), row), pl.BlockSpec((tm, LANES), row)]
    return pl.pallas_call(
        functools.partial(_outproj_kernel, router is not None),
        out_shape=tuple(out_shape),
        grid=(N // tm,),
        in_specs=in_specs,
        out_specs=tuple(out_specs),
        compiler_params=_params(("parallel",)),
        name="out_proj_ln",
    )(*args)


def _ffn_kernel(u_ref, wg_ref, wu_ref, wd_ref, x_ref, g2_ref, lnw_ref, lnb_ref, out_ref, acc_ref):
    f = pl.program_id(1)

    @pl.when(f == 0)
    def _():
        acc_ref[...] = jnp.zeros_like(acc_ref)

    u = u_ref[...]
    hg = jnp.dot(u, wg_ref[...], preferred_element_type=F32)
    hu = jnp.dot(u, wu_ref[...], preferred_element_type=F32)
    h = _silu(hg) * hu
    acc_ref[...] += jnp.dot(h.astype(BF16), wd_ref[...], preferred_element_type=F32)

    @pl.when(f == pl.num_programs(1) - 1)
    def _():
        r = ALPHA * x_ref[...] + (1.0 + g2_ref[...]) * acc_ref[...]
        out_ref[...] = _layer_norm_rows(r, lnw_ref[...], lnb_ref[...])


def _ffn(u2, wg, wu, wd, x1, g2, ln_w, ln_b, seq):
    N, D = u2.shape
    F = wg.shape[1]
    tm, tf = 512, 512
    tpb = seq // tm
    row = lambda m, f: (m, 0)
    const = lambda m, f: (0, 0)
    return pl.pallas_call(
        _ffn_kernel,
        out_shape=jax.ShapeDtypeStruct((N, D), F32),
        grid=(N // tm, F // tf),
        in_specs=[pl.BlockSpec((tm, D), row),
                  pl.BlockSpec((D, tf), lambda m, f: (0, f)),
                  pl.BlockSpec((D, tf), lambda m, f: (0, f)),
                  pl.BlockSpec((tf, D), lambda m, f: (f, 0)),
                  pl.BlockSpec((tm, D), row),
                  pl.BlockSpec((None, 1, D), lambda m, f: (m // tpb, 0, 0)),
                  pl.BlockSpec((1, D), const), pl.BlockSpec((1, D), const)],
        out_specs=pl.BlockSpec((tm, D), row),
        scratch_shapes=[pltpu.VMEM((tm, D), F32)],
        compiler_params=_params(("parallel", "arbitrary")),
        name="ffn_ln",
    )(u2, wg, wu, wd, x1, g2, ln_w.reshape(1, -1), ln_b.reshape(1, -1))


MOE_TM = 512


def _row_copy(src, src_row, dst, dst_row, sem):
    return pltpu.make_async_copy(src.at[pl.ds(src_row, 1), :], dst.at[pl.ds(dst_row, 1), :], sem)


def _start_rows(n_rows, make_copy):
    def body(i, carry):
        base = pl.multiple_of(i * 8, 8)
        for j in range(8):
            make_copy(base, j).start(priority=j % 2)
        return carry
    lax.fori_loop(0, n_rows // 8, body, 0)


def _wait_rows(n_rows, one_copy):
    def body(i, carry):
        one_copy.wait()
        return carry
    lax.fori_loop(0, n_rows, body, 0, unroll=8)


def _dispatch_kernel(pos_ref, u_ref, zeros_ref, out_ref, sem):
    del zeros_ref
    tm = u_ref.shape[0]
    first = pl.program_id(0) * tm

    def start(r, carry):
        for k in range(2):
            _row_copy(u_ref, r, out_ref, pos_ref[2 * (first + r) + k], sem).start(priority=k)
        return carry

    lax.fori_loop(0, tm, start, 0, unroll=8)
    _wait_rows(2 * tm, _row_copy(u_ref, 0, out_ref, 0, sem))


def _dispatch(u2, pos, n_rows):
    N, D = u2.shape
    tm = 256
    return pl.pallas_call(
        _dispatch_kernel,
        out_shape=jax.ShapeDtypeStruct((n_rows, D), u2.dtype),
        grid_spec=pltpu.PrefetchScalarGridSpec(
            num_scalar_prefetch=1,
            grid=(N // tm,),
            in_specs=[pl.BlockSpec((tm, D), lambda m, pos: (m, 0)),
                      pl.BlockSpec(memory_space=pl.ANY)],
            out_specs=pl.BlockSpec(memory_space=pl.ANY),
            scratch_shapes=[pltpu.SemaphoreType.DMA(())]),
        input_output_aliases={2: 0},
        compiler_params=_params(("arbitrary",)),
        name="moe_dispatch",
    )(pos, u2, jnp.zeros((n_rows, D), u2.dtype))


def _moe_ffn_kernel(te_ref, nv_ref, u_ref, wg_ref, wu_ref, wd_ref, out_ref):
    t = pl.program_id(0)
    f = pl.program_id(1)
    tf = wg_ref.shape[1]
    half = (tf // LANES // 2) * LANES

    @pl.when(t < nv_ref[0])
    def _():
        u = u_ref[...].astype(BF16)
        y = None
        for c0, c1 in ((0, half), (half, tf)):
            hg = jnp.dot(u, wg_ref[:, c0:c1], preferred_element_type=F32)
            hu = jnp.dot(u, wu_ref[:, c0:c1], preferred_element_type=F32)
            h = (_silu(hg) * hu).astype(BF16)
            part = jnp.dot(h, wd_ref[c0:c1, :], preferred_element_type=F32)
            y = part if y is None else y + part

        @pl.when(f == 0)
        def _():
            out_ref[...] = y

        @pl.when(f > 0)
        def _():
            out_ref[...] += y

    @pl.when((t >= nv_ref[0]) & (f == 0))
    def _():
        out_ref[...] = jnp.zeros_like(out_ref)


def _moe_ffn(u_sorted, tile_expert, n_valid, wg, wu, wd):
    R, D = u_sorted.shape
    F = wg.shape[2]
    tm, tf = MOE_TM, F // 2
    nf = F // tf
    tile = lambda t, nv: jnp.minimum(t, nv[0] - 1)
    fidx = lambda t, f, nv: jnp.where(t < nv[0], f, nf - 1)
    return pl.pallas_call(
        _moe_ffn_kernel,
        out_shape=jax.ShapeDtypeStruct((R, D), F32),
        grid_spec=pltpu.PrefetchScalarGridSpec(
            num_scalar_prefetch=2,
            grid=(R // tm, nf),
            in_specs=[pl.BlockSpec((tm, D), lambda t, f, te, nv: (tile(t, nv), 0)),
                      pl.BlockSpec((None, D, tf), lambda t, f, te, nv: (te[t], 0, fidx(t, f, nv))),
                      pl.BlockSpec((None, D, tf), lambda t, f, te, nv: (te[t], 0, fidx(t, f, nv))),
                      pl.BlockSpec((None, tf, D), lambda t, f, te, nv: (te[t], fidx(t, f, nv), 0))],
            out_specs=pl.BlockSpec((tm, D), lambda t, f, te, nv: (t, 0))),
        compiler_params=_params(("arbitrary", "arbitrary"), vmem=60 * 1024 * 1024),
        name="moe_ffn",
    )(tile_expert, n_valid, u_sorted, wg, wu, wd)


def _combine_kernel(pos_ref, y_ref, topw_ref, x_ref, g2_ref, lnw_ref, lnb_ref, out_ref, ybuf, sem):
    tm = x_ref.shape[0]
    m = pl.program_id(0)
    slot = m % 2

    def gather(step, buf):
        for k in range(2):
            _start_rows(tm, lambda base, j: _row_copy(y_ref, pos_ref[2 * (step * tm + base + j) + k],
                                                      ybuf.at[buf, k], base + j, sem.at[buf]))

    @pl.when(m == 0)
    def _():
        gather(0, 0)

    @pl.when(m + 1 < pl.num_programs(0))
    def _():
        gather(m + 1, 1 - slot)

    _wait_rows(2 * tm, _row_copy(y_ref, 0, ybuf.at[slot, 0], 0, sem.at[slot]))
    tw = topw_ref[...]
    y = tw[:, 0:1] * ybuf[slot, 0] + tw[:, 1:2] * ybuf[slot, 1]
    r = ALPHA * x_ref[...] + (1.0 + g2_ref[...]) * y
    out_ref[...] = _layer_norm_rows(r, lnw_ref[...], lnb_ref[...])


def _combine(y_sorted, pos, topw, x1, g2, ln_w, ln_b, seq):
    N, D = x1.shape
    tm = 256
    tpb = seq // tm
    row = lambda m, pos: (m, 0)
    const = lambda m, pos: (0, 0)
    return pl.pallas_call(
        _combine_kernel,
        out_shape=jax.ShapeDtypeStruct((N, D), F32),
        grid_spec=pltpu.PrefetchScalarGridSpec(
            num_scalar_prefetch=1,
            grid=(N // tm,),
            in_specs=[pl.BlockSpec(memory_space=pl.ANY),
                      pl.BlockSpec((tm, LANES), row),
                      pl.BlockSpec((tm, D), row),
                      pl.BlockSpec((None, 1, D), lambda m, pos: (m // tpb, 0, 0)),
                      pl.BlockSpec((1, D), const), pl.BlockSpec((1, D), const)],
            out_specs=pl.BlockSpec((tm, D), row),
            scratch_shapes=[pltpu.VMEM((2, 2, tm, D), F32), pltpu.SemaphoreType.DMA((2,))]),
        compiler_params=_params(("arbitrary",)),
        name="moe_combine_ln",
    )(pos, y_sorted, topw, x1, g2, ln_w.reshape(1, -1), ln_b.reshape(1, -1))


def _routing_plan(ids, n_tiles):
    e_flat = ids[:, :2].reshape(-1)
    onehot = (e_flat[:, None] == jnp.arange(N_EXPERTS, dtype=jnp.int32)[None, :]).astype(jnp.int32)
    csum = jnp.cumsum(onehot, axis=0)
    rank = jnp.sum((csum - onehot) * onehot, axis=1)
    counts = csum[-1]
    padded = ((counts + MOE_TM - 1) // MOE_TM) * MOE_TM
    ends = jnp.cumsum(padded)
    offs = ends - padded
    pos = (jnp.sum(onehot * offs[None, :], axis=1) + rank).astype(jnp.int32)
    tile_start = jnp.arange(n_tiles, dtype=jnp.int32) * MOE_TM
    n_valid = (ends[-1] // MOE_TM).astype(jnp.int32)
    tile_start = jnp.minimum(tile_start, (n_valid - 1) * MOE_TM)
    tile_expert = jnp.sum((tile_start[:, None] >= ends[None, :]).astype(jnp.int32), axis=1).astype(jnp.int32)
    return pos, tile_expert, n_valid.reshape(1)


def _moe(u2, ids, topw, wg, wu, wd, x1, g2, ln_w, ln_b, seq):
    N = u2.shape[0]
    n_tiles = 2 * N // MOE_TM + N_EXPERTS
    pos, tile_expert, n_valid = _routing_plan(ids, n_tiles)
    u_sorted = _dispatch(u2, pos, n_tiles * MOE_TM)
    y_sorted = _moe_ffn(u_sorted, tile_expert, n_valid, wg, wu, wd)
    return _combine(y_sorted, pos, topw, x1, g2, ln_w, ln_b, seq)


def _split_w_in_kernel(w_ref, main_ref, gate_ref):
    g0 = 4 * GROUP_WIDTH
    g1 = g0 + 2 * N_HEADS
    w = w_ref[...]
    main_ref[:, :g0] = w[:, :g0].astype(BF16)
    main_ref[:, g0:] = w[:, g1:].astype(BF16)
    lane = lax.broadcasted_iota(jnp.int32, (w.shape[0], LANES), 1)
    gate_ref[...] = jnp.where(lane < 2 * N_HEADS, w[:, g0:g0 + LANES], 0.0).astype(BF16)


def _split_w_in(w_in):
    L, D, C = w_in.shape
    tr = 256
    return pl.pallas_call(
        _split_w_in_kernel,
        out_shape=(jax.ShapeDtypeStruct((L, D, C - 2 * N_HEADS), BF16),
                   jax.ShapeDtypeStruct((L, D, LANES), BF16)),
        grid=(L, D // tr),
        in_specs=[pl.BlockSpec((None, tr, C), lambda l, r: (l, r, 0))],
        out_specs=(pl.BlockSpec((None, tr, C - 2 * N_HEADS), lambda l, r: (l, r, 0)),
                   pl.BlockSpec((None, tr, LANES), lambda l, r: (l, r, 0))),
        compiler_params=_params(("parallel", "parallel")),
        name="split_w_in",
    )(w_in)


def kernel(x, c, w_in, mlstm_conv_w, mlstm_conv_b, mlstm_gate_b, mlstm_norm_w, pool_w, pool_scale, conv_dw_w, conv_dw_b, conv_ln_w, conv_ln_b, conv_pw_w, conv_pw_b, group_norm_w, w_out, ada_w, ada_b, ln1_w, ln1_b, ln2_w, ln2_b, ffn_w_gate, ffn_w_up, ffn_w_down, moe_router_w, moe_router_b, moe_w_gate, moe_w_up, moe_w_down):
    B, S, D = x.shape
    GW = GROUP_WIDTH
    ada = _ada_all(c, ada_w, ada_b)
    x2d = x.reshape(B * S, D)
    w_main, w_gate = _split_w_in(w_in)
    for l in range(DEPTH):
        mod = [ada[l, :, i * D:(i + 1) * D].reshape(B, 1, D) for i in range(6)]
        sh1, sc1, g1, sh2, sc2, g2 = mod
        proj, gates = _in_proj(x2d, sc1, sh1, w_main[l], w_gate[l], S)
        proj = proj.reshape(B, S, -1)
        gates = gates.reshape(B, S, LANES)
        gn_b, gn_c, gn_d = (group_norm_w[l, i * GW:(i + 1) * GW] for i in range(3))
        ya = _mlstm_mixer(proj, gates, mlstm_conv_w[l], mlstm_conv_b[l], mlstm_gate_b[l],
                          mlstm_norm_w[l], B, S)
        yb = _pool_mixer(proj, pool_w[l], pool_scale[l], gn_b, B, S)
        yc = _sb_mixer(proj, B, S)
        yd = _conv_mixer(proj, conv_dw_w[l], conv_dw_b[l], conv_ln_w[l], conv_ln_b[l],
                         conv_pw_w[l], conv_pw_b[l], gn_d, B, S)
        flat = lambda t: t.reshape(B * S, GW)
        j = l // 2
        router = None
        if l % 2 == 1:
            rw = jnp.zeros((D, LANES), F32).at[:, :N_EXPERTS].set(moe_router_w[j])
            rw_hi = rw.astype(BF16)
            rw_lo = (rw - rw_hi.astype(F32)).astype(BF16)
            rb = jnp.zeros((1, LANES), F32).at[0, :N_EXPERTS].set(moe_router_b[j])
            router = (jnp.concatenate([rw_hi, rw_lo], axis=1), rb)
        outs = _out_proj(flat(ya), flat(yb), flat(yc), flat(yd), gn_c, w_out[l].astype(BF16), x2d,
                         g1, ln1_w[l], ln1_b[l], sc2, sh2, S, router)
        if l % 2 == 0:
            x1, u2 = outs
            x2d = _ffn(u2, ffn_w_gate[j].astype(BF16), ffn_w_up[j].astype(BF16),
                       ffn_w_down[j].astype(BF16), x1, g2, ln2_w[l], ln2_b[l], S)
        else:
            x1, u2, ids, topw = outs
            x2d = _moe(u2, ids, topw, moe_w_gate[j].astype(BF16), moe_w_up[j].astype(BF16),
                       moe_w_down[j].astype(BF16), x1, g2, ln2_w[l], ln2_b[l], S)
    return x2d.reshape(B, S, D)
```

```python
import functools

import jax
import jax.numpy as jnp
from jax import lax
from jax.experimental import pallas as pl
from jax.experimental.pallas import tpu as pltpu

F32 = jnp.float32
BF16 = jnp.bfloat16

DEPTH = 2
CHUNK = 256
GROUP_WIDTH = 512
N_HEADS = 4
HEAD_DIM = 128
MLSTM_CONV = 4
POOL_WINDOWS = (2, 4, 8, 16)
CONV_WIDTH = 31
SB_KEY_BLOCK = 256
N_EXPERTS = 8
ALPHA = (2.0 * DEPTH) ** 0.25
EPS = 1e-5
LOG2E = 1.4426950408889634
LANES = 128
VMEM_LIMIT = 56 * 1024 * 1024


def _params(sem, vmem=VMEM_LIMIT):
    return pltpu.CompilerParams(dimension_semantics=sem, vmem_limit_bytes=vmem)


def _silu(x):
    return x * (1.0 / (1.0 + jnp.exp(-x)))


def _sigmoid(x):
    return 1.0 / (1.0 + jnp.exp(-x))


def _neg_softplus(x):
    return -(jnp.maximum(x, 0.0) + jnp.log(1.0 + jnp.exp(-jnp.abs(x))))


def _layer_norm_rows(r, w, b):
    mu = jnp.mean(r, axis=-1, keepdims=True)
    d = r - mu
    var = jnp.mean(d * d, axis=-1, keepdims=True)
    return d * lax.rsqrt(var + EPS) * w + b


def _rms_rows(y, w):
    return y * lax.rsqrt(jnp.mean(y * y, axis=-1, keepdims=True) + EPS) * w


def _bdot(a, b):
    return jnp.dot(a.astype(BF16), b.astype(BF16), preferred_element_type=F32)


def _ada_kernel(c_ref, w_ref, b_ref, out_ref):
    out_ref[...] = _bdot(_silu(c_ref[...]), w_ref[...]) + b_ref[...]


def _ada_all(c, ada_w, ada_b):
    L, D, D6 = ada_w.shape
    Bn = c.shape[0]
    tn = 1024
    return pl.pallas_call(
        _ada_kernel,
        out_shape=jax.ShapeDtypeStruct((L, Bn, D6), F32),
        grid=(L, D6 // tn),
        in_specs=[pl.BlockSpec((Bn, D), lambda l, n: (0, 0)),
                  pl.BlockSpec((None, D, tn), lambda l, n: (l, 0, n)),
                  pl.BlockSpec((None, 1, tn), lambda l, n: (l, 0, n))],
        out_specs=pl.BlockSpec((None, Bn, tn), lambda l, n: (l, 0, n)),
        compiler_params=_params(("parallel", "parallel")),
        name="ada_mod",
    )(c, ada_w, ada_b.reshape(L, 1, D6))


def _inproj_kernel(x_ref, sc_ref, sh_ref, w_ref, wg_ref, proj_ref, gates_ref, u_ref):
    @pl.when(pl.program_id(1) == 0)
    def _():
        u = (x_ref[...] * (1.0 + sc_ref[...]) + sh_ref[...]).astype(BF16)
        u_ref[...] = u
        gates_ref[...] = jnp.dot(u, wg_ref[...], preferred_element_type=F32)

    proj_ref[...] = jnp.dot(u_ref[...], w_ref[...], preferred_element_type=F32)


def _in_proj(x2d, sc, sh, w_main, w_gate, layer, seq):
    N, D = x2d.shape
    NC = w_main.shape[2]
    tm, tn = 1024, 1280
    tpb = seq // tm
    return pl.pallas_call(
        _inproj_kernel,
        out_shape=(jax.ShapeDtypeStruct((N, NC), F32), jax.ShapeDtypeStruct((N, LANES), F32)),
        grid=(N // tm, NC // tn),
        in_specs=[pl.BlockSpec((tm, D), lambda m, n: (m, 0)),
                  pl.BlockSpec((None, 1, D), lambda m, n: (m // tpb, 0, 0)),
                  pl.BlockSpec((None, 1, D), lambda m, n: (m // tpb, 0, 0)),
                  pl.BlockSpec((None, D, tn), lambda m, n: (layer, 0, n)),
                  pl.BlockSpec((None, D, LANES), lambda m, n: (layer, 0, 0))],
        out_specs=(pl.BlockSpec((tm, tn), lambda m, n: (m, n)),
                   pl.BlockSpec((tm, LANES), lambda m, n: (m, 0))),
        scratch_shapes=[pltpu.VMEM((tm, D), BF16)],
        compiler_params=_params(("parallel", "arbitrary")),
        name="in_proj",
    )(x2d, sc, sh, w_main, w_gate)


def _cumsum_rows(x):
    n = x.shape[0]
    row = lax.broadcasted_iota(jnp.int32, x.shape, 0)
    k = 1
    while k < n:
        x = x + jnp.where(row >= k, pltpu.roll(x, k, axis=0), 0.0)
        k *= 2
    return x


def _mlstm_kernel(qk_ref, v_ref, o_ref, g_ref, cw_ref, cb_ref, gb_ref, nw_ref, out_ref,
                  c_state, n_state, m_state, tail_ref):
    T = qk_ref.shape[0]
    GW = GROUP_WIDTH

    @pl.when(pl.program_id(1) == 0)
    def _():
        c_state[...] = jnp.zeros_like(c_state)
        n_state[...] = jnp.zeros_like(n_state)
        m_state[...] = jnp.zeros_like(m_state)
        tail_ref[...] = jnp.zeros_like(tail_ref)

    xx = jnp.concatenate([tail_ref[...], qk_ref[...]], axis=0)
    tail_ref[...] = qk_ref[T - 8:T, :]
    conv = cb_ref[...]
    for j in range(MLSTM_CONV):
        off = 8 - (MLSTM_CONV - 1) + j
        conv = conv + cw_ref[j:j + 1, :] * xx[off:off + T, :]
    qk = _silu(conv)

    gates = g_ref[...] + gb_ref[...]
    logf = _neg_softplus(-gates)
    causal = (lax.broadcasted_iota(jnp.int32, (CHUNK, CHUNK), 1)
              <= lax.broadcasted_iota(jnp.int32, (CHUNK, CHUNK), 0))

    for c in range(T // CHUNK):
        rows = slice(c * CHUNK, (c + 1) * CHUNK)
        gi = gates[rows, :]
        bcum = _cumsum_rows(logf[rows, :])
        gi_t = gi.T
        bcum_t = bcum.T
        for h in range(N_HEADS):
            cols = slice(h * HEAD_DIM, (h + 1) * HEAD_DIM)
            q = qk[rows, cols] * (HEAD_DIM ** -0.5)
            k = qk[rows, GW + h * HEAD_DIM:GW + (h + 1) * HEAD_DIM]
            v = v_ref[rows, cols]
            b_col = bcum[:, N_HEADS + h:N_HEADS + h + 1]
            b_row = bcum_t[N_HEADS + h:N_HEADS + h + 1, :]
            i_col = gi[:, h:h + 1]
            i_row = gi_t[h:h + 1, :]
            g_tot = b_col[CHUNK - 1:CHUNK, :]
            m_prev = m_state[h:h + 1, 0:1]
            c_prev = c_state[h]
            n_prev = n_state[h:h + 1, :]

            log_d = jnp.where(causal, b_col - b_row + i_row, -jnp.inf)
            m_inter = b_col + m_prev
            m_t = jnp.maximum(m_inter, jnp.max(log_d, axis=-1, keepdims=True))
            s = lax.dot_general(q.astype(BF16), k.astype(BF16), (((1,), (1,)), ((), ())),
                                preferred_element_type=F32)
            w = jnp.exp(log_d - m_t) * s
            inter = jnp.exp(m_inter - m_t)
            num = inter * _bdot(q, c_prev) + _bdot(w, v)
            den = inter * jnp.sum(q * n_prev, axis=-1, keepdims=True) + jnp.sum(w, axis=-1, keepdims=True)
            hcur = num / jnp.maximum(jnp.abs(den), jnp.exp(-m_t))

            a_col = g_tot - b_col + i_col
            m_new = jnp.maximum(g_tot + m_prev, jnp.max(a_col, axis=0, keepdims=True))
            decay = jnp.exp(g_tot + m_prev - m_new)
            kw = k * jnp.exp(a_col - m_new)
            c_state[h] = decay * c_prev + _bdot(kw.T, v)
            n_state[h:h + 1, :] = decay * n_prev + jnp.sum(kw, axis=0, keepdims=True)
            m_state[h:h + 1, :] = jnp.broadcast_to(m_new, (1, LANES))

            mu = jnp.mean(hcur, axis=-1, keepdims=True)
            dlt = hcur - mu
            var = jnp.mean(dlt * dlt, axis=-1, keepdims=True)
            hn = dlt * lax.rsqrt(var + EPS) * nw_ref[:, cols]
            out_ref[rows, cols] = (_sigmoid(o_ref[rows, cols]) * hn).astype(out_ref.dtype)


def _mlstm_mixer(proj, gates, conv_w, conv_b, gate_b, norm_w, B, S):
    T = 256
    GW = GROUP_WIDTH
    gb = jnp.zeros((1, LANES), F32).at[0, :2 * N_HEADS].set(gate_b)
    return pl.pallas_call(
        _mlstm_kernel,
        out_shape=jax.ShapeDtypeStruct((B, S, GW), BF16),
        grid=(B, S // T),
        in_specs=[pl.BlockSpec((None, T, 2 * GW), lambda b, t: (b, t, 0)),
                  pl.BlockSpec((None, T, GW), lambda b, t: (b, t, 2)),
                  pl.BlockSpec((None, T, GW), lambda b, t: (b, t, 3)),
                  pl.BlockSpec((None, T, LANES), lambda b, t: (b, t, 0)),
                  pl.BlockSpec((MLSTM_CONV, 2 * GW), lambda b, t: (0, 0)),
                  pl.BlockSpec((1, 2 * GW), lambda b, t: (0, 0)),
                  pl.BlockSpec((1, LANES), lambda b, t: (0, 0)),
                  pl.BlockSpec((1, GW), lambda b, t: (0, 0))],
        out_specs=pl.BlockSpec((None, T, GW), lambda b, t: (b, t, 0)),
        scratch_shapes=[pltpu.VMEM((N_HEADS, HEAD_DIM, HEAD_DIM), F32),
                        pltpu.VMEM((8, LANES), F32),
                        pltpu.VMEM((8, LANES), F32),
                        pltpu.VMEM((8, 2 * GW), F32)],
        compiler_params=_params(("parallel", "arbitrary")),
        name="mlstm_mixer",
    )(proj, proj, proj, gates, conv_w, conv_b.reshape(1, -1), gb, norm_w.reshape(1, -1))


def _pool_kernel(x_ref, w_ref, scale_ref, gn_ref, out_ref, tail_ref):
    T = x_ref.shape[0]
    HALO = 16
    t_blk = pl.program_id(1)

    @pl.when(t_blk == 0)
    def _():
        tail_ref[...] = jnp.zeros_like(tail_ref)

    x = x_ref[...]
    xx = jnp.concatenate([tail_ref[...], x], axis=0)
    tail_ref[...] = x[T - HALO:T, :]
    pos = t_blk * T + lax.broadcasted_iota(jnp.int32, (T, 1), 0) + 1
    outs = []
    for g, win in enumerate(POOL_WINDOWS):
        cols = slice(g * LANES, (g + 1) * LANES)
        s = xx[:, cols]
        k = 1
        while k < win:
            s = s + pltpu.roll(s, k, axis=0)
            k *= 2
        cnt = jnp.minimum(pos, win).astype(F32)
        yg = s[HALO:, :] / cnt - x[:, cols]
        outs.append(_bdot(yg, w_ref[g]))
    y = jnp.concatenate(outs, axis=-1) * scale_ref[...]
    out_ref[...] = _rms_rows(y, gn_ref[...]).astype(out_ref.dtype)


def _pool_mixer(proj, pool_w, pool_scale, gn, B, S):
    T = 512
    GW = GROUP_WIDTH
    return pl.pallas_call(
        _pool_kernel,
        out_shape=jax.ShapeDtypeStruct((B, S, GW), BF16),
        grid=(B, S // T),
        in_specs=[pl.BlockSpec((None, T, GW), lambda b, t: (b, t, 4)),
                  pl.BlockSpec((len(POOL_WINDOWS), LANES, LANES), lambda b, t: (0, 0, 0)),
                  pl.BlockSpec((1, GW), lambda b, t: (0, 0)),
                  pl.BlockSpec((1, GW), lambda b, t: (0, 0))],
        out_specs=pl.BlockSpec((None, T, GW), lambda b, t: (b, t, 0)),
        scratch_shapes=[pltpu.VMEM((16, GW), F32)],
        compiler_params=_params(("parallel", "arbitrary")),
        name="pool_mixer",
    )(proj, pool_w.astype(BF16), pool_scale.reshape(1, -1), gn.reshape(1, -1))


def _sb_kernel(q_ref, k_ref, v_ref, out_ref, kb_ref, vb_ref):
    LQ = q_ref.shape[0]
    LK = SB_KEY_BLOCK
    ratio = LQ // LK
    qi = pl.program_id(1)

    @pl.when(qi == 0)
    def _():
        kb_ref[...] = k_ref[...].astype(BF16)
        vb_ref[...] = v_ref[...].astype(BF16)

    scale2 = HEAD_DIM ** -0.5 * LOG2E
    tri = (lax.broadcasted_iota(jnp.int32, (LK, LK), 0)
           > lax.broadcasted_iota(jnp.int32, (LK, LK), 1)).astype(BF16)
    row = lax.broadcasted_iota(jnp.int32, (LQ, LK), 0)
    col = lax.broadcasted_iota(jnp.int32, (LQ, LK), 1)
    heads = [slice(h * HEAD_DIM, (h + 1) * HEAD_DIM) for h in range(N_HEADS)]
    qs = [(q_ref[:, hs] * scale2).astype(BF16) for hs in heads]

    def sweep(kb, carry, strict):
        start = pl.multiple_of(kb * LK, LK)
        zs, lks = [], []
        for h, hs in enumerate(heads):
            k_blk = kb_ref[pl.ds(start, LK), hs]
            z = lax.dot_general(qs[h], k_blk, (((1,), (1,)), ((), ())), preferred_element_type=F32)
            lk = -(jnp.maximum(z, 0.0) + jnp.log2(1.0 + jnp.exp2(-jnp.abs(z))))
            if strict is not None:
                lk = jnp.where(strict, lk, 0.0)
            zs.append(z)
            lks.append(lk)
        inner = jnp.dot(jnp.concatenate([lk.astype(BF16) for lk in lks], axis=0), tri,
                        preferred_element_type=F32)
        new = []
        for h, hs in enumerate(heads):
            after, acc = carry[h]
            v_blk = vb_ref[pl.ds(start, LK), hs]
            a = jnp.exp2(zs[h] + lks[h] + inner[h * LQ:(h + 1) * LQ, :] + after)
            if strict is not None:
                a = jnp.where(strict, a, 0.0)
            acc = acc + jnp.dot(a.astype(BF16), v_blk, preferred_element_type=F32)
            after = after + jnp.sum(lks[h], axis=-1, keepdims=True)
            new.append((after, acc))
        return tuple(new)

    init = tuple((jnp.zeros((LQ, 1), F32), jnp.zeros((LQ, HEAD_DIM), F32)) for _ in heads)
    carry = init
    for j in reversed(range(ratio)):
        carry = sweep(ratio * qi + j, carry, (j * LK + col) < row)
    carry = lax.fori_loop(0, ratio * qi, lambda it, c: sweep(ratio * qi - 1 - it, c, None), carry)
    for h, hs in enumerate(heads):
        out_ref[:, hs] = carry[h][1]


def _sb_mixer(proj, B, S):
    LQ = 512
    GW = GROUP_WIDTH
    return pl.pallas_call(
        _sb_kernel,
        out_shape=jax.ShapeDtypeStruct((B, S, GW), F32),
        grid=(B, S // LQ),
        in_specs=[pl.BlockSpec((None, LQ, GW), lambda b, i: (b, i, 5)),
                  pl.BlockSpec((None, S, GW), lambda b, i: (b, 0, 6)),
                  pl.BlockSpec((None, S, GW), lambda b, i: (b, 0, 7))],
        out_specs=pl.BlockSpec((None, LQ, GW), lambda b, i: (b, i, 0)),
        scratch_shapes=[pltpu.VMEM((S, GW), BF16), pltpu.VMEM((S, GW), BF16)],
        compiler_params=_params(("parallel", "arbitrary")),
        name="sb_attention",
    )(proj, proj, proj)


def _conv_kernel(x_ref, dw_ref, dwb_ref, lnw_ref, lnb_ref, pw_ref, pwb_ref, gn_ref, out_ref,
                 hbuf, ybuf):
    T = x_ref.shape[0]
    GW = GROUP_WIDTH
    HALO = 32
    SUB = 64

    @pl.when(pl.program_id(1) == 0)
    def _():
        hbuf[0:HALO, :] = jnp.zeros((HALO, GW), F32)

    @pl.when(pl.program_id(1) > 0)
    def _():
        hbuf[0:HALO, :] = hbuf[T:T + HALO, :]

    hbuf[HALO:HALO + T, :] = x_ref[:, 0:GW] * _sigmoid(x_ref[:, GW:2 * GW])

    def sub(i, carry):
        r0 = pl.multiple_of(i * SUB, SUB)
        acc = jnp.broadcast_to(dwb_ref[...], (SUB, GW))
        win = hbuf[pl.ds(r0, SUB + HALO), :]
        for res in range(8):
            shifted = win if res == 0 else pltpu.roll(win, SUB + HALO - res, axis=0)
            for j in range(CONV_WIDTH):
                off = HALO - CONV_WIDTH + 1 + j
                if off % 8 == res:
                    base = off - res
                    acc = acc + dw_ref[j:j + 1, :] * shifted[base:base + SUB, :]
        ybuf[pl.ds(r0, SUB), :] = acc
        return carry

    lax.fori_loop(0, T // SUB, sub, 0)
    hn = _silu(_layer_norm_rows(ybuf[...], lnw_ref[...], lnb_ref[...]))
    y = _bdot(hn, pw_ref[...]) + pwb_ref[...]
    out_ref[...] = _rms_rows(y, gn_ref[...]).astype(out_ref.dtype)


def _conv_mixer(proj, dw_w, dw_b, ln_w, ln_b, pw_w, pw_b, gn, B, S):
    T = 256
    GW = GROUP_WIDTH
    dw_pad = jnp.zeros((32, GW), F32).at[:CONV_WIDTH].set(dw_w)
    r = lambda a: a.reshape(1, -1)
    return pl.pallas_call(
        _conv_kernel,
        out_shape=jax.ShapeDtypeStruct((B, S, GW), BF16),
        grid=(B, S // T),
        in_specs=[pl.BlockSpec((None, T, 2 * GW), lambda b, t: (b, t, 4)),
                  pl.BlockSpec((32, GW), lambda b, t: (0, 0)),
                  pl.BlockSpec((1, GW), lambda b, t: (0, 0)),
                  pl.BlockSpec((1, GW), lambda b, t: (0, 0)),
                  pl.BlockSpec((1, GW), lambda b, t: (0, 0)),
                  pl.BlockSpec((GW, GW), lambda b, t: (0, 0)),
                  pl.BlockSpec((1, GW), lambda b, t: (0, 0)),
                  pl.BlockSpec((1, GW), lambda b, t: (0, 0))],
        out_specs=pl.BlockSpec((None, T, GW), lambda b, t: (b, t, 0)),
        scratch_shapes=[pltpu.VMEM((T + 32, GW), F32), pltpu.VMEM((T, GW), F32)],
        compiler_params=_params(("parallel", "arbitrary")),
        name="conv_mixer",
    )(proj, dw_pad, r(dw_b), r(ln_w), r(ln_b), pw_w.astype(BF16), r(pw_b), r(gn))


def _outproj_kernel(with_router, ya_ref, yb_ref, yc_ref, yd_ref, gnc_ref, w_ref, x_ref, g1_ref,
                    lnw_ref, lnb_ref, sc_ref, sh_ref, *rest):
    GW = GROUP_WIDTH
    if with_router:
        rw_ref, rb_ref, x1_ref, u2_ref, ids_ref, topw_ref = rest
    else:
        x1_ref, u2_ref = rest
    yc = _rms_rows(yc_ref[...], gnc_ref[...]).astype(BF16)
    acc = jnp.dot(ya_ref[...], w_ref[0:GW, :], preferred_element_type=F32)
    acc = acc + jnp.dot(yb_ref[...], w_ref[GW:2 * GW, :], preferred_element_type=F32)
    acc = acc + jnp.dot(yc, w_ref[2 * GW:3 * GW, :], preferred_element_type=F32)
    acc = acc + jnp.dot(yd_ref[...], w_ref[3 * GW:4 * GW, :], preferred_element_type=F32)
    r = ALPHA * x_ref[...] + (1.0 + g1_ref[...]) * acc
    x1 = _layer_norm_rows(r, lnw_ref[...], lnb_ref[...])
    x1_ref[...] = x1
    u2 = x1 * (1.0 + sc_ref[...]) + sh_ref[...]
    u2_ref[...] = u2.astype(u2_ref.dtype)
    if with_router:
        u_hi = u2.astype(BF16)
        u_lo = (u2 - u_hi.astype(F32)).astype(BF16)
        p = jnp.dot(u_hi, rw_ref[...], preferred_element_type=F32)
        logits = (p[:, :LANES] + p[:, LANES:]
                  + jnp.dot(u_lo, rw_ref[:, :LANES], preferred_element_type=F32) + rb_ref[...])
        lane = lax.broadcasted_iota(jnp.int32, logits.shape, 1)
        lg = jnp.where(lane < N_EXPERTS, logits, -jnp.inf)
        m1 = jnp.max(lg, axis=-1, keepdims=True)
        i1 = jnp.min(jnp.where(lg == m1, lane, LANES), axis=-1, keepdims=True)
        lg2 = jnp.where(lane == i1, -jnp.inf, lg)
        m2 = jnp.max(lg2, axis=-1, keepdims=True)
        i2 = jnp.min(jnp.where(lg2 == m2, lane, LANES), axis=-1, keepdims=True)
        e2 = jnp.exp(m2 - m1)
        w1 = 1.0 / (1.0 + e2)
        w2 = e2 / (1.0 + e2)
        ids_ref[...] = jnp.where(lane == 0, i1, jnp.where(lane == 1, i2, 0))
        topw_ref[...] = jnp.where(lane == 0, w1, jnp.where(lane == 1, w2, 0.0))


def _out_proj(ya, yb, yc, yd, gnc, w_out, x2d, g1, ln_w, ln_b, sc2, sh2, seq, router=None):
    N, D = x2d.shape
    GW = GROUP_WIDTH
    tm = 512
    tpb = seq // tm
    row = lambda m: (m, 0)
    const = lambda m: (0, 0)
    perb = lambda m: (m // tpb, 0, 0)
    in_specs = [pl.BlockSpec((tm, GW), row), pl.BlockSpec((tm, GW), row),
                pl.BlockSpec((tm, GW), row), pl.BlockSpec((tm, GW), row),
                pl.BlockSpec((1, GW), const),
                pl.BlockSpec((D, D), const),
                pl.BlockSpec((tm, D), row),
                pl.BlockSpec((None, 1, D), perb),
                pl.BlockSpec((1, D), const), pl.BlockSpec((1, D), const),
                pl.BlockSpec((None, 1, D), perb), pl.BlockSpec((None, 1, D), perb)]
    args = [ya, yb, yc, yd, gnc.reshape(1, -1), w_out, x2d, g1, ln_w.reshape(1, -1),
            ln_b.reshape(1, -1), sc2, sh2]
    out_shape = [jax.ShapeDtypeStruct((N, D), F32),
                 jax.ShapeDtypeStruct((N, D), BF16 if router is None else F32)]
    out_specs = [pl.BlockSpec((tm, D), row), pl.BlockSpec((tm, D), row)]
    if router is not None:
        rw, rb = router
        in_specs += [pl.BlockSpec((D, 2 * LANES), const), pl.BlockSpec((1, LANES), const)]
        args += [rw, rb]
        out_shape += [jax.ShapeDtypeStruct((N, LANES), jnp.int32), jax.ShapeDtypeStruct((N, LANES), F32)]
        out_specs += [pl.BlockSpec((tm, LANES), row), pl.BlockSpec((tm, LANES), row)]
    return pl.pallas_call(
        functools.partial(_outproj_kernel, router is not None),
        out_shape=tuple(out_shape),
        grid=(N // tm,),
        in_specs=in_specs,
        out_specs=tuple(out_specs),
        compiler_params=_params(("parallel",)),
        name="out_proj_ln",
    )(*args)


def _ffn_kernel(u_ref, wg_ref, wu_ref, wd_ref, x_ref, g2_ref, lnw_ref, lnb_ref, out_ref, acc_ref):
    f = pl.program_id(1)

    @pl.when(f == 0)
    def _():
        acc_ref[...] = jnp.zeros_like(acc_ref)

    u = u_ref[...]
    hg = jnp.dot(u, wg_ref[...], preferred_element_type=F32)
    hu = jnp.dot(u, wu_ref[...], preferred_element_type=F32)
    h = _silu(hg) * hu
    acc_ref[...] += jnp.dot(h.astype(BF16), wd_ref[...], preferred_element_type=F32)

    @pl.when(f == pl.num_programs(1) - 1)
    def _():
        r = ALPHA * x_ref[...] + (1.0 + g2_ref[...]) * acc_ref[...]
        out_ref[...] = _layer_norm_rows(r, lnw_ref[...], lnb_ref[...])


def _ffn(u2, wg, wu, wd, x1, g2, ln_w, ln_b, seq):
    N, D = u2.shape
    F = wg.shape[1]
    tm, tf = 512, 512
    tpb = seq // tm
    row = lambda m, f: (m, 0)
    const = lambda m, f: (0, 0)
    return pl.pallas_call(
        _ffn_kernel,
        out_shape=jax.ShapeDtypeStruct((N, D), F32),
        grid=(N // tm, F // tf),
        in_specs=[pl.BlockSpec((tm, D), row),
                  pl.BlockSpec((D, tf), lambda m, f: (0, f)),
                  pl.BlockSpec((D, tf), lambda m, f: (0, f)),
                  pl.BlockSpec((tf, D), lambda m, f: (f, 0)),
                  pl.BlockSpec((tm, D), row),
                  pl.BlockSpec((None, 1, D), lambda m, f: (m // tpb, 0, 0)),
                  pl.BlockSpec((1, D), const), pl.BlockSpec((1, D), const)],
        out_specs=pl.BlockSpec((tm, D), row),
        scratch_shapes=[pltpu.VMEM((tm, D), F32)],
        compiler_params=_params(("parallel", "arbitrary")),
        name="ffn_ln",
    )(u2, wg, wu, wd, x1, g2, ln_w.reshape(1, -1), ln_b.reshape(1, -1))


MOE_TM = 768
MOE_CW = 256


def _row_copy(src, src_row, dst, dst_row, sem):
    return pltpu.make_async_copy(src.at[pl.ds(src_row, 1), :], dst.at[pl.ds(dst_row, 1), :], sem)


def _start_rows(n_rows, make_copy):
    def body(i, carry):
        base = pl.multiple_of(i * 8, 8)
        for j in range(8):
            make_copy(base, j).start(priority=j % 2)
        return carry
    lax.fori_loop(0, n_rows // 8, body, 0)


def _wait_rows(n_rows, one_copy):
    def body(i, carry):
        one_copy.wait()
        return carry
    lax.fori_loop(0, n_rows, body, 0, unroll=8)


def _dispatch_kernel(pos_ref, u_ref, zeros_ref, out_ref, sem):
    del zeros_ref
    tm = u_ref.shape[0]
    first = pl.program_id(0) * tm

    def start(r, carry):
        for k in range(2):
            _row_copy(u_ref, r, out_ref, pos_ref[2 * (first + r) + k], sem).start(priority=k)
        return carry

    lax.fori_loop(0, tm, start, 0, unroll=8)
    _wait_rows(2 * tm, _row_copy(u_ref, 0, out_ref, 0, sem))


def _dispatch(u2, pos, n_rows):
    N, D = u2.shape
    tm = 256
    return pl.pallas_call(
        _dispatch_kernel,
        out_shape=jax.ShapeDtypeStruct((n_rows, D), u2.dtype),
        grid_spec=pltpu.PrefetchScalarGridSpec(
            num_scalar_prefetch=1,
            grid=(N // tm,),
            in_specs=[pl.BlockSpec((tm, D), lambda m, pos: (m, 0)),
                      pl.BlockSpec(memory_space=pl.ANY)],
            out_specs=pl.BlockSpec(memory_space=pl.ANY),
            scratch_shapes=[pltpu.SemaphoreType.DMA(())]),
        input_output_aliases={2: 0},
        compiler_params=_params(("arbitrary",)),
        name="moe_dispatch",
    )(pos, u2, jnp.zeros((n_rows, D), u2.dtype))


def _moe_ffn_kernel(te_ref, nv_ref, u_ref, wg_hbm, wu_hbm, wd_hbm, out_ref, wgbuf, wubuf, wdbuf, sem):
    t = pl.program_id(0)
    nv = nv_ref[0]
    F = wg_hbm.shape[2]
    n_chunks = F // MOE_CW

    def chunk_copies(tile, ci, slot):
        e = te_ref[tile]
        cols = pl.ds(ci * MOE_CW, MOE_CW)
        return (pltpu.make_async_copy(wg_hbm.at[e, :, cols], wgbuf.at[slot], sem.at[slot]),
                pltpu.make_async_copy(wu_hbm.at[e, :, cols], wubuf.at[slot], sem.at[slot]),
                pltpu.make_async_copy(wd_hbm.at[e, cols, :], wdbuf.at[slot], sem.at[slot]))

    @pl.when(t == 0)
    def _():
        for cp in chunk_copies(0, 0, 0):
            cp.start()

    @pl.when(t < nv)
    def _():
        u = u_ref[...].astype(BF16)
        for ci in range(n_chunks):
            slot = (t * n_chunks + ci) % 2
            for cp in chunk_copies(t, ci, slot):
                cp.wait()
            if ci + 1 < n_chunks:
                for cp in chunk_copies(t, ci + 1, 1 - slot):
                    cp.start()
            else:
                @pl.when(t + 1 < nv)
                def _():
                    for cp in chunk_copies(t + 1, 0, 1 - slot):
                        cp.start()
            hg = jnp.dot(u, wgbuf[slot].astype(BF16), preferred_element_type=F32)
            hu = jnp.dot(u, wubuf[slot].astype(BF16), preferred_element_type=F32)
            h = (_silu(hg) * hu).astype(BF16)
            part = jnp.dot(h, wdbuf[slot].astype(BF16), preferred_element_type=F32)
            if ci == 0:
                out_ref[...] = part
            else:
                out_ref[...] += part

    @pl.when(t >= nv)
    def _():
        out_ref[...] = jnp.zeros_like(out_ref)


def _moe_ffn(u_sorted, tile_expert, n_valid, wg, wu, wd):
    R, D = u_sorted.shape
    tm = MOE_TM
    tile = lambda t, nv: jnp.minimum(t, nv[0] - 1)
    return pl.pallas_call(
        _moe_ffn_kernel,
        out_shape=jax.ShapeDtypeStruct((R, D), F32),
        grid_spec=pltpu.PrefetchScalarGridSpec(
            num_scalar_prefetch=2,
            grid=(R // tm,),
            in_specs=[pl.BlockSpec((tm, D), lambda t, te, nv: (tile(t, nv), 0)),
                      pl.BlockSpec(memory_space=pl.ANY),
                      pl.BlockSpec(memory_space=pl.ANY),
                      pl.BlockSpec(memory_space=pl.ANY)],
            out_specs=pl.BlockSpec((tm, D), lambda t, te, nv: (t, 0)),
            scratch_shapes=[pltpu.VMEM((2, D, MOE_CW), F32), pltpu.VMEM((2, D, MOE_CW), F32),
                            pltpu.VMEM((2, MOE_CW, D), F32), pltpu.SemaphoreType.DMA((2,))]),
        compiler_params=_params(("arbitrary",)),
        name="moe_ffn",
    )(tile_expert, n_valid, u_sorted, wg, wu, wd)


def _combine_kernel(pos_ref, y_ref, topw_ref, x_ref, g2_ref, lnw_ref, lnb_ref, out_ref, ybuf, sem):
    tm = x_ref.shape[0]
    m = pl.program_id(0)
    slot = m % 2

    def gather(step, buf):
        for k in range(2):
            _start_rows(tm, lambda base, j: _row_copy(y_ref, pos_ref[2 * (step * tm + base + j) + k],
                                                      ybuf.at[buf, k], base + j, sem.at[buf]))

    @pl.when(m == 0)
    def _():
        gather(0, 0)

    @pl.when(m + 1 < pl.num_programs(0))
    def _():
        gather(m + 1, 1 - slot)

    _wait_rows(2 * tm, _row_copy(y_ref, 0, ybuf.at[slot, 0], 0, sem.at[slot]))
    tw = topw_ref[...]
    y = tw[:, 0:1] * ybuf[slot, 0] + tw[:, 1:2] * ybuf[slot, 1]
    r = ALPHA * x_ref[...] + (1.0 + g2_ref[...]) * y
    out_ref[...] = _layer_norm_rows(r, lnw_ref[...], lnb_ref[...])


def _combine(y_sorted, pos, topw, x1, g2, ln_w, ln_b, seq):
    N, D = x1.shape
    tm = 256
    tpb = seq // tm
    row = lambda m, pos: (m, 0)
    const = lambda m, pos: (0, 0)
    return pl.pallas_call(
        _combine_kernel,
        out_shape=jax.ShapeDtypeStruct((N, D), F32),
        grid_spec=pltpu.PrefetchScalarGridSpec(
            num_scalar_prefetch=1,
            grid=(N // tm,),
            in_specs=[pl.BlockSpec(memory_space=pl.ANY),
                      pl.BlockSpec((tm, LANES), row),
                      pl.BlockSpec((tm, D), row),
                      pl.BlockSpec((None, 1, D), lambda m, pos: (m // tpb, 0, 0)),
                      pl.BlockSpec((1, D), const), pl.BlockSpec((1, D), const)],
            out_specs=pl.BlockSpec((tm, D), row),
            scratch_shapes=[pltpu.VMEM((2, 2, tm, D), F32), pltpu.SemaphoreType.DMA((2,))]),
        compiler_params=_params(("arbitrary",)),
        name="moe_combine_ln",
    )(pos, y_sorted, topw, x1, g2, ln_w.reshape(1, -1), ln_b.reshape(1, -1))


def _routing_plan(ids, n_tiles):
    e_flat = ids[:, :2].reshape(-1)
    onehot = (e_flat[:, None] == jnp.arange(N_EXPERTS, dtype=jnp.int32)[None, :]).astype(jnp.int32)
    csum = jnp.cumsum(onehot, axis=0)
    rank = jnp.sum((csum - onehot) * onehot, axis=1)
    counts = csum[-1]
    padded = ((counts + MOE_TM - 1) // MOE_TM) * MOE_TM
    ends = jnp.cumsum(padded)
    offs = ends - padded
    pos = (jnp.sum(onehot * offs[None, :], axis=1) + rank).astype(jnp.int32)
    tile_start = jnp.arange(n_tiles, dtype=jnp.int32) * MOE_TM
    n_valid = (ends[-1] // MOE_TM).astype(jnp.int32)
    tile_start = jnp.minimum(tile_start, (n_valid - 1) * MOE_TM)
    tile_expert = jnp.sum((tile_start[:, None] >= ends[None, :]).astype(jnp.int32), axis=1).astype(jnp.int32)
    return pos, tile_expert, n_valid.reshape(1)


def _moe(u2, ids, topw, wg, wu, wd, x1, g2, ln_w, ln_b, seq):
    N = u2.shape[0]
    n_tiles = -(-2 * N // MOE_TM) + N_EXPERTS
    pos, tile_expert, n_valid = _routing_plan(ids, n_tiles)
    u_sorted = _dispatch(u2, pos, n_tiles * MOE_TM)
    y_sorted = _moe_ffn(u_sorted, tile_expert, n_valid, wg, wu, wd)
    return _combine(y_sorted, pos, topw, x1, g2, ln_w, ln_b, seq)


def _split_w_in_kernel(wt_ref, gt_ref, main_ref, gate_ref):
    main_ref[...] = wt_ref[0].T.astype(BF16)

    @pl.when(pl.program_id(1) == 0)
    def _():
        g = gt_ref[...]
        g = jnp.concatenate([g, jnp.zeros((LANES - g.shape[0], g.shape[1]), F32)], axis=0)
        gate_ref[...] = g.T.astype(BF16)


def _split_w_in(w_in):
    L, D, C = w_in.shape
    w_t = jnp.swapaxes(w_in, 1, 2)
    g0 = 4 * GROUP_WIDTH
    ng = 2 * N_HEADS
    tc = 512
    src_col = lambda j: (j * (tc // ng) + jnp.where(j * tc >= g0, 1, 0)) * ng
    return pl.pallas_call(
        _split_w_in_kernel,
        out_shape=(jax.ShapeDtypeStruct((L, D, C - ng), BF16),
                   jax.ShapeDtypeStruct((L, D, LANES), BF16)),
        grid=(L, (C - ng) // tc),
        in_specs=[pl.BlockSpec((pl.Element(1), pl.Element(tc), pl.Element(D)),
                               lambda l, j: (l, src_col(j), 0)),
                  pl.BlockSpec((None, ng, D), lambda l, j: (l, g0 // ng, 0))],
        out_specs=(pl.BlockSpec((None, D, tc), lambda l, j: (l, 0, j)),
                   pl.BlockSpec((None, D, LANES), lambda l, j: (l, 0, 0))),
        compiler_params=_params(("parallel", "arbitrary")),
        name="split_w_in",
    )(w_t, w_t)


def kernel(x, c, w_in, mlstm_conv_w, mlstm_conv_b, mlstm_gate_b, mlstm_norm_w, pool_w, pool_scale, conv_dw_w, conv_dw_b, conv_ln_w, conv_ln_b, conv_pw_w, conv_pw_b, group_norm_w, w_out, ada_w, ada_b, ln1_w, ln1_b, ln2_w, ln2_b, ffn_w_gate, ffn_w_up, ffn_w_down, moe_router_w, moe_router_b, moe_w_gate, moe_w_up, moe_w_down):
    B, S, D = x.shape
    GW = GROUP_WIDTH
    ada = _ada_all(c, ada_w, ada_b)
    x2d = x.reshape(B * S, D)
    w_main, w_gate = _split_w_in(w_in)
    for l in range(DEPTH):
        mod = [ada[l, :, i * D:(i + 1) * D].reshape(B, 1, D) for i in range(6)]
        sh1, sc1, g1, sh2, sc2, g2 = mod
        proj, gates = _in_proj(x2d, sc1, sh1, w_main, w_gate, l, S)
        proj = proj.reshape(B, S, -1)
        gates = gates.reshape(B, S, LANES)
        gn_b, gn_c, gn_d = (group_norm_w[l, i * GW:(i + 1) * GW] for i in range(3))
        ya = _mlstm_mixer(proj, gates, mlstm_conv_w[l], mlstm_conv_b[l], mlstm_gate_b[l],
                          mlstm_norm_w[l], B, S)
        yb = _pool_mixer(proj, pool_w[l], pool_scale[l], gn_b, B, S)
        yc = _sb_mixer(proj, B, S)
        yd = _conv_mixer(proj, conv_dw_w[l], conv_dw_b[l], conv_ln_w[l], conv_ln_b[l],
                         conv_pw_w[l], conv_pw_b[l], gn_d, B, S)
        flat = lambda t: t.reshape(B * S, GW)
        j = l // 2
        router = None
        if l % 2 == 1:
            rw = jnp.zeros((D, LANES), F32).at[:, :N_EXPERTS].set(moe_router_w[j])
            rw_hi = rw.astype(BF16)
            rw_lo = (rw - rw_hi.astype(F32)).astype(BF16)
            rb = jnp.zeros((1, LANES), F32).at[0, :N_EXPERTS].set(moe_router_b[j])
            router = (jnp.concatenate([rw_hi, rw_lo], axis=1), rb)
        outs = _out_proj(flat(ya), flat(yb), flat(yc), flat(yd), gn_c, w_out[l].astype(BF16), x2d,
                         g1, ln1_w[l], ln1_b[l], sc2, sh2, S, router)
        if l % 2 == 0:
            x1, u2 = outs
            x2d = _ffn(u2, ffn_w_gate[j].astype(BF16), ffn_w_up[j].astype(BF16),
                       ffn_w_down[j].astype(BF16), x1, g2, ln2_w[l], ln2_b[l], S)
        else:
            x1, u2, ids, topw = outs
            x2d = _moe(u2, ids, topw, moe_w_gate[j], moe_w_up[j], moe_w_down[j], x1, g2,
                       ln2_w[l], ln2_b[l], S)
    return x2d.reshape(B, S, D)
```

```python
import functools

import jax
import jax.numpy as jnp
from jax import lax
from jax.experimental import pallas as pl
from jax.experimental.pallas import tpu as pltpu

F32 = jnp.float32
BF16 = jnp.bfloat16

DEPTH = 2
CHUNK = 256
GROUP_WIDTH = 512
N_HEADS = 4
HEAD_DIM = 128
MLSTM_CONV = 4
POOL_WINDOWS = (2, 4, 8, 16)
CONV_WIDTH = 31
SB_KEY_BLOCK = 256
N_EXPERTS = 8
ALPHA = (2.0 * DEPTH) ** 0.25
EPS = 1e-5
LOG2E = 1.4426950408889634
LANES = 128
VMEM_LIMIT = 56 * 1024 * 1024


def _params(sem, vmem=VMEM_LIMIT):
    return pltpu.CompilerParams(dimension_semantics=sem, vmem_limit_bytes=vmem)


def _silu(x):
    return x * (1.0 / (1.0 + jnp.exp(-x)))


def _sigmoid(x):
    return 1.0 / (1.0 + jnp.exp(-x))


def _neg_softplus(x):
    return -(jnp.maximum(x, 0.0) + jnp.log(1.0 + jnp.exp(-jnp.abs(x))))


def _layer_norm_rows(r, w, b):
    mu = jnp.mean(r, axis=-1, keepdims=True)
    d = r - mu
    var = jnp.mean(d * d, axis=-1, keepdims=True)
    return d * lax.rsqrt(var + EPS) * w + b


def _rms_rows(y, w):
    return y * lax.rsqrt(jnp.mean(y * y, axis=-1, keepdims=True) + EPS) * w


def _bdot(a, b):
    return jnp.dot(a.astype(BF16), b.astype(BF16), preferred_element_type=F32)


def _ada_kernel(c_ref, w_ref, b_ref, out_ref):
    out_ref[...] = _bdot(_silu(c_ref[...]), w_ref[...]) + b_ref[...]


def _ada_all(c, ada_w, ada_b):
    L, D, D6 = ada_w.shape
    Bn = c.shape[0]
    tn = 1024
    return pl.pallas_call(
        _ada_kernel,
        out_shape=jax.ShapeDtypeStruct((L, Bn, D6), F32),
        grid=(L, D6 // tn),
        in_specs=[pl.BlockSpec((Bn, D), lambda l, n: (0, 0)),
                  pl.BlockSpec((None, D, tn), lambda l, n: (l, 0, n)),
                  pl.BlockSpec((None, 1, tn), lambda l, n: (l, 0, n))],
        out_specs=pl.BlockSpec((None, Bn, tn), lambda l, n: (l, 0, n)),
        compiler_params=_params(("parallel", "parallel")),
        name="ada_mod",
    )(c, ada_w, ada_b.reshape(L, 1, D6))


def _inproj_kernel(x_ref, sc_ref, sh_ref, w_ref, wg_ref, proj_ref, gates_ref, u_ref):
    @pl.when(pl.program_id(1) == 0)
    def _():
        u = (x_ref[...] * (1.0 + sc_ref[...]) + sh_ref[...]).astype(BF16)
        u_ref[...] = u
        gates_ref[...] = jnp.dot(u, wg_ref[...], preferred_element_type=F32)

    proj_ref[...] = jnp.dot(u_ref[...], w_ref[...], preferred_element_type=F32)


def _in_proj(x2d, sc, sh, w_main, w_gate, layer, seq):
    N, D = x2d.shape
    NC = w_main.shape[2]
    tm, tn = 1024, 1280
    tpb = seq // tm
    return pl.pallas_call(
        _inproj_kernel,
        out_shape=(jax.ShapeDtypeStruct((N, NC), F32), jax.ShapeDtypeStruct((N, LANES), F32)),
        grid=(N // tm, NC // tn),
        in_specs=[pl.BlockSpec((tm, D), lambda m, n: (m, 0)),
                  pl.BlockSpec((None, 1, D), lambda m, n: (m // tpb, 0, 0)),
                  pl.BlockSpec((None, 1, D), lambda m, n: (m // tpb, 0, 0)),
                  pl.BlockSpec((None, D, tn), lambda m, n: (layer, 0, n)),
                  pl.BlockSpec((None, D, LANES), lambda m, n: (layer, 0, 0))],
        out_specs=(pl.BlockSpec((tm, tn), lambda m, n: (m, n)),
                   pl.BlockSpec((tm, LANES), lambda m, n: (m, 0))),
        scratch_shapes=[pltpu.VMEM((tm, D), BF16)],
        compiler_params=_params(("parallel", "arbitrary")),
        name="in_proj",
    )(x2d, sc, sh, w_main, w_gate)


def _cumsum_rows(x):
    n = x.shape[0]
    row = lax.broadcasted_iota(jnp.int32, x.shape, 0)
    k = 1
    while k < n:
        x = x + jnp.where(row >= k, pltpu.roll(x, k, axis=0), 0.0)
        k *= 2
    return x


def _mlstm_kernel(qk_ref, v_ref, o_ref, g_ref, cw_ref, cb_ref, gb_ref, nw_ref, out_ref,
                  c_state, n_state, m_state, tail_ref):
    @pl.when(pl.program_id(1) == 0)
    def _():
        c_state[...] = jnp.zeros_like(c_state)
        n_state[...] = jnp.zeros_like(n_state)
        m_state[...] = jnp.zeros_like(m_state)
        tail_ref[...] = jnp.zeros_like(tail_ref)

    for bb in range(qk_ref.shape[0]):
        _mlstm_block(qk_ref.at[bb], v_ref.at[bb], o_ref.at[bb], g_ref.at[bb], cw_ref, cb_ref, gb_ref,
                     nw_ref, out_ref.at[bb], c_state.at[bb], n_state.at[bb], m_state.at[bb],
                     tail_ref.at[bb])


def _mlstm_block(qk_ref, v_ref, o_ref, g_ref, cw_ref, cb_ref, gb_ref, nw_ref, out_ref,
                 c_state, n_state, m_state, tail_ref):
    T = qk_ref.shape[0]
    GW = GROUP_WIDTH

    xx = jnp.concatenate([tail_ref[...], qk_ref[...]], axis=0)
    tail_ref[...] = qk_ref[T - 8:T, :]
    conv = cb_ref[...]
    for j in range(MLSTM_CONV):
        off = 8 - (MLSTM_CONV - 1) + j
        conv = conv + cw_ref[j:j + 1, :] * xx[off:off + T, :]
    qk = _silu(conv)

    gates = g_ref[...] + gb_ref[...]
    logf = _neg_softplus(-gates)
    causal = (lax.broadcasted_iota(jnp.int32, (CHUNK, CHUNK), 1)
              <= lax.broadcasted_iota(jnp.int32, (CHUNK, CHUNK), 0))

    for c in range(T // CHUNK):
        rows = slice(c * CHUNK, (c + 1) * CHUNK)
        gi = gates[rows, :]
        bcum = _cumsum_rows(logf[rows, :])
        gi_t = gi.T
        bcum_t = bcum.T
        for h in range(N_HEADS):
            cols = slice(h * HEAD_DIM, (h + 1) * HEAD_DIM)
            q = qk[rows, cols] * (HEAD_DIM ** -0.5)
            k = qk[rows, GW + h * HEAD_DIM:GW + (h + 1) * HEAD_DIM]
            v = v_ref[rows, cols]
            b_col = bcum[:, N_HEADS + h:N_HEADS + h + 1]
            b_row = bcum_t[N_HEADS + h:N_HEADS + h + 1, :]
            i_col = gi[:, h:h + 1]
            i_row = gi_t[h:h + 1, :]
            g_tot = b_col[CHUNK - 1:CHUNK, :]
            m_prev = m_state[h:h + 1, 0:1]
            c_prev = c_state[h]
            n_prev = n_state[h:h + 1, :]

            log_d = jnp.where(causal, b_col - b_row + i_row, -jnp.inf)
            m_inter = b_col + m_prev
            m_t = jnp.maximum(m_inter, jnp.max(log_d, axis=-1, keepdims=True))
            s = lax.dot_general(q.astype(BF16), k.astype(BF16), (((1,), (1,)), ((), ())),
                                preferred_element_type=F32)
            w = jnp.exp(log_d - m_t) * s
            inter = jnp.exp(m_inter - m_t)
            num = inter * _bdot(q, c_prev) + _bdot(w, v)
            den = inter * jnp.sum(q * n_prev, axis=-1, keepdims=True) + jnp.sum(w, axis=-1, keepdims=True)
            hcur = num / jnp.maximum(jnp.abs(den), jnp.exp(-m_t))

            a_col = g_tot - b_col + i_col
            m_new = jnp.maximum(g_tot + m_prev, jnp.max(a_col, axis=0, keepdims=True))
            decay = jnp.exp(g_tot + m_prev - m_new)
            kw = k * jnp.exp(a_col - m_new)
            c_state[h] = decay * c_prev + _bdot(kw.T, v)
            n_state[h:h + 1, :] = decay * n_prev + jnp.sum(kw, axis=0, keepdims=True)
            m_state[h:h + 1, :] = jnp.broadcast_to(m_new, (1, LANES))

            mu = jnp.mean(hcur, axis=-1, keepdims=True)
            dlt = hcur - mu
            var = jnp.mean(dlt * dlt, axis=-1, keepdims=True)
            hn = dlt * lax.rsqrt(var + EPS) * nw_ref[:, cols]
            out_ref[rows, cols] = (_sigmoid(o_ref[rows, cols]) * hn).astype(out_ref.dtype)


def _mlstm_mixer(proj, gates, conv_w, conv_b, gate_b, norm_w, B, S):
    T = 256
    NB = 2
    GW = GROUP_WIDTH
    gb = jnp.zeros((1, LANES), F32).at[0, :2 * N_HEADS].set(gate_b)
    return pl.pallas_call(
        _mlstm_kernel,
        out_shape=jax.ShapeDtypeStruct((B, S, GW), BF16),
        grid=(B // NB, S // T),
        in_specs=[pl.BlockSpec((NB, T, 2 * GW), lambda b, t: (b, t, 0)),
                  pl.BlockSpec((NB, T, GW), lambda b, t: (b, t, 2)),
                  pl.BlockSpec((NB, T, GW), lambda b, t: (b, t, 3)),
                  pl.BlockSpec((NB, T, LANES), lambda b, t: (b, t, 0)),
                  pl.BlockSpec((MLSTM_CONV, 2 * GW), lambda b, t: (0, 0)),
                  pl.BlockSpec((1, 2 * GW), lambda b, t: (0, 0)),
                  pl.BlockSpec((1, LANES), lambda b, t: (0, 0)),
                  pl.BlockSpec((1, GW), lambda b, t: (0, 0))],
        out_specs=pl.BlockSpec((NB, T, GW), lambda b, t: (b, t, 0)),
        scratch_shapes=[pltpu.VMEM((NB, N_HEADS, HEAD_DIM, HEAD_DIM), F32),
                        pltpu.VMEM((NB, 8, LANES), F32),
                        pltpu.VMEM((NB, 8, LANES), F32),
                        pltpu.VMEM((NB, 8, 2 * GW), F32)],
        compiler_params=_params(("parallel", "arbitrary")),
        name="mlstm_mixer",
    )(proj, proj, proj, gates, conv_w, conv_b.reshape(1, -1), gb, norm_w.reshape(1, -1))


def _pool_kernel(x_ref, w_ref, scale_ref, gn_ref, out_ref, tail_ref):
    T = x_ref.shape[0]
    HALO = 16
    t_blk = pl.program_id(1)

    @pl.when(t_blk == 0)
    def _():
        tail_ref[...] = jnp.zeros_like(tail_ref)

    x = x_ref[...]
    xx = jnp.concatenate([tail_ref[...], x], axis=0)
    tail_ref[...] = x[T - HALO:T, :]
    pos = t_blk * T + lax.broadcasted_iota(jnp.int32, (T, 1), 0) + 1
    outs = []
    for g, win in enumerate(POOL_WINDOWS):
        cols = slice(g * LANES, (g + 1) * LANES)
        s = xx[:, cols]
        k = 1
        while k < win:
            s = s + pltpu.roll(s, k, axis=0)
            k *= 2
        cnt = jnp.minimum(pos, win).astype(F32)
        yg = s[HALO:, :] / cnt - x[:, cols]
        outs.append(_bdot(yg, w_ref[g]))
    y = jnp.concatenate(outs, axis=-1) * scale_ref[...]
    out_ref[...] = _rms_rows(y, gn_ref[...]).astype(out_ref.dtype)


def _pool_mixer(proj, pool_w, pool_scale, gn, B, S):
    T = 512
    GW = GROUP_WIDTH
    return pl.pallas_call(
        _pool_kernel,
        out_shape=jax.ShapeDtypeStruct((B, S, GW), BF16),
        grid=(B, S // T),
        in_specs=[pl.BlockSpec((None, T, GW), lambda b, t: (b, t, 4)),
                  pl.BlockSpec((len(POOL_WINDOWS), LANES, LANES), lambda b, t: (0, 0, 0)),
                  pl.BlockSpec((1, GW), lambda b, t: (0, 0)),
                  pl.BlockSpec((1, GW), lambda b, t: (0, 0))],
        out_specs=pl.BlockSpec((None, T, GW), lambda b, t: (b, t, 0)),
        scratch_shapes=[pltpu.VMEM((16, GW), F32)],
        compiler_params=_params(("parallel", "arbitrary")),
        name="pool_mixer",
    )(proj, pool_w.astype(BF16), pool_scale.reshape(1, -1), gn.reshape(1, -1))


def _sb_kernel(q_ref, k_ref, v_ref, out_ref, kb_ref, vb_ref):
    LQ = q_ref.shape[0]
    LK = SB_KEY_BLOCK
    ratio = LQ // LK
    qi = pl.program_id(1)

    @pl.when(qi == 0)
    def _():
        kb_ref[...] = k_ref[...].astype(BF16)
        vb_ref[...] = v_ref[...].astype(BF16)

    scale2 = HEAD_DIM ** -0.5 * LOG2E
    tri = (lax.broadcasted_iota(jnp.int32, (LK, LK), 0)
           > lax.broadcasted_iota(jnp.int32, (LK, LK), 1)).astype(BF16)
    row = lax.broadcasted_iota(jnp.int32, (LQ, LK), 0)
    col = lax.broadcasted_iota(jnp.int32, (LQ, LK), 1)
    heads = [slice(h * HEAD_DIM, (h + 1) * HEAD_DIM) for h in range(N_HEADS)]
    qs = [(q_ref[:, hs] * scale2).astype(BF16) for hs in heads]

    def sweep(kb, carry, strict):
        start = pl.multiple_of(kb * LK, LK)
        zs, lks = [], []
        for h, hs in enumerate(heads):
            k_blk = kb_ref[pl.ds(start, LK), hs]
            z = lax.dot_general(qs[h], k_blk, (((1,), (1,)), ((), ())), preferred_element_type=F32)
            lk = -(jnp.maximum(z, 0.0) + jnp.log2(1.0 + jnp.exp2(-jnp.abs(z))))
            if strict is not None:
                lk = jnp.where(strict, lk, 0.0)
            zs.append(z)
            lks.append(lk)
        inner = jnp.dot(jnp.concatenate([lk.astype(BF16) for lk in lks], axis=0), tri,
                        preferred_element_type=F32)
        new = []
        for h, hs in enumerate(heads):
            after, acc = carry[h]
            v_blk = vb_ref[pl.ds(start, LK), hs]
            a = jnp.exp2(zs[h] + lks[h] + inner[h * LQ:(h + 1) * LQ, :] + after)
            if strict is not None:
                a = jnp.where(strict, a, 0.0)
            acc = acc + jnp.dot(a.astype(BF16), v_blk, preferred_element_type=F32)
            after = after + jnp.sum(lks[h], axis=-1, keepdims=True)
            new.append((after, acc))
        return tuple(new)

    init = tuple((jnp.zeros((LQ, 1), F32), jnp.zeros((LQ, HEAD_DIM), F32)) for _ in heads)
    carry = init
    for j in reversed(range(ratio)):
        carry = sweep(ratio * qi + j, carry, (j * LK + col) < row)
    carry = lax.fori_loop(0, ratio * qi, lambda it, c: sweep(ratio * qi - 1 - it, c, None), carry)
    for h, hs in enumerate(heads):
        out_ref[:, hs] = carry[h][1]


def _sb_mixer(proj, B, S):
    LQ = 512
    GW = GROUP_WIDTH
    return pl.pallas_call(
        _sb_kernel,
        out_shape=jax.ShapeDtypeStruct((B, S, GW), F32),
        grid=(B, S // LQ),
        in_specs=[pl.BlockSpec((None, LQ, GW), lambda b, i: (b, i, 5)),
                  pl.BlockSpec((None, S, GW), lambda b, i: (b, 0, 6)),
                  pl.BlockSpec((None, S, GW), lambda b, i: (b, 0, 7))],
        out_specs=pl.BlockSpec((None, LQ, GW), lambda b, i: (b, i, 0)),
        scratch_shapes=[pltpu.VMEM((S, GW), BF16), pltpu.VMEM((S, GW), BF16)],
        compiler_params=_params(("parallel", "arbitrary")),
        name="sb_attention",
    )(proj, proj, proj)


def _conv_kernel(x_ref, dw_ref, dwb_ref, lnw_ref, lnb_ref, pw_ref, pwb_ref, gn_ref, out_ref,
                 hbuf, ybuf):
    T = x_ref.shape[0]
    GW = GROUP_WIDTH
    HALO = 32
    SUB = 64

    @pl.when(pl.program_id(1) == 0)
    def _():
        hbuf[0:HALO, :] = jnp.zeros((HALO, GW), F32)

    @pl.when(pl.program_id(1) > 0)
    def _():
        hbuf[0:HALO, :] = hbuf[T:T + HALO, :]

    hbuf[HALO:HALO + T, :] = x_ref[:, 0:GW] * _sigmoid(x_ref[:, GW:2 * GW])

    def sub(i, carry):
        r0 = pl.multiple_of(i * SUB, SUB)
        acc = jnp.broadcast_to(dwb_ref[...], (SUB, GW))
        win = hbuf[pl.ds(r0, SUB + HALO), :]
        for res in range(8):
            shifted = win if res == 0 else pltpu.roll(win, SUB + HALO - res, axis=0)
            for j in range(CONV_WIDTH):
                off = HALO - CONV_WIDTH + 1 + j
                if off % 8 == res:
                    base = off - res
                    acc = acc + dw_ref[j:j + 1, :] * shifted[base:base + SUB, :]
        ybuf[pl.ds(r0, SUB), :] = acc
        return carry

    lax.fori_loop(0, T // SUB, sub, 0)
    hn = _silu(_layer_norm_rows(ybuf[...], lnw_ref[...], lnb_ref[...]))
    y = _bdot(hn, pw_ref[...]) + pwb_ref[...]
    out_ref[...] = _rms_rows(y, gn_ref[...]).astype(out_ref.dtype)


def _conv_mixer(proj, dw_w, dw_b, ln_w, ln_b, pw_w, pw_b, gn, B, S):
    T = 256
    GW = GROUP_WIDTH
    dw_pad = jnp.zeros((32, GW), F32).at[:CONV_WIDTH].set(dw_w)
    r = lambda a: a.reshape(1, -1)
    return pl.pallas_call(
        _conv_kernel,
        out_shape=jax.ShapeDtypeStruct((B, S, GW), BF16),
        grid=(B, S // T),
        in_specs=[pl.BlockSpec((None, T, 2 * GW), lambda b, t: (b, t, 4)),
                  pl.BlockSpec((32, GW), lambda b, t: (0, 0)),
                  pl.BlockSpec((1, GW), lambda b, t: (0, 0)),
                  pl.BlockSpec((1, GW), lambda b, t: (0, 0)),
                  pl.BlockSpec((1, GW), lambda b, t: (0, 0)),
                  pl.BlockSpec((GW, GW), lambda b, t: (0, 0)),
                  pl.BlockSpec((1, GW), lambda b, t: (0, 0)),
                  pl.BlockSpec((1, GW), lambda b, t: (0, 0))],
        out_specs=pl.BlockSpec((None, T, GW), lambda b, t: (b, t, 0)),
        scratch_shapes=[pltpu.VMEM((T + 32, GW), F32), pltpu.VMEM((T, GW), F32)],
        compiler_params=_params(("parallel", "arbitrary")),
        name="conv_mixer",
    )(proj, dw_pad, r(dw_b), r(ln_w), r(ln_b), pw_w.astype(BF16), r(pw_b), r(gn))


def _outproj_kernel(with_router, ya_ref, yb_ref, yc_ref, yd_ref, gnc_ref, w_ref, x_ref, g1_ref,
                    lnw_ref, lnb_ref, sc_ref, sh_ref, *rest):
    GW = GROUP_WIDTH
    if with_router:
        rw_ref, rb_ref, x1_ref, u2_ref, ids_ref, topw_ref = rest
    else:
        x1_ref, u2_ref = rest
    yc = _rms_rows(yc_ref[...], gnc_ref[...]).astype(BF16)
    acc = jnp.dot(ya_ref[...], w_ref[0:GW, :], preferred_element_type=F32)
    acc = acc + jnp.dot(yb_ref[...], w_ref[GW:2 * GW, :], preferred_element_type=F32)
    acc = acc + jnp.dot(yc, w_ref[2 * GW:3 * GW, :], preferred_element_type=F32)
    acc = acc + jnp.dot(yd_ref[...], w_ref[3 * GW:4 * GW, :], preferred_element_type=F32)
    r = ALPHA * x_ref[...] + (1.0 + g1_ref[...]) * acc
    x1 = _layer_norm_rows(r, lnw_ref[...], lnb_ref[...])
    x1_ref[...] = x1
    u2 = x1 * (1.0 + sc_ref[...]) + sh_ref[...]
    u2_ref[...] = u2.astype(u2_ref.dtype)
    if with_router:
        u_hi = u2.astype(BF16)
        u_lo = (u2 - u_hi.astype(F32)).astype(BF16)
        p = jnp.dot(u_hi, rw_ref[...], preferred_element_type=F32)
        logits = (p[:, :LANES] + p[:, LANES:]
                  + jnp.dot(u_lo, rw_ref[:, :LANES], preferred_element_type=F32) + rb_ref[...])
        lane = lax.broadcasted_iota(jnp.int32, logits.shape, 1)
        lg = jnp.where(lane < N_EXPERTS, logits, -jnp.inf)
        m1 = jnp.max(lg, axis=-1, keepdims=True)
        i1 = jnp.min(jnp.where(lg == m1, lane, LANES), axis=-1, keepdims=True)
        lg2 = jnp.where(lane == i1, -jnp.inf, lg)
        m2 = jnp.max(lg2, axis=-1, keepdims=True)
        i2 = jnp.min(jnp.where(lg2 == m2, lane, LANES), axis=-1, keepdims=True)
        e2 = jnp.exp(m2 - m1)
        w1 = 1.0 / (1.0 + e2)
        w2 = e2 / (1.0 + e2)
        ids_ref[...] = jnp.where(lane == 0, i1, jnp.where(lane == 1, i2, 0))
        topw_ref[...] = jnp.where(lane == 0, w1, jnp.where(lane == 1, w2, 0.0))


def _out_proj(ya, yb, yc, yd, gnc, w_out, x2d, g1, ln_w, ln_b, sc2, sh2, seq, router=None):
    N, D = x2d.shape
    GW = GROUP_WIDTH
    tm = 512
    tpb = seq // tm
    row = lambda m: (m, 0)
    const = lambda m: (0, 0)
    perb = lambda m: (m // tpb, 0, 0)
    in_specs = [pl.BlockSpec((tm, GW), row), pl.BlockSpec((tm, GW), row),
                pl.BlockSpec((tm, GW), row), pl.BlockSpec((tm, GW), row),
                pl.BlockSpec((1, GW), const),
                pl.BlockSpec((D, D), const),
                pl.BlockSpec((tm, D), row),
                pl.BlockSpec((None, 1, D), perb),
                pl.BlockSpec((1, D), const), pl.BlockSpec((1, D), const),
                pl.BlockSpec((None, 1, D), perb), pl.BlockSpec((None, 1, D), perb)]
    args = [ya, yb, yc, yd, gnc.reshape(1, -1), w_out, x2d, g1, ln_w.reshape(1, -1),
            ln_b.reshape(1, -1), sc2, sh2]
    out_shape = [jax.ShapeDtypeStruct((N, D), F32),
                 jax.ShapeDtypeStruct((N, D), BF16 if router is None else F32)]
    out_specs = [pl.BlockSpec((tm, D), row), pl.BlockSpec((tm, D), row)]
    if router is not None:
        rw, rb = router
        in_specs += [pl.BlockSpec((D, 2 * LANES), const), pl.BlockSpec((1, LANES), const)]
        args += [rw, rb]
        out_shape += [jax.ShapeDtypeStruct((N, LANES), jnp.int32), jax.ShapeDtypeStruct((N, LANES), F32)]
        out_specs += [pl.BlockSpec((tm, LANES), row), pl.BlockSpec((tm, LANES), row)]
    return pl.pallas_call(
        functools.partial(_outproj_kernel, router is not None),
        out_shape=tuple(out_shape),
        grid=(N // tm,),
        in_specs=in_specs,
        out_specs=tuple(out_specs),
        compiler_params=_params(("parallel",)),
        name="out_proj_ln",
    )(*args)


def _ffn_kernel(u_ref, wg_ref, wu_ref, wd_ref, x_ref, g2_ref, lnw_ref, lnb_ref, out_ref, acc_ref):
    f = pl.program_id(1)

    @pl.when(f == 0)
    def _():
        acc_ref[...] = jnp.zeros_like(acc_ref)

    u = u_ref[...]
    hg = jnp.dot(u, wg_ref[...], preferred_element_type=F32)
    hu = jnp.dot(u, wu_ref[...], preferred_element_type=F32)
    h = _silu(hg) * hu
    acc_ref[...] += jnp.dot(h.astype(BF16), wd_ref[...], preferred_element_type=F32)

    @pl.when(f == pl.num_programs(1) - 1)
    def _():
        r = ALPHA * x_ref[...] + (1.0 + g2_ref[...]) * acc_ref[...]
        out_ref[...] = _layer_norm_rows(r, lnw_ref[...], lnb_ref[...])


def _ffn(u2, wg, wu, wd, x1, g2, ln_w, ln_b, seq):
    N, D = u2.shape
    F = wg.shape[1]
    tm, tf = 512, 512
    tpb = seq // tm
    row = lambda m, f: (m, 0)
    const = lambda m, f: (0, 0)
    return pl.pallas_call(
        _ffn_kernel,
        out_shape=jax.ShapeDtypeStruct((N, D), F32),
        grid=(N // tm, F // tf),
        in_specs=[pl.BlockSpec((tm, D), row),
                  pl.BlockSpec((D, tf), lambda m, f: (0, f)),
                  pl.BlockSpec((D, tf), lambda m, f: (0, f)),
                  pl.BlockSpec((tf, D), lambda m, f: (f, 0)),
                  pl.BlockSpec((tm, D), row),
                  pl.BlockSpec((None, 1, D), lambda m, f: (m // tpb, 0, 0)),
                  pl.BlockSpec((1, D), const), pl.BlockSpec((1, D), const)],
        out_specs=pl.BlockSpec((tm, D), row),
        scratch_shapes=[pltpu.VMEM((tm, D), F32)],
        compiler_params=_params(("parallel", "arbitrary")),
        name="ffn_ln",
    )(u2, wg, wu, wd, x1, g2, ln_w.reshape(1, -1), ln_b.reshape(1, -1))


MOE_TM = 1024
MOE_SUB = 512
MOE_CW = 256


def _row_copy(src, src_row, dst, dst_row, sem):
    return pltpu.make_async_copy(src.at[pl.ds(src_row, 1), :], dst.at[pl.ds(dst_row, 1), :], sem)


def _start_rows(n_rows, make_copy):
    def body(i, carry):
        base = pl.multiple_of(i * 8, 8)
        for j in range(8):
            make_copy(base, j).start(priority=j % 2)
        return carry
    lax.fori_loop(0, n_rows // 8, body, 0)


def _wait_rows(n_rows, one_copy):
    def body(i, carry):
        one_copy.wait()
        return carry
    lax.fori_loop(0, n_rows, body, 0, unroll=8)


def _dispatch_kernel(pos_ref, gap_ref, u_ref, out_ref, zbuf, sem, zsem):
    tm = u_ref.shape[0]
    first = pl.program_id(0) * tm

    def start(r, carry):
        for k in range(2):
            _row_copy(u_ref, r, out_ref, pos_ref[2 * (first + r) + k], sem).start(priority=k)
        return carry

    lax.fori_loop(0, tm, start, 0, unroll=8)

    @pl.when(pl.program_id(0) == pl.num_programs(0) - 1)
    def _():
        zbuf[...] = jnp.zeros_like(zbuf)
        n_gaps = gap_ref.shape[0] // 2

        def block_copy(i):
            return pltpu.make_async_copy(zbuf, out_ref.at[pl.ds(pl.multiple_of(i * 8, 8), 8), :], zsem.at[1])

        for g in range(n_gaps):
            lo, hi = gap_ref[2 * g], gap_ref[2 * g + 1]
            mid = jnp.minimum((lo + 7) // 8 * 8, hi)
            lax.fori_loop(lo, mid, lambda r, c: (_row_copy(zbuf, 0, out_ref, r, zsem.at[0]).start(), c)[1], 0)
            lax.fori_loop(mid // 8, hi // 8, lambda i, c: (block_copy(i).start(), c)[1], 0)
        for g in range(n_gaps):
            lo, hi = gap_ref[2 * g], gap_ref[2 * g + 1]
            mid = jnp.minimum((lo + 7) // 8 * 8, hi)
            lax.fori_loop(lo, mid, lambda r, c: (_row_copy(zbuf, 0, out_ref, 0, zsem.at[0]).wait(), c)[1], 0)
            lax.fori_loop(mid // 8, hi // 8, lambda i, c: (block_copy(0).wait(), c)[1], 0)

    _wait_rows(2 * tm, _row_copy(u_ref, 0, out_ref, 0, sem))


def _dispatch(u2, pos, gaps, n_rows):
    N, D = u2.shape
    tm = 256
    return pl.pallas_call(
        _dispatch_kernel,
        out_shape=jax.ShapeDtypeStruct((n_rows, D), u2.dtype),
        grid_spec=pltpu.PrefetchScalarGridSpec(
            num_scalar_prefetch=2,
            grid=(N // tm,),
            in_specs=[pl.BlockSpec((tm, D), lambda m, pos, gaps: (m, 0))],
            out_specs=pl.BlockSpec(memory_space=pl.ANY),
            scratch_shapes=[pltpu.VMEM((8, D), u2.dtype), pltpu.SemaphoreType.DMA(()),
                            pltpu.SemaphoreType.DMA((2,))]),
        compiler_params=_params(("arbitrary",)),
        name="moe_dispatch",
    )(pos, gaps, u2)


def _moe_ffn_kernel(te_ref, nv_ref, ns_ref, u_ref, wg_hbm, wu_hbm, wd_hbm, out_ref,
                    ub_ref, wgbuf, wubuf, wdbuf, wgb, wub, wdb, sem):
    t = pl.program_id(0)
    nv = nv_ref[0]
    n_sub = ns_ref[t]
    subs = [slice(s * MOE_SUB, (s + 1) * MOE_SUB) for s in range(MOE_TM // MOE_SUB)]
    F = wg_hbm.shape[2]
    n_chunks = F // MOE_CW

    def chunk_copies(tile, ci, slot):
        e = te_ref[tile]
        cols = pl.ds(ci * MOE_CW, MOE_CW)
        return (pltpu.make_async_copy(wg_hbm.at[e, :, cols], wgbuf.at[slot], sem.at[slot]),
                pltpu.make_async_copy(wu_hbm.at[e, :, cols], wubuf.at[slot], sem.at[slot]),
                pltpu.make_async_copy(wd_hbm.at[e, cols, :], wdbuf.at[slot], sem.at[slot]))

    @pl.when(t == 0)
    def _():
        for cp in chunk_copies(0, 0, 0):
            cp.start()

    @pl.when(t < nv)
    def _():
        ub_ref[...] = u_ref[...].astype(BF16)
        for s, rows in enumerate(subs):
            @pl.when(s >= n_sub)
            def _():
                out_ref[rows, :] = jnp.zeros((MOE_SUB, out_ref.shape[1]), F32)

        for ci in range(n_chunks):
            slot = (t * n_chunks + ci) % 2
            for cp in chunk_copies(t, ci, slot):
                cp.wait()
            if ci + 1 < n_chunks:
                for cp in chunk_copies(t, ci + 1, 1 - slot):
                    cp.start()
            else:
                @pl.when(t + 1 < nv)
                def _():
                    for cp in chunk_copies(t + 1, 0, 1 - slot):
                        cp.start()
            wgb[...] = wgbuf[slot].astype(BF16)
            wub[...] = wubuf[slot].astype(BF16)
            wdb[...] = wdbuf[slot].astype(BF16)
            for s, rows in enumerate(subs):
                @pl.when(s < n_sub)
                def _():
                    u = ub_ref[rows, :]
                    hg = jnp.dot(u, wgb[...], preferred_element_type=F32)
                    hu = jnp.dot(u, wub[...], preferred_element_type=F32)
                    h = (_silu(hg) * hu).astype(BF16)
                    part = jnp.dot(h, wdb[...], preferred_element_type=F32)
                    if ci == 0:
                        out_ref[rows, :] = part
                    else:
                        out_ref[rows, :] += part

    @pl.when(t >= nv)
    def _():
        out_ref[...] = jnp.zeros_like(out_ref)


def _moe_ffn(u_sorted, tile_expert, n_valid, n_sub, wg, wu, wd):
    R, D = u_sorted.shape
    tm = MOE_TM
    tile = lambda t, nv: jnp.minimum(t, nv[0] - 1)
    return pl.pallas_call(
        _moe_ffn_kernel,
        out_shape=jax.ShapeDtypeStruct((R, D), F32),
        grid_spec=pltpu.PrefetchScalarGridSpec(
            num_scalar_prefetch=3,
            grid=(R // tm,),
            in_specs=[pl.BlockSpec((tm, D), lambda t, te, nv, ns: (tile(t, nv), 0)),
                      pl.BlockSpec(memory_space=pl.ANY),
                      pl.BlockSpec(memory_space=pl.ANY),
                      pl.BlockSpec(memory_space=pl.ANY)],
            out_specs=pl.BlockSpec((tm, D), lambda t, te, nv, ns: (t, 0)),
            scratch_shapes=[pltpu.VMEM((tm, D), BF16),
                            pltpu.VMEM((2, D, MOE_CW), F32), pltpu.VMEM((2, D, MOE_CW), F32),
                            pltpu.VMEM((2, MOE_CW, D), F32),
                            pltpu.VMEM((D, MOE_CW), BF16), pltpu.VMEM((D, MOE_CW), BF16),
                            pltpu.VMEM((MOE_CW, D), BF16),
                            pltpu.SemaphoreType.DMA((2,))]),
        compiler_params=_params(("arbitrary",), vmem=60 * 1024 * 1024),
        name="moe_ffn",
    )(tile_expert, n_valid, n_sub, u_sorted, wg, wu, wd)


def _combine_kernel(pos_ref, y_ref, topw_ref, x_ref, g2_ref, lnw_ref, lnb_ref, out_ref, ybuf, sem):
    tm = x_ref.shape[0]
    m = pl.program_id(0)
    slot = m % 2

    def gather(step, buf):
        for k in range(2):
            _start_rows(tm, lambda base, j: _row_copy(y_ref, pos_ref[2 * (step * tm + base + j) + k],
                                                      ybuf.at[buf, k], base + j, sem.at[buf]))

    @pl.when(m == 0)
    def _():
        gather(0, 0)

    @pl.when(m + 1 < pl.num_programs(0))
    def _():
        gather(m + 1, 1 - slot)

    _wait_rows(2 * tm, _row_copy(y_ref, 0, ybuf.at[slot, 0], 0, sem.at[slot]))
    tw = topw_ref[...]
    y = tw[:, 0:1] * ybuf[slot, 0] + tw[:, 1:2] * ybuf[slot, 1]
    r = ALPHA * x_ref[...] + (1.0 + g2_ref[...]) * y
    out_ref[...] = _layer_norm_rows(r, lnw_ref[...], lnb_ref[...])


def _combine(y_sorted, pos, topw, x1, g2, ln_w, ln_b, seq):
    N, D = x1.shape
    tm = 256
    tpb = seq // tm
    row = lambda m, pos: (m, 0)
    const = lambda m, pos: (0, 0)
    return pl.pallas_call(
        _combine_kernel,
        out_shape=jax.ShapeDtypeStruct((N, D), F32),
        grid_spec=pltpu.PrefetchScalarGridSpec(
            num_scalar_prefetch=1,
            grid=(N // tm,),
            in_specs=[pl.BlockSpec(memory_space=pl.ANY),
                      pl.BlockSpec((tm, LANES), row),
                      pl.BlockSpec((tm, D), row),
                      pl.BlockSpec((None, 1, D), lambda m, pos: (m // tpb, 0, 0)),
                      pl.BlockSpec((1, D), const), pl.BlockSpec((1, D), const)],
            out_specs=pl.BlockSpec((tm, D), row),
            scratch_shapes=[pltpu.VMEM((2, 2, tm, D), F32), pltpu.SemaphoreType.DMA((2,))]),
        compiler_params=_params(("arbitrary",)),
        name="moe_combine_ln",
    )(pos, y_sorted, topw, x1, g2, ln_w.reshape(1, -1), ln_b.reshape(1, -1))


def _routing_plan(ids, n_tiles):
    e_flat = ids[:, :2].reshape(-1)
    onehot = (e_flat[:, None] == jnp.arange(N_EXPERTS, dtype=jnp.int32)[None, :]).astype(jnp.int32)
    csum = jnp.cumsum(onehot, axis=0)
    rank = jnp.sum((csum - onehot) * onehot, axis=1)
    counts = csum[-1]
    padded = ((counts + MOE_TM - 1) // MOE_TM) * MOE_TM
    ends = jnp.cumsum(padded)
    offs = ends - padded
    pos = (jnp.sum(onehot * offs[None, :], axis=1) + rank).astype(jnp.int32)
    tile_start = jnp.arange(n_tiles, dtype=jnp.int32) * MOE_TM
    n_valid = (ends[-1] // MOE_TM).astype(jnp.int32)
    tile_start = jnp.minimum(tile_start, (n_valid - 1) * MOE_TM)
    tile_expert = jnp.sum((tile_start[:, None] >= ends[None, :]).astype(jnp.int32), axis=1).astype(jnp.int32)
    used = jnp.clip((offs + counts)[tile_expert] - tile_start, 0, MOE_TM)
    n_sub = ((used + MOE_SUB - 1) // MOE_SUB).astype(jnp.int32)
    total = jnp.full((1,), n_tiles * MOE_TM, jnp.int32)
    gaps = jnp.stack([jnp.concatenate([offs + counts, ends[-1:]]),
                      jnp.concatenate([ends, total])], axis=1).reshape(-1).astype(jnp.int32)
    return pos, tile_expert, n_valid.reshape(1), n_sub, gaps


def _moe(u2, ids, topw, wg, wu, wd, x1, g2, ln_w, ln_b, seq):
    N = u2.shape[0]
    n_tiles = -(-2 * N // MOE_TM) + N_EXPERTS
    pos, tile_expert, n_valid, n_sub, gaps = _routing_plan(ids, n_tiles)
    u_sorted = _dispatch(u2, pos, gaps, n_tiles * MOE_TM)
    y_sorted = _moe_ffn(u_sorted, tile_expert, n_valid, n_sub, wg, wu, wd)
    return _combine(y_sorted, pos, topw, x1, g2, ln_w, ln_b, seq)


def _split_w_in_kernel(wt_ref, gt_ref, main_ref, gate_ref):
    main_ref[...] = wt_ref[0].T.astype(BF16)

    @pl.when(pl.program_id(1) == 0)
    def _():
        g = gt_ref[...]
        g = jnp.concatenate([g, jnp.zeros((LANES - g.shape[0], g.shape[1]), F32)], axis=0)
        gate_ref[...] = g.T.astype(BF16)


def _split_w_in(w_in):
    L, D, C = w_in.shape
    w_t = jnp.swapaxes(w_in, 1, 2)
    g0 = 4 * GROUP_WIDTH
    ng = 2 * N_HEADS
    tc = 512
    src_col = lambda j: (j * (tc // ng) + jnp.where(j * tc >= g0, 1, 0)) * ng
    return pl.pallas_call(
        _split_w_in_kernel,
        out_shape=(jax.ShapeDtypeStruct((L, D, C - ng), BF16),
                   jax.ShapeDtypeStruct((L, D, LANES), BF16)),
        grid=(L, (C - ng) // tc),
        in_specs=[pl.BlockSpec((pl.Element(1), pl.Element(tc), pl.Element(D)),
                               lambda l, j: (l, src_col(j), 0)),
                  pl.BlockSpec((None, ng, D), lambda l, j: (l, g0 // ng, 0))],
        out_specs=(pl.BlockSpec((None, D, tc), lambda l, j: (l, 0, j)),
                   pl.BlockSpec((None, D, LANES), lambda l, j: (l, 0, 0))),
        compiler_params=_params(("parallel", "arbitrary")),
        name="split_w_in",
    )(w_t, w_t)


def kernel(x, c, w_in, mlstm_conv_w, mlstm_conv_b, mlstm_gate_b, mlstm_norm_w, pool_w, pool_scale, conv_dw_w, conv_dw_b, conv_ln_w, conv_ln_b, conv_pw_w, conv_pw_b, group_norm_w, w_out, ada_w, ada_b, ln1_w, ln1_b, ln2_w, ln2_b, ffn_w_gate, ffn_w_up, ffn_w_down, moe_router_w, moe_router_b, moe_w_gate, moe_w_up, moe_w_down):
    B, S, D = x.shape
    GW = GROUP_WIDTH
    ada = _ada_all(c, ada_w, ada_b)
    x2d = x.reshape(B * S, D)
    w_main, w_gate = _split_w_in(w_in)
    for l in range(DEPTH):
        mod = [ada[l, :, i * D:(i + 1) * D].reshape(B, 1, D) for i in range(6)]
        sh1, sc1, g1, sh2, sc2, g2 = mod
        proj, gates = _in_proj(x2d, sc1, sh1, w_main, w_gate, l, S)
        proj = proj.reshape(B, S, -1)
        gates = gates.reshape(B, S, LANES)
        gn_b, gn_c, gn_d = (group_norm_w[l, i * GW:(i + 1) * GW] for i in range(3))
        ya = _mlstm_mixer(proj, gates, mlstm_conv_w[l], mlstm_conv_b[l], mlstm_gate_b[l],
                          mlstm_norm_w[l], B, S)
        yb = _pool_mixer(proj, pool_w[l], pool_scale[l], gn_b, B, S)
        yc = _sb_mixer(proj, B, S)
        yd = _conv_mixer(proj, conv_dw_w[l], conv_dw_b[l], conv_ln_w[l], conv_ln_b[l],
                         conv_pw_w[l], conv_pw_b[l], gn_d, B, S)
        flat = lambda t: t.reshape(B * S, GW)
        j = l // 2
        router = None
        if l % 2 == 1:
            rw = jnp.zeros((D, LANES), F32).at[:, :N_EXPERTS].set(moe_router_w[j])
            rw_hi = rw.astype(BF16)
            rw_lo = (rw - rw_hi.astype(F32)).astype(BF16)
            rb = jnp.zeros((1, LANES), F32).at[0, :N_EXPERTS].set(moe_router_b[j])
            router = (jnp.concatenate([rw_hi, rw_lo], axis=1), rb)
        outs = _out_proj(flat(ya), flat(yb), flat(yc), flat(yd), gn_c, w_out[l].astype(BF16), x2d,
                         g1, ln1_w[l], ln1_b[l], sc2, sh2, S, router)
        if l % 2 == 0:
            x1, u2 = outs
            x2d = _ffn(u2, ffn_w_gate[j].astype(BF16), ffn_w_up[j].astype(BF16),
                       ffn_w_down[j].astype(BF16), x1, g2, ln2_w[l], ln2_b[l], S)
        else:
            x1, u2, ids, topw = outs
            x2d = _moe(u2, ids, topw, moe_w_gate[j], moe_w_up[j], moe_w_down[j], x1, g2,
                       ln2_w[l], ln2_b[l], S)
    return x2d.reshape(B, S, D)
```

```python
import functools

import jax
import jax.numpy as jnp
from jax import lax
from jax.experimental import pallas as pl
from jax.experimental.pallas import tpu as pltpu

F32 = jnp.float32
BF16 = jnp.bfloat16

DEPTH = 2
CHUNK = 256
GROUP_WIDTH = 512
N_HEADS = 4
HEAD_DIM = 128
MLSTM_CONV = 4
POOL_WINDOWS = (2, 4, 8, 16)
CONV_WIDTH = 31
SB_KEY_BLOCK = 256
N_EXPERTS = 8
ALPHA = (2.0 * DEPTH) ** 0.25
EPS = 1e-5
LOG2E = 1.4426950408889634
LANES = 128
VMEM_LIMIT = 56 * 1024 * 1024


def _params(sem, vmem=VMEM_LIMIT):
    return pltpu.CompilerParams(dimension_semantics=sem, vmem_limit_bytes=vmem)


def _silu(x):
    return x * (1.0 / (1.0 + jnp.exp(-x)))


def _sigmoid(x):
    return 1.0 / (1.0 + jnp.exp(-x))


def _neg_softplus(x):
    return -(jnp.maximum(x, 0.0) + jnp.log(1.0 + jnp.exp(-jnp.abs(x))))


def _layer_norm_rows(r, w, b):
    mu = jnp.mean(r, axis=-1, keepdims=True)
    d = r - mu
    var = jnp.mean(d * d, axis=-1, keepdims=True)
    return d * lax.rsqrt(var + EPS) * w + b


def _rms_rows(y, w):
    return y * lax.rsqrt(jnp.mean(y * y, axis=-1, keepdims=True) + EPS) * w


def _bdot(a, b):
    return jnp.dot(a.astype(BF16), b.astype(BF16), preferred_element_type=F32)


def _ada_kernel(c_ref, w_ref, b_ref, out_ref):
    out_ref[...] = _bdot(_silu(c_ref[...]), w_ref[...]) + b_ref[...]


def _ada_all(c, ada_w, ada_b):
    L, D, D6 = ada_w.shape
    Bn = c.shape[0]
    tn = 1024
    return pl.pallas_call(
        _ada_kernel,
        out_shape=jax.ShapeDtypeStruct((L, Bn, D6), F32),
        grid=(L, D6 // tn),
        in_specs=[pl.BlockSpec((Bn, D), lambda l, n: (0, 0)),
                  pl.BlockSpec((None, D, tn), lambda l, n: (l, 0, n)),
                  pl.BlockSpec((None, 1, tn), lambda l, n: (l, 0, n))],
        out_specs=pl.BlockSpec((None, Bn, tn), lambda l, n: (l, 0, n)),
        compiler_params=_params(("parallel", "parallel")),
        name="ada_mod",
    )(c, ada_w, ada_b.reshape(L, 1, D6))


def _inproj_kernel(x_ref, sc_ref, sh_ref, w_ref, wg_ref, proj_ref, gates_ref, u_ref):
    @pl.when(pl.program_id(1) == 0)
    def _():
        u = (x_ref[...] * (1.0 + sc_ref[...]) + sh_ref[...]).astype(BF16)
        u_ref[...] = u
        gates_ref[...] = jnp.dot(u, wg_ref[...], preferred_element_type=F32)

    proj_ref[...] = jnp.dot(u_ref[...], w_ref[...], preferred_element_type=F32)


def _in_proj(x2d, sc, sh, w_main, w_gate, layer, seq):
    N, D = x2d.shape
    NC = w_main.shape[2]
    tm, tn = 1024, 1280
    tpb = seq // tm
    return pl.pallas_call(
        _inproj_kernel,
        out_shape=(jax.ShapeDtypeStruct((N, NC), F32), jax.ShapeDtypeStruct((N, LANES), F32)),
        grid=(N // tm, NC // tn),
        in_specs=[pl.BlockSpec((tm, D), lambda m, n: (m, 0)),
                  pl.BlockSpec((None, 1, D), lambda m, n: (m // tpb, 0, 0)),
                  pl.BlockSpec((None, 1, D), lambda m, n: (m // tpb, 0, 0)),
                  pl.BlockSpec((None, D, tn), lambda m, n: (layer, 0, n)),
                  pl.BlockSpec((None, D, LANES), lambda m, n: (layer, 0, 0))],
        out_specs=(pl.BlockSpec((tm, tn), lambda m, n: (m, n)),
                   pl.BlockSpec((tm, LANES), lambda m, n: (m, 0))),
        scratch_shapes=[pltpu.VMEM((tm, D), BF16)],
        compiler_params=_params(("parallel", "arbitrary")),
        name="in_proj",
    )(x2d, sc, sh, w_main, w_gate)


def _cumsum_rows(x):
    n = x.shape[0]
    row = lax.broadcasted_iota(jnp.int32, x.shape, 0)
    k = 1
    while k < n:
        x = x + jnp.where(row >= k, pltpu.roll(x, k, axis=0), 0.0)
        k *= 2
    return x


def _mlstm_kernel(qk_ref, v_ref, o_ref, g_ref, cw_ref, cb_ref, gb_ref, nw_ref, out_ref,
                  c_state, n_state, m_state, tail_ref):
    @pl.when(pl.program_id(1) == 0)
    def _():
        c_state[...] = jnp.zeros_like(c_state)
        n_state[...] = jnp.zeros_like(n_state)
        m_state[...] = jnp.zeros_like(m_state)
        tail_ref[...] = jnp.zeros_like(tail_ref)

    for bb in range(qk_ref.shape[0]):
        _mlstm_block(qk_ref.at[bb], v_ref.at[bb], o_ref.at[bb], g_ref.at[bb], cw_ref, cb_ref, gb_ref,
                     nw_ref, out_ref.at[bb], c_state.at[bb], n_state.at[bb], m_state.at[bb],
                     tail_ref.at[bb])


def _mlstm_block(qk_ref, v_ref, o_ref, g_ref, cw_ref, cb_ref, gb_ref, nw_ref, out_ref,
                 c_state, n_state, m_state, tail_ref):
    T = qk_ref.shape[0]
    GW = GROUP_WIDTH

    xx = jnp.concatenate([tail_ref[...], qk_ref[...]], axis=0)
    tail_ref[...] = qk_ref[T - 8:T, :]
    conv = cb_ref[...]
    for j in range(MLSTM_CONV):
        off = 8 - (MLSTM_CONV - 1) + j
        conv = conv + cw_ref[j:j + 1, :] * xx[off:off + T, :]
    qk = _silu(conv)

    gates = g_ref[...] + gb_ref[...]
    logf = _neg_softplus(-gates)
    causal = (lax.broadcasted_iota(jnp.int32, (CHUNK, CHUNK), 1)
              <= lax.broadcasted_iota(jnp.int32, (CHUNK, CHUNK), 0))

    for c in range(T // CHUNK):
        rows = slice(c * CHUNK, (c + 1) * CHUNK)
        gi = gates[rows, :]
        bcum = _cumsum_rows(logf[rows, :])
        gi_t = gi.T
        bcum_t = bcum.T
        for h in range(N_HEADS):
            cols = slice(h * HEAD_DIM, (h + 1) * HEAD_DIM)
            q = qk[rows, cols] * (HEAD_DIM ** -0.5)
            k = qk[rows, GW + h * HEAD_DIM:GW + (h + 1) * HEAD_DIM]
            v = v_ref[rows, cols]
            b_col = bcum[:, N_HEADS + h:N_HEADS + h + 1]
            b_row = bcum_t[N_HEADS + h:N_HEADS + h + 1, :]
            i_col = gi[:, h:h + 1]
            i_row = gi_t[h:h + 1, :]
            g_tot = b_col[CHUNK - 1:CHUNK, :]
            m_prev = m_state[h:h + 1, 0:1]
            c_prev = c_state[h]
            n_prev = n_state[h:h + 1, :]

            log_d = jnp.where(causal, b_col - b_row + i_row, -jnp.inf)
            m_inter = b_col + m_prev
            m_t = jnp.maximum(m_inter, jnp.max(log_d, axis=-1, keepdims=True))
            s = lax.dot_general(q.astype(BF16), k.astype(BF16), (((1,), (1,)), ((), ())),
                                preferred_element_type=F32)
            w = jnp.exp(log_d - m_t) * s
            inter = jnp.exp(m_inter - m_t)
            num = inter * _bdot(q, c_prev) + _bdot(w, v)
            den = inter * jnp.sum(q * n_prev, axis=-1, keepdims=True) + jnp.sum(w, axis=-1, keepdims=True)
            hcur = num / jnp.maximum(jnp.abs(den), jnp.exp(-m_t))

            a_col = g_tot - b_col + i_col
            m_new = jnp.maximum(g_tot + m_prev, jnp.max(a_col, axis=0, keepdims=True))
            decay = jnp.exp(g_tot + m_prev - m_new)
            kw = k * jnp.exp(a_col - m_new)
            c_state[h] = decay * c_prev + _bdot(kw.T, v)
            n_state[h:h + 1, :] = decay * n_prev + jnp.sum(kw, axis=0, keepdims=True)
            m_state[h:h + 1, :] = jnp.broadcast_to(m_new, (1, LANES))

            mu = jnp.mean(hcur, axis=-1, keepdims=True)
            dlt = hcur - mu
            var = jnp.mean(dlt * dlt, axis=-1, keepdims=True)
            hn = dlt * lax.rsqrt(var + EPS) * nw_ref[:, cols]
            out_ref[rows, cols] = (_sigmoid(o_ref[rows, cols]) * hn).astype(out_ref.dtype)


def _mlstm_mixer(proj, gates, conv_w, conv_b, gate_b, norm_w, B, S):
    T = 256
    NB = 2
    GW = GROUP_WIDTH
    gb = jnp.zeros((1, LANES), F32).at[0, :2 * N_HEADS].set(gate_b)
    return pl.pallas_call(
        _mlstm_kernel,
        out_shape=jax.ShapeDtypeStruct((B, S, GW), BF16),
        grid=(B // NB, S // T),
        in_specs=[pl.BlockSpec((NB, T, 2 * GW), lambda b, t: (b, t, 0)),
                  pl.BlockSpec((NB, T, GW), lambda b, t: (b, t, 2)),
                  pl.BlockSpec((NB, T, GW), lambda b, t: (b, t, 3)),
                  pl.BlockSpec((NB, T, LANES), lambda b, t: (b, t, 0)),
                  pl.BlockSpec((MLSTM_CONV, 2 * GW), lambda b, t: (0, 0)),
                  pl.BlockSpec((1, 2 * GW), lambda b, t: (0, 0)),
                  pl.BlockSpec((1, LANES), lambda b, t: (0, 0)),
                  pl.BlockSpec((1, GW), lambda b, t: (0, 0))],
        out_specs=pl.BlockSpec((NB, T, GW), lambda b, t: (b, t, 0)),
        scratch_shapes=[pltpu.VMEM((NB, N_HEADS, HEAD_DIM, HEAD_DIM), F32),
                        pltpu.VMEM((NB, 8, LANES), F32),
                        pltpu.VMEM((NB, 8, LANES), F32),
                        pltpu.VMEM((NB, 8, 2 * GW), F32)],
        compiler_params=_params(("parallel", "arbitrary")),
        name="mlstm_mixer",
    )(proj, proj, proj, gates, conv_w, conv_b.reshape(1, -1), gb, norm_w.reshape(1, -1))


def _pool_kernel(x_ref, w_ref, scale_ref, gn_ref, out_ref, tail_ref):
    T = x_ref.shape[0]
    HALO = 16
    t_blk = pl.program_id(1)

    @pl.when(t_blk == 0)
    def _():
        tail_ref[...] = jnp.zeros_like(tail_ref)

    x = x_ref[...]
    xx = jnp.concatenate([tail_ref[...], x], axis=0)
    tail_ref[...] = x[T - HALO:T, :]
    pos = t_blk * T + lax.broadcasted_iota(jnp.int32, (T, 1), 0) + 1
    outs = []
    for g, win in enumerate(POOL_WINDOWS):
        cols = slice(g * LANES, (g + 1) * LANES)
        s = xx[:, cols]
        k = 1
        while k < win:
            s = s + pltpu.roll(s, k, axis=0)
            k *= 2
        cnt = jnp.minimum(pos, win).astype(F32)
        yg = s[HALO:, :] / cnt - x[:, cols]
        outs.append(_bdot(yg, w_ref[g]))
    y = jnp.concatenate(outs, axis=-1) * scale_ref[...]
    out_ref[...] = _rms_rows(y, gn_ref[...]).astype(out_ref.dtype)


def _pool_mixer(proj, pool_w, pool_scale, gn, B, S):
    T = 512
    GW = GROUP_WIDTH
    return pl.pallas_call(
        _pool_kernel,
        out_shape=jax.ShapeDtypeStruct((B, S, GW), BF16),
        grid=(B, S // T),
        in_specs=[pl.BlockSpec((None, T, GW), lambda b, t: (b, t, 4)),
                  pl.BlockSpec((len(POOL_WINDOWS), LANES, LANES), lambda b, t: (0, 0, 0)),
                  pl.BlockSpec((1, GW), lambda b, t: (0, 0)),
                  pl.BlockSpec((1, GW), lambda b, t: (0, 0))],
        out_specs=pl.BlockSpec((None, T, GW), lambda b, t: (b, t, 0)),
        scratch_shapes=[pltpu.VMEM((16, GW), F32)],
        compiler_params=_params(("parallel", "arbitrary")),
        name="pool_mixer",
    )(proj, pool_w.astype(BF16), pool_scale.reshape(1, -1), gn.reshape(1, -1))


def _sb_kernel(q_ref, k_ref, v_ref, out_ref, kb_ref, vb_ref):
    LQ = q_ref.shape[0]
    LK = SB_KEY_BLOCK
    ratio = LQ // LK
    qi = pl.program_id(1)

    @pl.when(qi == 0)
    def _():
        kb_ref[...] = k_ref[...].astype(BF16)
        vb_ref[...] = v_ref[...].astype(BF16)

    scale2 = HEAD_DIM ** -0.5 * LOG2E
    tri = (lax.broadcasted_iota(jnp.int32, (LK, LK), 0)
           > lax.broadcasted_iota(jnp.int32, (LK, LK), 1)).astype(BF16)
    row = lax.broadcasted_iota(jnp.int32, (LQ, LK), 0)
    col = lax.broadcasted_iota(jnp.int32, (LQ, LK), 1)
    heads = [slice(h * HEAD_DIM, (h + 1) * HEAD_DIM) for h in range(N_HEADS)]
    qs = [(q_ref[:, hs] * scale2).astype(BF16) for hs in heads]

    def sweep(kb, carry, strict):
        start = pl.multiple_of(kb * LK, LK)
        zs, lks = [], []
        for h, hs in enumerate(heads):
            k_blk = kb_ref[pl.ds(start, LK), hs]
            z = lax.dot_general(qs[h], k_blk, (((1,), (1,)), ((), ())), preferred_element_type=F32)
            lk = -(jnp.maximum(z, 0.0) + jnp.log2(1.0 + jnp.exp2(-jnp.abs(z))))
            if strict is not None:
                lk = jnp.where(strict, lk, 0.0)
            zs.append(z)
            lks.append(lk)
        inner = jnp.dot(jnp.concatenate([lk.astype(BF16) for lk in lks], axis=0), tri,
                        preferred_element_type=F32)
        new = []
        for h, hs in enumerate(heads):
            after, acc = carry[h]
            v_blk = vb_ref[pl.ds(start, LK), hs]
            a = jnp.exp2(zs[h] + lks[h] + inner[h * LQ:(h + 1) * LQ, :] + after)
            if strict is not None:
                a = jnp.where(strict, a, 0.0)
            acc = acc + jnp.dot(a.astype(BF16), v_blk, preferred_element_type=F32)
            after = after + jnp.sum(lks[h], axis=-1, keepdims=True)
            new.append((after, acc))
        return tuple(new)

    init = tuple((jnp.zeros((LQ, 1), F32), jnp.zeros((LQ, HEAD_DIM), F32)) for _ in heads)
    carry = init
    for j in reversed(range(ratio)):
        carry = sweep(ratio * qi + j, carry, (j * LK + col) < row)
    carry = lax.fori_loop(0, ratio * qi, lambda it, c: sweep(ratio * qi - 1 - it, c, None), carry)
    for h, hs in enumerate(heads):
        out_ref[:, hs] = carry[h][1]


def _sb_mixer(proj, B, S):
    LQ = 512
    GW = GROUP_WIDTH
    return pl.pallas_call(
        _sb_kernel,
        out_shape=jax.ShapeDtypeStruct((B, S, GW), F32),
        grid=(B, S // LQ),
        in_specs=[pl.BlockSpec((None, LQ, GW), lambda b, i: (b, i, 5)),
                  pl.BlockSpec((None, S, GW), lambda b, i: (b, 0, 6)),
                  pl.BlockSpec((None, S, GW), lambda b, i: (b, 0, 7))],
        out_specs=pl.BlockSpec((None, LQ, GW), lambda b, i: (b, i, 0)),
        scratch_shapes=[pltpu.VMEM((S, GW), BF16), pltpu.VMEM((S, GW), BF16)],
        compiler_params=_params(("parallel", "arbitrary")),
        name="sb_attention",
    )(proj, proj, proj)


def _conv_kernel(x_ref, dw_ref, dwb_ref, lnw_ref, lnb_ref, pw_ref, pwb_ref, gn_ref, out_ref,
                 hbuf, ybuf):
    T = x_ref.shape[0]
    GW = GROUP_WIDTH
    HALO = 32
    SUB = 64

    @pl.when(pl.program_id(1) == 0)
    def _():
        hbuf[0:HALO, :] = jnp.zeros((HALO, GW), F32)

    @pl.when(pl.program_id(1) > 0)
    def _():
        hbuf[0:HALO, :] = hbuf[T:T + HALO, :]

    hbuf[HALO:HALO + T, :] = x_ref[:, 0:GW] * _sigmoid(x_ref[:, GW:2 * GW])

    def sub(i, carry):
        r0 = pl.multiple_of(i * SUB, SUB)
        acc = jnp.broadcast_to(dwb_ref[...], (SUB, GW))
        win = hbuf[pl.ds(r0, SUB + HALO), :]
        for res in range(8):
            shifted = win if res == 0 else pltpu.roll(win, SUB + HALO - res, axis=0)
            for j in range(CONV_WIDTH):
                off = HALO - CONV_WIDTH + 1 + j
                if off % 8 == res:
                    base = off - res
                    acc = acc + dw_ref[j:j + 1, :] * shifted[base:base + SUB, :]
        ybuf[pl.ds(r0, SUB), :] = acc
        return carry

    lax.fori_loop(0, T // SUB, sub, 0)
    hn = _silu(_layer_norm_rows(ybuf[...], lnw_ref[...], lnb_ref[...]))
    y = _bdot(hn, pw_ref[...]) + pwb_ref[...]
    out_ref[...] = _rms_rows(y, gn_ref[...]).astype(out_ref.dtype)


def _conv_mixer(proj, dw_w, dw_b, ln_w, ln_b, pw_w, pw_b, gn, B, S):
    T = 256
    GW = GROUP_WIDTH
    dw_pad = jnp.zeros((32, GW), F32).at[:CONV_WIDTH].set(dw_w)
    r = lambda a: a.reshape(1, -1)
    return pl.pallas_call(
        _conv_kernel,
        out_shape=jax.ShapeDtypeStruct((B, S, GW), BF16),
        grid=(B, S // T),
        in_specs=[pl.BlockSpec((None, T, 2 * GW), lambda b, t: (b, t, 4)),
                  pl.BlockSpec((32, GW), lambda b, t: (0, 0)),
                  pl.BlockSpec((1, GW), lambda b, t: (0, 0)),
                  pl.BlockSpec((1, GW), lambda b, t: (0, 0)),
                  pl.BlockSpec((1, GW), lambda b, t: (0, 0)),
                  pl.BlockSpec((GW, GW), lambda b, t: (0, 0)),
                  pl.BlockSpec((1, GW), lambda b, t: (0, 0)),
                  pl.BlockSpec((1, GW), lambda b, t: (0, 0))],
        out_specs=pl.BlockSpec((None, T, GW), lambda b, t: (b, t, 0)),
        scratch_shapes=[pltpu.VMEM((T + 32, GW), F32), pltpu.VMEM((T, GW), F32)],
        compiler_params=_params(("parallel", "arbitrary")),
        name="conv_mixer",
    )(proj, dw_pad, r(dw_b), r(ln_w), r(ln_b), pw_w.astype(BF16), r(pw_b), r(gn))


def _outproj_kernel(with_router, ya_ref, yb_ref, yc_ref, yd_ref, gnc_ref, w_ref, x_ref, g1_ref,
                    lnw_ref, lnb_ref, sc_ref, sh_ref, *rest):
    GW = GROUP_WIDTH
    if with_router:
        rw_ref, rb_ref, x1_ref, u2_ref, ids_ref, topw_ref = rest
    else:
        x1_ref, u2_ref = rest
    yc = _rms_rows(yc_ref[...], gnc_ref[...]).astype(BF16)
    acc = jnp.dot(ya_ref[...], w_ref[0:GW, :], preferred_element_type=F32)
    acc = acc + jnp.dot(yb_ref[...], w_ref[GW:2 * GW, :], preferred_element_type=F32)
    acc = acc + jnp.dot(yc, w_ref[2 * GW:3 * GW, :], preferred_element_type=F32)
    acc = acc + jnp.dot(yd_ref[...], w_ref[3 * GW:4 * GW, :], preferred_element_type=F32)
    r = ALPHA * x_ref[...] + (1.0 + g1_ref[...]) * acc
    x1 = _layer_norm_rows(r, lnw_ref[...], lnb_ref[...])
    x1_ref[...] = x1
    u2 = x1 * (1.0 + sc_ref[...]) + sh_ref[...]
    u2_ref[...] = u2.astype(u2_ref.dtype)
    if with_router:
        u_hi = u2.astype(BF16)
        u_lo = (u2 - u_hi.astype(F32)).astype(BF16)
        p = jnp.dot(u_hi, rw_ref[...], preferred_element_type=F32)
        logits = (p[:, :LANES] + p[:, LANES:]
                  + jnp.dot(u_lo, rw_ref[:, :LANES], preferred_element_type=F32) + rb_ref[...])
        lane = lax.broadcasted_iota(jnp.int32, logits.shape, 1)
        lg = jnp.where(lane < N_EXPERTS, logits, -jnp.inf)
        m1 = jnp.max(lg, axis=-1, keepdims=True)
        i1 = jnp.min(jnp.where(lg == m1, lane, LANES), axis=-1, keepdims=True)
        lg2 = jnp.where(lane == i1, -jnp.inf, lg)
        m2 = jnp.max(lg2, axis=-1, keepdims=True)
        i2 = jnp.min(jnp.where(lg2 == m2, lane, LANES), axis=-1, keepdims=True)
        e2 = jnp.exp(m2 - m1)
        w1 = 1.0 / (1.0 + e2)
        w2 = e2 / (1.0 + e2)
        ids_ref[...] = jnp.where(lane == 0, i1, jnp.where(lane == 1, i2, 0))
        topw_ref[...] = jnp.where(lane == 0, w1, jnp.where(lane == 1, w2, 0.0))


def _out_proj(ya, yb, yc, yd, gnc, w_out, x2d, g1, ln_w, ln_b, sc2, sh2, seq, router=None):
    N, D = x2d.shape
    GW = GROUP_WIDTH
    tm = 512
    tpb = seq // tm
    row = lambda m: (m, 0)
    const = lambda m: (0, 0)
    perb = lambda m: (m // tpb, 0, 0)
    in_specs = [pl.BlockSpec((tm, GW), row), pl.BlockSpec((tm, GW), row),
                pl.BlockSpec((tm, GW), row), pl.BlockSpec((tm, GW), row),
                pl.BlockSpec((1, GW), const),
                pl.BlockSpec((D, D), const),
                pl.BlockSpec((tm, D), row),
                pl.BlockSpec((None, 1, D), perb),
                pl.BlockSpec((1, D), const), pl.BlockSpec((1, D), const),
                pl.BlockSpec((None, 1, D), perb), pl.BlockSpec((None, 1, D), perb)]
    args = [ya, yb, yc, yd, gnc.reshape(1, -1), w_out, x2d, g1, ln_w.reshape(1, -1),
            ln_b.reshape(1, -1), sc2, sh2]
    out_shape = [jax.ShapeDtypeStruct((N, D), F32),
                 jax.ShapeDtypeStruct((N, D), BF16 if router is None else F32)]
    out_specs = [pl.BlockSpec((tm, D), row), pl.BlockSpec((tm, D), row)]
    if router is not None:
        rw, rb = router
        in_specs += [pl.BlockSpec((D, 2 * LANES), const), pl.BlockSpec((1, LANES), const)]
        args += [rw, rb]
        out_shape += [jax.ShapeDtypeStruct((N, LANES), jnp.int32), jax.ShapeDtypeStruct((N, LANES), F32)]
        out_specs += [pl.BlockSpec((tm, LANES), row), pl.BlockSpec((tm, LANES), row)]
    return pl.pallas_call(
        functools.partial(_outproj_kernel, router is not None),
        out_shape=tuple(out_shape),
        grid=(N // tm,),
        in_specs=in_specs,
        out_specs=tuple(out_specs),
        compiler_params=_params(("parallel",)),
        name="out_proj_ln",
    )(*args)


def _ffn_kernel(u_ref, wg_ref, wu_ref, wd_ref, x_ref, g2_ref, lnw_ref, lnb_ref, out_ref, acc_ref):
    f = pl.program_id(1)

    @pl.when(f == 0)
    def _():
        acc_ref[...] = jnp.zeros_like(acc_ref)

    u = u_ref[...]
    hg = jnp.dot(u, wg_ref[...], preferred_element_type=F32)
    hu = jnp.dot(u, wu_ref[...], preferred_element_type=F32)
    h = _silu(hg) * hu
    acc_ref[...] += jnp.dot(h.astype(BF16), wd_ref[...], preferred_element_type=F32)

    @pl.when(f == pl.num_programs(1) - 1)
    def _():
        r = ALPHA * x_ref[...] + (1.0 + g2_ref[...]) * acc_ref[...]
        out_ref[...] = _layer_norm_rows(r, lnw_ref[...], lnb_ref[...])


def _ffn(u2, wg, wu, wd, x1, g2, ln_w, ln_b, seq):
    N, D = u2.shape
    F = wg.shape[1]
    tm, tf = 512, 512
    tpb = seq // tm
    row = lambda m, f: (m, 0)
    const = lambda m, f: (0, 0)
    return pl.pallas_call(
        _ffn_kernel,
        out_shape=jax.ShapeDtypeStruct((N, D), F32),
        grid=(N // tm, F // tf),
        in_specs=[pl.BlockSpec((tm, D), row),
                  pl.BlockSpec((D, tf), lambda m, f: (0, f)),
                  pl.BlockSpec((D, tf), lambda m, f: (0, f)),
                  pl.BlockSpec((tf, D), lambda m, f: (f, 0)),
                  pl.BlockSpec((tm, D), row),
                  pl.BlockSpec((None, 1, D), lambda m, f: (m // tpb, 0, 0)),
                  pl.BlockSpec((1, D), const), pl.BlockSpec((1, D), const)],
        out_specs=pl.BlockSpec((tm, D), row),
        scratch_shapes=[pltpu.VMEM((tm, D), F32)],
        compiler_params=_params(("parallel", "arbitrary")),
        name="ffn_ln",
    )(u2, wg, wu, wd, x1, g2, ln_w.reshape(1, -1), ln_b.reshape(1, -1))


MOE_TM = 768
MOE_CW = 256


def _row_copy(src, src_row, dst, dst_row, sem):
    return pltpu.make_async_copy(src.at[pl.ds(src_row, 1), :], dst.at[pl.ds(dst_row, 1), :], sem)


def _start_rows(n_rows, make_copy):
    def body(i, carry):
        base = pl.multiple_of(i * 8, 8)
        for j in range(8):
            make_copy(base, j).start(priority=j % 2)
        return carry
    lax.fori_loop(0, n_rows // 8, body, 0)


def _wait_rows(n_rows, one_copy):
    def body(i, carry):
        one_copy.wait()
        return carry
    lax.fori_loop(0, n_rows, body, 0, unroll=8)


def _dispatch_kernel(pos_ref, gap_ref, u_ref, out_ref, zbuf, sem, zsem):
    tm = u_ref.shape[0]
    first = pl.program_id(0) * tm

    def start(r, carry):
        for k in range(2):
            _row_copy(u_ref, r, out_ref, pos_ref[2 * (first + r) + k], sem).start(priority=k)
        return carry

    lax.fori_loop(0, tm, start, 0, unroll=8)

    @pl.when(pl.program_id(0) == pl.num_programs(0) - 1)
    def _():
        zbuf[...] = jnp.zeros_like(zbuf)
        n_gaps = gap_ref.shape[0] // 2

        def block_copy(i):
            return pltpu.make_async_copy(zbuf, out_ref.at[pl.ds(pl.multiple_of(i * 8, 8), 8), :], zsem.at[1])

        for g in range(n_gaps):
            lo, hi = gap_ref[2 * g], gap_ref[2 * g + 1]
            mid = jnp.minimum((lo + 7) // 8 * 8, hi)
            lax.fori_loop(lo, mid, lambda r, c: (_row_copy(zbuf, 0, out_ref, r, zsem.at[0]).start(), c)[1], 0)
            lax.fori_loop(mid // 8, hi // 8, lambda i, c: (block_copy(i).start(), c)[1], 0)
        for g in range(n_gaps):
            lo, hi = gap_ref[2 * g], gap_ref[2 * g + 1]
            mid = jnp.minimum((lo + 7) // 8 * 8, hi)
            lax.fori_loop(lo, mid, lambda r, c: (_row_copy(zbuf, 0, out_ref, 0, zsem.at[0]).wait(), c)[1], 0)
            lax.fori_loop(mid // 8, hi // 8, lambda i, c: (block_copy(0).wait(), c)[1], 0)

    _wait_rows(2 * tm, _row_copy(u_ref, 0, out_ref, 0, sem))


def _dispatch(u2, pos, gaps, n_rows):
    N, D = u2.shape
    tm = 256
    return pl.pallas_call(
        _dispatch_kernel,
        out_shape=jax.ShapeDtypeStruct((n_rows, D), u2.dtype),
        grid_spec=pltpu.PrefetchScalarGridSpec(
            num_scalar_prefetch=2,
            grid=(N // tm,),
            in_specs=[pl.BlockSpec((tm, D), lambda m, pos, gaps: (m, 0))],
            out_specs=pl.BlockSpec(memory_space=pl.ANY),
            scratch_shapes=[pltpu.VMEM((8, D), u2.dtype), pltpu.SemaphoreType.DMA(()),
                            pltpu.SemaphoreType.DMA((2,))]),
        compiler_params=_params(("arbitrary",)),
        name="moe_dispatch",
    )(pos, gaps, u2)


def _moe_ffn_kernel(te_ref, nv_ref, u_ref, wg_hbm, wu_hbm, wd_hbm, out_ref, wgbuf, wubuf, wdbuf, sem):
    t = pl.program_id(0)
    nv = nv_ref[0]
    F = wg_hbm.shape[2]
    n_chunks = F // MOE_CW
    n_slots = wgbuf.shape[0]
    ahead = n_slots - 1

    def chunk_copies(g):
        tile, ci = g // n_chunks, g % n_chunks
        e = te_ref[tile]
        slot = g % n_slots
        cols = pl.ds(pl.multiple_of(ci * MOE_CW, MOE_CW), MOE_CW)
        return (pltpu.make_async_copy(wg_hbm.at[e, :, cols], wgbuf.at[slot], sem.at[slot]),
                pltpu.make_async_copy(wu_hbm.at[e, :, cols], wubuf.at[slot], sem.at[slot]),
                pltpu.make_async_copy(wd_hbm.at[e, cols, :], wdbuf.at[slot], sem.at[slot]))

    @pl.when(t == 0)
    def _():
        for g in range(ahead):
            for cp in chunk_copies(g):
                cp.start()

    @pl.when(t < nv)
    def _():
        u = u_ref[...].astype(BF16)
        for ci in range(n_chunks):
            g = t * n_chunks + ci
            slot = g % n_slots
            for cp in chunk_copies(g):
                cp.wait()

            @pl.when(g + ahead < nv * n_chunks)
            def _():
                for cp in chunk_copies(g + ahead):
                    cp.start()

            hg = jnp.dot(u, wgbuf[slot].astype(BF16), preferred_element_type=F32)
            hu = jnp.dot(u, wubuf[slot].astype(BF16), preferred_element_type=F32)
            h = (_silu(hg) * hu).astype(BF16)
            part = jnp.dot(h, wdbuf[slot].astype(BF16), preferred_element_type=F32)
            if ci == 0:
                out_ref[...] = part
            else:
                out_ref[...] += part

    @pl.when(t >= nv)
    def _():
        out_ref[...] = jnp.zeros_like(out_ref)


def _moe_ffn(u_sorted, tile_expert, n_valid, wg, wu, wd):
    R, D = u_sorted.shape
    tm = MOE_TM
    n_slots = 3
    tile = lambda t, nv: jnp.minimum(t, nv[0] - 1)
    return pl.pallas_call(
        _moe_ffn_kernel,
        out_shape=jax.ShapeDtypeStruct((R, D), F32),
        grid_spec=pltpu.PrefetchScalarGridSpec(
            num_scalar_prefetch=2,
            grid=(R // tm,),
            in_specs=[pl.BlockSpec((tm, D), lambda t, te, nv: (tile(t, nv), 0)),
                      pl.BlockSpec(memory_space=pl.ANY),
                      pl.BlockSpec(memory_space=pl.ANY),
                      pl.BlockSpec(memory_space=pl.ANY)],
            out_specs=pl.BlockSpec((tm, D), lambda t, te, nv: (t, 0)),
            scratch_shapes=[pltpu.VMEM((n_slots, D, MOE_CW), F32), pltpu.VMEM((n_slots, D, MOE_CW), F32),
                            pltpu.VMEM((n_slots, MOE_CW, D), F32),
                            pltpu.SemaphoreType.DMA((n_slots,))]),
        compiler_params=_params(("arbitrary",), vmem=60 * 1024 * 1024),
        name="moe_ffn",
    )(tile_expert, n_valid, u_sorted, wg, wu, wd)


def _combine_kernel(pos_ref, y_ref, topw_ref, x_ref, g2_ref, lnw_ref, lnb_ref, out_ref, ybuf, sem):
    tm = x_ref.shape[0]
    m = pl.program_id(0)
    slot = m % 2

    def gather(step, buf):
        for k in range(2):
            _start_rows(tm, lambda base, j: _row_copy(y_ref, pos_ref[2 * (step * tm + base + j) + k],
                                                      ybuf.at[buf, k], base + j, sem.at[buf]))

    @pl.when(m == 0)
    def _():
        gather(0, 0)

    @pl.when(m + 1 < pl.num_programs(0))
    def _():
        gather(m + 1, 1 - slot)

    _wait_rows(2 * tm, _row_copy(y_ref, 0, ybuf.at[slot, 0], 0, sem.at[slot]))
    tw = topw_ref[...]
    y = tw[:, 0:1] * ybuf[slot, 0] + tw[:, 1:2] * ybuf[slot, 1]
    r = ALPHA * x_ref[...] + (1.0 + g2_ref[...]) * y
    out_ref[...] = _layer_norm_rows(r, lnw_ref[...], lnb_ref[...])


def _combine(y_sorted, pos, topw, x1, g2, ln_w, ln_b, seq):
    N, D = x1.shape
    tm = 256
    tpb = seq // tm
    row = lambda m, pos: (m, 0)
    const = lambda m, pos: (0, 0)
    return pl.pallas_call(
        _combine_kernel,
        out_shape=jax.ShapeDtypeStruct((N, D), F32),
        grid_spec=pltpu.PrefetchScalarGridSpec(
            num_scalar_prefetch=1,
            grid=(N // tm,),
            in_specs=[pl.BlockSpec(memory_space=pl.ANY),
                      pl.BlockSpec((tm, LANES), row),
                      pl.BlockSpec((tm, D), row),
                      pl.BlockSpec((None, 1, D), lambda m, pos: (m // tpb, 0, 0)),
                      pl.BlockSpec((1, D), const), pl.BlockSpec((1, D), const)],
            out_specs=pl.BlockSpec((tm, D), row),
            scratch_shapes=[pltpu.VMEM((2, 2, tm, D), F32), pltpu.SemaphoreType.DMA((2,))]),
        compiler_params=_params(("arbitrary",)),
        name="moe_combine_ln",
    )(pos, y_sorted, topw, x1, g2, ln_w.reshape(1, -1), ln_b.reshape(1, -1))


def _routing_plan(ids, n_tiles):
    e_flat = ids[:, :2].reshape(-1)
    onehot = (e_flat[:, None] == jnp.arange(N_EXPERTS, dtype=jnp.int32)[None, :]).astype(jnp.int32)
    csum = jnp.cumsum(onehot, axis=0)
    rank = jnp.sum((csum - onehot) * onehot, axis=1)
    counts = csum[-1]
    padded = ((counts + MOE_TM - 1) // MOE_TM) * MOE_TM
    ends = jnp.cumsum(padded)
    offs = ends - padded
    pos = (jnp.sum(onehot * offs[None, :], axis=1) + rank).astype(jnp.int32)
    tile_start = jnp.arange(n_tiles, dtype=jnp.int32) * MOE_TM
    n_valid = (ends[-1] // MOE_TM).astype(jnp.int32)
    tile_start = jnp.minimum(tile_start, (n_valid - 1) * MOE_TM)
    tile_expert = jnp.sum((tile_start[:, None] >= ends[None, :]).astype(jnp.int32), axis=1).astype(jnp.int32)
    total = jnp.full((1,), n_tiles * MOE_TM, jnp.int32)
    gaps = jnp.stack([jnp.concatenate([offs + counts, ends[-1:]]),
                      jnp.concatenate([ends, total])], axis=1).reshape(-1).astype(jnp.int32)
    return pos, tile_expert, n_valid.reshape(1), gaps


def _moe(u2, ids, topw, wg, wu, wd, x1, g2, ln_w, ln_b, seq):
    N = u2.shape[0]
    n_tiles = -(-2 * N // MOE_TM) + N_EXPERTS
    pos, tile_expert, n_valid, gaps = _routing_plan(ids, n_tiles)
    u_sorted = _dispatch(u2, pos, gaps, n_tiles * MOE_TM)
    y_sorted = _moe_ffn(u_sorted, tile_expert, n_valid, wg, wu, wd)
    return _combine(y_sorted, pos, topw, x1, g2, ln_w, ln_b, seq)


def _split_w_in_kernel(wt_ref, gt_ref, main_ref, gate_ref):
    main_ref[...] = wt_ref[0].T.astype(BF16)

    @pl.when(pl.program_id(1) == 0)
    def _():
        g = gt_ref[...]
        g = jnp.concatenate([g, jnp.zeros((LANES - g.shape[0], g.shape[1]), F32)], axis=0)
        gate_ref[...] = g.T.astype(BF16)


def _split_w_in(w_in):
    L, D, C = w_in.shape
    w_t = jnp.swapaxes(w_in, 1, 2)
    g0 = 4 * GROUP_WIDTH
    ng = 2 * N_HEADS
    tc = 512
    src_col = lambda j: (j * (tc // ng) + jnp.where(j * tc >= g0, 1, 0)) * ng
    return pl.pallas_call(
        _split_w_in_kernel,
        out_shape=(jax.ShapeDtypeStruct((L, D, C - ng), BF16),
                   jax.ShapeDtypeStruct((L, D, LANES), BF16)),
        grid=(L, (C - ng) // tc),
        in_specs=[pl.BlockSpec((pl.Element(1), pl.Element(tc), pl.Element(D)),
                               lambda l, j: (l, src_col(j), 0)),
                  pl.BlockSpec((None, ng, D), lambda l, j: (l, g0 // ng, 0))],
        out_specs=(pl.BlockSpec((None, D, tc), lambda l, j: (l, 0, j)),
                   pl.BlockSpec((None, D, LANES), lambda l, j: (l, 0, 0))),
        compiler_params=_params(("parallel", "arbitrary")),
        name="split_w_in",
    )(w_t, w_t)


def kernel(x, c, w_in, mlstm_conv_w, mlstm_conv_b, mlstm_gate_b, mlstm_norm_w, pool_w, pool_scale, conv_dw_w, conv_dw_b, conv_ln_w, conv_ln_b, conv_pw_w, conv_pw_b, group_norm_w, w_out, ada_w, ada_b, ln1_w, ln1_b, ln2_w, ln2_b, ffn_w_gate, ffn_w_up, ffn_w_down, moe_router_w, moe_router_b, moe_w_gate, moe_w_up, moe_w_down):
    B, S, D = x.shape
    GW = GROUP_WIDTH
    ada = _ada_all(c, ada_w, ada_b)
    x2d = x.reshape(B * S, D)
    w_main, w_gate = _split_w_in(w_in)
    for l in range(DEPTH):
        mod = [ada[l, :, i * D:(i + 1) * D].reshape(B, 1, D) for i in range(6)]
        sh1, sc1, g1, sh2, sc2, g2 = mod
        proj, gates = _in_proj(x2d, sc1, sh1, w_main, w_gate, l, S)
        proj = proj.reshape(B, S, -1)
        gates = gates.reshape(B, S, LANES)
        gn_b, gn_c, gn_d = (group_norm_w[l, i * GW:(i + 1) * GW] for i in range(3))
        ya = _mlstm_mixer(proj, gates, mlstm_conv_w[l], mlstm_conv_b[l], mlstm_gate_b[l],
                          mlstm_norm_w[l], B, S)
        yb = _pool_mixer(proj, pool_w[l], pool_scale[l], gn_b, B, S)
        yc = _sb_mixer(proj, B, S)
        yd = _conv_mixer(proj, conv_dw_w[l], conv_dw_b[l], conv_ln_w[l], conv_ln_b[l],
                         conv_pw_w[l], conv_pw_b[l], gn_d, B, S)
        flat = lambda t: t.reshape(B * S, GW)
        j = l // 2
        router = None
        if l % 2 == 1:
            rw = jnp.zeros((D, LANES), F32).at[:, :N_EXPERTS].set(moe_router_w[j])
            rw_hi = rw.astype(BF16)
            rw_lo = (rw - rw_hi.astype(F32)).astype(BF16)
            rb = jnp.zeros((1, LANES), F32).at[0, :N_EXPERTS].set(moe_router_b[j])
            router = (jnp.concatenate([rw_hi, rw_lo], axis=1), rb)
        outs = _out_proj(flat(ya), flat(yb), flat(yc), flat(yd), gn_c, w_out[l].astype(BF16), x2d,
                         g1, ln1_w[l], ln1_b[l], sc2, sh2, S, router)
        if l % 2 == 0:
            x1, u2 = outs
            x2d = _ffn(u2, ffn_w_gate[j].astype(BF16), ffn_w_up[j].astype(BF16),
                       ffn_w_down[j].astype(BF16), x1, g2, ln2_w[l], ln2_b[l], S)
        else:
            x1, u2, ids, topw = outs
            x2d = _moe(u2, ids, topw, moe_w_gate[j], moe_w_up[j], moe_w_down[j], x1, g2,
                       ln2_w[l], ln2_b[l], S)
    return x2d.reshape(B, S, D)
```

```python
import functools

import jax
import jax.numpy as jnp
from jax import lax
from jax.experimental import pallas as pl
from jax.experimental.pallas import tpu as pltpu

F32 = jnp.float32
BF16 = jnp.bfloat16

DEPTH = 2
CHUNK = 256
GROUP_WIDTH = 512
N_HEADS = 4
HEAD_DIM = 128
MLSTM_CONV = 4
POOL_WINDOWS = (2, 4, 8, 16)
CONV_WIDTH = 31
SB_KEY_BLOCK = 256
N_EXPERTS = 8
ALPHA = (2.0 * DEPTH) ** 0.25
EPS = 1e-5
LOG2E = 1.4426950408889634
LANES = 128
VMEM_LIMIT = 56 * 1024 * 1024


def _params(sem, vmem=VMEM_LIMIT):
    return pltpu.CompilerParams(dimension_semantics=sem, vmem_limit_bytes=vmem)


def _silu(x):
    return x * (1.0 / (1.0 + jnp.exp(-x)))


def _sigmoid(x):
    return 1.0 / (1.0 + jnp.exp(-x))


def _neg_softplus(x):
    return -(jnp.maximum(x, 0.0) + jnp.log(1.0 + jnp.exp(-jnp.abs(x))))


def _layer_norm_rows(r, w, b):
    mu = jnp.mean(r, axis=-1, keepdims=True)
    d = r - mu
    var = jnp.mean(d * d, axis=-1, keepdims=True)
    return d * lax.rsqrt(var + EPS) * w + b


def _rms_rows(y, w):
    return y * lax.rsqrt(jnp.mean(y * y, axis=-1, keepdims=True) + EPS) * w


def _bdot(a, b):
    return jnp.dot(a.astype(BF16), b.astype(BF16), preferred_element_type=F32)


def _ada_kernel(c_ref, w_ref, b_ref, out_ref):
    out_ref[...] = _bdot(_silu(c_ref[...]), w_ref[...]) + b_ref[...]


def _ada_all(c, ada_w, ada_b):
    L, D, D6 = ada_w.shape
    Bn = c.shape[0]
    tn = 1024
    return pl.pallas_call(
        _ada_kernel,
        out_shape=jax.ShapeDtypeStruct((L, Bn, D6), F32),
        grid=(L, D6 // tn),
        in_specs=[pl.BlockSpec((Bn, D), lambda l, n: (0, 0)),
                  pl.BlockSpec((None, D, tn), lambda l, n: (l, 0, n)),
                  pl.BlockSpec((None, 1, tn), lambda l, n: (l, 0, n))],
        out_specs=pl.BlockSpec((None, Bn, tn), lambda l, n: (l, 0, n)),
        compiler_params=_params(("parallel", "parallel")),
        name="ada_mod",
    )(c, ada_w, ada_b.reshape(L, 1, D6))


def _inproj_kernel(x_ref, sc_ref, sh_ref, w_ref, wg_ref, proj_ref, gates_ref, u_ref):
    @pl.when(pl.program_id(1) == 0)
    def _():
        u = (x_ref[...] * (1.0 + sc_ref[...]) + sh_ref[...]).astype(BF16)
        u_ref[...] = u
        gates_ref[...] = jnp.dot(u, wg_ref[...], preferred_element_type=F32)

    proj_ref[...] = jnp.dot(u_ref[...], w_ref[...], preferred_element_type=F32)


def _in_proj(x2d, sc, sh, w_main, w_gate, layer, seq):
    N, D = x2d.shape
    NC = w_main.shape[2]
    tm, tn = 1024, 1280
    tpb = seq // tm
    return pl.pallas_call(
        _inproj_kernel,
        out_shape=(jax.ShapeDtypeStruct((N, NC), F32), jax.ShapeDtypeStruct((N, LANES), F32)),
        grid=(N // tm, NC // tn),
        in_specs=[pl.BlockSpec((tm, D), lambda m, n: (m, 0)),
                  pl.BlockSpec((None, 1, D), lambda m, n: (m // tpb, 0, 0)),
                  pl.BlockSpec((None, 1, D), lambda m, n: (m // tpb, 0, 0)),
                  pl.BlockSpec((None, D, tn), lambda m, n: (layer, 0, n)),
                  pl.BlockSpec((None, D, LANES), lambda m, n: (layer, 0, 0))],
        out_specs=(pl.BlockSpec((tm, tn), lambda m, n: (m, n)),
                   pl.BlockSpec((tm, LANES), lambda m, n: (m, 0))),
        scratch_shapes=[pltpu.VMEM((tm, D), BF16)],
        compiler_params=_params(("parallel", "arbitrary")),
        name="in_proj",
    )(x2d, sc, sh, w_main, w_gate)


def _cumsum_rows(x):
    n = x.shape[0]
    row = lax.broadcasted_iota(jnp.int32, x.shape, 0)
    k = 1
    while k < n:
        x = x + jnp.where(row >= k, pltpu.roll(x, k, axis=0), 0.0)
        k *= 2
    return x


def _mlstm_kernel(qk_ref, v_ref, o_ref, g_ref, cw_ref, cb_ref, gb_ref, nw_ref, out_ref,
                  c_state, n_state, m_state, tail_ref):
    @pl.when(pl.program_id(1) == 0)
    def _():
        c_state[...] = jnp.zeros_like(c_state)
        n_state[...] = jnp.zeros_like(n_state)
        m_state[...] = jnp.zeros_like(m_state)
        tail_ref[...] = jnp.zeros_like(tail_ref)

    for bb in range(qk_ref.shape[0]):
        _mlstm_block(qk_ref.at[bb], v_ref.at[bb], o_ref.at[bb], g_ref.at[bb], cw_ref, cb_ref, gb_ref,
                     nw_ref, out_ref.at[bb], c_state.at[bb], n_state.at[bb], m_state.at[bb],
                     tail_ref.at[bb])


def _mlstm_block(qk_ref, v_ref, o_ref, g_ref, cw_ref, cb_ref, gb_ref, nw_ref, out_ref,
                 c_state, n_state, m_state, tail_ref):
    T = qk_ref.shape[0]
    GW = GROUP_WIDTH

    xx = jnp.concatenate([tail_ref[...], qk_ref[...]], axis=0)
    tail_ref[...] = qk_ref[T - 8:T, :]
    conv = cb_ref[...]
    for j in range(MLSTM_CONV):
        off = 8 - (MLSTM_CONV - 1) + j
        tap = xx[off:off + T, :] if off % 8 == 0 else pltpu.roll(xx, T + 8 - off, axis=0)[0:T, :]
        conv = conv + cw_ref[j:j + 1, :] * tap
    qk = _silu(conv)

    gates = g_ref[...] + gb_ref[...]
    logf = _neg_softplus(-gates)
    causal = (lax.broadcasted_iota(jnp.int32, (CHUNK, CHUNK), 1)
              <= lax.broadcasted_iota(jnp.int32, (CHUNK, CHUNK), 0))

    for c in range(T // CHUNK):
        rows = slice(c * CHUNK, (c + 1) * CHUNK)
        gi = gates[rows, :]
        bcum = _cumsum_rows(logf[rows, :])
        gi_t = gi.T
        bcum_t = bcum.T
        for h in range(N_HEADS):
            cols = slice(h * HEAD_DIM, (h + 1) * HEAD_DIM)
            q = qk[rows, cols] * (HEAD_DIM ** -0.5)
            k = qk[rows, GW + h * HEAD_DIM:GW + (h + 1) * HEAD_DIM]
            v = v_ref[rows, cols]
            b_col = bcum[:, N_HEADS + h:N_HEADS + h + 1]
            b_row = bcum_t[N_HEADS + h:N_HEADS + h + 1, :]
            i_col = gi[:, h:h + 1]
            i_row = gi_t[h:h + 1, :]
            g_tot = b_col[CHUNK - 1:CHUNK, :]
            m_prev = m_state[h:h + 1, 0:1]
            c_prev = c_state[h]
            n_prev = n_state[h:h + 1, :]

            log_d = jnp.where(causal, b_col - b_row + i_row, -jnp.inf)
            m_inter = b_col + m_prev
            m_t = jnp.maximum(m_inter, jnp.max(log_d, axis=-1, keepdims=True))
            s = lax.dot_general(q.astype(BF16), k.astype(BF16), (((1,), (1,)), ((), ())),
                                preferred_element_type=F32)
            w = jnp.exp(log_d - m_t) * s
            inter = jnp.exp(m_inter - m_t)
            num = inter * _bdot(q, c_prev) + _bdot(w, v)
            den = inter * jnp.sum(q * n_prev, axis=-1, keepdims=True) + jnp.sum(w, axis=-1, keepdims=True)
            hcur = num / jnp.maximum(jnp.abs(den), jnp.exp(-m_t))

            a_col = g_tot - b_col + i_col
            m_new = jnp.maximum(g_tot + m_prev, jnp.max(a_col, axis=0, keepdims=True))
            decay = jnp.exp(g_tot + m_prev - m_new)
            kw = k * jnp.exp(a_col - m_new)
            c_state[h] = decay * c_prev + _bdot(kw.T, v)
            n_state[h:h + 1, :] = decay * n_prev + jnp.sum(kw, axis=0, keepdims=True)
            m_state[h:h + 1, :] = jnp.broadcast_to(m_new, (1, LANES))

            mu = jnp.mean(hcur, axis=-1, keepdims=True)
            dlt = hcur - mu
            var = jnp.mean(dlt * dlt, axis=-1, keepdims=True)
            hn = dlt * lax.rsqrt(var + EPS) * nw_ref[:, cols]
            out_ref[rows, cols] = (_sigmoid(o_ref[rows, cols]) * hn).astype(out_ref.dtype)


def _mlstm_mixer(proj, gates, conv_w, conv_b, gate_b, norm_w, B, S):
    T = 256
    NB = 2
    GW = GROUP_WIDTH
    gb = jnp.zeros((1, LANES), F32).at[0, :2 * N_HEADS].set(gate_b)
    return pl.pallas_call(
        _mlstm_kernel,
        out_shape=jax.ShapeDtypeStruct((B, S, GW), BF16),
        grid=(B // NB, S // T),
        in_specs=[pl.BlockSpec((NB, T, 2 * GW), lambda b, t: (b, t, 0)),
                  pl.BlockSpec((NB, T, GW), lambda b, t: (b, t, 2)),
                  pl.BlockSpec((NB, T, GW), lambda b, t: (b, t, 3)),
                  pl.BlockSpec((NB, T, LANES), lambda b, t: (b, t, 0)),
                  pl.BlockSpec((MLSTM_CONV, 2 * GW), lambda b, t: (0, 0)),
                  pl.BlockSpec((1, 2 * GW), lambda b, t: (0, 0)),
                  pl.BlockSpec((1, LANES), lambda b, t: (0, 0)),
                  pl.BlockSpec((1, GW), lambda b, t: (0, 0))],
        out_specs=pl.BlockSpec((NB, T, GW), lambda b, t: (b, t, 0)),
        scratch_shapes=[pltpu.VMEM((NB, N_HEADS, HEAD_DIM, HEAD_DIM), F32),
                        pltpu.VMEM((NB, 8, LANES), F32),
                        pltpu.VMEM((NB, 8, LANES), F32),
                        pltpu.VMEM((NB, 8, 2 * GW), F32)],
        compiler_params=_params(("parallel", "arbitrary")),
        name="mlstm_mixer",
    )(proj, proj, proj, gates, conv_w, conv_b.reshape(1, -1), gb, norm_w.reshape(1, -1))


def _pool_kernel(x_ref, w_ref, scale_ref, gn_ref, out_ref, tail_ref):
    T = x_ref.shape[0]
    HALO = 16
    t_blk = pl.program_id(1)

    @pl.when(t_blk == 0)
    def _():
        tail_ref[...] = jnp.zeros_like(tail_ref)

    x = x_ref[...]
    xx = jnp.concatenate([tail_ref[...], x], axis=0)
    tail_ref[...] = x[T - HALO:T, :]
    pos = t_blk * T + lax.broadcasted_iota(jnp.int32, (T, 1), 0) + 1
    outs = []
    for g, win in enumerate(POOL_WINDOWS):
        cols = slice(g * LANES, (g + 1) * LANES)
        s = xx[:, cols]
        k = 1
        while k < win:
            s = s + pltpu.roll(s, k, axis=0)
            k *= 2
        cnt = jnp.minimum(pos, win).astype(F32)
        yg = s[HALO:, :] / cnt - x[:, cols]
        outs.append(_bdot(yg, w_ref[g]))
    y = jnp.concatenate(outs, axis=-1) * scale_ref[...]
    out_ref[...] = _rms_rows(y, gn_ref[...]).astype(out_ref.dtype)


def _pool_mixer(proj, pool_w, pool_scale, gn, B, S):
    T = 512
    GW = GROUP_WIDTH
    return pl.pallas_call(
        _pool_kernel,
        out_shape=jax.ShapeDtypeStruct((B, S, GW), BF16),
        grid=(B, S // T),
        in_specs=[pl.BlockSpec((None, T, GW), lambda b, t: (b, t, 4)),
                  pl.BlockSpec((len(POOL_WINDOWS), LANES, LANES), lambda b, t: (0, 0, 0)),
                  pl.BlockSpec((1, GW), lambda b, t: (0, 0)),
                  pl.BlockSpec((1, GW), lambda b, t: (0, 0))],
        out_specs=pl.BlockSpec((None, T, GW), lambda b, t: (b, t, 0)),
        scratch_shapes=[pltpu.VMEM((16, GW), F32)],
        compiler_params=_params(("parallel", "arbitrary")),
        name="pool_mixer",
    )(proj, pool_w.astype(BF16), pool_scale.reshape(1, -1), gn.reshape(1, -1))


def _sb_kernel(q_ref, k_ref, v_ref, out_ref, kb_ref, vb_ref):
    LQ = q_ref.shape[0]
    LK = SB_KEY_BLOCK
    ratio = LQ // LK
    qi = pl.program_id(1)

    @pl.when(qi == 0)
    def _():
        kb_ref[...] = k_ref[...].astype(BF16)
        vb_ref[...] = v_ref[...].astype(BF16)

    scale2 = HEAD_DIM ** -0.5 * LOG2E
    tri = (lax.broadcasted_iota(jnp.int32, (LK, LK), 0)
           > lax.broadcasted_iota(jnp.int32, (LK, LK), 1)).astype(BF16)
    row = lax.broadcasted_iota(jnp.int32, (LQ, LK), 0)
    col = lax.broadcasted_iota(jnp.int32, (LQ, LK), 1)
    heads = [slice(h * HEAD_DIM, (h + 1) * HEAD_DIM) for h in range(N_HEADS)]
    qs = [(q_ref[:, hs] * scale2).astype(BF16) for hs in heads]

    def sweep(kb, carry, strict):
        start = pl.multiple_of(kb * LK, LK)
        zs, lks = [], []
        for h, hs in enumerate(heads):
            k_blk = kb_ref[pl.ds(start, LK), hs]
            z = lax.dot_general(qs[h], k_blk, (((1,), (1,)), ((), ())), preferred_element_type=F32)
            lk = -(jnp.maximum(z, 0.0) + jnp.log2(1.0 + jnp.exp2(-jnp.abs(z))))
            if strict is not None:
                lk = jnp.where(strict, lk, 0.0)
            zs.append(z)
            lks.append(lk)
        inner = jnp.dot(jnp.concatenate([lk.astype(BF16) for lk in lks], axis=0), tri,
                        preferred_element_type=F32)
        new = []
        for h, hs in enumerate(heads):
            after, acc = carry[h]
            v_blk = vb_ref[pl.ds(start, LK), hs]
            a = jnp.exp2(zs[h] + lks[h] + inner[h * LQ:(h + 1) * LQ, :] + after)
            if strict is not None:
                a = jnp.where(strict, a, 0.0)
            acc = acc + jnp.dot(a.astype(BF16), v_blk, preferred_element_type=F32)
            after = after + jnp.sum(lks[h], axis=-1, keepdims=True)
            new.append((after, acc))
        return tuple(new)

    init = tuple((jnp.zeros((LQ, 1), F32), jnp.zeros((LQ, HEAD_DIM), F32)) for _ in heads)
    carry = init
    for j in reversed(range(ratio)):
        carry = sweep(ratio * qi + j, carry, (j * LK + col) < row)
    carry = lax.fori_loop(0, ratio * qi, lambda it, c: sweep(ratio * qi - 1 - it, c, None), carry)
    for h, hs in enumerate(heads):
        out_ref[:, hs] = carry[h][1]


def _sb_mixer(proj, B, S):
    LQ = 512
    GW = GROUP_WIDTH
    return pl.pallas_call(
        _sb_kernel,
        out_shape=jax.ShapeDtypeStruct((B, S, GW), F32),
        grid=(B, S // LQ),
        in_specs=[pl.BlockSpec((None, LQ, GW), lambda b, i: (b, i, 5)),
                  pl.BlockSpec((None, S, GW), lambda b, i: (b, 0, 6)),
                  pl.BlockSpec((None, S, GW), lambda b, i: (b, 0, 7))],
        out_specs=pl.BlockSpec((None, LQ, GW), lambda b, i: (b, i, 0)),
        scratch_shapes=[pltpu.VMEM((S, GW), BF16), pltpu.VMEM((S, GW), BF16)],
        compiler_params=_params(("parallel", "arbitrary")),
        name="sb_attention",
    )(proj, proj, proj)


def _conv_kernel(x_ref, dw_ref, dwb_ref, lnw_ref, lnb_ref, pw_ref, pwb_ref, gn_ref, out_ref,
                 hbuf, ybuf):
    T = x_ref.shape[0]
    GW = GROUP_WIDTH
    HALO = 32
    SUB = 64

    @pl.when(pl.program_id(1) == 0)
    def _():
        hbuf[0:HALO, :] = jnp.zeros((HALO, GW), F32)

    @pl.when(pl.program_id(1) > 0)
    def _():
        hbuf[0:HALO, :] = hbuf[T:T + HALO, :]

    hbuf[HALO:HALO + T, :] = x_ref[:, 0:GW] * _sigmoid(x_ref[:, GW:2 * GW])

    def sub(i, carry):
        r0 = pl.multiple_of(i * SUB, SUB)
        acc = jnp.broadcast_to(dwb_ref[...], (SUB, GW))
        win = hbuf[pl.ds(r0, SUB + HALO), :]
        for res in range(8):
            shifted = win if res == 0 else pltpu.roll(win, SUB + HALO - res, axis=0)
            for j in range(CONV_WIDTH):
                off = HALO - CONV_WIDTH + 1 + j
                if off % 8 == res:
                    base = off - res
                    acc = acc + dw_ref[j:j + 1, :] * shifted[base:base + SUB, :]
        ybuf[pl.ds(r0, SUB), :] = acc
        return carry

    lax.fori_loop(0, T // SUB, sub, 0)
    hn = _silu(_layer_norm_rows(ybuf[...], lnw_ref[...], lnb_ref[...]))
    y = _bdot(hn, pw_ref[...]) + pwb_ref[...]
    out_ref[...] = _rms_rows(y, gn_ref[...]).astype(out_ref.dtype)


def _conv_mixer(proj, dw_w, dw_b, ln_w, ln_b, pw_w, pw_b, gn, B, S):
    T = 256
    GW = GROUP_WIDTH
    dw_pad = jnp.zeros((32, GW), F32).at[:CONV_WIDTH].set(dw_w)
    r = lambda a: a.reshape(1, -1)
    return pl.pallas_call(
        _conv_kernel,
        out_shape=jax.ShapeDtypeStruct((B, S, GW), BF16),
        grid=(B, S // T),
        in_specs=[pl.BlockSpec((None, T, 2 * GW), lambda b, t: (b, t, 4)),
                  pl.BlockSpec((32, GW), lambda b, t: (0, 0)),
                  pl.BlockSpec((1, GW), lambda b, t: (0, 0)),
                  pl.BlockSpec((1, GW), lambda b, t: (0, 0)),
                  pl.BlockSpec((1, GW), lambda b, t: (0, 0)),
                  pl.BlockSpec((GW, GW), lambda b, t: (0, 0)),
                  pl.BlockSpec((1, GW), lambda b, t: (0, 0)),
                  pl.BlockSpec((1, GW), lambda b, t: (0, 0))],
        out_specs=pl.BlockSpec((None, T, GW), lambda b, t: (b, t, 0)),
        scratch_shapes=[pltpu.VMEM((T + 32, GW), F32), pltpu.VMEM((T, GW), F32)],
        compiler_params=_params(("parallel", "arbitrary")),
        name="conv_mixer",
    )(proj, dw_pad, r(dw_b), r(ln_w), r(ln_b), pw_w.astype(BF16), r(pw_b), r(gn))


def _outproj_kernel(with_router, ya_ref, yb_ref, yc_ref, yd_ref, gnc_ref, w_ref, x_ref, g1_ref,
                    lnw_ref, lnb_ref, sc_ref, sh_ref, *rest):
    GW = GROUP_WIDTH
    if with_router:
        rw_ref, rb_ref, x1_ref, u2_ref, ids_ref, topw_ref = rest
    else:
        x1_ref, u2_ref = rest
    yc = _rms_rows(yc_ref[...], gnc_ref[...]).astype(BF16)
    acc = jnp.dot(ya_ref[...], w_ref[0:GW, :], preferred_element_type=F32)
    acc = acc + jnp.dot(yb_ref[...], w_ref[GW:2 * GW, :], preferred_element_type=F32)
    acc = acc + jnp.dot(yc, w_ref[2 * GW:3 * GW, :], preferred_element_type=F32)
    acc = acc + jnp.dot(yd_ref[...], w_ref[3 * GW:4 * GW, :], preferred_element_type=F32)
    r = ALPHA * x_ref[...] + (1.0 + g1_ref[...]) * acc
    x1 = _layer_norm_rows(r, lnw_ref[...], lnb_ref[...])
    x1_ref[...] = x1
    u2 = x1 * (1.0 + sc_ref[...]) + sh_ref[...]
    u2_ref[...] = u2.astype(u2_ref.dtype)
    if with_router:
        u_hi = u2.astype(BF16)
        u_lo = (u2 - u_hi.astype(F32)).astype(BF16)
        p = jnp.dot(u_hi, rw_ref[...], preferred_element_type=F32)
        logits = (p[:, :LANES] + p[:, LANES:]
                  + jnp.dot(u_lo, rw_ref[:, :LANES], preferred_element_type=F32) + rb_ref[...])
        lane = lax.broadcasted_iota(jnp.int32, logits.shape, 1)
        lg = jnp.where(lane < N_EXPERTS, logits, -jnp.inf)
        m1 = jnp.max(lg, axis=-1, keepdims=True)
        i1 = jnp.min(jnp.where(lg == m1, lane, LANES), axis=-1, keepdims=True)
        lg2 = jnp.where(lane == i1, -jnp.inf, lg)
        m2 = jnp.max(lg2, axis=-1, keepdims=True)
        i2 = jnp.min(jnp.where(lg2 == m2, lane, LANES), axis=-1, keepdims=True)
        e2 = jnp.exp(m2 - m1)
        w1 = 1.0 / (1.0 + e2)
        w2 = e2 / (1.0 + e2)
        ids_ref[...] = jnp.where(lane == 0, i1, jnp.where(lane == 1, i2, 0))
        topw_ref[...] = jnp.where(lane == 0, w1, jnp.where(lane == 1, w2, 0.0))


def _out_proj(ya, yb, yc, yd, gnc, w_out, x2d, g1, ln_w, ln_b, sc2, sh2, seq, router=None):
    N, D = x2d.shape
    GW = GROUP_WIDTH
    tm = 512
    tpb = seq // tm
    row = lambda m: (m, 0)
    const = lambda m: (0, 0)
    perb = lambda m: (m // tpb, 0, 0)
    in_specs = [pl.BlockSpec((tm, GW), row), pl.BlockSpec((tm, GW), row),
                pl.BlockSpec((tm, GW), row), pl.BlockSpec((tm, GW), row),
                pl.BlockSpec((1, GW), const),
                pl.BlockSpec((D, D), const),
                pl.BlockSpec((tm, D), row),
                pl.BlockSpec((None, 1, D), perb),
                pl.BlockSpec((1, D), const), pl.BlockSpec((1, D), const),
                pl.BlockSpec((None, 1, D), perb), pl.BlockSpec((None, 1, D), perb)]
    args = [ya, yb, yc, yd, gnc.reshape(1, -1), w_out, x2d, g1, ln_w.reshape(1, -1),
            ln_b.reshape(1, -1), sc2, sh2]
    out_shape = [jax.ShapeDtypeStruct((N, D), F32),
                 jax.ShapeDtypeStruct((N, D), BF16 if router is None else F32)]
    out_specs = [pl.BlockSpec((tm, D), row), pl.BlockSpec((tm, D), row)]
    if router is not None:
        rw, rb = router
        in_specs += [pl.BlockSpec((D, 2 * LANES), const), pl.BlockSpec((1, LANES), const)]
        args += [rw, rb]
        out_shape += [jax.ShapeDtypeStruct((N, LANES), jnp.int32), jax.ShapeDtypeStruct((N, LANES), F32)]
        out_specs += [pl.BlockSpec((tm, LANES), row), pl.BlockSpec((tm, LANES), row)]
    return pl.pallas_call(
        functools.partial(_outproj_kernel, router is not None),
        out_shape=tuple(out_shape),
        grid=(N // tm,),
        in_specs=in_specs,
        out_specs=tuple(out_specs),
        compiler_params=_params(("parallel",)),
        name="out_proj_ln",
    )(*args)


FFN_CW = 512


def _ffn_kernel(u_ref, wg_hbm, wu_hbm, wd_hbm, x_ref, g2_ref, lnw_ref, lnb_ref, out_ref,
                acc_ref, wgbuf, wubuf, wdbuf, sem):
    m = pl.program_id(0)
    n_chunks = wg_hbm.shape[1] // FFN_CW
    total = pl.num_programs(0) * n_chunks
    n_slots = wgbuf.shape[0]
    ahead = n_slots - 1

    def chunk_copies(g):
        slot = g % n_slots
        cols = pl.ds(pl.multiple_of((g % n_chunks) * FFN_CW, FFN_CW), FFN_CW)
        return (pltpu.make_async_copy(wg_hbm.at[:, cols], wgbuf.at[slot], sem.at[slot]),
                pltpu.make_async_copy(wu_hbm.at[:, cols], wubuf.at[slot], sem.at[slot]),
                pltpu.make_async_copy(wd_hbm.at[cols, :], wdbuf.at[slot], sem.at[slot]))

    @pl.when(m == 0)
    def _():
        for g in range(ahead):
            for cp in chunk_copies(g):
                cp.start()

    u = u_ref[...]
    for ci in range(n_chunks):
        g = m * n_chunks + ci
        slot = g % n_slots
        for cp in chunk_copies(g):
            cp.wait()

        @pl.when(g + ahead < total)
        def _():
            for cp in chunk_copies(g + ahead):
                cp.start()

        hg = jnp.dot(u, wgbuf[slot], preferred_element_type=F32)
        hu = jnp.dot(u, wubuf[slot], preferred_element_type=F32)
        h = (_silu(hg) * hu).astype(BF16)
        part = jnp.dot(h, wdbuf[slot], preferred_element_type=F32)
        if ci == 0:
            acc_ref[...] = part
        else:
            acc_ref[...] += part

    r = ALPHA * x_ref[...] + (1.0 + g2_ref[...]) * acc_ref[...]
    out_ref[...] = _layer_norm_rows(r, lnw_ref[...], lnb_ref[...])


def _ffn(u2, wg, wu, wd, x1, g2, ln_w, ln_b, seq):
    N, D = u2.shape
    tm = 512
    n_slots = 3
    tpb = seq // tm
    row = lambda m: (m, 0)
    const = lambda m: (0, 0)
    hbm = pl.BlockSpec(memory_space=pl.ANY)
    return pl.pallas_call(
        _ffn_kernel,
        out_shape=jax.ShapeDtypeStruct((N, D), F32),
        grid=(N // tm,),
        in_specs=[pl.BlockSpec((tm, D), row), hbm, hbm, hbm,
                  pl.BlockSpec((tm, D), row),
                  pl.BlockSpec((None, 1, D), lambda m: (m // tpb, 0, 0)),
                  pl.BlockSpec((1, D), const), pl.BlockSpec((1, D), const)],
        out_specs=pl.BlockSpec((tm, D), row),
        scratch_shapes=[pltpu.VMEM((tm, D), F32),
                        pltpu.VMEM((n_slots, D, FFN_CW), BF16), pltpu.VMEM((n_slots, D, FFN_CW), BF16),
                        pltpu.VMEM((n_slots, FFN_CW, D), BF16),
                        pltpu.SemaphoreType.DMA((n_slots,))],
        compiler_params=_params(("arbitrary",)),
        name="ffn_ln",
    )(u2, wg, wu, wd, x1, g2, ln_w.reshape(1, -1), ln_b.reshape(1, -1))


MOE_TM = 768
MOE_CW = 256


def _row_copy(src, src_row, dst, dst_row, sem):
    return pltpu.make_async_copy(src.at[pl.ds(src_row, 1), :], dst.at[pl.ds(dst_row, 1), :], sem)


def _start_rows(n_rows, make_copy):
    def body(i, carry):
        base = pl.multiple_of(i * 8, 8)
        for j in range(8):
            make_copy(base, j).start(priority=j % 2)
        return carry
    lax.fori_loop(0, n_rows // 8, body, 0)


def _wait_rows(n_rows, one_copy):
    def body(i, carry):
        one_copy.wait()
        return carry
    lax.fori_loop(0, n_rows, body, 0, unroll=8)


def _dispatch_kernel(pos_ref, gap_ref, u_ref, out_ref, zbuf, sem, zsem):
    tm = u_ref.shape[0]
    first = pl.program_id(0) * tm

    def start(r, carry):
        for k in range(2):
            _row_copy(u_ref, r, out_ref, pos_ref[2 * (first + r) + k], sem).start(priority=k)
        return carry

    lax.fori_loop(0, tm, start, 0, unroll=8)

    @pl.when(pl.program_id(0) == pl.num_programs(0) - 1)
    def _():
        zbuf[...] = jnp.zeros_like(zbuf)
        n_gaps = gap_ref.shape[0] // 2

        def block_copy(i):
            return pltpu.make_async_copy(zbuf, out_ref.at[pl.ds(pl.multiple_of(i * 8, 8), 8), :], zsem.at[1])

        for g in range(n_gaps):
            lo, hi = gap_ref[2 * g], gap_ref[2 * g + 1]
            mid = jnp.minimum((lo + 7) // 8 * 8, hi)
            lax.fori_loop(lo, mid, lambda r, c: (_row_copy(zbuf, 0, out_ref, r, zsem.at[0]).start(), c)[1], 0)
            lax.fori_loop(mid // 8, hi // 8, lambda i, c: (block_copy(i).start(), c)[1], 0)
        for g in range(n_gaps):
            lo, hi = gap_ref[2 * g], gap_ref[2 * g + 1]
            mid = jnp.minimum((lo + 7) // 8 * 8, hi)
            lax.fori_loop(lo, mid, lambda r, c: (_row_copy(zbuf, 0, out_ref, 0, zsem.at[0]).wait(), c)[1], 0)
            lax.fori_loop(mid // 8, hi // 8, lambda i, c: (block_copy(0).wait(), c)[1], 0)

    _wait_rows(2 * tm, _row_copy(u_ref, 0, out_ref, 0, sem))


def _dispatch(u2, pos, gaps, n_rows):
    N, D = u2.shape
    tm = 256
    return pl.pallas_call(
        _dispatch_kernel,
        out_shape=jax.ShapeDtypeStruct((n_rows, D), u2.dtype),
        grid_spec=pltpu.PrefetchScalarGridSpec(
            num_scalar_prefetch=2,
            grid=(N // tm,),
            in_specs=[pl.BlockSpec((tm, D), lambda m, pos, gaps: (m, 0))],
            out_specs=pl.BlockSpec(memory_space=pl.ANY),
            scratch_shapes=[pltpu.VMEM((8, D), u2.dtype), pltpu.SemaphoreType.DMA(()),
                            pltpu.SemaphoreType.DMA((2,))]),
        compiler_params=_params(("arbitrary",)),
        name="moe_dispatch",
    )(pos, gaps, u2)


def _moe_ffn_kernel(te_ref, nv_ref, u_ref, wg_hbm, wu_hbm, wd_hbm, out_ref, wgbuf, wubuf, wdbuf, sem):
    t = pl.program_id(0)
    nv = nv_ref[0]
    F = wg_hbm.shape[2]
    n_chunks = F // MOE_CW
    n_slots = wgbuf.shape[0]
    ahead = n_slots - 1

    def chunk_copies(g):
        tile, ci = g // n_chunks, g % n_chunks
        e = te_ref[tile]
        slot = g % n_slots
        cols = pl.ds(pl.multiple_of(ci * MOE_CW, MOE_CW), MOE_CW)
        return (pltpu.make_async_copy(wg_hbm.at[e, :, cols], wgbuf.at[slot], sem.at[slot]),
                pltpu.make_async_copy(wu_hbm.at[e, :, cols], wubuf.at[slot], sem.at[slot]),
                pltpu.make_async_copy(wd_hbm.at[e, cols, :], wdbuf.at[slot], sem.at[slot]))

    @pl.when(t == 0)
    def _():
        for g in range(ahead):
            for cp in chunk_copies(g):
                cp.start()

    @pl.when(t < nv)
    def _():
        u = u_ref[...].astype(BF16)
        for ci in range(n_chunks):
            g = t * n_chunks + ci
            slot = g % n_slots
            for cp in chunk_copies(g):
                cp.wait()

            @pl.when(g + ahead < nv * n_chunks)
            def _():
                for cp in chunk_copies(g + ahead):
                    cp.start()

            hg = jnp.dot(u, wgbuf[slot].astype(BF16), preferred_element_type=F32)
            hu = jnp.dot(u, wubuf[slot].astype(BF16), preferred_element_type=F32)
            h = (_silu(hg) * hu).astype(BF16)
            part = jnp.dot(h, wdbuf[slot].astype(BF16), preferred_element_type=F32)
            if ci == 0:
                out_ref[...] = part
            else:
                out_ref[...] += part

    @pl.when(t >= nv)
    def _():
        out_ref[...] = jnp.zeros_like(out_ref)


def _moe_ffn(u_sorted, tile_expert, n_valid, wg, wu, wd):
    R, D = u_sorted.shape
    tm = MOE_TM
    n_slots = 3
    tile = lambda t, nv: jnp.minimum(t, nv[0] - 1)
    return pl.pallas_call(
        _moe_ffn_kernel,
        out_shape=jax.ShapeDtypeStruct((R, D), F32),
        grid_spec=pltpu.PrefetchScalarGridSpec(
            num_scalar_prefetch=2,
            grid=(R // tm,),
            in_specs=[pl.BlockSpec((tm, D), lambda t, te, nv: (tile(t, nv), 0)),
                      pl.BlockSpec(memory_space=pl.ANY),
                      pl.BlockSpec(memory_space=pl.ANY),
                      pl.BlockSpec(memory_space=pl.ANY)],
            out_specs=pl.BlockSpec((tm, D), lambda t, te, nv: (t, 0)),
            scratch_shapes=[pltpu.VMEM((n_slots, D, MOE_CW), F32), pltpu.VMEM((n_slots, D, MOE_CW), F32),
                            pltpu.VMEM((n_slots, MOE_CW, D), F32),
                            pltpu.SemaphoreType.DMA((n_slots,))]),
        compiler_params=_params(("arbitrary",), vmem=60 * 1024 * 1024),
        name="moe_ffn",
    )(tile_expert, n_valid, u_sorted, wg, wu, wd)


def _combine_kernel(pos_ref, y_ref, topw_ref, x_ref, g2_ref, lnw_ref, lnb_ref, out_ref, ybuf, sem):
    tm = x_ref.shape[0]
    m = pl.program_id(0)
    slot = m % 2

    def gather(step, buf):
        for k in range(2):
            _start_rows(tm, lambda base, j: _row_copy(y_ref, pos_ref[2 * (step * tm + base + j) + k],
                                                      ybuf.at[buf, k], base + j, sem.at[buf]))

    @pl.when(m == 0)
    def _():
        gather(0, 0)

    @pl.when(m + 1 < pl.num_programs(0))
    def _():
        gather(m + 1, 1 - slot)

    _wait_rows(2 * tm, _row_copy(y_ref, 0, ybuf.at[slot, 0], 0, sem.at[slot]))
    tw = topw_ref[...]
    y = tw[:, 0:1] * ybuf[slot, 0] + tw[:, 1:2] * ybuf[slot, 1]
    r = ALPHA * x_ref[...] + (1.0 + g2_ref[...]) * y
    out_ref[...] = _layer_norm_rows(r, lnw_ref[...], lnb_ref[...])


def _combine(y_sorted, pos, topw, x1, g2, ln_w, ln_b, seq):
    N, D = x1.shape
    tm = 256
    tpb = seq // tm
    row = lambda m, pos: (m, 0)
    const = lambda m, pos: (0, 0)
    return pl.pallas_call(
        _combine_kernel,
        out_shape=jax.ShapeDtypeStruct((N, D), F32),
        grid_spec=pltpu.PrefetchScalarGridSpec(
            num_scalar_prefetch=1,
            grid=(N // tm,),
            in_specs=[pl.BlockSpec(memory_space=pl.ANY),
                      pl.BlockSpec((tm, LANES), row),
                      pl.BlockSpec((tm, D), row),
                      pl.BlockSpec((None, 1, D), lambda m, pos: (m // tpb, 0, 0)),
                      pl.BlockSpec((1, D), const), pl.BlockSpec((1, D), const)],
            out_specs=pl.BlockSpec((tm, D), row),
            scratch_shapes=[pltpu.VMEM((2, 2, tm, D), F32), pltpu.SemaphoreType.DMA((2,))]),
        compiler_params=_params(("arbitrary",)),
        name="moe_combine_ln",
    )(pos, y_sorted, topw, x1, g2, ln_w.reshape(1, -1), ln_b.reshape(1, -1))


def _routing_plan(ids, n_tiles):
    e_flat = ids[:, :2].reshape(-1)
    onehot = (e_flat[:, None] == jnp.arange(N_EXPERTS, dtype=jnp.int32)[None, :]).astype(jnp.int32)
    csum = jnp.cumsum(onehot, axis=0)
    rank = jnp.sum((csum - onehot) * onehot, axis=1)
    counts = csum[-1]
    padded = ((counts + MOE_TM - 1) // MOE_TM) * MOE_TM
    ends = jnp.cumsum(padded)
    offs = ends - padded
    pos = (jnp.sum(onehot * offs[None, :], axis=1) + rank).astype(jnp.int32)
    tile_start = jnp.arange(n_tiles, dtype=jnp.int32) * MOE_TM
    n_valid = (ends[-1] // MOE_TM).astype(jnp.int32)
    tile_start = jnp.minimum(tile_start, (n_valid - 1) * MOE_TM)
    tile_expert = jnp.sum((tile_start[:, None] >= ends[None, :]).astype(jnp.int32), axis=1).astype(jnp.int32)
    total = jnp.full((1,), n_tiles * MOE_TM, jnp.int32)
    gaps = jnp.stack([jnp.concatenate([offs + counts, ends[-1:]]),
                      jnp.concatenate([ends, total])], axis=1).reshape(-1).astype(jnp.int32)
    return pos, tile_expert, n_valid.reshape(1), gaps


def _moe(u2, ids, topw, wg, wu, wd, x1, g2, ln_w, ln_b, seq):
    N = u2.shape[0]
    n_tiles = -(-2 * N // MOE_TM) + N_EXPERTS
    pos, tile_expert, n_valid, gaps = _routing_plan(ids, n_tiles)
    u_sorted = _dispatch(u2, pos, gaps, n_tiles * MOE_TM)
    y_sorted = _moe_ffn(u_sorted, tile_expert, n_valid, wg, wu, wd)
    return _combine(y_sorted, pos, topw, x1, g2, ln_w, ln_b, seq)


def _split_w_in_kernel(wt_ref, gt_ref, main_ref, gate_ref):
    main_ref[...] = wt_ref[0].T.astype(BF16)

    @pl.when(pl.program_id(1) == 0)
    def _():
        g = gt_ref[...]
        g = jnp.concatenate([g, jnp.zeros((LANES - g.shape[0], g.shape[1]), F32)], axis=0)
        gate_ref[...] = g.T.astype(BF16)


def _split_w_in(w_in):
    L, D, C = w_in.shape
    w_t = jnp.swapaxes(w_in, 1, 2)
    g0 = 4 * GROUP_WIDTH
    ng = 2 * N_HEADS
    tc = 512
    src_col = lambda j: (j * (tc // ng) + jnp.where(j * tc >= g0, 1, 0)) * ng
    return pl.pallas_call(
        _split_w_in_kernel,
        out_shape=(jax.ShapeDtypeStruct((L, D, C - ng), BF16),
                   jax.ShapeDtypeStruct((L, D, LANES), BF16)),
        grid=(L, (C - ng) // tc),
        in_specs=[pl.BlockSpec((pl.Element(1), pl.Element(tc), pl.Element(D)),
                               lambda l, j: (l, src_col(j), 0)),
                  pl.BlockSpec((None, ng, D), lambda l, j: (l, g0 // ng, 0))],
        out_specs=(pl.BlockSpec((None, D, tc), lambda l, j: (l, 0, j)),
                   pl.BlockSpec((None, D, LANES), lambda l, j: (l, 0, 0))),
        compiler_params=_params(("parallel", "arbitrary")),
        name="split_w_in",
    )(w_t, w_t)


def kernel(x, c, w_in, mlstm_conv_w, mlstm_conv_b, mlstm_gate_b, mlstm_norm_w, pool_w, pool_scale, conv_dw_w, conv_dw_b, conv_ln_w, conv_ln_b, conv_pw_w, conv_pw_b, group_norm_w, w_out, ada_w, ada_b, ln1_w, ln1_b, ln2_w, ln2_b, ffn_w_gate, ffn_w_up, ffn_w_down, moe_router_w, moe_router_b, moe_w_gate, moe_w_up, moe_w_down):
    B, S, D = x.shape
    GW = GROUP_WIDTH
    ada = _ada_all(c, ada_w, ada_b)
    x2d = x.reshape(B * S, D)
    w_main, w_gate = _split_w_in(w_in)
    for l in range(DEPTH):
        mod = [ada[l, :, i * D:(i + 1) * D].reshape(B, 1, D) for i in range(6)]
        sh1, sc1, g1, sh2, sc2, g2 = mod
        proj, gates = _in_proj(x2d, sc1, sh1, w_main, w_gate, l, S)
        proj = proj.reshape(B, S, -1)
        gates = gates.reshape(B, S, LANES)
        gn_b, gn_c, gn_d = (group_norm_w[l, i * GW:(i + 1) * GW] for i in range(3))
        ya = _mlstm_mixer(proj, gates, mlstm_conv_w[l], mlstm_conv_b[l], mlstm_gate_b[l],
                          mlstm_norm_w[l], B, S)
        yb = _pool_mixer(proj, pool_w[l], pool_scale[l], gn_b, B, S)
        yc = _sb_mixer(proj, B, S)
        yd = _conv_mixer(proj, conv_dw_w[l], conv_dw_b[l], conv_ln_w[l], conv_ln_b[l],
                         conv_pw_w[l], conv_pw_b[l], gn_d, B, S)
        flat = lambda t: t.reshape(B * S, GW)
        j = l // 2
        router = None
        if l % 2 == 1:
            rw = jnp.zeros((D, LANES), F32).at[:, :N_EXPERTS].set(moe_router_w[j])
            rw_hi = rw.astype(BF16)
            rw_lo = (rw - rw_hi.astype(F32)).astype(BF16)
            rb = jnp.zeros((1, LANES), F32).at[0, :N_EXPERTS].set(moe_router_b[j])
            router = (jnp.concatenate([rw_hi, rw_lo], axis=1), rb)
        outs = _out_proj(flat(ya), flat(yb), flat(yc), flat(yd), gn_c, w_out[l].astype(BF16), x2d,
                         g1, ln1_w[l], ln1_b[l], sc2, sh2, S, router)
        if l % 2 == 0:
            x1, u2 = outs
            x2d = _ffn(u2, ffn_w_gate[j].astype(BF16), ffn_w_up[j].astype(BF16),
                       ffn_w_down[j].astype(BF16), x1, g2, ln2_w[l], ln2_b[l], S)
        else:
            x1, u2, ids, topw = outs
            x2d = _moe(u2, ids, topw, moe_w_gate[j], moe_w_up[j], moe_w_down[j], x1, g2,
                       ln2_w[l], ln2_b[l], S)
    return x2d.reshape(B, S, D)
```

```python
import functools

import jax
import jax.numpy as jnp
from jax import lax
from jax.experimental import pallas as pl
from jax.experimental.pallas import tpu as pltpu

F32 = jnp.float32
BF16 = jnp.bfloat16

DEPTH = 2
CHUNK = 256
GROUP_WIDTH = 512
N_HEADS = 4
HEAD_DIM = 128
MLSTM_CONV = 4
POOL_WINDOWS = (2, 4, 8, 16)
CONV_WIDTH = 31
SB_KEY_BLOCK = 256
N_EXPERTS = 8
ALPHA = (2.0 * DEPTH) ** 0.25
EPS = 1e-5
LOG2E = 1.4426950408889634
LANES = 128
VMEM_LIMIT = 56 * 1024 * 1024


def _params(sem, vmem=VMEM_LIMIT):
    return pltpu.CompilerParams(dimension_semantics=sem, vmem_limit_bytes=vmem)


def _silu(x):
    return x * (1.0 / (1.0 + jnp.exp(-x)))


def _sigmoid(x):
    return 1.0 / (1.0 + jnp.exp(-x))


def _neg_softplus(x):
    return -(jnp.maximum(x, 0.0) + jnp.log(1.0 + jnp.exp(-jnp.abs(x))))


def _layer_norm_rows(r, w, b):
    mu = jnp.mean(r, axis=-1, keepdims=True)
    d = r - mu
    var = jnp.mean(d * d, axis=-1, keepdims=True)
    return d * lax.rsqrt(var + EPS) * w + b


def _rms_rows(y, w):
    return y * lax.rsqrt(jnp.mean(y * y, axis=-1, keepdims=True) + EPS) * w


def _bdot(a, b):
    return jnp.dot(a.astype(BF16), b.astype(BF16), preferred_element_type=F32)


def _ada_kernel(c_ref, w_ref, b_ref, out_ref):
    out_ref[...] = _bdot(_silu(c_ref[...]), w_ref[...]) + b_ref[...]


def _ada_all(c, ada_w, ada_b):
    L, D, D6 = ada_w.shape
    Bn = c.shape[0]
    tn = 1024
    return pl.pallas_call(
        _ada_kernel,
        out_shape=jax.ShapeDtypeStruct((L, Bn, D6), F32),
        grid=(L, D6 // tn),
        in_specs=[pl.BlockSpec((Bn, D), lambda l, n: (0, 0)),
                  pl.BlockSpec((None, D, tn), lambda l, n: (l, 0, n)),
                  pl.BlockSpec((None, 1, tn), lambda l, n: (l, 0, n))],
        out_specs=pl.BlockSpec((None, Bn, tn), lambda l, n: (l, 0, n)),
        compiler_params=_params(("parallel", "parallel")),
        name="ada_mod",
    )(c, ada_w, ada_b.reshape(L, 1, D6))


INPROJ_CW = 512


def _inproj_kernel(layer, x_ref, sc_ref, sh_ref, w_hbm, wg_ref, proj_hbm, gates_ref,
                   wbuf, obuf, wsem, osem):
    m = pl.program_id(0)
    tm = x_ref.shape[0]
    n_chunks = proj_hbm.shape[1] // INPROJ_CW
    total = pl.num_programs(0) * n_chunks
    n_wslots = wbuf.shape[0]
    n_oslots = obuf.shape[0]
    ahead = n_wslots - 1

    def cols_of(g):
        return pl.ds(pl.multiple_of((g % n_chunks) * INPROJ_CW, INPROJ_CW), INPROJ_CW)

    def w_copy(g):
        slot = g % n_wslots
        return pltpu.make_async_copy(w_hbm.at[layer, :, cols_of(g)], wbuf.at[slot], wsem.at[slot])

    def o_copy(g):
        slot = g % n_oslots
        rows = pl.ds(pl.multiple_of((g // n_chunks) * tm, tm), tm)
        return pltpu.make_async_copy(obuf.at[slot], proj_hbm.at[rows, cols_of(g)], osem.at[slot])

    @pl.when(m == 0)
    def _():
        for g in range(ahead):
            w_copy(g).start()

    u = (x_ref[...] * (1.0 + sc_ref[...]) + sh_ref[...]).astype(BF16)
    gates_ref[...] = jnp.dot(u, wg_ref[...], preferred_element_type=F32)
    for ci in range(n_chunks):
        g = m * n_chunks + ci
        w_copy(g).wait()

        @pl.when(g + ahead < total)
        def _():
            w_copy(g + ahead).start()

        @pl.when(g >= n_oslots)
        def _():
            o_copy(g - n_oslots).wait()

        obuf[g % n_oslots] = jnp.dot(u, wbuf[g % n_wslots], preferred_element_type=F32)
        o_copy(g).start()

    @pl.when(m == pl.num_programs(0) - 1)
    def _():
        for g in range(total - n_oslots, total):
            o_copy(g).wait()


def _in_proj(x2d, sc, sh, w_main, w_gate, layer, seq):
    N, D = x2d.shape
    NC = w_main.shape[2]
    tm = 1024
    tpb = seq // tm
    return pl.pallas_call(
        functools.partial(_inproj_kernel, layer),
        out_shape=(jax.ShapeDtypeStruct((N, NC), F32), jax.ShapeDtypeStruct((N, LANES), F32)),
        grid=(N // tm,),
        in_specs=[pl.BlockSpec((tm, D), lambda m: (m, 0)),
                  pl.BlockSpec((None, 1, D), lambda m: (m // tpb, 0, 0)),
                  pl.BlockSpec((None, 1, D), lambda m: (m // tpb, 0, 0)),
                  pl.BlockSpec(memory_space=pl.ANY),
                  pl.BlockSpec((None, D, LANES), lambda m: (layer, 0, 0))],
        out_specs=(pl.BlockSpec(memory_space=pl.ANY),
                   pl.BlockSpec((tm, LANES), lambda m: (m, 0))),
        scratch_shapes=[pltpu.VMEM((3, D, INPROJ_CW), BF16), pltpu.VMEM((2, tm, INPROJ_CW), F32),
                        pltpu.SemaphoreType.DMA((3,)), pltpu.SemaphoreType.DMA((2,))],
        compiler_params=_params(("arbitrary",)),
        name="in_proj",
    )(x2d, sc, sh, w_main, w_gate)


def _cumsum_rows(x):
    n = x.shape[0]
    row = lax.broadcasted_iota(jnp.int32, x.shape, 0)
    k = 1
    while k < n:
        x = x + jnp.where(row >= k, pltpu.roll(x, k, axis=0), 0.0)
        k *= 2
    return x


def _mlstm_kernel(qk_ref, v_ref, o_ref, g_ref, cw_ref, cb_ref, gb_ref, nw_ref, out_ref,
                  c_state, n_state, m_state, tail_ref):
    @pl.when(pl.program_id(1) == 0)
    def _():
        c_state[...] = jnp.zeros_like(c_state)
        n_state[...] = jnp.zeros_like(n_state)
        m_state[...] = jnp.zeros_like(m_state)
        tail_ref[...] = jnp.zeros_like(tail_ref)

    for bb in range(qk_ref.shape[0]):
        _mlstm_block(qk_ref.at[bb], v_ref.at[bb], o_ref.at[bb], g_ref.at[bb], cw_ref, cb_ref, gb_ref,
                     nw_ref, out_ref.at[bb], c_state.at[bb], n_state.at[bb], m_state.at[bb],
                     tail_ref.at[bb])


def _mlstm_block(qk_ref, v_ref, o_ref, g_ref, cw_ref, cb_ref, gb_ref, nw_ref, out_ref,
                 c_state, n_state, m_state, tail_ref):
    T = qk_ref.shape[0]
    GW = GROUP_WIDTH

    xx = jnp.concatenate([tail_ref[...], qk_ref[...]], axis=0)
    tail_ref[...] = qk_ref[T - 8:T, :]
    conv = cb_ref[...]
    for j in range(MLSTM_CONV):
        off = 8 - (MLSTM_CONV - 1) + j
        tap = xx[off:off + T, :] if off % 8 == 0 else pltpu.roll(xx, T + 8 - off, axis=0)[0:T, :]
        conv = conv + cw_ref[j:j + 1, :] * tap
    qk = _silu(conv)

    gates = g_ref[...] + gb_ref[...]
    logf = _neg_softplus(-gates)
    causal = (lax.broadcasted_iota(jnp.int32, (CHUNK, CHUNK), 1)
              <= lax.broadcasted_iota(jnp.int32, (CHUNK, CHUNK), 0))

    for c in range(T // CHUNK):
        rows = slice(c * CHUNK, (c + 1) * CHUNK)
        gi = gates[rows, :]
        bcum = _cumsum_rows(logf[rows, :])
        gi_t = gi.T
        bcum_t = bcum.T
        for h in range(N_HEADS):
            cols = slice(h * HEAD_DIM, (h + 1) * HEAD_DIM)
            q = qk[rows, cols] * (HEAD_DIM ** -0.5)
            k = qk[rows, GW + h * HEAD_DIM:GW + (h + 1) * HEAD_DIM]
            v = v_ref[rows, cols]
            b_col = bcum[:, N_HEADS + h:N_HEADS + h + 1]
            b_row = bcum_t[N_HEADS + h:N_HEADS + h + 1, :]
            i_col = gi[:, h:h + 1]
            i_row = gi_t[h:h + 1, :]
            g_tot = b_col[CHUNK - 1:CHUNK, :]
            m_prev = m_state[h:h + 1, 0:1]
            c_prev = c_state[h]
            n_prev = n_state[h:h + 1, :]

            log_d = jnp.where(causal, b_col - b_row + i_row, -jnp.inf)
            m_inter = b_col + m_prev
            m_t = jnp.maximum(m_inter, jnp.max(log_d, axis=-1, keepdims=True))
            s = lax.dot_general(q.astype(BF16), k.astype(BF16), (((1,), (1,)), ((), ())),
                                preferred_element_type=F32)
            w = jnp.exp(log_d - m_t) * s
            inter = jnp.exp(m_inter - m_t)
            num = inter * _bdot(q, c_prev) + _bdot(w, v)
            den = inter * jnp.sum(q * n_prev, axis=-1, keepdims=True) + jnp.sum(w, axis=-1, keepdims=True)
            hcur = num / jnp.maximum(jnp.abs(den), jnp.exp(-m_t))

            a_col = g_tot - b_col + i_col
            m_new = jnp.maximum(g_tot + m_prev, jnp.max(a_col, axis=0, keepdims=True))
            decay = jnp.exp(g_tot + m_prev - m_new)
            kw = k * jnp.exp(a_col - m_new)
            c_state[h] = decay * c_prev + _bdot(kw.T, v)
            n_state[h:h + 1, :] = decay * n_prev + jnp.sum(kw, axis=0, keepdims=True)
            m_state[h:h + 1, :] = jnp.broadcast_to(m_new, (1, LANES))

            mu = jnp.mean(hcur, axis=-1, keepdims=True)
            dlt = hcur - mu
            var = jnp.mean(dlt * dlt, axis=-1, keepdims=True)
            hn = dlt * lax.rsqrt(var + EPS) * nw_ref[:, cols]
            out_ref[rows, cols] = (_sigmoid(o_ref[rows, cols]) * hn).astype(out_ref.dtype)


def _mlstm_mixer(proj, gates, conv_w, conv_b, gate_b, norm_w, B, S):
    T = 256
    NB = 2
    GW = GROUP_WIDTH
    gb = jnp.zeros((1, LANES), F32).at[0, :2 * N_HEADS].set(gate_b)
    return pl.pallas_call(
        _mlstm_kernel,
        out_shape=jax.ShapeDtypeStruct((B, S, GW), BF16),
        grid=(B // NB, S // T),
        in_specs=[pl.BlockSpec((NB, T, 2 * GW), lambda b, t: (b, t, 0)),
                  pl.BlockSpec((NB, T, GW), lambda b, t: (b, t, 2)),
                  pl.BlockSpec((NB, T, GW), lambda b, t: (b, t, 3)),
                  pl.BlockSpec((NB, T, LANES), lambda b, t: (b, t, 0)),
                  pl.BlockSpec((MLSTM_CONV, 2 * GW), lambda b, t: (0, 0)),
                  pl.BlockSpec((1, 2 * GW), lambda b, t: (0, 0)),
                  pl.BlockSpec((1, LANES), lambda b, t: (0, 0)),
                  pl.BlockSpec((1, GW), lambda b, t: (0, 0))],
        out_specs=pl.BlockSpec((NB, T, GW), lambda b, t: (b, t, 0)),
        scratch_shapes=[pltpu.VMEM((NB, N_HEADS, HEAD_DIM, HEAD_DIM), F32),
                        pltpu.VMEM((NB, 8, LANES), F32),
                        pltpu.VMEM((NB, 8, LANES), F32),
                        pltpu.VMEM((NB, 8, 2 * GW), F32)],
        compiler_params=_params(("parallel", "arbitrary")),
        name="mlstm_mixer",
    )(proj, proj, proj, gates, conv_w, conv_b.reshape(1, -1), gb, norm_w.reshape(1, -1))


def _pool_kernel(x_ref, w_ref, scale_ref, gn_ref, out_ref, tail_ref):
    T = x_ref.shape[0]
    HALO = 16
    t_blk = pl.program_id(1)

    @pl.when(t_blk == 0)
    def _():
        tail_ref[...] = jnp.zeros_like(tail_ref)

    x = x_ref[...]
    xx = jnp.concatenate([tail_ref[...], x], axis=0)
    tail_ref[...] = x[T - HALO:T, :]
    pos = t_blk * T + lax.broadcasted_iota(jnp.int32, (T, 1), 0) + 1
    outs = []
    for g, win in enumerate(POOL_WINDOWS):
        cols = slice(g * LANES, (g + 1) * LANES)
        s = xx[:, cols]
        k = 1
        while k < win:
            s = s + pltpu.roll(s, k, axis=0)
            k *= 2
        cnt = jnp.minimum(pos, win).astype(F32)
        yg = s[HALO:, :] / cnt - x[:, cols]
        outs.append(_bdot(yg, w_ref[g]))
    y = jnp.concatenate(outs, axis=-1) * scale_ref[...]
    out_ref[...] = _rms_rows(y, gn_ref[...]).astype(out_ref.dtype)


def _pool_mixer(proj, pool_w, pool_scale, gn, B, S):
    T = 512
    GW = GROUP_WIDTH
    return pl.pallas_call(
        _pool_kernel,
        out_shape=jax.ShapeDtypeStruct((B, S, GW), BF16),
        grid=(B, S // T),
        in_specs=[pl.BlockSpec((None, T, GW), lambda b, t: (b, t, 4)),
                  pl.BlockSpec((len(POOL_WINDOWS), LANES, LANES), lambda b, t: (0, 0, 0)),
                  pl.BlockSpec((1, GW), lambda b, t: (0, 0)),
                  pl.BlockSpec((1, GW), lambda b, t: (0, 0))],
        out_specs=pl.BlockSpec((None, T, GW), lambda b, t: (b, t, 0)),
        scratch_shapes=[pltpu.VMEM((16, GW), F32)],
        compiler_params=_params(("parallel", "arbitrary")),
        name="pool_mixer",
    )(proj, pool_w.astype(BF16), pool_scale.reshape(1, -1), gn.reshape(1, -1))


def _sb_kernel(q_ref, k_ref, v_ref, out_ref, kb_ref, vb_ref):
    LQ = q_ref.shape[0]
    LK = SB_KEY_BLOCK
    ratio = LQ // LK
    qi = pl.program_id(1)

    @pl.when(qi == 0)
    def _():
        kb_ref[...] = k_ref[...].astype(BF16)
        vb_ref[...] = v_ref[...].astype(BF16)

    scale2 = HEAD_DIM ** -0.5 * LOG2E
    tri = (lax.broadcasted_iota(jnp.int32, (LK, LK), 0)
           > lax.broadcasted_iota(jnp.int32, (LK, LK), 1)).astype(BF16)
    row = lax.broadcasted_iota(jnp.int32, (LQ, LK), 0)
    col = lax.broadcasted_iota(jnp.int32, (LQ, LK), 1)
    heads = [slice(h * HEAD_DIM, (h + 1) * HEAD_DIM) for h in range(N_HEADS)]
    qs = [(q_ref[:, hs] * scale2).astype(BF16) for hs in heads]

    def sweep(kb, carry, strict):
        start = pl.multiple_of(kb * LK, LK)
        zs, lks = [], []
        for h, hs in enumerate(heads):
            k_blk = kb_ref[pl.ds(start, LK), hs]
            z = lax.dot_general(qs[h], k_blk, (((1,), (1,)), ((), ())), preferred_element_type=F32)
            lk = -(jnp.maximum(z, 0.0) + jnp.log2(1.0 + jnp.exp2(-jnp.abs(z))))
            if strict is not None:
                lk = jnp.where(strict, lk, 0.0)
            zs.append(z)
            lks.append(lk)
        inner = jnp.dot(jnp.concatenate([lk.astype(BF16) for lk in lks], axis=0), tri,
                        preferred_element_type=F32)
        new = []
        for h, hs in enumerate(heads):
            after, acc = carry[h]
            v_blk = vb_ref[pl.ds(start, LK), hs]
            a = jnp.exp2(zs[h] + lks[h] + inner[h * LQ:(h + 1) * LQ, :] + after)
            if strict is not None:
                a = jnp.where(strict, a, 0.0)
            acc = acc + jnp.dot(a.astype(BF16), v_blk, preferred_element_type=F32)
            after = after + jnp.sum(lks[h], axis=-1, keepdims=True)
            new.append((after, acc))
        return tuple(new)

    init = tuple((jnp.zeros((LQ, 1), F32), jnp.zeros((LQ, HEAD_DIM), F32)) for _ in heads)
    carry = init
    for j in reversed(range(ratio)):
        carry = sweep(ratio * qi + j, carry, (j * LK + col) < row)
    carry = lax.fori_loop(0, ratio * qi, lambda it, c: sweep(ratio * qi - 1 - it, c, None), carry)
    for h, hs in enumerate(heads):
        out_ref[:, hs] = carry[h][1]


def _sb_mixer(proj, B, S):
    LQ = 512
    GW = GROUP_WIDTH
    return pl.pallas_call(
        _sb_kernel,
        out_shape=jax.ShapeDtypeStruct((B, S, GW), F32),
        grid=(B, S // LQ),
        in_specs=[pl.BlockSpec((None, LQ, GW), lambda b, i: (b, i, 5)),
                  pl.BlockSpec((None, S, GW), lambda b, i: (b, 0, 6)),
                  pl.BlockSpec((None, S, GW), lambda b, i: (b, 0, 7))],
        out_specs=pl.BlockSpec((None, LQ, GW), lambda b, i: (b, i, 0)),
        scratch_shapes=[pltpu.VMEM((S, GW), BF16), pltpu.VMEM((S, GW), BF16)],
        compiler_params=_params(("parallel", "arbitrary")),
        name="sb_attention",
    )(proj, proj, proj)


def _conv_kernel(x_ref, dw_ref, dwb_ref, lnw_ref, lnb_ref, pw_ref, pwb_ref, gn_ref, out_ref,
                 hbuf, ybuf):
    T = x_ref.shape[0]
    GW = GROUP_WIDTH
    HALO = 32
    SUB = 64

    @pl.when(pl.program_id(1) == 0)
    def _():
        hbuf[0:HALO, :] = jnp.zeros((HALO, GW), F32)

    @pl.when(pl.program_id(1) > 0)
    def _():
        hbuf[0:HALO, :] = hbuf[T:T + HALO, :]

    hbuf[HALO:HALO + T, :] = x_ref[:, 0:GW] * _sigmoid(x_ref[:, GW:2 * GW])

    def sub(i, carry):
        r0 = pl.multiple_of(i * SUB, SUB)
        acc = jnp.broadcast_to(dwb_ref[...], (SUB, GW))
        win = hbuf[pl.ds(r0, SUB + HALO), :]
        for res in range(8):
            shifted = win if res == 0 else pltpu.roll(win, SUB + HALO - res, axis=0)
            for j in range(CONV_WIDTH):
                off = HALO - CONV_WIDTH + 1 + j
                if off % 8 == res:
                    base = off - res
                    acc = acc + dw_ref[j:j + 1, :] * shifted[base:base + SUB, :]
        ybuf[pl.ds(r0, SUB), :] = acc
        return carry

    lax.fori_loop(0, T // SUB, sub, 0)
    hn = _silu(_layer_norm_rows(ybuf[...], lnw_ref[...], lnb_ref[...]))
    y = _bdot(hn, pw_ref[...]) + pwb_ref[...]
    out_ref[...] = _rms_rows(y, gn_ref[...]).astype(out_ref.dtype)


def _conv_mixer(proj, dw_w, dw_b, ln_w, ln_b, pw_w, pw_b, gn, B, S):
    T = 256
    GW = GROUP_WIDTH
    dw_pad = jnp.zeros((32, GW), F32).at[:CONV_WIDTH].set(dw_w)
    r = lambda a: a.reshape(1, -1)
    return pl.pallas_call(
        _conv_kernel,
        out_shape=jax.ShapeDtypeStruct((B, S, GW), BF16),
        grid=(B, S // T),
        in_specs=[pl.BlockSpec((None, T, 2 * GW), lambda b, t: (b, t, 4)),
                  pl.BlockSpec((32, GW), lambda b, t: (0, 0)),
                  pl.BlockSpec((1, GW), lambda b, t: (0, 0)),
                  pl.BlockSpec((1, GW), lambda b, t: (0, 0)),
                  pl.BlockSpec((1, GW), lambda b, t: (0, 0)),
                  pl.BlockSpec((GW, GW), lambda b, t: (0, 0)),
                  pl.BlockSpec((1, GW), lambda b, t: (0, 0)),
                  pl.BlockSpec((1, GW), lambda b, t: (0, 0))],
        out_specs=pl.BlockSpec((None, T, GW), lambda b, t: (b, t, 0)),
        scratch_shapes=[pltpu.VMEM((T + 32, GW), F32), pltpu.VMEM((T, GW), F32)],
        compiler_params=_params(("parallel", "arbitrary")),
        name="conv_mixer",
    )(proj, dw_pad, r(dw_b), r(ln_w), r(ln_b), pw_w.astype(BF16), r(pw_b), r(gn))


def _outproj_kernel(with_router, ya_ref, yb_ref, yc_ref, yd_ref, gnc_ref, w_ref, x_ref, g1_ref,
                    lnw_ref, lnb_ref, sc_ref, sh_ref, *rest):
    GW = GROUP_WIDTH
    if with_router:
        rw_ref, rb_ref, x1_ref, u2_ref, ids_ref, topw_ref = rest
    else:
        x1_ref, u2_ref = rest
    yc = _rms_rows(yc_ref[...], gnc_ref[...]).astype(BF16)
    acc = jnp.dot(ya_ref[...], w_ref[0:GW, :], preferred_element_type=F32)
    acc = acc + jnp.dot(yb_ref[...], w_ref[GW:2 * GW, :], preferred_element_type=F32)
    acc = acc + jnp.dot(yc, w_ref[2 * GW:3 * GW, :], preferred_element_type=F32)
    acc = acc + jnp.dot(yd_ref[...], w_ref[3 * GW:4 * GW, :], preferred_element_type=F32)
    r = ALPHA * x_ref[...] + (1.0 + g1_ref[...]) * acc
    x1 = _layer_norm_rows(r, lnw_ref[...], lnb_ref[...])
    x1_ref[...] = x1
    u2 = x1 * (1.0 + sc_ref[...]) + sh_ref[...]
    u2_ref[...] = u2.astype(u2_ref.dtype)
    if with_router:
        u_hi = u2.astype(BF16)
        u_lo = (u2 - u_hi.astype(F32)).astype(BF16)
        p = jnp.dot(u_hi, rw_ref[...], preferred_element_type=F32)
        logits = (p[:, :LANES] + p[:, LANES:]
                  + jnp.dot(u_lo, rw_ref[:, :LANES], preferred_element_type=F32) + rb_ref[...])
        lane = lax.broadcasted_iota(jnp.int32, logits.shape, 1)
        lg = jnp.where(lane < N_EXPERTS, logits, -jnp.inf)
        m1 = jnp.max(lg, axis=-1, keepdims=True)
        i1 = jnp.min(jnp.where(lg == m1, lane, LANES), axis=-1, keepdims=True)
        lg2 = jnp.where(lane == i1, -jnp.inf, lg)
        m2 = jnp.max(lg2, axis=-1, keepdims=True)
        i2 = jnp.min(jnp.where(lg2 == m2, lane, LANES), axis=-1, keepdims=True)
        e2 = jnp.exp(m2 - m1)
        w1 = 1.0 / (1.0 + e2)
        w2 = e2 / (1.0 + e2)
        ids_ref[...] = jnp.where(lane == 0, i1, jnp.where(lane == 1, i2, 0))
        topw_ref[...] = jnp.where(lane == 0, w1, jnp.where(lane == 1, w2, 0.0))


def _out_proj(ya, yb, yc, yd, gnc, w_out, x2d, g1, ln_w, ln_b, sc2, sh2, seq, router=None):
    N, D = x2d.shape
    GW = GROUP_WIDTH
    tm = 512
    tpb = seq // tm
    row = lambda m: (m, 0)
    const = lambda m: (0, 0)
    perb = lambda m: (m // tpb, 0, 0)
    in_specs = [pl.BlockSpec((tm, GW), row), pl.BlockSpec((tm, GW), row),
                pl.BlockSpec((tm, GW), row), pl.BlockSpec((tm, GW), row),
                pl.BlockSpec((1, GW), const),
                pl.BlockSpec((D, D), const),
                pl.BlockSpec((tm, D), row),
                pl.BlockSpec((None, 1, D), perb),
                pl.BlockSpec((1, D), const), pl.BlockSpec((1, D), const),
                pl.BlockSpec((None, 1, D), perb), pl.BlockSpec((None, 1, D), perb)]
    args = [ya, yb, yc, yd, gnc.reshape(1, -1), w_out, x2d, g1, ln_w.reshape(1, -1),
            ln_b.reshape(1, -1), sc2, sh2]
    out_shape = [jax.ShapeDtypeStruct((N, D), F32),
                 jax.ShapeDtypeStruct((N, D), BF16 if router is None else F32)]
    out_specs = [pl.BlockSpec((tm, D), row), pl.BlockSpec((tm, D), row)]
    if router is not None:
        rw, rb = router
        in_specs += [pl.BlockSpec((D, 2 * LANES), const), pl.BlockSpec((1, LANES), const)]
        args += [rw, rb]
        out_shape += [jax.ShapeDtypeStruct((N, LANES), jnp.int32), jax.ShapeDtypeStruct((N, LANES), F32)]
        out_specs += [pl.BlockSpec((tm, LANES), row), pl.BlockSpec((tm, LANES), row)]
    return pl.pallas_call(
        functools.partial(_outproj_kernel, router is not None),
        out_shape=tuple(out_shape),
        grid=(N // tm,),
        in_specs=in_specs,
        out_specs=tuple(out_specs),
        compiler_params=_params(("parallel",)),
        name="out_proj_ln",
    )(*args)


FFN_CW = 512


def _ffn_kernel(u_ref, wg_hbm, wu_hbm, wd_hbm, x_ref, g2_ref, lnw_ref, lnb_ref, out_ref,
                acc_ref, wgbuf, wubuf, wdbuf, sem):
    m = pl.program_id(0)
    n_chunks = wg_hbm.shape[1] // FFN_CW
    total = pl.num_programs(0) * n_chunks
    n_slots = wgbuf.shape[0]
    ahead = n_slots - 1

    def chunk_copies(g):
        slot = g % n_slots
        cols = pl.ds(pl.multiple_of((g % n_chunks) * FFN_CW, FFN_CW), FFN_CW)
        return (pltpu.make_async_copy(wg_hbm.at[:, cols], wgbuf.at[slot], sem.at[slot]),
                pltpu.make_async_copy(wu_hbm.at[:, cols], wubuf.at[slot], sem.at[slot]),
                pltpu.make_async_copy(wd_hbm.at[cols, :], wdbuf.at[slot], sem.at[slot]))

    @pl.when(m == 0)
    def _():
        for g in range(ahead):
            for cp in chunk_copies(g):
                cp.start()

    u = u_ref[...]
    for ci in range(n_chunks):
        g = m * n_chunks + ci
        slot = g % n_slots
        for cp in chunk_copies(g):
            cp.wait()

        @pl.when(g + ahead < total)
        def _():
            for cp in chunk_copies(g + ahead):
                cp.start()

        hg = jnp.dot(u, wgbuf[slot], preferred_element_type=F32)
        hu = jnp.dot(u, wubuf[slot], preferred_element_type=F32)
        h = (_silu(hg) * hu).astype(BF16)
        part = jnp.dot(h, wdbuf[slot], preferred_element_type=F32)
        if ci == 0:
            acc_ref[...] = part
        else:
            acc_ref[...] += part

    r = ALPHA * x_ref[...] + (1.0 + g2_ref[...]) * acc_ref[...]
    out_ref[...] = _layer_norm_rows(r, lnw_ref[...], lnb_ref[...])


def _ffn(u2, wg, wu, wd, x1, g2, ln_w, ln_b, seq):
    N, D = u2.shape
    tm = 512
    n_slots = 3
    tpb = seq // tm
    row = lambda m: (m, 0)
    const = lambda m: (0, 0)
    hbm = pl.BlockSpec(memory_space=pl.ANY)
    return pl.pallas_call(
        _ffn_kernel,
        out_shape=jax.ShapeDtypeStruct((N, D), F32),
        grid=(N // tm,),
        in_specs=[pl.BlockSpec((tm, D), row), hbm, hbm, hbm,
                  pl.BlockSpec((tm, D), row),
                  pl.BlockSpec((None, 1, D), lambda m: (m // tpb, 0, 0)),
                  pl.BlockSpec((1, D), const), pl.BlockSpec((1, D), const)],
        out_specs=pl.BlockSpec((tm, D), row),
        scratch_shapes=[pltpu.VMEM((tm, D), F32),
                        pltpu.VMEM((n_slots, D, FFN_CW), BF16), pltpu.VMEM((n_slots, D, FFN_CW), BF16),
                        pltpu.VMEM((n_slots, FFN_CW, D), BF16),
                        pltpu.SemaphoreType.DMA((n_slots,))],
        compiler_params=_params(("arbitrary",)),
        name="ffn_ln",
    )(u2, wg, wu, wd, x1, g2, ln_w.reshape(1, -1), ln_b.reshape(1, -1))


MOE_TM = 768
MOE_CW = 256


def _row_copy(src, src_row, dst, dst_row, sem):
    return pltpu.make_async_copy(src.at[pl.ds(src_row, 1), :], dst.at[pl.ds(dst_row, 1), :], sem)


def _start_rows(n_rows, make_copy):
    def body(i, carry):
        base = pl.multiple_of(i * 8, 8)
        for j in range(8):
            make_copy(base, j).start(priority=j % 2)
        return carry
    lax.fori_loop(0, n_rows // 8, body, 0)


def _wait_rows(n_rows, one_copy):
    def body(i, carry):
        one_copy.wait()
        return carry
    lax.fori_loop(0, n_rows, body, 0, unroll=8)


def _dispatch_kernel(pos_ref, gap_ref, u_ref, out_ref, zbuf, sem, zsem):
    tm = u_ref.shape[0]
    first = pl.program_id(0) * tm

    def start(r, carry):
        for k in range(2):
            _row_copy(u_ref, r, out_ref, pos_ref[2 * (first + r) + k], sem).start(priority=k)
        return carry

    lax.fori_loop(0, tm, start, 0, unroll=8)

    @pl.when(pl.program_id(0) == pl.num_programs(0) - 1)
    def _():
        zbuf[...] = jnp.zeros_like(zbuf)
        n_gaps = gap_ref.shape[0] // 2

        def block_copy(i):
            return pltpu.make_async_copy(zbuf, out_ref.at[pl.ds(pl.multiple_of(i * 8, 8), 8), :], zsem.at[1])

        for g in range(n_gaps):
            lo, hi = gap_ref[2 * g], gap_ref[2 * g + 1]
            mid = jnp.minimum((lo + 7) // 8 * 8, hi)
            lax.fori_loop(lo, mid, lambda r, c: (_row_copy(zbuf, 0, out_ref, r, zsem.at[0]).start(), c)[1], 0)
            lax.fori_loop(mid // 8, hi // 8, lambda i, c: (block_copy(i).start(), c)[1], 0)
        for g in range(n_gaps):
            lo, hi = gap_ref[2 * g], gap_ref[2 * g + 1]
            mid = jnp.minimum((lo + 7) // 8 * 8, hi)
            lax.fori_loop(lo, mid, lambda r, c: (_row_copy(zbuf, 0, out_ref, 0, zsem.at[0]).wait(), c)[1], 0)
            lax.fori_loop(mid // 8, hi // 8, lambda i, c: (block_copy(0).wait(), c)[1], 0)

    _wait_rows(2 * tm, _row_copy(u_ref, 0, out_ref, 0, sem))


def _dispatch(u2, pos, gaps, n_rows):
    N, D = u2.shape
    tm = 256
    return pl.pallas_call(
        _dispatch_kernel,
        out_shape=jax.ShapeDtypeStruct((n_rows, D), u2.dtype),
        grid_spec=pltpu.PrefetchScalarGridSpec(
            num_scalar_prefetch=2,
            grid=(N // tm,),
            in_specs=[pl.BlockSpec((tm, D), lambda m, pos, gaps: (m, 0))],
            out_specs=pl.BlockSpec(memory_space=pl.ANY),
            scratch_shapes=[pltpu.VMEM((8, D), u2.dtype), pltpu.SemaphoreType.DMA(()),
                            pltpu.SemaphoreType.DMA((2,))]),
        compiler_params=_params(("arbitrary",)),
        name="moe_dispatch",
    )(pos, gaps, u2)


def _moe_ffn_kernel(te_ref, nv_ref, u_ref, wg_hbm, wu_hbm, wd_hbm, out_ref, wgbuf, wubuf, wdbuf, sem):
    t = pl.program_id(0)
    nv = nv_ref[0]
    F = wg_hbm.shape[2]
    n_chunks = F // MOE_CW
    n_slots = wgbuf.shape[0]
    ahead = n_slots - 1

    def chunk_copies(g):
        tile, ci = g // n_chunks, g % n_chunks
        e = te_ref[tile]
        slot = g % n_slots
        cols = pl.ds(pl.multiple_of(ci * MOE_CW, MOE_CW), MOE_CW)
        return (pltpu.make_async_copy(wg_hbm.at[e, :, cols], wgbuf.at[slot], sem.at[slot]),
                pltpu.make_async_copy(wu_hbm.at[e, :, cols], wubuf.at[slot], sem.at[slot]),
                pltpu.make_async_copy(wd_hbm.at[e, cols, :], wdbuf.at[slot], sem.at[slot]))

    @pl.when(t == 0)
    def _():
        for g in range(ahead):
            for cp in chunk_copies(g):
                cp.start()

    @pl.when(t < nv)
    def _():
        u = u_ref[...].astype(BF16)
        for ci in range(n_chunks):
            g = t * n_chunks + ci
            slot = g % n_slots
            for cp in chunk_copies(g):
                cp.wait()

            @pl.when(g + ahead < nv * n_chunks)
            def _():
                for cp in chunk_copies(g + ahead):
                    cp.start()

            hg = jnp.dot(u, wgbuf[slot].astype(BF16), preferred_element_type=F32)
            hu = jnp.dot(u, wubuf[slot].astype(BF16), preferred_element_type=F32)
            h = (_silu(hg) * hu).astype(BF16)
            part = jnp.dot(h, wdbuf[slot].astype(BF16), preferred_element_type=F32)
            if ci == 0:
                out_ref[...] = part
            else:
                out_ref[...] += part

    @pl.when(t >= nv)
    def _():
        out_ref[...] = jnp.zeros_like(out_ref)


def _moe_ffn(u_sorted, tile_expert, n_valid, wg, wu, wd):
    R, D = u_sorted.shape
    tm = MOE_TM
    n_slots = 3
    tile = lambda t, nv: jnp.minimum(t, nv[0] - 1)
    return pl.pallas_call(
        _moe_ffn_kernel,
        out_shape=jax.ShapeDtypeStruct((R, D), F32),
        grid_spec=pltpu.PrefetchScalarGridSpec(
            num_scalar_prefetch=2,
            grid=(R // tm,),
            in_specs=[pl.BlockSpec((tm, D), lambda t, te, nv: (tile(t, nv), 0)),
                      pl.BlockSpec(memory_space=pl.ANY),
                      pl.BlockSpec(memory_space=pl.ANY),
                      pl.BlockSpec(memory_space=pl.ANY)],
            out_specs=pl.BlockSpec((tm, D), lambda t, te, nv: (t, 0)),
            scratch_shapes=[pltpu.VMEM((n_slots, D, MOE_CW), F32), pltpu.VMEM((n_slots, D, MOE_CW), F32),
                            pltpu.VMEM((n_slots, MOE_CW, D), F32),
                            pltpu.SemaphoreType.DMA((n_slots,))]),
        compiler_params=_params(("arbitrary",), vmem=60 * 1024 * 1024),
        name="moe_ffn",
    )(tile_expert, n_valid, u_sorted, wg, wu, wd)


def _combine_kernel(pos_ref, y_ref, topw_ref, x_ref, g2_ref, lnw_ref, lnb_ref, out_ref, ybuf, sem):
    tm = x_ref.shape[0]
    m = pl.program_id(0)
    slot = m % 2

    def gather(step, buf):
        for k in range(2):
            _start_rows(tm, lambda base, j: _row_copy(y_ref, pos_ref[2 * (step * tm + base + j) + k],
                                                      ybuf.at[buf, k], base + j, sem.at[buf]))

    @pl.when(m == 0)
    def _():
        gather(0, 0)

    @pl.when(m + 1 < pl.num_programs(0))
    def _():
        gather(m + 1, 1 - slot)

    _wait_rows(2 * tm, _row_copy(y_ref, 0, ybuf.at[slot, 0], 0, sem.at[slot]))
    tw = topw_ref[...]
    y = tw[:, 0:1] * ybuf[slot, 0] + tw[:, 1:2] * ybuf[slot, 1]
    r = ALPHA * x_ref[...] + (1.0 + g2_ref[...]) * y
    out_ref[...] = _layer_norm_rows(r, lnw_ref[...], lnb_ref[...])


def _combine(y_sorted, pos, topw, x1, g2, ln_w, ln_b, seq):
    N, D = x1.shape
    tm = 256
    tpb = seq // tm
    row = lambda m, pos: (m, 0)
    const = lambda m, pos: (0, 0)
    return pl.pallas_call(
        _combine_kernel,
        out_shape=jax.ShapeDtypeStruct((N, D), F32),
        grid_spec=pltpu.PrefetchScalarGridSpec(
            num_scalar_prefetch=1,
            grid=(N // tm,),
            in_specs=[pl.BlockSpec(memory_space=pl.ANY),
                      pl.BlockSpec((tm, LANES), row),
                      pl.BlockSpec((tm, D), row),
                      pl.BlockSpec((None, 1, D), lambda m, pos: (m // tpb, 0, 0)),
                      pl.BlockSpec((1, D), const), pl.BlockSpec((1, D), const)],
            out_specs=pl.BlockSpec((tm, D), row),
            scratch_shapes=[pltpu.VMEM((2, 2, tm, D), F32), pltpu.SemaphoreType.DMA((2,))]),
        compiler_params=_params(("arbitrary",)),
        name="moe_combine_ln",
    )(pos, y_sorted, topw, x1, g2, ln_w.reshape(1, -1), ln_b.reshape(1, -1))


def _routing_plan(ids, n_tiles):
    e_flat = ids[:, :2].reshape(-1)
    onehot = (e_flat[:, None] == jnp.arange(N_EXPERTS, dtype=jnp.int32)[None, :]).astype(jnp.int32)
    csum = jnp.cumsum(onehot, axis=0)
    rank = jnp.sum((csum - onehot) * onehot, axis=1)
    counts = csum[-1]
    padded = ((counts + MOE_TM - 1) // MOE_TM) * MOE_TM
    ends = jnp.cumsum(padded)
    offs = ends - padded
    pos = (jnp.sum(onehot * offs[None, :], axis=1) + rank).astype(jnp.int32)
    tile_start = jnp.arange(n_tiles, dtype=jnp.int32) * MOE_TM
    n_valid = (ends[-1] // MOE_TM).astype(jnp.int32)
    tile_start = jnp.minimum(tile_start, (n_valid - 1) * MOE_TM)
    tile_expert = jnp.sum((tile_start[:, None] >= ends[None, :]).astype(jnp.int32), axis=1).astype(jnp.int32)
    total = jnp.full((1,), n_tiles * MOE_TM, jnp.int32)
    gaps = jnp.stack([jnp.concatenate([offs + counts, ends[-1:]]),
                      jnp.concatenate([ends, total])], axis=1).reshape(-1).astype(jnp.int32)
    return pos, tile_expert, n_valid.reshape(1), gaps


def _moe(u2, ids, topw, wg, wu, wd, x1, g2, ln_w, ln_b, seq):
    N = u2.shape[0]
    n_tiles = -(-2 * N // MOE_TM) + N_EXPERTS
    pos, tile_expert, n_valid, gaps = _routing_plan(ids, n_tiles)
    u_sorted = _dispatch(u2, pos, gaps, n_tiles * MOE_TM)
    y_sorted = _moe_ffn(u_sorted, tile_expert, n_valid, wg, wu, wd)
    return _combine(y_sorted, pos, topw, x1, g2, ln_w, ln_b, seq)


def _split_w_in_kernel(wt_ref, gt_ref, main_ref, gate_ref):
    main_ref[...] = wt_ref[0].T.astype(BF16)

    @pl.when(pl.program_id(1) == 0)
    def _():
        g = gt_ref[...]
        g = jnp.concatenate([g, jnp.zeros((LANES - g.shape[0], g.shape[1]), F32)], axis=0)
        gate_ref[...] = g.T.astype(BF16)


def _split_w_in(w_in):
    L, D, C = w_in.shape
    w_t = jnp.swapaxes(w_in, 1, 2)
    g0 = 4 * GROUP_WIDTH
    ng = 2 * N_HEADS
    tc = 512
    src_col = lambda j: (j * (tc // ng) + jnp.where(j * tc >= g0, 1, 0)) * ng
    return pl.pallas_call(
        _split_w_in_kernel,
        out_shape=(jax.ShapeDtypeStruct((L, D, C - ng), BF16),
                   jax.ShapeDtypeStruct((L, D, LANES), BF16)),
        grid=(L, (C - ng) // tc),
        in_specs=[pl.BlockSpec((pl.Element(1), pl.Element(tc), pl.Element(D)),
                               lambda l, j: (l, src_col(j), 0)),
                  pl.BlockSpec((None, ng, D), lambda l, j: (l, g0 // ng, 0))],
        out_specs=(pl.BlockSpec((None, D, tc), lambda l, j: (l, 0, j)),
                   pl.BlockSpec((None, D, LANES), lambda l, j: (l, 0, 0))),
        compiler_params=_params(("parallel", "arbitrary")),
        name="split_w_in",
    )(w_t, w_t)


def kernel(x, c, w_in, mlstm_conv_w, mlstm_conv_b, mlstm_gate_b, mlstm_norm_w, pool_w, pool_scale, conv_dw_w, conv_dw_b, conv_ln_w, conv_ln_b, conv_pw_w, conv_pw_b, group_norm_w, w_out, ada_w, ada_b, ln1_w, ln1_b, ln2_w, ln2_b, ffn_w_gate, ffn_w_up, ffn_w_down, moe_router_w, moe_router_b, moe_w_gate, moe_w_up, moe_w_down):
    B, S, D = x.shape
    GW = GROUP_WIDTH
    ada = _ada_all(c, ada_w, ada_b)
    x2d = x.reshape(B * S, D)
    w_main, w_gate = _split_w_in(w_in)
    for l in range(DEPTH):
        mod = [ada[l, :, i * D:(i + 1) * D].reshape(B, 1, D) for i in range(6)]
        sh1, sc1, g1, sh2, sc2, g2 = mod
        proj, gates = _in_proj(x2d, sc1, sh1, w_main, w_gate, l, S)
        proj = proj.reshape(B, S, -1)
        gates = gates.reshape(B, S, LANES)
        gn_b, gn_c, gn_d = (group_norm_w[l, i * GW:(i + 1) * GW] for i in range(3))
        ya = _mlstm_mixer(proj, gates, mlstm_conv_w[l], mlstm_conv_b[l], mlstm_gate_b[l],
                          mlstm_norm_w[l], B, S)
        yb = _pool_mixer(proj, pool_w[l], pool_scale[l], gn_b, B, S)
        yc = _sb_mixer(proj, B, S)
        yd = _conv_mixer(proj, conv_dw_w[l], conv_dw_b[l], conv_ln_w[l], conv_ln_b[l],
                         conv_pw_w[l], conv_pw_b[l], gn_d, B, S)
        flat = lambda t: t.reshape(B * S, GW)
        j = l // 2
        router = None
        if l % 2 == 1:
            rw = jnp.zeros((D, LANES), F32).at[:, :N_EXPERTS].set(moe_router_w[j])
            rw_hi = rw.astype(BF16)
            rw_lo = (rw - rw_hi.astype(F32)).astype(BF16)
            rb = jnp.zeros((1, LANES), F32).at[0, :N_EXPERTS].set(moe_router_b[j])
            router = (jnp.concatenate([rw_hi, rw_lo], axis=1), rb)
        outs = _out_proj(flat(ya), flat(yb), flat(yc), flat(yd), gn_c, w_out[l].astype(BF16), x2d,
                         g1, ln1_w[l], ln1_b[l], sc2, sh2, S, router)
        if l % 2 == 0:
            x1, u2 = outs
            x2d = _ffn(u2, ffn_w_gate[j].astype(BF16), ffn_w_up[j].astype(BF16),
                       ffn_w_down[j].astype(BF16), x1, g2, ln2_w[l], ln2_b[l], S)
        else:
            x1, u2, ids, topw = outs
            x2d = _moe(u2, ids, topw, moe_w_gate[j], moe_w_up[j], moe_w_down[j], x1, g2,
                       ln2_w[l], ln2_b[l], S)
    return x2d.reshape(B, S, D)
```

```python
import functools

import jax
import jax.numpy as jnp
from jax import lax
from jax.experimental import pallas as pl
from jax.experimental.pallas import tpu as pltpu

F32 = jnp.float32
BF16 = jnp.bfloat16

DEPTH = 2
CHUNK = 256
GROUP_WIDTH = 512
N_HEADS = 4
HEAD_DIM = 128
MLSTM_CONV = 4
POOL_WINDOWS = (2, 4, 8, 16)
CONV_WIDTH = 31
SB_KEY_BLOCK = 256
N_EXPERTS = 8
ALPHA = (2.0 * DEPTH) ** 0.25
EPS = 1e-5
LOG2E = 1.4426950408889634
LANES = 128
VMEM_LIMIT = 56 * 1024 * 1024


def _params(sem, vmem=VMEM_LIMIT):
    return pltpu.CompilerParams(dimension_semantics=sem, vmem_limit_bytes=vmem)


def _silu(x):
    return x * (1.0 / (1.0 + jnp.exp(-x)))


def _sigmoid(x):
    return 1.0 / (1.0 + jnp.exp(-x))


def _neg_softplus(x):
    return -(jnp.maximum(x, 0.0) + jnp.log(1.0 + jnp.exp(-jnp.abs(x))))


def _layer_norm_rows(r, w, b):
    mu = jnp.mean(r, axis=-1, keepdims=True)
    d = r - mu
    var = jnp.mean(d * d, axis=-1, keepdims=True)
    return d * lax.rsqrt(var + EPS) * w + b


def _rms_rows(y, w):
    return y * lax.rsqrt(jnp.mean(y * y, axis=-1, keepdims=True) + EPS) * w


def _bdot(a, b):
    return jnp.dot(a.astype(BF16), b.astype(BF16), preferred_element_type=F32)


def _ada_kernel(c_ref, w_ref, b_ref, out_ref):
    out_ref[...] = _bdot(_silu(c_ref[...]), w_ref[...]) + b_ref[...]


def _ada_all(c, ada_w, ada_b):
    L, D, D6 = ada_w.shape
    Bn = c.shape[0]
    tn = 1024
    return pl.pallas_call(
        _ada_kernel,
        out_shape=jax.ShapeDtypeStruct((L, Bn, D6), F32),
        grid=(L, D6 // tn),
        in_specs=[pl.BlockSpec((Bn, D), lambda l, n: (0, 0)),
                  pl.BlockSpec((None, D, tn), lambda l, n: (l, 0, n)),
                  pl.BlockSpec((None, 1, tn), lambda l, n: (l, 0, n))],
        out_specs=pl.BlockSpec((None, Bn, tn), lambda l, n: (l, 0, n)),
        compiler_params=_params(("parallel", "parallel")),
        name="ada_mod",
    )(c, ada_w, ada_b.reshape(L, 1, D6))


def _inproj_kernel(x_ref, sc_ref, sh_ref, w_ref, wg_ref, proj_ref, gates_ref, u_ref):
    @pl.when(pl.program_id(1) == 0)
    def _():
        u = (x_ref[...] * (1.0 + sc_ref[...]) + sh_ref[...]).astype(BF16)
        u_ref[...] = u
        gates_ref[...] = jnp.dot(u, wg_ref[...], preferred_element_type=F32)

    proj_ref[...] = jnp.dot(u_ref[...], w_ref[...], preferred_element_type=F32)


def _in_proj(x2d, sc, sh, w_main, w_gate, layer, seq):
    N, D = x2d.shape
    NC = w_main.shape[2]
    tm, tn = 1024, 1280
    tpb = seq // tm
    return pl.pallas_call(
        _inproj_kernel,
        out_shape=(jax.ShapeDtypeStruct((N, NC), F32), jax.ShapeDtypeStruct((N, LANES), F32)),
        grid=(N // tm, NC // tn),
        in_specs=[pl.BlockSpec((tm, D), lambda m, n: (m, 0)),
                  pl.BlockSpec((None, 1, D), lambda m, n: (m // tpb, 0, 0)),
                  pl.BlockSpec((None, 1, D), lambda m, n: (m // tpb, 0, 0)),
                  pl.BlockSpec((None, D, tn), lambda m, n: (layer, 0, n)),
                  pl.BlockSpec((None, D, LANES), lambda m, n: (layer, 0, 0))],
        out_specs=(pl.BlockSpec((tm, tn), lambda m, n: (m, n)),
                   pl.BlockSpec((tm, LANES), lambda m, n: (m, 0))),
        scratch_shapes=[pltpu.VMEM((tm, D), BF16)],
        compiler_params=_params(("parallel", "arbitrary")),
        name="in_proj",
    )(x2d, sc, sh, w_main, w_gate)


def _cumsum_rows(x):
    n = x.shape[0]
    row = lax.broadcasted_iota(jnp.int32, x.shape, 0)
    k = 1
    while k < n:
        x = x + jnp.where(row >= k, pltpu.roll(x, k, axis=0), 0.0)
        k *= 2
    return x


def _mlstm_kernel(qk_ref, v_ref, o_ref, g_ref, cw_ref, cb_ref, gb_ref, nw_ref, out_ref,
                  c_state, n_state, m_state, tail_ref):
    @pl.when(pl.program_id(1) == 0)
    def _():
        c_state[...] = jnp.zeros_like(c_state)
        n_state[...] = jnp.zeros_like(n_state)
        m_state[...] = jnp.zeros_like(m_state)
        tail_ref[...] = jnp.zeros_like(tail_ref)

    for bb in range(qk_ref.shape[0]):
        _mlstm_block(qk_ref.at[bb], v_ref.at[bb], o_ref.at[bb], g_ref.at[bb], cw_ref, cb_ref, gb_ref,
                     nw_ref, out_ref.at[bb], c_state.at[bb], n_state.at[bb], m_state.at[bb],
                     tail_ref.at[bb])


def _mlstm_block(qk_ref, v_ref, o_ref, g_ref, cw_ref, cb_ref, gb_ref, nw_ref, out_ref,
                 c_state, n_state, m_state, tail_ref):
    T = qk_ref.shape[0]
    GW = GROUP_WIDTH

    xx = jnp.concatenate([tail_ref[...], qk_ref[...]], axis=0)
    tail_ref[...] = qk_ref[T - 8:T, :]
    conv = cb_ref[...]
    for j in range(MLSTM_CONV):
        off = 8 - (MLSTM_CONV - 1) + j
        tap = xx[off:off + T, :] if off % 8 == 0 else pltpu.roll(xx, T + 8 - off, axis=0)[0:T, :]
        conv = conv + cw_ref[j:j + 1, :] * tap
    qk = _silu(conv)

    gates = g_ref[...] + gb_ref[...]
    logf = _neg_softplus(-gates)
    causal = (lax.broadcasted_iota(jnp.int32, (CHUNK, CHUNK), 1)
              <= lax.broadcasted_iota(jnp.int32, (CHUNK, CHUNK), 0))

    for c in range(T // CHUNK):
        rows = slice(c * CHUNK, (c + 1) * CHUNK)
        gi = gates[rows, :]
        bcum = _cumsum_rows(logf[rows, :])
        gi_t = gi.T
        bcum_t = bcum.T
        for h in range(N_HEADS):
            cols = slice(h * HEAD_DIM, (h + 1) * HEAD_DIM)
            q = qk[rows, cols] * (HEAD_DIM ** -0.5)
            k = qk[rows, GW + h * HEAD_DIM:GW + (h + 1) * HEAD_DIM]
            v = v_ref[rows, cols]
            b_col = bcum[:, N_HEADS + h:N_HEADS + h + 1]
            b_row = bcum_t[N_HEADS + h:N_HEADS + h + 1, :]
            i_col = gi[:, h:h + 1]
            i_row = gi_t[h:h + 1, :]
            g_tot = b_col[CHUNK - 1:CHUNK, :]
            m_prev = m_state[h:h + 1, 0:1]
            c_prev = c_state[h]
            n_prev = n_state[h:h + 1, :]

            log_d = jnp.where(causal, b_col - b_row + i_row, -jnp.inf)
            m_inter = b_col + m_prev
            m_t = jnp.maximum(m_inter, jnp.max(log_d, axis=-1, keepdims=True))
            s = lax.dot_general(q.astype(BF16), k.astype(BF16), (((1,), (1,)), ((), ())),
                                preferred_element_type=F32)
            w = jnp.exp(log_d - m_t) * s
            inter = jnp.exp(m_inter - m_t)
            num = inter * _bdot(q, c_prev) + _bdot(w, v)
            den = inter * jnp.sum(q * n_prev, axis=-1, keepdims=True) + jnp.sum(w, axis=-1, keepdims=True)
            hcur = num / jnp.maximum(jnp.abs(den), jnp.exp(-m_t))

            a_col = g_tot - b_col + i_col
            m_new = jnp.maximum(g_tot + m_prev, jnp.max(a_col, axis=0, keepdims=True))
            decay = jnp.exp(g_tot + m_prev - m_new)
            kw = k * jnp.exp(a_col - m_new)
            c_state[h] = decay * c_prev + _bdot(kw.T, v)
            n_state[h:h + 1, :] = decay * n_prev + jnp.sum(kw, axis=0, keepdims=True)
            m_state[h:h + 1, :] = jnp.broadcast_to(m_new, (1, LANES))

            mu = jnp.mean(hcur, axis=-1, keepdims=True)
            dlt = hcur - mu
            var = jnp.mean(dlt * dlt, axis=-1, keepdims=True)
            hn = dlt * lax.rsqrt(var + EPS) * nw_ref[:, cols]
            out_ref[rows, cols] = (_sigmoid(o_ref[rows, cols]) * hn).astype(out_ref.dtype)


def _mlstm_mixer(proj, gates, conv_w, conv_b, gate_b, norm_w, B, S):
    T = 256
    NB = 2
    GW = GROUP_WIDTH
    gb = jnp.zeros((1, LANES), F32).at[0, :2 * N_HEADS].set(gate_b)
    return pl.pallas_call(
        _mlstm_kernel,
        out_shape=jax.ShapeDtypeStruct((B, S, GW), BF16),
        grid=(B // NB, S // T),
        in_specs=[pl.BlockSpec((NB, T, 2 * GW), lambda b, t: (b, t, 0)),
                  pl.BlockSpec((NB, T, GW), lambda b, t: (b, t, 2)),
                  pl.BlockSpec((NB, T, GW), lambda b, t: (b, t, 3)),
                  pl.BlockSpec((NB, T, LANES), lambda b, t: (b, t, 0)),
                  pl.BlockSpec((MLSTM_CONV, 2 * GW), lambda b, t: (0, 0)),
                  pl.BlockSpec((1, 2 * GW), lambda b, t: (0, 0)),
                  pl.BlockSpec((1, LANES), lambda b, t: (0, 0)),
                  pl.BlockSpec((1, GW), lambda b, t: (0, 0))],
        out_specs=pl.BlockSpec((NB, T, GW), lambda b, t: (b, t, 0)),
        scratch_shapes=[pltpu.VMEM((NB, N_HEADS, HEAD_DIM, HEAD_DIM), F32),
                        pltpu.VMEM((NB, 8, LANES), F32),
                        pltpu.VMEM((NB, 8, LANES), F32),
                        pltpu.VMEM((NB, 8, 2 * GW), F32)],
        compiler_params=_params(("parallel", "arbitrary")),
        name="mlstm_mixer",
    )(proj, proj, proj, gates, conv_w, conv_b.reshape(1, -1), gb, norm_w.reshape(1, -1))


def _pool_kernel(x_ref, w_ref, scale_ref, gn_ref, out_ref, tail_ref):
    T = x_ref.shape[0]
    HALO = 16
    t_blk = pl.program_id(1)

    @pl.when(t_blk == 0)
    def _():
        tail_ref[...] = jnp.zeros_like(tail_ref)

    x = x_ref[...]
    xx = jnp.concatenate([tail_ref[...], x], axis=0)
    tail_ref[...] = x[T - HALO:T, :]
    pos = t_blk * T + lax.broadcasted_iota(jnp.int32, (T, 1), 0) + 1
    outs = []
    for g, win in enumerate(POOL_WINDOWS):
        cols = slice(g * LANES, (g + 1) * LANES)
        s = xx[:, cols]
        k = 1
        while k < win:
            s = s + pltpu.roll(s, k, axis=0)
            k *= 2
        cnt = jnp.minimum(pos, win).astype(F32)
        yg = s[HALO:, :] / cnt - x[:, cols]
        outs.append(_bdot(yg, w_ref[g]))
    y = jnp.concatenate(outs, axis=-1) * scale_ref[...]
    out_ref[...] = _rms_rows(y, gn_ref[...]).astype(out_ref.dtype)


def _pool_mixer(proj, pool_w, pool_scale, gn, B, S):
    T = 512
    GW = GROUP_WIDTH
    return pl.pallas_call(
        _pool_kernel,
        out_shape=jax.ShapeDtypeStruct((B, S, GW), BF16),
        grid=(B, S // T),
        in_specs=[pl.BlockSpec((None, T, GW), lambda b, t: (b, t, 4)),
                  pl.BlockSpec((len(POOL_WINDOWS), LANES, LANES), lambda b, t: (0, 0, 0)),
                  pl.BlockSpec((1, GW), lambda b, t: (0, 0)),
                  pl.BlockSpec((1, GW), lambda b, t: (0, 0))],
        out_specs=pl.BlockSpec((None, T, GW), lambda b, t: (b, t, 0)),
        scratch_shapes=[pltpu.VMEM((16, GW), F32)],
        compiler_params=_params(("parallel", "arbitrary")),
        name="pool_mixer",
    )(proj, pool_w.astype(BF16), pool_scale.reshape(1, -1), gn.reshape(1, -1))


def _sb_kernel(q_ref, k_ref, v_ref, out_ref, kb_ref, vb_ref):
    LQ = q_ref.shape[0]
    LK = SB_KEY_BLOCK
    ratio = LQ // LK
    qi = pl.program_id(1)

    @pl.when(qi == 0)
    def _():
        kb_ref[...] = k_ref[...].astype(BF16)
        vb_ref[...] = v_ref[...].astype(BF16)

    scale2 = HEAD_DIM ** -0.5 * LOG2E
    tri = (lax.broadcasted_iota(jnp.int32, (LK, LK), 0)
           > lax.broadcasted_iota(jnp.int32, (LK, LK), 1)).astype(BF16)
    row = lax.broadcasted_iota(jnp.int32, (LQ, LK), 0)
    col = lax.broadcasted_iota(jnp.int32, (LQ, LK), 1)
    heads = [slice(h * HEAD_DIM, (h + 1) * HEAD_DIM) for h in range(N_HEADS)]
    qs = [(q_ref[:, hs] * scale2).astype(BF16) for hs in heads]

    def sweep(kb, carry, strict):
        start = pl.multiple_of(kb * LK, LK)
        zs, lks = [], []
        for h, hs in enumerate(heads):
            k_blk = kb_ref[pl.ds(start, LK), hs]
            z = lax.dot_general(qs[h], k_blk, (((1,), (1,)), ((), ())), preferred_element_type=F32)
            lk = -(jnp.maximum(z, 0.0) + jnp.log2(1.0 + jnp.exp2(-jnp.abs(z))))
            if strict is not None:
                lk = jnp.where(strict, lk, 0.0)
            zs.append(z)
            lks.append(lk)
        inner = jnp.dot(jnp.concatenate([lk.astype(BF16) for lk in lks], axis=0), tri,
                        preferred_element_type=F32)
        new = []
        for h, hs in enumerate(heads):
            after, acc = carry[h]
            v_blk = vb_ref[pl.ds(start, LK), hs]
            a = jnp.exp2(zs[h] + lks[h] + inner[h * LQ:(h + 1) * LQ, :] + after)
            if strict is not None:
                a = jnp.where(strict, a, 0.0)
            acc = acc + jnp.dot(a.astype(BF16), v_blk, preferred_element_type=F32)
            after = after + jnp.sum(lks[h], axis=-1, keepdims=True)
            new.append((after, acc))
        return tuple(new)

    init = tuple((jnp.zeros((LQ, 1), F32), jnp.zeros((LQ, HEAD_DIM), F32)) for _ in heads)
    carry = init
    for j in reversed(range(ratio)):
        carry = sweep(ratio * qi + j, carry, (j * LK + col) < row)
    carry = lax.fori_loop(0, ratio * qi, lambda it, c: sweep(ratio * qi - 1 - it, c, None), carry)
    for h, hs in enumerate(heads):
        out_ref[:, hs] = carry[h][1]


def _sb_mixer(proj, B, S):
    LQ = 512
    GW = GROUP_WIDTH
    return pl.pallas_call(
        _sb_kernel,
        out_shape=jax.ShapeDtypeStruct((B, S, GW), F32),
        grid=(B, S // LQ),
        in_specs=[pl.BlockSpec((None, LQ, GW), lambda b, i: (b, i, 5)),
                  pl.BlockSpec((None, S, GW), lambda b, i: (b, 0, 6)),
                  pl.BlockSpec((None, S, GW), lambda b, i: (b, 0, 7))],
        out_specs=pl.BlockSpec((None, LQ, GW), lambda b, i: (b, i, 0)),
        scratch_shapes=[pltpu.VMEM((S, GW), BF16), pltpu.VMEM((S, GW), BF16)],
        compiler_params=_params(("parallel", "arbitrary")),
        name="sb_attention",
    )(proj, proj, proj)


def _conv_kernel(x_ref, dw_ref, dwb_ref, lnw_ref, lnb_ref, pw_ref, pwb_ref, gn_ref, out_ref,
                 hbuf, ybuf):
    T = x_ref.shape[0]
    GW = GROUP_WIDTH
    HALO = 32
    SUB = 64

    @pl.when(pl.program_id(1) == 0)
    def _():
        hbuf[0:HALO, :] = jnp.zeros((HALO, GW), F32)

    @pl.when(pl.program_id(1) > 0)
    def _():
        hbuf[0:HALO, :] = hbuf[T:T + HALO, :]

    hbuf[HALO:HALO + T, :] = x_ref[:, 0:GW] * _sigmoid(x_ref[:, GW:2 * GW])

    def sub(i, carry):
        r0 = pl.multiple_of(i * SUB, SUB)
        acc = jnp.broadcast_to(dwb_ref[...], (SUB, GW))
        win = hbuf[pl.ds(r0, SUB + HALO), :]
        for res in range(8):
            shifted = win if res == 0 else pltpu.roll(win, SUB + HALO - res, axis=0)
            for j in range(CONV_WIDTH):
                off = HALO - CONV_WIDTH + 1 + j
                if off % 8 == res:
                    base = off - res
                    acc = acc + dw_ref[j:j + 1, :] * shifted[base:base + SUB, :]
        ybuf[pl.ds(r0, SUB), :] = acc
        return carry

    lax.fori_loop(0, T // SUB, sub, 0)
    hn = _silu(_layer_norm_rows(ybuf[...], lnw_ref[...], lnb_ref[...]))
    y = _bdot(hn, pw_ref[...]) + pwb_ref[...]
    out_ref[...] = _rms_rows(y, gn_ref[...]).astype(out_ref.dtype)


def _conv_mixer(proj, dw_w, dw_b, ln_w, ln_b, pw_w, pw_b, gn, B, S):
    T = 256
    GW = GROUP_WIDTH
    dw_pad = jnp.zeros((32, GW), F32).at[:CONV_WIDTH].set(dw_w)
    r = lambda a: a.reshape(1, -1)
    return pl.pallas_call(
        _conv_kernel,
        out_shape=jax.ShapeDtypeStruct((B, S, GW), BF16),
        grid=(B, S // T),
        in_specs=[pl.BlockSpec((None, T, 2 * GW), lambda b, t: (b, t, 4)),
                  pl.BlockSpec((32, GW), lambda b, t: (0, 0)),
                  pl.BlockSpec((1, GW), lambda b, t: (0, 0)),
                  pl.BlockSpec((1, GW), lambda b, t: (0, 0)),
                  pl.BlockSpec((1, GW), lambda b, t: (0, 0)),
                  pl.BlockSpec((GW, GW), lambda b, t: (0, 0)),
                  pl.BlockSpec((1, GW), lambda b, t: (0, 0)),
                  pl.BlockSpec((1, GW), lambda b, t: (0, 0))],
        out_specs=pl.BlockSpec((None, T, GW), lambda b, t: (b, t, 0)),
        scratch_shapes=[pltpu.VMEM((T + 32, GW), F32), pltpu.VMEM((T, GW), F32)],
        compiler_params=_params(("parallel", "arbitrary")),
        name="conv_mixer",
    )(proj, dw_pad, r(dw_b), r(ln_w), r(ln_b), pw_w.astype(BF16), r(pw_b), r(gn))


def _outproj_kernel(with_router, ya_ref, yb_ref, yc_ref, yd_ref, gnc_ref, w_ref, x_ref, g1_ref,
                    lnw_ref, lnb_ref, sc_ref, sh_ref, *rest):
    GW = GROUP_WIDTH
    if with_router:
        rw_ref, rb_ref, x1_ref, u2_ref, ids_ref, topw_ref = rest
    else:
        x1_ref, u2_ref = rest
    yc = _rms_rows(yc_ref[...], gnc_ref[...]).astype(BF16)
    acc = jnp.dot(ya_ref[...], w_ref[0:GW, :], preferred_element_type=F32)
    acc = acc + jnp.dot(yb_ref[...], w_ref[GW:2 * GW, :], preferred_element_type=F32)
    acc = acc + jnp.dot(yc, w_ref[2 * GW:3 * GW, :], preferred_element_type=F32)
    acc = acc + jnp.dot(yd_ref[...], w_ref[3 * GW:4 * GW, :], preferred_element_type=F32)
    r = ALPHA * x_ref[...] + (1.0 + g1_ref[...]) * acc
    x1 = _layer_norm_rows(r, lnw_ref[...], lnb_ref[...])
    x1_ref[...] = x1
    u2 = x1 * (1.0 + sc_ref[...]) + sh_ref[...]
    u2_ref[...] = u2.astype(u2_ref.dtype)
    if with_router:
        u_hi = u2.astype(BF16)
        u_lo = (u2 - u_hi.astype(F32)).astype(BF16)
        p = jnp.dot(u_hi, rw_ref[...], preferred_element_type=F32)
        logits = (p[:, :LANES] + p[:, LANES:]
                  + jnp.dot(u_lo, rw_ref[:, :LANES], preferred_element_type=F32) + rb_ref[...])
        lane = lax.broadcasted_iota(jnp.int32, logits.shape, 1)
        lg = jnp.where(lane < N_EXPERTS, logits, -jnp.inf)
        m1 = jnp.max(lg, axis=-1, keepdims=True)
        i1 = jnp.min(jnp.where(lg == m1, lane, LANES), axis=-1, keepdims=True)
        lg2 = jnp.where(lane == i1, -jnp.inf, lg)
        m2 = jnp.max(lg2, axis=-1, keepdims=True)
        i2 = jnp.min(jnp.where(lg2 == m2, lane, LANES), axis=-1, keepdims=True)
        e2 = jnp.exp(m2 - m1)
        w1 = 1.0 / (1.0 + e2)
        w2 = e2 / (1.0 + e2)
        ids_ref[...] = jnp.where(lane == 0, i1, jnp.where(lane == 1, i2, 0))
        topw_ref[...] = jnp.where(lane == 0, w1, jnp.where(lane == 1, w2, 0.0))


def _out_proj(ya, yb, yc, yd, gnc, w_out, x2d, g1, ln_w, ln_b, sc2, sh2, seq, router=None):
    N, D = x2d.shape
    GW = GROUP_WIDTH
    tm = 512
    tpb = seq // tm
    row = lambda m: (m, 0)
    const = lambda m: (0, 0)
    perb = lambda m: (m // tpb, 0, 0)
    in_specs = [pl.BlockSpec((tm, GW), row), pl.BlockSpec((tm, GW), row),
                pl.BlockSpec((tm, GW), row), pl.BlockSpec((tm, GW), row),
                pl.BlockSpec((1, GW), const),
                pl.BlockSpec((D, D), const),
                pl.BlockSpec((tm, D), row),
                pl.BlockSpec((None, 1, D), perb),
                pl.BlockSpec((1, D), const), pl.BlockSpec((1, D), const),
                pl.BlockSpec((None, 1, D), perb), pl.BlockSpec((None, 1, D), perb)]
    args = [ya, yb, yc, yd, gnc.reshape(1, -1), w_out, x2d, g1, ln_w.reshape(1, -1),
            ln_b.reshape(1, -1), sc2, sh2]
    out_shape = [jax.ShapeDtypeStruct((N, D), F32),
                 jax.ShapeDtypeStruct((N, D), BF16 if router is None else F32)]
    out_specs = [pl.BlockSpec((tm, D), row), pl.BlockSpec((tm, D), row)]
    if router is not None:
        rw, rb = router
        in_specs += [pl.BlockSpec((D, 2 * LANES), const), pl.BlockSpec((1, LANES), const)]
        args += [rw, rb]
        out_shape += [jax.ShapeDtypeStruct((N, LANES), jnp.int32), jax.ShapeDtypeStruct((N, LANES), F32)]
        out_specs += [pl.BlockSpec((tm, LANES), row), pl.BlockSpec((tm, LANES), row)]
    return pl.pallas_call(
        functools.partial(_outproj_kernel, router is not None),
        out_shape=tuple(out_shape),
        grid=(N // tm,),
        in_specs=in_specs,
        out_specs=tuple(out_specs),
        compiler_params=_params(("parallel",)),
        name="out_proj_ln",
    )(*args)


FFN_CW = 512


def _ffn_kernel(u_ref, wg_hbm, wu_hbm, wd_hbm, x_ref, g2_ref, lnw_ref, lnb_ref, out_ref,
                acc_ref, wgbuf, wubuf, wdbuf, sem):
    m = pl.program_id(0)
    n_chunks = wg_hbm.shape[1] // FFN_CW
    total = pl.num_programs(0) * n_chunks
    n_slots = wgbuf.shape[0]
    ahead = n_slots - 1

    def chunk_copies(g):
        slot = g % n_slots
        cols = pl.ds(pl.multiple_of((g % n_chunks) * FFN_CW, FFN_CW), FFN_CW)
        return (pltpu.make_async_copy(wg_hbm.at[:, cols], wgbuf.at[slot], sem.at[slot]),
                pltpu.make_async_copy(wu_hbm.at[:, cols], wubuf.at[slot], sem.at[slot]),
                pltpu.make_async_copy(wd_hbm.at[cols, :], wdbuf.at[slot], sem.at[slot]))

    @pl.when(m == 0)
    def _():
        for g in range(ahead):
            for cp in chunk_copies(g):
                cp.start()

    u = u_ref[...]
    for ci in range(n_chunks):
        g = m * n_chunks + ci
        slot = g % n_slots
        for cp in chunk_copies(g):
            cp.wait()

        @pl.when(g + ahead < total)
        def _():
            for cp in chunk_copies(g + ahead):
                cp.start()

        hg = jnp.dot(u, wgbuf[slot], preferred_element_type=F32)
        hu = jnp.dot(u, wubuf[slot], preferred_element_type=F32)
        h = (_silu(hg) * hu).astype(BF16)
        part = jnp.dot(h, wdbuf[slot], preferred_element_type=F32)
        if ci == 0:
            acc_ref[...] = part
        else:
            acc_ref[...] += part

    r = ALPHA * x_ref[...] + (1.0 + g2_ref[...]) * acc_ref[...]
    out_ref[...] = _layer_norm_rows(r, lnw_ref[...], lnb_ref[...])


def _ffn(u2, wg, wu, wd, x1, g2, ln_w, ln_b, seq):
    N, D = u2.shape
    tm = 512
    n_slots = 3
    tpb = seq // tm
    row = lambda m: (m, 0)
    const = lambda m: (0, 0)
    hbm = pl.BlockSpec(memory_space=pl.ANY)
    return pl.pallas_call(
        _ffn_kernel,
        out_shape=jax.ShapeDtypeStruct((N, D), F32),
        grid=(N // tm,),
        in_specs=[pl.BlockSpec((tm, D), row), hbm, hbm, hbm,
                  pl.BlockSpec((tm, D), row),
                  pl.BlockSpec((None, 1, D), lambda m: (m // tpb, 0, 0)),
                  pl.BlockSpec((1, D), const), pl.BlockSpec((1, D), const)],
        out_specs=pl.BlockSpec((tm, D), row),
        scratch_shapes=[pltpu.VMEM((tm, D), F32),
                        pltpu.VMEM((n_slots, D, FFN_CW), BF16), pltpu.VMEM((n_slots, D, FFN_CW), BF16),
                        pltpu.VMEM((n_slots, FFN_CW, D), BF16),
                        pltpu.SemaphoreType.DMA((n_slots,))],
        compiler_params=_params(("arbitrary",)),
        name="ffn_ln",
    )(u2, wg, wu, wd, x1, g2, ln_w.reshape(1, -1), ln_b.reshape(1, -1))


MOE_TM = 640
MOE_CW = 256


def _row_copy(src, src_row, dst, dst_row, sem):
    return pltpu.make_async_copy(src.at[pl.ds(src_row, 1), :], dst.at[pl.ds(dst_row, 1), :], sem)


def _start_rows(n_rows, make_copy):
    def body(i, carry):
        base = pl.multiple_of(i * 8, 8)
        for j in range(8):
            make_copy(base, j).start(priority=j % 2)
        return carry
    lax.fori_loop(0, n_rows // 8, body, 0)


def _wait_rows(n_rows, one_copy):
    def body(i, carry):
        one_copy.wait()
        return carry
    lax.fori_loop(0, n_rows, body, 0, unroll=8)


def _dispatch_kernel(pos_ref, gap_ref, u_ref, out_ref, zbuf, sem, zsem):
    tm = u_ref.shape[0]
    first = pl.program_id(0) * tm

    def start(r, carry):
        for k in range(2):
            _row_copy(u_ref, r, out_ref, pos_ref[2 * (first + r) + k], sem).start(priority=k)
        return carry

    lax.fori_loop(0, tm, start, 0, unroll=8)

    @pl.when(pl.program_id(0) == pl.num_programs(0) - 1)
    def _():
        zbuf[...] = jnp.zeros_like(zbuf)
        n_gaps = gap_ref.shape[0] // 2

        def block_copy(i):
            return pltpu.make_async_copy(zbuf, out_ref.at[pl.ds(pl.multiple_of(i * 8, 8), 8), :], zsem.at[1])

        for g in range(n_gaps):
            lo, hi = gap_ref[2 * g], gap_ref[2 * g + 1]
            mid = jnp.minimum((lo + 7) // 8 * 8, hi)
            lax.fori_loop(lo, mid, lambda r, c: (_row_copy(zbuf, 0, out_ref, r, zsem.at[0]).start(), c)[1], 0)
            lax.fori_loop(mid // 8, hi // 8, lambda i, c: (block_copy(i).start(), c)[1], 0)
        for g in range(n_gaps):
            lo, hi = gap_ref[2 * g], gap_ref[2 * g + 1]
            mid = jnp.minimum((lo + 7) // 8 * 8, hi)
            lax.fori_loop(lo, mid, lambda r, c: (_row_copy(zbuf, 0, out_ref, 0, zsem.at[0]).wait(), c)[1], 0)
            lax.fori_loop(mid // 8, hi // 8, lambda i, c: (block_copy(0).wait(), c)[1], 0)

    _wait_rows(2 * tm, _row_copy(u_ref, 0, out_ref, 0, sem))


def _dispatch(u2, pos, gaps, n_rows):
    N, D = u2.shape
    tm = 512
    return pl.pallas_call(
        _dispatch_kernel,
        out_shape=jax.ShapeDtypeStruct((n_rows, D), u2.dtype),
        grid_spec=pltpu.PrefetchScalarGridSpec(
            num_scalar_prefetch=2,
            grid=(N // tm,),
            in_specs=[pl.BlockSpec((tm, D), lambda m, pos, gaps: (m, 0))],
            out_specs=pl.BlockSpec(memory_space=pl.ANY),
            scratch_shapes=[pltpu.VMEM((8, D), u2.dtype), pltpu.SemaphoreType.DMA(()),
                            pltpu.SemaphoreType.DMA((2,))]),
        compiler_params=_params(("arbitrary",)),
        name="moe_dispatch",
    )(pos, gaps, u2)


def _moe_ffn_kernel(te_ref, nv_ref, u_ref, wg_hbm, wu_hbm, wd_hbm, out_ref, wgbuf, wubuf, wdbuf, sem):
    t = pl.program_id(0)
    nv = nv_ref[0]
    F = wg_hbm.shape[2]
    n_chunks = F // MOE_CW
    n_slots = wgbuf.shape[0]
    ahead = n_slots - 1

    def chunk_copies(g):
        tile, ci = g // n_chunks, g % n_chunks
        e = te_ref[tile]
        slot = g % n_slots
        cols = pl.ds(pl.multiple_of(ci * MOE_CW, MOE_CW), MOE_CW)
        return (pltpu.make_async_copy(wg_hbm.at[e, :, cols], wgbuf.at[slot], sem.at[slot]),
                pltpu.make_async_copy(wu_hbm.at[e, :, cols], wubuf.at[slot], sem.at[slot]),
                pltpu.make_async_copy(wd_hbm.at[e, cols, :], wdbuf.at[slot], sem.at[slot]))

    @pl.when(t == 0)
    def _():
        for g in range(ahead):
            for cp in chunk_copies(g):
                cp.start()

    @pl.when(t < nv)
    def _():
        u = u_ref[...].astype(BF16)
        for ci in range(n_chunks):
            g = t * n_chunks + ci
            slot = g % n_slots
            for cp in chunk_copies(g):
                cp.wait()

            @pl.when(g + ahead < nv * n_chunks)
            def _():
                for cp in chunk_copies(g + ahead):
                    cp.start()

            hg = jnp.dot(u, wgbuf[slot].astype(BF16), preferred_element_type=F32)
            hu = jnp.dot(u, wubuf[slot].astype(BF16), preferred_element_type=F32)
            h = (_silu(hg) * hu).astype(BF16)
            part = jnp.dot(h, wdbuf[slot].astype(BF16), preferred_element_type=F32)
            if ci == 0:
                out_ref[...] = part
            else:
                out_ref[...] += part

    @pl.when(t >= nv)
    def _():
        out_ref[...] = jnp.zeros_like(out_ref)


def _moe_ffn(u_sorted, tile_expert, n_valid, wg, wu, wd):
    R, D = u_sorted.shape
    tm = MOE_TM
    n_slots = 3
    tile = lambda t, nv: jnp.minimum(t, nv[0] - 1)
    return pl.pallas_call(
        _moe_ffn_kernel,
        out_shape=jax.ShapeDtypeStruct((R, D), F32),
        grid_spec=pltpu.PrefetchScalarGridSpec(
            num_scalar_prefetch=2,
            grid=(R // tm,),
            in_specs=[pl.BlockSpec((tm, D), lambda t, te, nv: (tile(t, nv), 0)),
                      pl.BlockSpec(memory_space=pl.ANY),
                      pl.BlockSpec(memory_space=pl.ANY),
                      pl.BlockSpec(memory_space=pl.ANY)],
            out_specs=pl.BlockSpec((tm, D), lambda t, te, nv: (t, 0)),
            scratch_shapes=[pltpu.VMEM((n_slots, D, MOE_CW), F32), pltpu.VMEM((n_slots, D, MOE_CW), F32),
                            pltpu.VMEM((n_slots, MOE_CW, D), F32),
                            pltpu.SemaphoreType.DMA((n_slots,))]),
        compiler_params=_params(("arbitrary",), vmem=60 * 1024 * 1024),
        name="moe_ffn",
    )(tile_expert, n_valid, u_sorted, wg, wu, wd)


def _combine_kernel(pos_ref, y_ref, topw_ref, x_ref, g2_ref, lnw_ref, lnb_ref, out_ref, ybuf, sem):
    tm = x_ref.shape[0]
    m = pl.program_id(0)
    slot = m % 2

    def gather(step, buf):
        for k in range(2):
            _start_rows(tm, lambda base, j: _row_copy(y_ref, pos_ref[2 * (step * tm + base + j) + k],
                                                      ybuf.at[buf, k], base + j, sem.at[buf]))

    @pl.when(m == 0)
    def _():
        gather(0, 0)

    @pl.when(m + 1 < pl.num_programs(0))
    def _():
        gather(m + 1, 1 - slot)

    _wait_rows(2 * tm, _row_copy(y_ref, 0, ybuf.at[slot, 0], 0, sem.at[slot]))
    tw = topw_ref[...]
    y = tw[:, 0:1] * ybuf[slot, 0] + tw[:, 1:2] * ybuf[slot, 1]
    r = ALPHA * x_ref[...] + (1.0 + g2_ref[...]) * y
    out_ref[...] = _layer_norm_rows(r, lnw_ref[...], lnb_ref[...])


def _combine(y_sorted, pos, topw, x1, g2, ln_w, ln_b, seq):
    N, D = x1.shape
    tm = 512
    tpb = seq // tm
    row = lambda m, pos: (m, 0)
    const = lambda m, pos: (0, 0)
    return pl.pallas_call(
        _combine_kernel,
        out_shape=jax.ShapeDtypeStruct((N, D), F32),
        grid_spec=pltpu.PrefetchScalarGridSpec(
            num_scalar_prefetch=1,
            grid=(N // tm,),
            in_specs=[pl.BlockSpec(memory_space=pl.ANY),
                      pl.BlockSpec((tm, LANES), row),
                      pl.BlockSpec((tm, D), row),
                      pl.BlockSpec((None, 1, D), lambda m, pos: (m // tpb, 0, 0)),
                      pl.BlockSpec((1, D), const), pl.BlockSpec((1, D), const)],
            out_specs=pl.BlockSpec((tm, D), row),
            scratch_shapes=[pltpu.VMEM((2, 2, tm, D), F32), pltpu.SemaphoreType.DMA((2,))]),
        compiler_params=_params(("arbitrary",)),
        name="moe_combine_ln",
    )(pos, y_sorted, topw, x1, g2, ln_w.reshape(1, -1), ln_b.reshape(1, -1))


def _routing_plan(ids, n_tiles):
    e_flat = ids[:, :2].reshape(-1)
    onehot = (e_flat[:, None] == jnp.arange(N_EXPERTS, dtype=jnp.int32)[None, :]).astype(jnp.int32)
    csum = jnp.cumsum(onehot, axis=0)
    rank = jnp.sum((csum - onehot) * onehot, axis=1)
    counts = csum[-1]
    padded = ((counts + MOE_TM - 1) // MOE_TM) * MOE_TM
    ends = jnp.cumsum(padded)
    offs = ends - padded
    pos = (jnp.sum(onehot * offs[None, :], axis=1) + rank).astype(jnp.int32)
    tile_start = jnp.arange(n_tiles, dtype=jnp.int32) * MOE_TM
    n_valid = (ends[-1] // MOE_TM).astype(jnp.int32)
    tile_start = jnp.minimum(tile_start, (n_valid - 1) * MOE_TM)
    tile_expert = jnp.sum((tile_start[:, None] >= ends[None, :]).astype(jnp.int32), axis=1).astype(jnp.int32)
    total = jnp.full((1,), n_tiles * MOE_TM, jnp.int32)
    gaps = jnp.stack([jnp.concatenate([offs + counts, ends[-1:]]),
                      jnp.concatenate([ends, total])], axis=1).reshape(-1).astype(jnp.int32)
    return pos, tile_expert, n_valid.reshape(1), gaps


def _moe(u2, ids, topw, wg, wu, wd, x1, g2, ln_w, ln_b, seq):
    N = u2.shape[0]
    n_tiles = -(-2 * N // MOE_TM) + N_EXPERTS
    pos, tile_expert, n_valid, gaps = _routing_plan(ids, n_tiles)
    u_sorted = _dispatch(u2, pos, gaps, n_tiles * MOE_TM)
    y_sorted = _moe_ffn(u_sorted, tile_expert, n_valid, wg, wu, wd)
    return _combine(y_sorted, pos, topw, x1, g2, ln_w, ln_b, seq)


def _split_w_in_kernel(wt_ref, gt_ref, main_ref, gate_ref):
    main_ref[...] = wt_ref[0].T.astype(BF16)

    @pl.when(pl.program_id(1) == 0)
    def _():
        g = gt_ref[...]
        g = jnp.concatenate([g, jnp.zeros((LANES - g.shape[0], g.shape[1]), F32)], axis=0)
        gate_ref[...] = g.T.astype(BF16)


def _split_w_in(w_in):
    L, D, C = w_in.shape
    w_t = jnp.swapaxes(w_in, 1, 2)
    g0 = 4 * GROUP_WIDTH
    ng = 2 * N_HEADS
    tc = 512
    src_col = lambda j: (j * (tc // ng) + jnp.where(j * tc >= g0, 1, 0)) * ng
    return pl.pallas_call(
        _split_w_in_kernel,
        out_shape=(jax.ShapeDtypeStruct((L, D, C - ng), BF16),
                   jax.ShapeDtypeStruct((L, D, LANES), BF16)),
        grid=(L, (C - ng) // tc),
        in_specs=[pl.BlockSpec((pl.Element(1), pl.Element(tc), pl.Element(D)),
                               lambda l, j: (l, src_col(j), 0)),
                  pl.BlockSpec((None, ng, D), lambda l, j: (l, g0 // ng, 0))],
        out_specs=(pl.BlockSpec((None, D, tc), lambda l, j: (l, 0, j)),
                   pl.BlockSpec((None, D, LANES), lambda l, j: (l, 0, 0))),
        compiler_params=_params(("parallel", "arbitrary")),
        name="split_w_in",
    )(w_t, w_t)


def kernel(x, c, w_in, mlstm_conv_w, mlstm_conv_b, mlstm_gate_b, mlstm_norm_w, pool_w, pool_scale, conv_dw_w, conv_dw_b, conv_ln_w, conv_ln_b, conv_pw_w, conv_pw_b, group_norm_w, w_out, ada_w, ada_b, ln1_w, ln1_b, ln2_w, ln2_b, ffn_w_gate, ffn_w_up, ffn_w_down, moe_router_w, moe_router_b, moe_w_gate, moe_w_up, moe_w_down):
    B, S, D = x.shape
    GW = GROUP_WIDTH
    ada = _ada_all(c, ada_w, ada_b)
    x2d = x.reshape(B * S, D)
    w_main, w_gate = _split_w_in(w_in)
    for l in range(DEPTH):
        mod = [ada[l, :, i * D:(i + 1) * D].reshape(B, 1, D) for i in range(6)]
        sh1, sc1, g1, sh2, sc2, g2 = mod
        proj, gates = _in_proj(x2d, sc1, sh1, w_main, w_gate, l, S)
        proj = proj.reshape(B, S, -1)
        gates = gates.reshape(B, S, LANES)
        gn_b, gn_c, gn_d = (group_norm_w[l, i * GW:(i + 1) * GW] for i in range(3))
        ya = _mlstm_mixer(proj, gates, mlstm_conv_w[l], mlstm_conv_b[l], mlstm_gate_b[l],
                          mlstm_norm_w[l], B, S)
        yb = _pool_mixer(proj, pool_w[l], pool_scale[l], gn_b, B, S)
        yc = _sb_mixer(proj, B, S)
        yd = _conv_mixer(proj, conv_dw_w[l], conv_dw_b[l], conv_ln_w[l], conv_ln_b[l],
                         conv_pw_w[l], conv_pw_b[l], gn_d, B, S)
        flat = lambda t: t.reshape(B * S, GW)
        j = l // 2
        router = None
        if l % 2 == 1:
            rw = jnp.zeros((D, LANES), F32).at[:, :N_EXPERTS].set(moe_router_w[j])
            rw_hi = rw.astype(BF16)
            rw_lo = (rw - rw_hi.astype(F32)).astype(BF16)
            rb = jnp.zeros((1, LANES), F32).at[0, :N_EXPERTS].set(moe_router_b[j])
            router = (jnp.concatenate([rw_hi, rw_lo], axis=1), rb)
        outs = _out_proj(flat(ya), flat(yb), flat(yc), flat(yd), gn_c, w_out[l].astype(BF16), x2d,
                         g1, ln1_w[l], ln1_b[l], sc2, sh2, S, router)
        if l % 2 == 0:
            x1, u2 = outs
            x2d = _ffn(u2, ffn_w_gate[j].astype(BF16), ffn_w_up[j].astype(BF16),
                       ffn_w_down[j].astype(BF16), x1, g2, ln2_w[l], ln2_b[l], S)
        else:
            x1, u2, ids, topw = outs
            x2d = _moe(u2, ids, topw, moe_w_gate[j], moe_w_up[j], moe_w_down[j], x1, g2,
                       ln2_w[l], ln2_b[l], S)
    return x2d.reshape(B, S, D)
```

```python
import functools

import jax
import jax.numpy as jnp
from jax import lax
from jax.experimental import pallas as pl
from jax.experimental.pallas import tpu as pltpu

F32 = jnp.float32
BF16 = jnp.bfloat16

DEPTH = 2
CHUNK = 256
GROUP_WIDTH = 512
N_HEADS = 4
HEAD_DIM = 128
MLSTM_CONV = 4
POOL_WINDOWS = (2, 4, 8, 16)
CONV_WIDTH = 31
SB_KEY_BLOCK = 256
N_EXPERTS = 8
ALPHA = (2.0 * DEPTH) ** 0.25
EPS = 1e-5
LOG2E = 1.4426950408889634
LANES = 128
VMEM_LIMIT = 56 * 1024 * 1024


def _params(sem, vmem=VMEM_LIMIT):
    return pltpu.CompilerParams(dimension_semantics=sem, vmem_limit_bytes=vmem)


def _silu(x):
    return x * (1.0 / (1.0 + jnp.exp(-x)))


def _sigmoid(x):
    return 1.0 / (1.0 + jnp.exp(-x))


def _neg_softplus(x):
    return -(jnp.maximum(x, 0.0) + jnp.log(1.0 + jnp.exp(-jnp.abs(x))))


def _layer_norm_rows(r, w, b):
    mu = jnp.mean(r, axis=-1, keepdims=True)
    d = r - mu
    var = jnp.mean(d * d, axis=-1, keepdims=True)
    return d * lax.rsqrt(var + EPS) * w + b


def _rms_rows(y, w):
    return y * lax.rsqrt(jnp.mean(y * y, axis=-1, keepdims=True) + EPS) * w


def _bdot(a, b):
    return jnp.dot(a.astype(BF16), b.astype(BF16), preferred_element_type=F32)


def _ada_kernel(c_ref, w_ref, b_ref, out_ref):
    out_ref[...] = _bdot(_silu(c_ref[...]), w_ref[...]) + b_ref[...]


def _ada_all(c, ada_w, ada_b):
    L, D, D6 = ada_w.shape
    Bn = c.shape[0]
    tn = 2048
    return pl.pallas_call(
        _ada_kernel,
        out_shape=jax.ShapeDtypeStruct((L, Bn, D6), F32),
        grid=(L, D6 // tn),
        in_specs=[pl.BlockSpec((Bn, D), lambda l, n: (0, 0)),
                  pl.BlockSpec((None, D, tn), lambda l, n: (l, 0, n)),
                  pl.BlockSpec((None, 1, tn), lambda l, n: (l, 0, n))],
        out_specs=pl.BlockSpec((None, Bn, tn), lambda l, n: (l, 0, n)),
        compiler_params=_params(("parallel", "parallel")),
        name="ada_mod",
    )(c, ada_w, ada_b.reshape(L, 1, D6))


def _inproj_kernel(x_ref, sc_ref, sh_ref, w_ref, wg_ref, proj_ref, gates_ref, u_ref):
    @pl.when(pl.program_id(1) == 0)
    def _():
        u = (x_ref[...] * (1.0 + sc_ref[...]) + sh_ref[...]).astype(BF16)
        u_ref[...] = u
        gates_ref[...] = jnp.dot(u, wg_ref[...], preferred_element_type=F32)

    proj_ref[...] = jnp.dot(u_ref[...], w_ref[...], preferred_element_type=F32)


def _in_proj(x2d, sc, sh, w_main, w_gate, layer, seq):
    N, D = x2d.shape
    NC = w_main.shape[2]
    tm, tn = 1024, 1280
    tpb = seq // tm
    return pl.pallas_call(
        _inproj_kernel,
        out_shape=(jax.ShapeDtypeStruct((N, NC), F32), jax.ShapeDtypeStruct((N, LANES), F32)),
        grid=(N // tm, NC // tn),
        in_specs=[pl.BlockSpec((tm, D), lambda m, n: (m, 0)),
                  pl.BlockSpec((None, 1, D), lambda m, n: (m // tpb, 0, 0)),
                  pl.BlockSpec((None, 1, D), lambda m, n: (m // tpb, 0, 0)),
                  pl.BlockSpec((None, D, tn), lambda m, n: (layer, 0, n)),
                  pl.BlockSpec((None, D, LANES), lambda m, n: (layer, 0, 0))],
        out_specs=(pl.BlockSpec((tm, tn), lambda m, n: (m, n)),
                   pl.BlockSpec((tm, LANES), lambda m, n: (m, 0))),
        scratch_shapes=[pltpu.VMEM((tm, D), BF16)],
        compiler_params=_params(("parallel", "arbitrary")),
        name="in_proj",
    )(x2d, sc, sh, w_main, w_gate)


def _cumsum_rows(x):
    n = x.shape[0]
    row = lax.broadcasted_iota(jnp.int32, x.shape, 0)
    k = 1
    while k < n:
        x = x + jnp.where(row >= k, pltpu.roll(x, k, axis=0), 0.0)
        k *= 2
    return x


def _mlstm_kernel(qk_ref, v_ref, o_ref, g_ref, cw_ref, cb_ref, gb_ref, nw_ref, out_ref,
                  c_state, n_state, m_state, tail_ref):
    @pl.when(pl.program_id(1) == 0)
    def _():
        c_state[...] = jnp.zeros_like(c_state)
        n_state[...] = jnp.zeros_like(n_state)
        m_state[...] = jnp.zeros_like(m_state)
        tail_ref[...] = jnp.zeros_like(tail_ref)

    for bb in range(qk_ref.shape[0]):
        _mlstm_block(qk_ref.at[bb], v_ref.at[bb], o_ref.at[bb], g_ref.at[bb], cw_ref, cb_ref, gb_ref,
                     nw_ref, out_ref.at[bb], c_state.at[bb], n_state.at[bb], m_state.at[bb],
                     tail_ref.at[bb])


def _mlstm_block(qk_ref, v_ref, o_ref, g_ref, cw_ref, cb_ref, gb_ref, nw_ref, out_ref,
                 c_state, n_state, m_state, tail_ref):
    T = qk_ref.shape[0]
    GW = GROUP_WIDTH

    xx = jnp.concatenate([tail_ref[...], qk_ref[...]], axis=0)
    tail_ref[...] = qk_ref[T - 8:T, :]
    conv = cb_ref[...]
    for j in range(MLSTM_CONV):
        off = 8 - (MLSTM_CONV - 1) + j
        tap = xx[off:off + T, :] if off % 8 == 0 else pltpu.roll(xx, T + 8 - off, axis=0)[0:T, :]
        conv = conv + cw_ref[j:j + 1, :] * tap
    qk = _silu(conv)

    gates = g_ref[...] + gb_ref[...]
    logf = _neg_softplus(-gates)
    causal = (lax.broadcasted_iota(jnp.int32, (CHUNK, CHUNK), 1)
              <= lax.broadcasted_iota(jnp.int32, (CHUNK, CHUNK), 0))

    for c in range(T // CHUNK):
        rows = slice(c * CHUNK, (c + 1) * CHUNK)
        gi = gates[rows, :]
        bcum = _cumsum_rows(logf[rows, :])
        gi_t = gi.T
        bcum_t = bcum.T
        for h in range(N_HEADS):
            cols = slice(h * HEAD_DIM, (h + 1) * HEAD_DIM)
            q = qk[rows, cols] * (HEAD_DIM ** -0.5)
            k = qk[rows, GW + h * HEAD_DIM:GW + (h + 1) * HEAD_DIM]
            v = v_ref[rows, cols]
            b_col = bcum[:, N_HEADS + h:N_HEADS + h + 1]
            b_row = bcum_t[N_HEADS + h:N_HEADS + h + 1, :]
            i_col = gi[:, h:h + 1]
            i_row = gi_t[h:h + 1, :]
            g_tot = b_col[CHUNK - 1:CHUNK, :]
            m_prev = m_state[h:h + 1, 0:1]
            c_prev = c_state[h]
            n_prev = n_state[h:h + 1, :]

            log_d = jnp.where(causal, b_col - b_row + i_row, -jnp.inf)
            m_inter = b_col + m_prev
            m_t = jnp.maximum(m_inter, jnp.max(log_d, axis=-1, keepdims=True))
            s = lax.dot_general(q.astype(BF16), k.astype(BF16), (((1,), (1,)), ((), ())),
                                preferred_element_type=F32)
            w = jnp.exp(log_d - m_t) * s
            inter = jnp.exp(m_inter - m_t)
            num = inter * _bdot(q, c_prev) + _bdot(w, v)
            den = inter * jnp.sum(q * n_prev, axis=-1, keepdims=True) + jnp.sum(w, axis=-1, keepdims=True)
            hcur = num / jnp.maximum(jnp.abs(den), jnp.exp(-m_t))

            a_col = g_tot - b_col + i_col
            m_new = jnp.maximum(g_tot + m_prev, jnp.max(a_col, axis=0, keepdims=True))
            decay = jnp.exp(g_tot + m_prev - m_new)
            kw = k * jnp.exp(a_col - m_new)
            c_state[h] = decay * c_prev + _bdot(kw.T, v)
            n_state[h:h + 1, :] = decay * n_prev + jnp.sum(kw, axis=0, keepdims=True)
            m_state[h:h + 1, :] = jnp.broadcast_to(m_new, (1, LANES))

            mu = jnp.mean(hcur, axis=-1, keepdims=True)
            dlt = hcur - mu
            var = jnp.mean(dlt * dlt, axis=-1, keepdims=True)
            hn = dlt * lax.rsqrt(var + EPS) * nw_ref[:, cols]
            out_ref[rows, cols] = (_sigmoid(o_ref[rows, cols]) * hn).astype(out_ref.dtype)


def _mlstm_mixer(proj, gates, conv_w, conv_b, gate_b, norm_w, B, S):
    T = 256
    NB = 2
    GW = GROUP_WIDTH
    gb = jnp.zeros((1, LANES), F32).at[0, :2 * N_HEADS].set(gate_b)
    return pl.pallas_call(
        _mlstm_kernel,
        out_shape=jax.ShapeDtypeStruct((B, S, GW), BF16),
        grid=(B // NB, S // T),
        in_specs=[pl.BlockSpec((NB, T, 2 * GW), lambda b, t: (b, t, 0)),
                  pl.BlockSpec((NB, T, GW), lambda b, t: (b, t, 2)),
                  pl.BlockSpec((NB, T, GW), lambda b, t: (b, t, 3)),
                  pl.BlockSpec((NB, T, LANES), lambda b, t: (b, t, 0)),
                  pl.BlockSpec((MLSTM_CONV, 2 * GW), lambda b, t: (0, 0)),
                  pl.BlockSpec((1, 2 * GW), lambda b, t: (0, 0)),
                  pl.BlockSpec((1, LANES), lambda b, t: (0, 0)),
                  pl.BlockSpec((1, GW), lambda b, t: (0, 0))],
        out_specs=pl.BlockSpec((NB, T, GW), lambda b, t: (b, t, 0)),
        scratch_shapes=[pltpu.VMEM((NB, N_HEADS, HEAD_DIM, HEAD_DIM), F32),
                        pltpu.VMEM((NB, 8, LANES), F32),
                        pltpu.VMEM((NB, 8, LANES), F32),
                        pltpu.VMEM((NB, 8, 2 * GW), F32)],
        compiler_params=_params(("parallel", "arbitrary")),
        name="mlstm_mixer",
    )(proj, proj, proj, gates, conv_w, conv_b.reshape(1, -1), gb, norm_w.reshape(1, -1))


def _pool_kernel(x_ref, w_ref, scale_ref, gn_ref, out_ref, tail_ref):
    T = x_ref.shape[0]
    HALO = 16
    t_blk = pl.program_id(1)

    @pl.when(t_blk == 0)
    def _():
        tail_ref[...] = jnp.zeros_like(tail_ref)

    x = x_ref[...]
    xx = jnp.concatenate([tail_ref[...], x], axis=0)
    tail_ref[...] = x[T - HALO:T, :]
    pos = t_blk * T + lax.broadcasted_iota(jnp.int32, (T, 1), 0) + 1
    outs = []
    for g, win in enumerate(POOL_WINDOWS):
        cols = slice(g * LANES, (g + 1) * LANES)
        s = xx[:, cols]
        k = 1
        while k < win:
            s = s + pltpu.roll(s, k, axis=0)
            k *= 2
        cnt = jnp.minimum(pos, win).astype(F32)
        yg = s[HALO:, :] / cnt - x[:, cols]
        outs.append(_bdot(yg, w_ref[g]))
    y = jnp.concatenate(outs, axis=-1) * scale_ref[...]
    out_ref[...] = _rms_rows(y, gn_ref[...]).astype(out_ref.dtype)


def _pool_mixer(proj, pool_w, pool_scale, gn, B, S):
    T = 512
    GW = GROUP_WIDTH
    return pl.pallas_call(
        _pool_kernel,
        out_shape=jax.ShapeDtypeStruct((B, S, GW), BF16),
        grid=(B, S // T),
        in_specs=[pl.BlockSpec((None, T, GW), lambda b, t: (b, t, 4)),
                  pl.BlockSpec((len(POOL_WINDOWS), LANES, LANES), lambda b, t: (0, 0, 0)),
                  pl.BlockSpec((1, GW), lambda b, t: (0, 0)),
                  pl.BlockSpec((1, GW), lambda b, t: (0, 0))],
        out_specs=pl.BlockSpec((None, T, GW), lambda b, t: (b, t, 0)),
        scratch_shapes=[pltpu.VMEM((16, GW), F32)],
        compiler_params=_params(("parallel", "arbitrary")),
        name="pool_mixer",
    )(proj, pool_w.astype(BF16), pool_scale.reshape(1, -1), gn.reshape(1, -1))


def _sb_kernel(q_ref, k_ref, v_ref, out_ref, kb_ref, vb_ref):
    LQ = q_ref.shape[0]
    LK = SB_KEY_BLOCK
    ratio = LQ // LK
    qi = pl.program_id(1)

    @pl.when(qi == 0)
    def _():
        kb_ref[...] = k_ref[...].astype(BF16)
        vb_ref[...] = v_ref[...].astype(BF16)

    scale2 = HEAD_DIM ** -0.5 * LOG2E
    tri = (lax.broadcasted_iota(jnp.int32, (LK, LK), 0)
           > lax.broadcasted_iota(jnp.int32, (LK, LK), 1)).astype(BF16)
    row = lax.broadcasted_iota(jnp.int32, (LQ, LK), 0)
    col = lax.broadcasted_iota(jnp.int32, (LQ, LK), 1)
    heads = [slice(h * HEAD_DIM, (h + 1) * HEAD_DIM) for h in range(N_HEADS)]
    qs = [(q_ref[:, hs] * scale2).astype(BF16) for hs in heads]

    def sweep(kb, carry, strict):
        start = pl.multiple_of(kb * LK, LK)
        zs, lks = [], []
        for h, hs in enumerate(heads):
            k_blk = kb_ref[pl.ds(start, LK), hs]
            z = lax.dot_general(qs[h], k_blk, (((1,), (1,)), ((), ())), preferred_element_type=F32)
            lk = -(jnp.maximum(z, 0.0) + jnp.log2(1.0 + jnp.exp2(-jnp.abs(z))))
            if strict is not None:
                lk = jnp.where(strict, lk, 0.0)
            zs.append(z)
            lks.append(lk)
        inner = jnp.dot(jnp.concatenate([lk.astype(BF16) for lk in lks], axis=0), tri,
                        preferred_element_type=F32)
        new = []
        for h, hs in enumerate(heads):
            after, acc = carry[h]
            v_blk = vb_ref[pl.ds(start, LK), hs]
            a = jnp.exp2(zs[h] + lks[h] + inner[h * LQ:(h + 1) * LQ, :] + after)
            if strict is not None:
                a = jnp.where(strict, a, 0.0)
            acc = acc + jnp.dot(a.astype(BF16), v_blk, preferred_element_type=F32)
            after = after + jnp.sum(lks[h], axis=-1, keepdims=True)
            new.append((after, acc))
        return tuple(new)

    init = tuple((jnp.zeros((LQ, 1), F32), jnp.zeros((LQ, HEAD_DIM), F32)) for _ in heads)
    carry = init
    for j in reversed(range(ratio)):
        carry = sweep(ratio * qi + j, carry, (j * LK + col) < row)
    carry = lax.fori_loop(0, ratio * qi, lambda it, c: sweep(ratio * qi - 1 - it, c, None), carry)
    for h, hs in enumerate(heads):
        out_ref[:, hs] = carry[h][1]


def _sb_mixer(proj, B, S):
    LQ = 512
    GW = GROUP_WIDTH
    return pl.pallas_call(
        _sb_kernel,
        out_shape=jax.ShapeDtypeStruct((B, S, GW), F32),
        grid=(B, S // LQ),
        in_specs=[pl.BlockSpec((None, LQ, GW), lambda b, i: (b, i, 5)),
                  pl.BlockSpec((None, S, GW), lambda b, i: (b, 0, 6)),
                  pl.BlockSpec((None, S, GW), lambda b, i: (b, 0, 7))],
        out_specs=pl.BlockSpec((None, LQ, GW), lambda b, i: (b, i, 0)),
        scratch_shapes=[pltpu.VMEM((S, GW), BF16), pltpu.VMEM((S, GW), BF16)],
        compiler_params=_params(("parallel", "arbitrary")),
        name="sb_attention",
    )(proj, proj, proj)


def _conv_kernel(x_ref, dw_ref, dwb_ref, lnw_ref, lnb_ref, pw_ref, pwb_ref, gn_ref, out_ref,
                 hbuf, ybuf):
    T = x_ref.shape[0]
    GW = GROUP_WIDTH
    HALO = 32
    SUB = 64

    @pl.when(pl.program_id(1) == 0)
    def _():
        hbuf[0:HALO, :] = jnp.zeros((HALO, GW), F32)

    @pl.when(pl.program_id(1) > 0)
    def _():
        hbuf[0:HALO, :] = hbuf[T:T + HALO, :]

    hbuf[HALO:HALO + T, :] = x_ref[:, 0:GW] * _sigmoid(x_ref[:, GW:2 * GW])

    def sub(i, carry):
        r0 = pl.multiple_of(i * SUB, SUB)
        acc = jnp.broadcast_to(dwb_ref[...], (SUB, GW))
        win = hbuf[pl.ds(r0, SUB + HALO), :]
        for res in range(8):
            shifted = win if res == 0 else pltpu.roll(win, SUB + HALO - res, axis=0)
            for j in range(CONV_WIDTH):
                off = HALO - CONV_WIDTH + 1 + j
                if off % 8 == res:
                    base = off - res
                    acc = acc + dw_ref[j:j + 1, :] * shifted[base:base + SUB, :]
        ybuf[pl.ds(r0, SUB), :] = acc
        return carry

    lax.fori_loop(0, T // SUB, sub, 0)
    hn = _silu(_layer_norm_rows(ybuf[...], lnw_ref[...], lnb_ref[...]))
    y = _bdot(hn, pw_ref[...]) + pwb_ref[...]
    out_ref[...] = _rms_rows(y, gn_ref[...]).astype(out_ref.dtype)


def _conv_mixer(proj, dw_w, dw_b, ln_w, ln_b, pw_w, pw_b, gn, B, S):
    T = 512
    GW = GROUP_WIDTH
    dw_pad = jnp.zeros((32, GW), F32).at[:CONV_WIDTH].set(dw_w)
    r = lambda a: a.reshape(1, -1)
    return pl.pallas_call(
        _conv_kernel,
        out_shape=jax.ShapeDtypeStruct((B, S, GW), BF16),
        grid=(B, S // T),
        in_specs=[pl.BlockSpec((None, T, 2 * GW), lambda b, t: (b, t, 4)),
                  pl.BlockSpec((32, GW), lambda b, t: (0, 0)),
                  pl.BlockSpec((1, GW), lambda b, t: (0, 0)),
                  pl.BlockSpec((1, GW), lambda b, t: (0, 0)),
                  pl.BlockSpec((1, GW), lambda b, t: (0, 0)),
                  pl.BlockSpec((GW, GW), lambda b, t: (0, 0)),
                  pl.BlockSpec((1, GW), lambda b, t: (0, 0)),
                  pl.BlockSpec((1, GW), lambda b, t: (0, 0))],
        out_specs=pl.BlockSpec((None, T, GW), lambda b, t: (b, t, 0)),
        scratch_shapes=[pltpu.VMEM((T + 32, GW), F32), pltpu.VMEM((T, GW), F32)],
        compiler_params=_params(("parallel", "arbitrary")),
        name="conv_mixer",
    )(proj, dw_pad, r(dw_b), r(ln_w), r(ln_b), pw_w.astype(BF16), r(pw_b), r(gn))


def _outproj_kernel(with_router, ya_ref, yb_ref, yc_ref, yd_ref, gnc_ref, w_ref, x_ref, g1_ref,
                    lnw_ref, lnb_ref, sc_ref, sh_ref, *rest):
    GW = GROUP_WIDTH
    if with_router:
        rw_ref, rb_ref, x1_ref, u2_ref, ids_ref, topw_ref = rest
    else:
        x1_ref, u2_ref = rest
    yc = _rms_rows(yc_ref[...], gnc_ref[...]).astype(BF16)
    acc = jnp.dot(ya_ref[...], w_ref[0:GW, :], preferred_element_type=F32)
    acc = acc + jnp.dot(yb_ref[...], w_ref[GW:2 * GW, :], preferred_element_type=F32)
    acc = acc + jnp.dot(yc, w_ref[2 * GW:3 * GW, :], preferred_element_type=F32)
    acc = acc + jnp.dot(yd_ref[...], w_ref[3 * GW:4 * GW, :], preferred_element_type=F32)
    r = ALPHA * x_ref[...] + (1.0 + g1_ref[...]) * acc
    x1 = _layer_norm_rows(r, lnw_ref[...], lnb_ref[...])
    x1_ref[...] = x1
    u2 = x1 * (1.0 + sc_ref[...]) + sh_ref[...]
    u2_ref[...] = u2.astype(u2_ref.dtype)
    if with_router:
        u_hi = u2.astype(BF16)
        u_lo = (u2 - u_hi.astype(F32)).astype(BF16)
        p = jnp.dot(u_hi, rw_ref[...], preferred_element_type=F32)
        logits = (p[:, :LANES] + p[:, LANES:]
                  + jnp.dot(u_lo, rw_ref[:, :LANES], preferred_element_type=F32) + rb_ref[...])
        lane = lax.broadcasted_iota(jnp.int32, logits.shape, 1)
        lg = jnp.where(lane < N_EXPERTS, logits, -jnp.inf)
        m1 = jnp.max(lg, axis=-1, keepdims=True)
        i1 = jnp.min(jnp.where(lg == m1, lane, LANES), axis=-1, keepdims=True)
        lg2 = jnp.where(lane == i1, -jnp.inf, lg)
        m2 = jnp.max(lg2, axis=-1, keepdims=True)
        i2 = jnp.min(jnp.where(lg2 == m2, lane, LANES), axis=-1, keepdims=True)
        e2 = jnp.exp(m2 - m1)
        w1 = 1.0 / (1.0 + e2)
        w2 = e2 / (1.0 + e2)
        ids_ref[...] = jnp.where(lane == 0, i1, jnp.where(lane == 1, i2, 0))
        topw_ref[...] = jnp.where(lane == 0, w1, jnp.where(lane == 1, w2, 0.0))


def _out_proj(ya, yb, yc, yd, gnc, w_out, x2d, g1, ln_w, ln_b, sc2, sh2, seq, router=None):
    N, D = x2d.shape
    GW = GROUP_WIDTH
    tm = 512
    tpb = seq // tm
    row = lambda m: (m, 0)
    const = lambda m: (0, 0)
    perb = lambda m: (m // tpb, 0, 0)
    in_specs = [pl.BlockSpec((tm, GW), row), pl.BlockSpec((tm, GW), row),
                pl.BlockSpec((tm, GW), row), pl.BlockSpec((tm, GW), row),
                pl.BlockSpec((1, GW), const),
                pl.BlockSpec((D, D), const),
                pl.BlockSpec((tm, D), row),
                pl.BlockSpec((None, 1, D), perb),
                pl.BlockSpec((1, D), const), pl.BlockSpec((1, D), const),
                pl.BlockSpec((None, 1, D), perb), pl.BlockSpec((None, 1, D), perb)]
    args = [ya, yb, yc, yd, gnc.reshape(1, -1), w_out, x2d, g1, ln_w.reshape(1, -1),
            ln_b.reshape(1, -1), sc2, sh2]
    out_shape = [jax.ShapeDtypeStruct((N, D), F32),
                 jax.ShapeDtypeStruct((N, D), BF16 if router is None else F32)]
    out_specs = [pl.BlockSpec((tm, D), row), pl.BlockSpec((tm, D), row)]
    if router is not None:
        rw, rb = router
        in_specs += [pl.BlockSpec((D, 2 * LANES), const), pl.BlockSpec((1, LANES), const)]
        args += [rw, rb]
        out_shape += [jax.ShapeDtypeStruct((N, LANES), jnp.int32), jax.ShapeDtypeStruct((N, LANES), F32)]
        out_specs += [pl.BlockSpec((tm, LANES), row), pl.BlockSpec((tm, LANES), row)]
    return pl.pallas_call(
        functools.partial(_outproj_kernel, router is not None),
        out_shape=tuple(out_shape),
        grid=(N // tm,),
        in_specs=in_specs,
        out_specs=tuple(out_specs),
        compiler_params=_params(("parallel",)),
        name="out_proj_ln",
    )(*args)


FFN_CW = 512


def _ffn_kernel(u_ref, wg_hbm, wu_hbm, wd_hbm, x_ref, g2_ref, lnw_ref, lnb_ref, out_ref,
                acc_ref, wgbuf, wubuf, wdbuf, sem):
    m = pl.program_id(0)
    n_chunks = wg_hbm.shape[1] // FFN_CW
    total = pl.num_programs(0) * n_chunks
    n_slots = wgbuf.shape[0]
    ahead = n_slots - 1

    def chunk_copies(g):
        slot = g % n_slots
        cols = pl.ds(pl.multiple_of((g % n_chunks) * FFN_CW, FFN_CW), FFN_CW)
        return (pltpu.make_async_copy(wg_hbm.at[:, cols], wgbuf.at[slot], sem.at[slot]),
                pltpu.make_async_copy(wu_hbm.at[:, cols], wubuf.at[slot], sem.at[slot]),
                pltpu.make_async_copy(wd_hbm.at[cols, :], wdbuf.at[slot], sem.at[slot]))

    @pl.when(m == 0)
    def _():
        for g in range(ahead):
            for cp in chunk_copies(g):
                cp.start()

    u = u_ref[...]
    for ci in range(n_chunks):
        g = m * n_chunks + ci
        slot = g % n_slots
        for cp in chunk_copies(g):
            cp.wait()

        @pl.when(g + ahead < total)
        def _():
            for cp in chunk_copies(g + ahead):
                cp.start()

        hg = jnp.dot(u, wgbuf[slot], preferred_element_type=F32)
        hu = jnp.dot(u, wubuf[slot], preferred_element_type=F32)
        h = (_silu(hg) * hu).astype(BF16)
        part = jnp.dot(h, wdbuf[slot], preferred_element_type=F32)
        if ci == 0:
            acc_ref[...] = part
        else:
            acc_ref[...] += part

    r = ALPHA * x_ref[...] + (1.0 + g2_ref[...]) * acc_ref[...]
    out_ref[...] = _layer_norm_rows(r, lnw_ref[...], lnb_ref[...])


def _ffn(u2, wg, wu, wd, x1, g2, ln_w, ln_b, seq):
    N, D = u2.shape
    tm = 512
    n_slots = 3
    tpb = seq // tm
    row = lambda m: (m, 0)
    const = lambda m: (0, 0)
    hbm = pl.BlockSpec(memory_space=pl.ANY)
    return pl.pallas_call(
        _ffn_kernel,
        out_shape=jax.ShapeDtypeStruct((N, D), F32),
        grid=(N // tm,),
        in_specs=[pl.BlockSpec((tm, D), row), hbm, hbm, hbm,
                  pl.BlockSpec((tm, D), row),
                  pl.BlockSpec((None, 1, D), lambda m: (m // tpb, 0, 0)),
                  pl.BlockSpec((1, D), const), pl.BlockSpec((1, D), const)],
        out_specs=pl.BlockSpec((tm, D), row),
        scratch_shapes=[pltpu.VMEM((tm, D), F32),
                        pltpu.VMEM((n_slots, D, FFN_CW), BF16), pltpu.VMEM((n_slots, D, FFN_CW), BF16),
                        pltpu.VMEM((n_slots, FFN_CW, D), BF16),
                        pltpu.SemaphoreType.DMA((n_slots,))],
        compiler_params=_params(("arbitrary",)),
        name="ffn_ln",
    )(u2, wg, wu, wd, x1, g2, ln_w.reshape(1, -1), ln_b.reshape(1, -1))


MOE_TM = 768
MOE_CW = 256


def _row_copy(src, src_row, dst, dst_row, sem):
    return pltpu.make_async_copy(src.at[pl.ds(src_row, 1), :], dst.at[pl.ds(dst_row, 1), :], sem)


def _start_rows(n_rows, make_copy):
    def body(i, carry):
        base = pl.multiple_of(i * 8, 8)
        for j in range(8):
            make_copy(base, j).start(priority=j % 2)
        return carry
    lax.fori_loop(0, n_rows // 8, body, 0)


def _wait_rows(n_rows, one_copy):
    def body(i, carry):
        one_copy.wait()
        return carry
    lax.fori_loop(0, n_rows, body, 0, unroll=8)


def _dispatch_kernel(pos_ref, gap_ref, u_ref, out_ref, zbuf, sem, zsem):
    tm = u_ref.shape[0]
    first = pl.program_id(0) * tm

    def start(r, carry):
        for k in range(2):
            _row_copy(u_ref, r, out_ref, pos_ref[2 * (first + r) + k], sem).start(priority=k)
        return carry

    lax.fori_loop(0, tm, start, 0, unroll=8)

    @pl.when(pl.program_id(0) == pl.num_programs(0) - 1)
    def _():
        zbuf[...] = jnp.zeros_like(zbuf)
        n_gaps = gap_ref.shape[0] // 2

        def block_copy(i):
            return pltpu.make_async_copy(zbuf, out_ref.at[pl.ds(pl.multiple_of(i * 8, 8), 8), :], zsem.at[1])

        for g in range(n_gaps):
            lo, hi = gap_ref[2 * g], gap_ref[2 * g + 1]
            mid = jnp.minimum((lo + 7) // 8 * 8, hi)
            lax.fori_loop(lo, mid, lambda r, c: (_row_copy(zbuf, 0, out_ref, r, zsem.at[0]).start(), c)[1], 0)
            lax.fori_loop(mid // 8, hi // 8, lambda i, c: (block_copy(i).start(), c)[1], 0)
        for g in range(n_gaps):
            lo, hi = gap_ref[2 * g], gap_ref[2 * g + 1]
            mid = jnp.minimum((lo + 7) // 8 * 8, hi)
            lax.fori_loop(lo, mid, lambda r, c: (_row_copy(zbuf, 0, out_ref, 0, zsem.at[0]).wait(), c)[1], 0)
            lax.fori_loop(mid // 8, hi // 8, lambda i, c: (block_copy(0).wait(), c)[1], 0)

    _wait_rows(2 * tm, _row_copy(u_ref, 0, out_ref, 0, sem))


def _dispatch(u2, pos, gaps, n_rows):
    N, D = u2.shape
    tm = 512
    return pl.pallas_call(
        _dispatch_kernel,
        out_shape=jax.ShapeDtypeStruct((n_rows, D), u2.dtype),
        grid_spec=pltpu.PrefetchScalarGridSpec(
            num_scalar_prefetch=2,
            grid=(N // tm,),
            in_specs=[pl.BlockSpec((tm, D), lambda m, pos, gaps: (m, 0))],
            out_specs=pl.BlockSpec(memory_space=pl.ANY),
            scratch_shapes=[pltpu.VMEM((8, D), u2.dtype), pltpu.SemaphoreType.DMA(()),
                            pltpu.SemaphoreType.DMA((2,))]),
        compiler_params=_params(("arbitrary",)),
        name="moe_dispatch",
    )(pos, gaps, u2)


def _moe_ffn_kernel(te_ref, nv_ref, u_ref, wg_hbm, wu_hbm, wd_hbm, out_ref, wgbuf, wubuf, wdbuf, sem):
    t = pl.program_id(0)
    nv = nv_ref[0]
    F = wg_hbm.shape[2]
    n_chunks = F // MOE_CW
    n_slots = wgbuf.shape[0]
    ahead = n_slots - 1

    def chunk_copies(g):
        tile, ci = g // n_chunks, g % n_chunks
        e = te_ref[tile]
        slot = g % n_slots
        cols = pl.ds(pl.multiple_of(ci * MOE_CW, MOE_CW), MOE_CW)
        return (pltpu.make_async_copy(wg_hbm.at[e, :, cols], wgbuf.at[slot], sem.at[slot]),
                pltpu.make_async_copy(wu_hbm.at[e, :, cols], wubuf.at[slot], sem.at[slot]),
                pltpu.make_async_copy(wd_hbm.at[e, cols, :], wdbuf.at[slot], sem.at[slot]))

    @pl.when(t == 0)
    def _():
        for g in range(ahead):
            for cp in chunk_copies(g):
                cp.start()

    @pl.when(t < nv)
    def _():
        u = u_ref[...].astype(BF16)
        for ci in range(n_chunks):
            g = t * n_chunks + ci
            slot = g % n_slots
            for cp in chunk_copies(g):
                cp.wait()

            @pl.when(g + ahead < nv * n_chunks)
            def _():
                for cp in chunk_copies(g + ahead):
                    cp.start()

            hg = jnp.dot(u, wgbuf[slot].astype(BF16), preferred_element_type=F32)
            hu = jnp.dot(u, wubuf[slot].astype(BF16), preferred_element_type=F32)
            h = (_silu(hg) * hu).astype(BF16)
            part = jnp.dot(h, wdbuf[slot].astype(BF16), preferred_element_type=F32)
            if ci == 0:
                out_ref[...] = part
            else:
                out_ref[...] += part

    @pl.when(t >= nv)
    def _():
        out_ref[...] = jnp.zeros_like(out_ref)


def _moe_ffn(u_sorted, tile_expert, n_valid, wg, wu, wd):
    R, D = u_sorted.shape
    tm = MOE_TM
    n_slots = 3
    tile = lambda t, nv: jnp.minimum(t, nv[0] - 1)
    return pl.pallas_call(
        _moe_ffn_kernel,
        out_shape=jax.ShapeDtypeStruct((R, D), F32),
        grid_spec=pltpu.PrefetchScalarGridSpec(
            num_scalar_prefetch=2,
            grid=(R // tm,),
            in_specs=[pl.BlockSpec((tm, D), lambda t, te, nv: (tile(t, nv), 0)),
                      pl.BlockSpec(memory_space=pl.ANY),
                      pl.BlockSpec(memory_space=pl.ANY),
                      pl.BlockSpec(memory_space=pl.ANY)],
            out_specs=pl.BlockSpec((tm, D), lambda t, te, nv: (t, 0)),
            scratch_shapes=[pltpu.VMEM((n_slots, D, MOE_CW), F32), pltpu.VMEM((n_slots, D, MOE_CW), F32),
                            pltpu.VMEM((n_slots, MOE_CW, D), F32),
                            pltpu.SemaphoreType.DMA((n_slots,))]),
        compiler_params=_params(("arbitrary",), vmem=60 * 1024 * 1024),
        name="moe_ffn",
    )(tile_expert, n_valid, u_sorted, wg, wu, wd)


def _combine_kernel(pos_ref, y_ref, topw_ref, x_ref, g2_ref, lnw_ref, lnb_ref, out_ref, ybuf, sem):
    tm = x_ref.shape[0]
    m = pl.program_id(0)
    slot = m % 2

    def gather(step, buf):
        for k in range(2):
            _start_rows(tm, lambda base, j: _row_copy(y_ref, pos_ref[2 * (step * tm + base + j) + k],
                                                      ybuf.at[buf, k], base + j, sem.at[buf]))

    @pl.when(m == 0)
    def _():
        gather(0, 0)

    @pl.when(m + 1 < pl.num_programs(0))
    def _():
        gather(m + 1, 1 - slot)

    _wait_rows(2 * tm, _row_copy(y_ref, 0, ybuf.at[slot, 0], 0, sem.at[slot]))
    tw = topw_ref[...]
    y = tw[:, 0:1] * ybuf[slot, 0] + tw[:, 1:2] * ybuf[slot, 1]
    r = ALPHA * x_ref[...] + (1.0 + g2_ref[...]) * y
    out_ref[...] = _layer_norm_rows(r, lnw_ref[...], lnb_ref[...])


def _combine(y_sorted, pos, topw, x1, g2, ln_w, ln_b, seq):
    N, D = x1.shape
    tm = 512
    tpb = seq // tm
    row = lambda m, pos: (m, 0)
    const = lambda m, pos: (0, 0)
    return pl.pallas_call(
        _combine_kernel,
        out_shape=jax.ShapeDtypeStruct((N, D), F32),
        grid_spec=pltpu.PrefetchScalarGridSpec(
            num_scalar_prefetch=1,
            grid=(N // tm,),
            in_specs=[pl.BlockSpec(memory_space=pl.ANY),
                      pl.BlockSpec((tm, LANES), row),
                      pl.BlockSpec((tm, D), row),
                      pl.BlockSpec((None, 1, D), lambda m, pos: (m // tpb, 0, 0)),
                      pl.BlockSpec((1, D), const), pl.BlockSpec((1, D), const)],
            out_specs=pl.BlockSpec((tm, D), row),
            scratch_shapes=[pltpu.VMEM((2, 2, tm, D), F32), pltpu.SemaphoreType.DMA((2,))]),
        compiler_params=_params(("arbitrary",)),
        name="moe_combine_ln",
    )(pos, y_sorted, topw, x1, g2, ln_w.reshape(1, -1), ln_b.reshape(1, -1))


def _routing_plan(ids, n_tiles):
    e_flat = ids[:, :2].reshape(-1)
    onehot = (e_flat[:, None] == jnp.arange(N_EXPERTS, dtype=jnp.int32)[None, :]).astype(jnp.int32)
    csum = jnp.cumsum(onehot, axis=0)
    rank = jnp.sum((csum - onehot) * onehot, axis=1)
    counts = csum[-1]
    padded = ((counts + MOE_TM - 1) // MOE_TM) * MOE_TM
    ends = jnp.cumsum(padded)
    offs = ends - padded
    pos = (jnp.sum(onehot * offs[None, :], axis=1) + rank).astype(jnp.int32)
    tile_start = jnp.arange(n_tiles, dtype=jnp.int32) * MOE_TM
    n_valid = (ends[-1] // MOE_TM).astype(jnp.int32)
    tile_start = jnp.minimum(tile_start, (n_valid - 1) * MOE_TM)
    tile_expert = jnp.sum((tile_start[:, None] >= ends[None, :]).astype(jnp.int32), axis=1).astype(jnp.int32)
    total = jnp.full((1,), n_tiles * MOE_TM, jnp.int32)
    gaps = jnp.stack([jnp.concatenate([offs + counts, ends[-1:]]),
                      jnp.concatenate([ends, total])], axis=1).reshape(-1).astype(jnp.int32)
    return pos, tile_expert, n_valid.reshape(1), gaps


def _moe(u2, ids, topw, wg, wu, wd, x1, g2, ln_w, ln_b, seq):
    N = u2.shape[0]
    n_tiles = -(-2 * N // MOE_TM) + N_EXPERTS
    pos, tile_expert, n_valid, gaps = _routing_plan(ids, n_tiles)
    u_sorted = _dispatch(u2, pos, gaps, n_tiles * MOE_TM)
    y_sorted = _moe_ffn(u_sorted, tile_expert, n_valid, wg, wu, wd)
    return _combine(y_sorted, pos, topw, x1, g2, ln_w, ln_b, seq)


def _split_w_in_kernel(wt_ref, gt_ref, main_ref, gate_ref):
    main_ref[...] = wt_ref[0].T.astype(BF16)

    @pl.when(pl.program_id(1) == 0)
    def _():
        g = gt_ref[...]
        g = jnp.concatenate([g, jnp.zeros((LANES - g.shape[0], g.shape[1]), F32)], axis=0)
        gate_ref[...] = g.T.astype(BF16)


def _split_w_in(w_in):
    L, D, C = w_in.shape
    w_t = jnp.swapaxes(w_in, 1, 2)
    g0 = 4 * GROUP_WIDTH
    ng = 2 * N_HEADS
    tc = 1024
    src_col = lambda j: (j * (tc // ng) + jnp.where(j * tc >= g0, 1, 0)) * ng
    return pl.pallas_call(
        _split_w_in_kernel,
        out_shape=(jax.ShapeDtypeStruct((L, D, C - ng), BF16),
                   jax.ShapeDtypeStruct((L, D, LANES), BF16)),
        grid=(L, (C - ng) // tc),
        in_specs=[pl.BlockSpec((pl.Element(1), pl.Element(tc), pl.Element(D)),
                               lambda l, j: (l, src_col(j), 0)),
                  pl.BlockSpec((None, ng, D), lambda l, j: (l, g0 // ng, 0))],
        out_specs=(pl.BlockSpec((None, D, tc), lambda l, j: (l, 0, j)),
                   pl.BlockSpec((None, D, LANES), lambda l, j: (l, 0, 0))),
        compiler_params=_params(("parallel", "arbitrary")),
        name="split_w_in",
    )(w_t, w_t)


def kernel(x, c, w_in, mlstm_conv_w, mlstm_conv_b, mlstm_gate_b, mlstm_norm_w, pool_w, pool_scale, conv_dw_w, conv_dw_b, conv_ln_w, conv_ln_b, conv_pw_w, conv_pw_b, group_norm_w, w_out, ada_w, ada_b, ln1_w, ln1_b, ln2_w, ln2_b, ffn_w_gate, ffn_w_up, ffn_w_down, moe_router_w, moe_router_b, moe_w_gate, moe_w_up, moe_w_down):
    B, S, D = x.shape
    GW = GROUP_WIDTH
    ada = _ada_all(c, ada_w, ada_b)
    x2d = x.reshape(B * S, D)
    w_main, w_gate = _split_w_in(w_in)
    for l in range(DEPTH):
        mod = [ada[l, :, i * D:(i + 1) * D].reshape(B, 1, D) for i in range(6)]
        sh1, sc1, g1, sh2, sc2, g2 = mod
        proj, gates = _in_proj(x2d, sc1, sh1, w_main, w_gate, l, S)
        proj = proj.reshape(B, S, -1)
        gates = gates.reshape(B, S, LANES)
        gn_b, gn_c, gn_d = (group_norm_w[l, i * GW:(i + 1) * GW] for i in range(3))
        ya = _mlstm_mixer(proj, gates, mlstm_conv_w[l], mlstm_conv_b[l], mlstm_gate_b[l],
                          mlstm_norm_w[l], B, S)
        yb = _pool_mixer(proj, pool_w[l], pool_scale[l], gn_b, B, S)
        yc = _sb_mixer(proj, B, S)
        yd = _conv_mixer(proj, conv_dw_w[l], conv_dw_b[l], conv_ln_w[l], conv_ln_b[l],
                         conv_pw_w[l], conv_pw_b[l], gn_d, B, S)
        flat = lambda t: t.reshape(B * S, GW)
        j = l // 2
        router = None
        if l % 2 == 1:
            rw = jnp.zeros((D, LANES), F32).at[:, :N_EXPERTS].set(moe_router_w[j])
            rw_hi = rw.astype(BF16)
            rw_lo = (rw - rw_hi.astype(F32)).astype(BF16)
            rb = jnp.zeros((1, LANES), F32).at[0, :N_EXPERTS].set(moe_router_b[j])
            router = (jnp.concatenate([rw_hi, rw_lo], axis=1), rb)
        outs = _out_proj(flat(ya), flat(yb), flat(yc), flat(yd), gn_c, w_out[l].astype(BF16), x2d,
                         g1, ln1_w[l], ln1_b[l], sc2, sh2, S, router)
        if l % 2 == 0:
            x1, u2 = outs
            x2d = _ffn(u2, ffn_w_gate[j].astype(BF16), ffn_w_up[j].astype(BF16),
                       ffn_w_down[j].astype(BF16), x1, g2, ln2_w[l], ln2_b[l], S)
        else:
            x1, u2, ids, topw = outs
            x2d = _moe(u2, ids, topw, moe_w_gate[j], moe_w_up[j], moe_w_down[j], x1, g2,
                       ln2_w[l], ln2_b[l], S)
    return x2d.reshape(B, S, D)
```

```python
import functools

import jax
import jax.numpy as jnp
from jax import lax
from jax.experimental import pallas as pl
from jax.experimental.pallas import tpu as pltpu

F32 = jnp.float32
BF16 = jnp.bfloat16

DEPTH = 2
CHUNK = 256
GROUP_WIDTH = 512
N_HEADS = 4
HEAD_DIM = 128
MLSTM_CONV = 4
POOL_WINDOWS = (2, 4, 8, 16)
CONV_WIDTH = 31
SB_KEY_BLOCK = 256
N_EXPERTS = 8
ALPHA = (2.0 * DEPTH) ** 0.25
EPS = 1e-5
LOG2E = 1.4426950408889634
LANES = 128
VMEM_LIMIT = 56 * 1024 * 1024


def _params(sem, vmem=VMEM_LIMIT):
    return pltpu.CompilerParams(dimension_semantics=sem, vmem_limit_bytes=vmem)


def _silu(x):
    return x * (1.0 / (1.0 + jnp.exp(-x)))


def _sigmoid(x):
    return 1.0 / (1.0 + jnp.exp(-x))


def _neg_softplus(x):
    return -(jnp.maximum(x, 0.0) + jnp.log(1.0 + jnp.exp(-jnp.abs(x))))


def _layer_norm_rows(r, w, b):
    mu = jnp.mean(r, axis=-1, keepdims=True)
    d = r - mu
    var = jnp.mean(d * d, axis=-1, keepdims=True)
    return d * lax.rsqrt(var + EPS) * w + b


def _rms_rows(y, w):
    return y * lax.rsqrt(jnp.mean(y * y, axis=-1, keepdims=True) + EPS) * w


def _bdot(a, b):
    return jnp.dot(a.astype(BF16), b.astype(BF16), preferred_element_type=F32)


def _ada_kernel(c_ref, w_ref, b_ref, out_ref):
    out_ref[...] = _bdot(_silu(c_ref[...]), w_ref[...]) + b_ref[...]


def _ada_all(c, ada_w, ada_b):
    L, D, D6 = ada_w.shape
    Bn = c.shape[0]
    tn = 2048
    return pl.pallas_call(
        _ada_kernel,
        out_shape=jax.ShapeDtypeStruct((L, Bn, D6), F32),
        grid=(L, D6 // tn),
        in_specs=[pl.BlockSpec((Bn, D), lambda l, n: (0, 0)),
                  pl.BlockSpec((None, D, tn), lambda l, n: (l, 0, n)),
                  pl.BlockSpec((None, 1, tn), lambda l, n: (l, 0, n))],
        out_specs=pl.BlockSpec((None, Bn, tn), lambda l, n: (l, 0, n)),
        compiler_params=_params(("parallel", "parallel")),
        name="ada_mod",
    )(c, ada_w, ada_b.reshape(L, 1, D6))


def _inproj_kernel(x_ref, sc_ref, sh_ref, w_ref, wg_ref, proj_ref, gates_ref, u_ref):
    @pl.when(pl.program_id(1) == 0)
    def _():
        u = (x_ref[...] * (1.0 + sc_ref[...]) + sh_ref[...]).astype(BF16)
        u_ref[...] = u
        gates_ref[...] = jnp.dot(u, wg_ref[...], preferred_element_type=F32)

    proj_ref[...] = jnp.dot(u_ref[...], w_ref[...], preferred_element_type=F32)


def _in_proj(x2d, sc, sh, w_main, w_gate, layer, seq):
    N, D = x2d.shape
    NC = w_main.shape[2]
    tm, tn = 1024, 1280
    tpb = seq // tm
    return pl.pallas_call(
        _inproj_kernel,
        out_shape=(jax.ShapeDtypeStruct((N, NC), F32), jax.ShapeDtypeStruct((N, LANES), F32)),
        grid=(N // tm, NC // tn),
        in_specs=[pl.BlockSpec((tm, D), lambda m, n: (m, 0)),
                  pl.BlockSpec((None, 1, D), lambda m, n: (m // tpb, 0, 0)),
                  pl.BlockSpec((None, 1, D), lambda m, n: (m // tpb, 0, 0)),
                  pl.BlockSpec((None, D, tn), lambda m, n: (layer, 0, n)),
                  pl.BlockSpec((None, D, LANES), lambda m, n: (layer, 0, 0))],
        out_specs=(pl.BlockSpec((tm, tn), lambda m, n: (m, n)),
                   pl.BlockSpec((tm, LANES), lambda m, n: (m, 0))),
        scratch_shapes=[pltpu.VMEM((tm, D), BF16)],
        compiler_params=_params(("parallel", "arbitrary")),
        name="in_proj",
    )(x2d, sc, sh, w_main, w_gate)


def _cumsum_rows(x):
    n = x.shape[0]
    row = lax.broadcasted_iota(jnp.int32, x.shape, 0)
    k = 1
    while k < n:
        x = x + jnp.where(row >= k, pltpu.roll(x, k, axis=0), 0.0)
        k *= 2
    return x


def _mlstm_kernel(qk_ref, v_ref, o_ref, g_ref, cw_ref, cb_ref, gb_ref, nw_ref, out_ref,
                  c_state, n_state, m_state, tail_ref):
    @pl.when(pl.program_id(1) == 0)
    def _():
        c_state[...] = jnp.zeros_like(c_state)
        n_state[...] = jnp.zeros_like(n_state)
        m_state[...] = jnp.zeros_like(m_state)
        tail_ref[...] = jnp.zeros_like(tail_ref)

    for bb in range(qk_ref.shape[0]):
        _mlstm_block(qk_ref.at[bb], v_ref.at[bb], o_ref.at[bb], g_ref.at[bb], cw_ref, cb_ref, gb_ref,
                     nw_ref, out_ref.at[bb], c_state.at[bb], n_state.at[bb], m_state.at[bb],
                     tail_ref.at[bb])


def _mlstm_block(qk_ref, v_ref, o_ref, g_ref, cw_ref, cb_ref, gb_ref, nw_ref, out_ref,
                 c_state, n_state, m_state, tail_ref):
    T = qk_ref.shape[0]
    GW = GROUP_WIDTH

    xx = jnp.concatenate([tail_ref[...], qk_ref[...]], axis=0)
    tail_ref[...] = qk_ref[T - 8:T, :]
    conv = cb_ref[...]
    for j in range(MLSTM_CONV):
        off = 8 - (MLSTM_CONV - 1) + j
        tap = xx[off:off + T, :] if off % 8 == 0 else pltpu.roll(xx, T + 8 - off, axis=0)[0:T, :]
        conv = conv + cw_ref[j:j + 1, :] * tap
    qk = _silu(conv)

    gates = g_ref[...] + gb_ref[...]
    logf = _neg_softplus(-gates)
    causal = (lax.broadcasted_iota(jnp.int32, (CHUNK, CHUNK), 1)
              <= lax.broadcasted_iota(jnp.int32, (CHUNK, CHUNK), 0))

    for c in range(T // CHUNK):
        rows = slice(c * CHUNK, (c + 1) * CHUNK)
        gi = gates[rows, :]
        bcum = _cumsum_rows(logf[rows, :])
        gi_t = gi.T
        bcum_t = bcum.T
        for h in range(N_HEADS):
            cols = slice(h * HEAD_DIM, (h + 1) * HEAD_DIM)
            q = qk[rows, cols] * (HEAD_DIM ** -0.5)
            k = qk[rows, GW + h * HEAD_DIM:GW + (h + 1) * HEAD_DIM]
            v = v_ref[rows, cols]
            b_col = bcum[:, N_HEADS + h:N_HEADS + h + 1]
            b_row = bcum_t[N_HEADS + h:N_HEADS + h + 1, :]
            i_col = gi[:, h:h + 1]
            i_row = gi_t[h:h + 1, :]
            g_tot = b_col[CHUNK - 1:CHUNK, :]
            m_prev = m_state[h:h + 1, 0:1]
            c_prev = c_state[h]
            n_prev = n_state[h:h + 1, :]

            log_d = jnp.where(causal, b_col - b_row + i_row, -jnp.inf)
            m_inter = b_col + m_prev
            m_t = jnp.maximum(m_inter, jnp.max(log_d, axis=-1, keepdims=True))
            s = lax.dot_general(q.astype(BF16), k.astype(BF16), (((1,), (1,)), ((), ())),
                                preferred_element_type=F32)
            w = jnp.exp(log_d - m_t) * s
            inter = jnp.exp(m_inter - m_t)
            num = inter * _bdot(q, c_prev) + _bdot(w, v)
            den = inter * jnp.sum(q * n_prev, axis=-1, keepdims=True) + jnp.sum(w, axis=-1, keepdims=True)
            hcur = num / jnp.maximum(jnp.abs(den), jnp.exp(-m_t))

            a_col = g_tot - b_col + i_col
            m_new = jnp.maximum(g_tot + m_prev, jnp.max(a_col, axis=0, keepdims=True))
            decay = jnp.exp(g_tot + m_prev - m_new)
            kw = k * jnp.exp(a_col - m_new)
            c_state[h] = decay * c_prev + _bdot(kw.T, v)
            n_state[h:h + 1, :] = decay * n_prev + jnp.sum(kw, axis=0, keepdims=True)
            m_state[h:h + 1, :] = jnp.broadcast_to(m_new, (1, LANES))

            mu = jnp.mean(hcur, axis=-1, keepdims=True)
            dlt = hcur - mu
            var = jnp.mean(dlt * dlt, axis=-1, keepdims=True)
            hn = dlt * lax.rsqrt(var + EPS) * nw_ref[:, cols]
            out_ref[rows, cols] = (_sigmoid(o_ref[rows, cols]) * hn).astype(out_ref.dtype)


def _mlstm_mixer(proj, gates, conv_w, conv_b, gate_b, norm_w, B, S):
    T = 256
    NB = 2
    GW = GROUP_WIDTH
    gb = jnp.zeros((1, LANES), F32).at[0, :2 * N_HEADS].set(gate_b)
    return pl.pallas_call(
        _mlstm_kernel,
        out_shape=jax.ShapeDtypeStruct((B, S, GW), BF16),
        grid=(B // NB, S // T),
        in_specs=[pl.BlockSpec((NB, T, 2 * GW), lambda b, t: (b, t, 0)),
                  pl.BlockSpec((NB, T, GW), lambda b, t: (b, t, 2)),
                  pl.BlockSpec((NB, T, GW), lambda b, t: (b, t, 3)),
                  pl.BlockSpec((NB, T, LANES), lambda b, t: (b, t, 0)),
                  pl.BlockSpec((MLSTM_CONV, 2 * GW), lambda b, t: (0, 0)),
                  pl.BlockSpec((1, 2 * GW), lambda b, t: (0, 0)),
                  pl.BlockSpec((1, LANES), lambda b, t: (0, 0)),
                  pl.BlockSpec((1, GW), lambda b, t: (0, 0))],
        out_specs=pl.BlockSpec((NB, T, GW), lambda b, t: (b, t, 0)),
        scratch_shapes=[pltpu.VMEM((NB, N_HEADS, HEAD_DIM, HEAD_DIM), F32),
                        pltpu.VMEM((NB, 8, LANES), F32),
                        pltpu.VMEM((NB, 8, LANES), F32),
                        pltpu.VMEM((NB, 8, 2 * GW), F32)],
        compiler_params=_params(("parallel", "arbitrary")),
        name="mlstm_mixer",
    )(proj, proj, proj, gates, conv_w, conv_b.reshape(1, -1), gb, norm_w.reshape(1, -1))


def _pool_kernel(x_ref, w_ref, scale_ref, gn_ref, out_ref, tail_ref):
    T = x_ref.shape[0]
    HALO = 16
    t_blk = pl.program_id(1)

    @pl.when(t_blk == 0)
    def _():
        tail_ref[...] = jnp.zeros_like(tail_ref)

    x = x_ref[...]
    xx = jnp.concatenate([tail_ref[...], x], axis=0)
    tail_ref[...] = x[T - HALO:T, :]
    pos = t_blk * T + lax.broadcasted_iota(jnp.int32, (T, 1), 0) + 1
    outs = []
    for g, win in enumerate(POOL_WINDOWS):
        cols = slice(g * LANES, (g + 1) * LANES)
        s = xx[:, cols]
        k = 1
        while k < win:
            s = s + pltpu.roll(s, k, axis=0)
            k *= 2
        cnt = jnp.minimum(pos, win).astype(F32)
        yg = s[HALO:, :] / cnt - x[:, cols]
        outs.append(_bdot(yg, w_ref[g]))
    y = jnp.concatenate(outs, axis=-1) * scale_ref[...]
    out_ref[...] = _rms_rows(y, gn_ref[...]).astype(out_ref.dtype)


def _pool_mixer(proj, pool_w, pool_scale, gn, B, S):
    T = 512
    GW = GROUP_WIDTH
    return pl.pallas_call(
        _pool_kernel,
        out_shape=jax.ShapeDtypeStruct((B, S, GW), BF16),
        grid=(B, S // T),
        in_specs=[pl.BlockSpec((None, T, GW), lambda b, t: (b, t, 4)),
                  pl.BlockSpec((len(POOL_WINDOWS), LANES, LANES), lambda b, t: (0, 0, 0)),
                  pl.BlockSpec((1, GW), lambda b, t: (0, 0)),
                  pl.BlockSpec((1, GW), lambda b, t: (0, 0))],
        out_specs=pl.BlockSpec((None, T, GW), lambda b, t: (b, t, 0)),
        scratch_shapes=[pltpu.VMEM((16, GW), F32)],
        compiler_params=_params(("parallel", "arbitrary")),
        name="pool_mixer",
    )(proj, pool_w.astype(BF16), pool_scale.reshape(1, -1), gn.reshape(1, -1))


def _sb_kernel(q_ref, k_ref, v_ref, out_ref, kb_ref, vb_ref):
    LQ = q_ref.shape[0]
    LK = SB_KEY_BLOCK
    ratio = LQ // LK
    qi = pl.program_id(1)

    @pl.when(qi == 0)
    def _():
        kb_ref[...] = k_ref[...].astype(BF16)
        vb_ref[...] = v_ref[...].astype(BF16)

    scale2 = HEAD_DIM ** -0.5 * LOG2E
    tri = (lax.broadcasted_iota(jnp.int32, (LK, LK), 0)
           > lax.broadcasted_iota(jnp.int32, (LK, LK), 1)).astype(BF16)
    row = lax.broadcasted_iota(jnp.int32, (LQ, LK), 0)
    col = lax.broadcasted_iota(jnp.int32, (LQ, LK), 1)
    heads = [slice(h * HEAD_DIM, (h + 1) * HEAD_DIM) for h in range(N_HEADS)]
    qs = [(q_ref[:, hs] * scale2).astype(BF16) for hs in heads]

    def sweep(kb, carry, strict):
        start = pl.multiple_of(kb * LK, LK)
        zs, lks = [], []
        for h, hs in enumerate(heads):
            k_blk = kb_ref[pl.ds(start, LK), hs]
            z = lax.dot_general(qs[h], k_blk, (((1,), (1,)), ((), ())), preferred_element_type=F32)
            lk = -(jnp.maximum(z, 0.0) + jnp.log2(1.0 + jnp.exp2(-jnp.abs(z))))
            if strict is not None:
                lk = jnp.where(strict, lk, 0.0)
            zs.append(z)
            lks.append(lk)
        inner = jnp.dot(jnp.concatenate([lk.astype(BF16) for lk in lks], axis=0), tri,
                        preferred_element_type=F32)
        new = []
        for h, hs in enumerate(heads):
            after, acc = carry[h]
            v_blk = vb_ref[pl.ds(start, LK), hs]
            a = jnp.exp2(zs[h] + lks[h] + inner[h * LQ:(h + 1) * LQ, :] + after)
            if strict is not None:
                a = jnp.where(strict, a, 0.0)
            acc = acc + jnp.dot(a.astype(BF16), v_blk, preferred_element_type=F32)
            after = after + jnp.sum(lks[h], axis=-1, keepdims=True)
            new.append((after, acc))
        return tuple(new)

    init = tuple((jnp.zeros((LQ, 1), F32), jnp.zeros((LQ, HEAD_DIM), F32)) for _ in heads)
    carry = init
    for j in reversed(range(ratio)):
        carry = sweep(ratio * qi + j, carry, (j * LK + col) < row)
    def sweep_group(it, c):
        for j in range(ratio):
            c = sweep(ratio * (qi - it) - 1 - j, c, None)
        return c

    carry = lax.fori_loop(0, qi, sweep_group, carry)
    for h, hs in enumerate(heads):
        out_ref[:, hs] = carry[h][1]


def _sb_mixer(proj, B, S):
    LQ = 512
    GW = GROUP_WIDTH
    return pl.pallas_call(
        _sb_kernel,
        out_shape=jax.ShapeDtypeStruct((B, S, GW), F32),
        grid=(B, S // LQ),
        in_specs=[pl.BlockSpec((None, LQ, GW), lambda b, i: (b, i, 5)),
                  pl.BlockSpec((None, S, GW), lambda b, i: (b, 0, 6)),
                  pl.BlockSpec((None, S, GW), lambda b, i: (b, 0, 7))],
        out_specs=pl.BlockSpec((None, LQ, GW), lambda b, i: (b, i, 0)),
        scratch_shapes=[pltpu.VMEM((S, GW), BF16), pltpu.VMEM((S, GW), BF16)],
        compiler_params=_params(("parallel", "arbitrary")),
        name="sb_attention",
    )(proj, proj, proj)


def _conv_kernel(x_ref, dw_ref, dwb_ref, lnw_ref, lnb_ref, pw_ref, pwb_ref, gn_ref, out_ref,
                 hbuf, ybuf):
    T = x_ref.shape[0]
    GW = GROUP_WIDTH
    HALO = 32
    SUB = 64

    @pl.when(pl.program_id(1) == 0)
    def _():
        hbuf[0:HALO, :] = jnp.zeros((HALO, GW), F32)

    @pl.when(pl.program_id(1) > 0)
    def _():
        hbuf[0:HALO, :] = hbuf[T:T + HALO, :]

    hbuf[HALO:HALO + T, :] = x_ref[:, 0:GW] * _sigmoid(x_ref[:, GW:2 * GW])

    def sub(i, carry):
        r0 = pl.multiple_of(i * SUB, SUB)
        acc = jnp.broadcast_to(dwb_ref[...], (SUB, GW))
        win = hbuf[pl.ds(r0, SUB + HALO), :]
        for res in range(8):
            shifted = win if res == 0 else pltpu.roll(win, SUB + HALO - res, axis=0)
            for j in range(CONV_WIDTH):
                off = HALO - CONV_WIDTH + 1 + j
                if off % 8 == res:
                    base = off - res
                    acc = acc + dw_ref[j:j + 1, :] * shifted[base:base + SUB, :]
        ybuf[pl.ds(r0, SUB), :] = acc
        return carry

    lax.fori_loop(0, T // SUB, sub, 0)
    hn = _silu(_layer_norm_rows(ybuf[...], lnw_ref[...], lnb_ref[...]))
    y = _bdot(hn, pw_ref[...]) + pwb_ref[...]
    out_ref[...] = _rms_rows(y, gn_ref[...]).astype(out_ref.dtype)


def _conv_mixer(proj, dw_w, dw_b, ln_w, ln_b, pw_w, pw_b, gn, B, S):
    T = 512
    GW = GROUP_WIDTH
    dw_pad = jnp.zeros((32, GW), F32).at[:CONV_WIDTH].set(dw_w)
    r = lambda a: a.reshape(1, -1)
    return pl.pallas_call(
        _conv_kernel,
        out_shape=jax.ShapeDtypeStruct((B, S, GW), BF16),
        grid=(B, S // T),
        in_specs=[pl.BlockSpec((None, T, 2 * GW), lambda b, t: (b, t, 4)),
                  pl.BlockSpec((32, GW), lambda b, t: (0, 0)),
                  pl.BlockSpec((1, GW), lambda b, t: (0, 0)),
                  pl.BlockSpec((1, GW), lambda b, t: (0, 0)),
                  pl.BlockSpec((1, GW), lambda b, t: (0, 0)),
                  pl.BlockSpec((GW, GW), lambda b, t: (0, 0)),
                  pl.BlockSpec((1, GW), lambda b, t: (0, 0)),
                  pl.BlockSpec((1, GW), lambda b, t: (0, 0))],
        out_specs=pl.BlockSpec((None, T, GW), lambda b, t: (b, t, 0)),
        scratch_shapes=[pltpu.VMEM((T + 32, GW), F32), pltpu.VMEM((T, GW), F32)],
        compiler_params=_params(("parallel", "arbitrary")),
        name="conv_mixer",
    )(proj, dw_pad, r(dw_b), r(ln_w), r(ln_b), pw_w.astype(BF16), r(pw_b), r(gn))


def _outproj_kernel(with_router, ya_ref, yb_ref, yc_ref, yd_ref, gnc_ref, w_ref, x_ref, g1_ref,
                    lnw_ref, lnb_ref, sc_ref, sh_ref, *rest):
    GW = GROUP_WIDTH
    if with_router:
        rw_ref, rb_ref, x1_ref, u2_ref, ids_ref, topw_ref = rest
    else:
        x1_ref, u2_ref = rest
    yc = _rms_rows(yc_ref[...], gnc_ref[...]).astype(BF16)
    acc = jnp.dot(ya_ref[...], w_ref[0:GW, :], preferred_element_type=F32)
    acc = acc + jnp.dot(yb_ref[...], w_ref[GW:2 * GW, :], preferred_element_type=F32)
    acc = acc + jnp.dot(yc, w_ref[2 * GW:3 * GW, :], preferred_element_type=F32)
    acc = acc + jnp.dot(yd_ref[...], w_ref[3 * GW:4 * GW, :], preferred_element_type=F32)
    r = ALPHA * x_ref[...] + (1.0 + g1_ref[...]) * acc
    x1 = _layer_norm_rows(r, lnw_ref[...], lnb_ref[...])
    x1_ref[...] = x1
    u2 = x1 * (1.0 + sc_ref[...]) + sh_ref[...]
    u2_ref[...] = u2.astype(u2_ref.dtype)
    if with_router:
        u_hi = u2.astype(BF16)
        u_lo = (u2 - u_hi.astype(F32)).astype(BF16)
        p = jnp.dot(u_hi, rw_ref[...], preferred_element_type=F32)
        logits = (p[:, :LANES] + p[:, LANES:]
                  + jnp.dot(u_lo, rw_ref[:, :LANES], preferred_element_type=F32) + rb_ref[...])
        lane = lax.broadcasted_iota(jnp.int32, logits.shape, 1)
        lg = jnp.where(lane < N_EXPERTS, logits, -jnp.inf)
        m1 = jnp.max(lg, axis=-1, keepdims=True)
        i1 = jnp.min(jnp.where(lg == m1, lane, LANES), axis=-1, keepdims=True)
        lg2 = jnp.where(lane == i1, -jnp.inf, lg)
        m2 = jnp.max(lg2, axis=-1, keepdims=True)
        i2 = jnp.min(jnp.where(lg2 == m2, lane, LANES), axis=-1, keepdims=True)
        e2 = jnp.exp(m2 - m1)
        w1 = 1.0 / (1.0 + e2)
        w2 = e2 / (1.0 + e2)
        ids_ref[...] = jnp.where(lane == 0, i1, jnp.where(lane == 1, i2, 0))
        topw_ref[...] = jnp.where(lane == 0, w1, jnp.where(lane == 1, w2, 0.0))


def _out_proj(ya, yb, yc, yd, gnc, w_out, x2d, g1, ln_w, ln_b, sc2, sh2, seq, router=None):
    N, D = x2d.shape
    GW = GROUP_WIDTH
    tm = 512
    tpb = seq // tm
    row = lambda m: (m, 0)
    const = lambda m: (0, 0)
    perb = lambda m: (m // tpb, 0, 0)
    in_specs = [pl.BlockSpec((tm, GW), row), pl.BlockSpec((tm, GW), row),
                pl.BlockSpec((tm, GW), row), pl.BlockSpec((tm, GW), row),
                pl.BlockSpec((1, GW), const),
                pl.BlockSpec((D, D), const),
                pl.BlockSpec((tm, D), row),
                pl.BlockSpec((None, 1, D), perb),
                pl.BlockSpec((1, D), const), pl.BlockSpec((1, D), const),
                pl.BlockSpec((None, 1, D), perb), pl.BlockSpec((None, 1, D), perb)]
    args = [ya, yb, yc, yd, gnc.reshape(1, -1), w_out, x2d, g1, ln_w.reshape(1, -1),
            ln_b.reshape(1, -1), sc2, sh2]
    out_shape = [jax.ShapeDtypeStruct((N, D), F32),
                 jax.ShapeDtypeStruct((N, D), BF16 if router is None else F32)]
    out_specs = [pl.BlockSpec((tm, D), row), pl.BlockSpec((tm, D), row)]
    if router is not None:
        rw, rb = router
        in_specs += [pl.BlockSpec((D, 2 * LANES), const), pl.BlockSpec((1, LANES), const)]
        args += [rw, rb]
        out_shape += [jax.ShapeDtypeStruct((N, LANES), jnp.int32), jax.ShapeDtypeStruct((N, LANES), F32)]
        out_specs += [pl.BlockSpec((tm, LANES), row), pl.BlockSpec((tm, LANES), row)]
    return pl.pallas_call(
        functools.partial(_outproj_kernel, router is not None),
        out_shape=tuple(out_shape),
        grid=(N // tm,),
        in_specs=in_specs,
        out_specs=tuple(out_specs),
        compiler_params=_params(("parallel",)),
        name="out_proj_ln",
    )(*args)


FFN_CW = 512


def _ffn_kernel(u_ref, wg_hbm, wu_hbm, wd_hbm, x_ref, g2_ref, lnw_ref, lnb_ref, out_ref,
                acc_ref, wgbuf, wubuf, wdbuf, sem):
    m = pl.program_id(0)
    n_chunks = wg_hbm.shape[1] // FFN_CW
    total = pl.num_programs(0) * n_chunks
    n_slots = wgbuf.shape[0]
    ahead = n_slots - 1

    def chunk_copies(g):
        slot = g % n_slots
        cols = pl.ds(pl.multiple_of((g % n_chunks) * FFN_CW, FFN_CW), FFN_CW)
        return (pltpu.make_async_copy(wg_hbm.at[:, cols], wgbuf.at[slot], sem.at[slot]),
                pltpu.make_async_copy(wu_hbm.at[:, cols], wubuf.at[slot], sem.at[slot]),
                pltpu.make_async_copy(wd_hbm.at[cols, :], wdbuf.at[slot], sem.at[slot]))

    @pl.when(m == 0)
    def _():
        for g in range(ahead):
            for cp in chunk_copies(g):
                cp.start()

    u = u_ref[...]
    for ci in range(n_chunks):
        g = m * n_chunks + ci
        slot = g % n_slots
        for cp in chunk_copies(g):
            cp.wait()

        @pl.when(g + ahead < total)
        def _():
            for cp in chunk_copies(g + ahead):
                cp.start()

        hg = jnp.dot(u, wgbuf[slot], preferred_element_type=F32)
        hu = jnp.dot(u, wubuf[slot], preferred_element_type=F32)
        h = (_silu(hg) * hu).astype(BF16)
        part = jnp.dot(h, wdbuf[slot], preferred_element_type=F32)
        if ci == 0:
            acc_ref[...] = part
        else:
            acc_ref[...] += part

    r = ALPHA * x_ref[...] + (1.0 + g2_ref[...]) * acc_ref[...]
    out_ref[...] = _layer_norm_rows(r, lnw_ref[...], lnb_ref[...])


def _ffn(u2, wg, wu, wd, x1, g2, ln_w, ln_b, seq):
    N, D = u2.shape
    tm = 512
    n_slots = 3
    tpb = seq // tm
    row = lambda m: (m, 0)
    const = lambda m: (0, 0)
    hbm = pl.BlockSpec(memory_space=pl.ANY)
    return pl.pallas_call(
        _ffn_kernel,
        out_shape=jax.ShapeDtypeStruct((N, D), F32),
        grid=(N // tm,),
        in_specs=[pl.BlockSpec((tm, D), row), hbm, hbm, hbm,
                  pl.BlockSpec((tm, D), row),
                  pl.BlockSpec((None, 1, D), lambda m: (m // tpb, 0, 0)),
                  pl.BlockSpec((1, D), const), pl.BlockSpec((1, D), const)],
        out_specs=pl.BlockSpec((tm, D), row),
        scratch_shapes=[pltpu.VMEM((tm, D), F32),
                        pltpu.VMEM((n_slots, D, FFN_CW), BF16), pltpu.VMEM((n_slots, D, FFN_CW), BF16),
                        pltpu.VMEM((n_slots, FFN_CW, D), BF16),
                        pltpu.SemaphoreType.DMA((n_slots,))],
        compiler_params=_params(("arbitrary",)),
        name="ffn_ln",
    )(u2, wg, wu, wd, x1, g2, ln_w.reshape(1, -1), ln_b.reshape(1, -1))


MOE_TM = 768
MOE_CW = 256


def _row_copy(src, src_row, dst, dst_row, sem):
    return pltpu.make_async_copy(src.at[pl.ds(src_row, 1), :], dst.at[pl.ds(dst_row, 1), :], sem)


def _start_rows(n_rows, make_copy):
    def body(i, carry):
        base = pl.multiple_of(i * 8, 8)
        for j in range(8):
            make_copy(base, j).start(priority=j % 2)
        return carry
    lax.fori_loop(0, n_rows // 8, body, 0)


def _wait_rows(n_rows, one_copy):
    def body(i, carry):
        one_copy.wait()
        return carry
    lax.fori_loop(0, n_rows, body, 0, unroll=8)


def _dispatch_kernel(pos_ref, gap_ref, u_ref, out_ref, zbuf, sem, zsem):
    tm = u_ref.shape[0]
    first = pl.program_id(0) * tm

    def start(r, carry):
        for k in range(2):
            _row_copy(u_ref, r, out_ref, pos_ref[2 * (first + r) + k], sem).start(priority=k)
        return carry

    lax.fori_loop(0, tm, start, 0, unroll=8)

    @pl.when(pl.program_id(0) == pl.num_programs(0) - 1)
    def _():
        zbuf[...] = jnp.zeros_like(zbuf)
        n_gaps = gap_ref.shape[0] // 2

        def block_copy(i):
            return pltpu.make_async_copy(zbuf, out_ref.at[pl.ds(pl.multiple_of(i * 8, 8), 8), :], zsem.at[1])

        for g in range(n_gaps):
            lo, hi = gap_ref[2 * g], gap_ref[2 * g + 1]
            mid = jnp.minimum((lo + 7) // 8 * 8, hi)
            lax.fori_loop(lo, mid, lambda r, c: (_row_copy(zbuf, 0, out_ref, r, zsem.at[0]).start(), c)[1], 0)
            lax.fori_loop(mid // 8, hi // 8, lambda i, c: (block_copy(i).start(), c)[1], 0)
        for g in range(n_gaps):
            lo, hi = gap_ref[2 * g], gap_ref[2 * g + 1]
            mid = jnp.minimum((lo + 7) // 8 * 8, hi)
            lax.fori_loop(lo, mid, lambda r, c: (_row_copy(zbuf, 0, out_ref, 0, zsem.at[0]).wait(), c)[1], 0)
            lax.fori_loop(mid // 8, hi // 8, lambda i, c: (block_copy(0).wait(), c)[1], 0)

    _wait_rows(2 * tm, _row_copy(u_ref, 0, out_ref, 0, sem))


def _dispatch(u2, pos, gaps, n_rows):
    N, D = u2.shape
    tm = 512
    return pl.pallas_call(
        _dispatch_kernel,
        out_shape=jax.ShapeDtypeStruct((n_rows, D), u2.dtype),
        grid_spec=pltpu.PrefetchScalarGridSpec(
            num_scalar_prefetch=2,
            grid=(N // tm,),
            in_specs=[pl.BlockSpec((tm, D), lambda m, pos, gaps: (m, 0))],
            out_specs=pl.BlockSpec(memory_space=pl.ANY),
            scratch_shapes=[pltpu.VMEM((8, D), u2.dtype), pltpu.SemaphoreType.DMA(()),
                            pltpu.SemaphoreType.DMA((2,))]),
        compiler_params=_params(("arbitrary",)),
        name="moe_dispatch",
    )(pos, gaps, u2)


def _moe_ffn_kernel(te_ref, nv_ref, u_ref, wg_hbm, wu_hbm, wd_hbm, out_ref, wgbuf, wubuf, wdbuf, sem):
    t = pl.program_id(0)
    nv = nv_ref[0]
    F = wg_hbm.shape[2]
    n_chunks = F // MOE_CW
    n_slots = wgbuf.shape[0]
    ahead = n_slots - 1

    def chunk_copies(g):
        tile, ci = g // n_chunks, g % n_chunks
        e = te_ref[tile]
        slot = g % n_slots
        cols = pl.ds(pl.multiple_of(ci * MOE_CW, MOE_CW), MOE_CW)
        return (pltpu.make_async_copy(wg_hbm.at[e, :, cols], wgbuf.at[slot], sem.at[slot]),
                pltpu.make_async_copy(wu_hbm.at[e, :, cols], wubuf.at[slot], sem.at[slot]),
                pltpu.make_async_copy(wd_hbm.at[e, cols, :], wdbuf.at[slot], sem.at[slot]))

    @pl.when(t == 0)
    def _():
        for g in range(ahead):
            for cp in chunk_copies(g):
                cp.start()

    @pl.when(t < nv)
    def _():
        u = u_ref[...].astype(BF16)
        for ci in range(n_chunks):
            g = t * n_chunks + ci
            slot = g % n_slots
            for cp in chunk_copies(g):
                cp.wait()

            @pl.when(g + ahead < nv * n_chunks)
            def _():
                for cp in chunk_copies(g + ahead):
                    cp.start()

            hg = jnp.dot(u, wgbuf[slot].astype(BF16), preferred_element_type=F32)
            hu = jnp.dot(u, wubuf[slot].astype(BF16), preferred_element_type=F32)
            h = (_silu(hg) * hu).astype(BF16)
            part = jnp.dot(h, wdbuf[slot].astype(BF16), preferred_element_type=F32)
            if ci == 0:
                out_ref[...] = part
            else:
                out_ref[...] += part

    @pl.when(t >= nv)
    def _():
        out_ref[...] = jnp.zeros_like(out_ref)


def _moe_ffn(u_sorted, tile_expert, n_valid, wg, wu, wd):
    R, D = u_sorted.shape
    tm = MOE_TM
    n_slots = 3
    tile = lambda t, nv: jnp.minimum(t, nv[0] - 1)
    return pl.pallas_call(
        _moe_ffn_kernel,
        out_shape=jax.ShapeDtypeStruct((R, D), F32),
        grid_spec=pltpu.PrefetchScalarGridSpec(
            num_scalar_prefetch=2,
            grid=(R // tm,),
            in_specs=[pl.BlockSpec((tm, D), lambda t, te, nv: (tile(t, nv), 0)),
                      pl.BlockSpec(memory_space=pl.ANY),
                      pl.BlockSpec(memory_space=pl.ANY),
                      pl.BlockSpec(memory_space=pl.ANY)],
            out_specs=pl.BlockSpec((tm, D), lambda t, te, nv: (t, 0)),
            scratch_shapes=[pltpu.VMEM((n_slots, D, MOE_CW), F32), pltpu.VMEM((n_slots, D, MOE_CW), F32),
                            pltpu.VMEM((n_slots, MOE_CW, D), F32),
                            pltpu.SemaphoreType.DMA((n_slots,))]),
        compiler_params=_params(("arbitrary",), vmem=60 * 1024 * 1024),
        name="moe_ffn",
    )(tile_expert, n_valid, u_sorted, wg, wu, wd)


def _combine_kernel(pos_ref, y_ref, topw_ref, x_ref, g2_ref, lnw_ref, lnb_ref, out_ref, ybuf, sem):
    tm = x_ref.shape[0]
    m = pl.program_id(0)
    slot = m % 2

    def gather(step, buf):
        for k in range(2):
            _start_rows(tm, lambda base, j: _row_copy(y_ref, pos_ref[2 * (step * tm + base + j) + k],
                                                      ybuf.at[buf, k], base + j, sem.at[buf]))

    @pl.when(m == 0)
    def _():
        gather(0, 0)

    @pl.when(m + 1 < pl.num_programs(0))
    def _():
        gather(m + 1, 1 - slot)

    _wait_rows(2 * tm, _row_copy(y_ref, 0, ybuf.at[slot, 0], 0, sem.at[slot]))
    tw = topw_ref[...]
    y = tw[:, 0:1] * ybuf[slot, 0] + tw[:, 1:2] * ybuf[slot, 1]
    r = ALPHA * x_ref[...] + (1.0 + g2_ref[...]) * y
    out_ref[...] = _layer_norm_rows(r, lnw_ref[...], lnb_ref[...])


def _combine(y_sorted, pos, topw, x1, g2, ln_w, ln_b, seq):
    N, D = x1.shape
    tm = 512
    tpb = seq // tm
    row = lambda m, pos: (m, 0)
    const = lambda m, pos: (0, 0)
    return pl.pallas_call(
        _combine_kernel,
        out_shape=jax.ShapeDtypeStruct((N, D), F32),
        grid_spec=pltpu.PrefetchScalarGridSpec(
            num_scalar_prefetch=1,
            grid=(N // tm,),
            in_specs=[pl.BlockSpec(memory_space=pl.ANY),
                      pl.BlockSpec((tm, LANES), row),
                      pl.BlockSpec((tm, D), row),
                      pl.BlockSpec((None, 1, D), lambda m, pos: (m // tpb, 0, 0)),
                      pl.BlockSpec((1, D), const), pl.BlockSpec((1, D), const)],
            out_specs=pl.BlockSpec((tm, D), row),
            scratch_shapes=[pltpu.VMEM((2, 2, tm, D), F32), pltpu.SemaphoreType.DMA((2,))]),
        compiler_params=_params(("arbitrary",)),
        name="moe_combine_ln",
    )(pos, y_sorted, topw, x1, g2, ln_w.reshape(1, -1), ln_b.reshape(1, -1))


def _routing_plan(ids, n_tiles):
    e_flat = ids[:, :2].reshape(-1)
    onehot = (e_flat[:, None] == jnp.arange(N_EXPERTS, dtype=jnp.int32)[None, :]).astype(jnp.int32)
    csum = jnp.cumsum(onehot, axis=0)
    rank = jnp.sum((csum - onehot) * onehot, axis=1)
    counts = csum[-1]
    padded = ((counts + MOE_TM - 1) // MOE_TM) * MOE_TM
    ends = jnp.cumsum(padded)
    offs = ends - padded
    pos = (jnp.sum(onehot * offs[None, :], axis=1) + rank).astype(jnp.int32)
    tile_start = jnp.arange(n_tiles, dtype=jnp.int32) * MOE_TM
    n_valid = (ends[-1] // MOE_TM).astype(jnp.int32)
    tile_start = jnp.minimum(tile_start, (n_valid - 1) * MOE_TM)
    tile_expert = jnp.sum((tile_start[:, None] >= ends[None, :]).astype(jnp.int32), axis=1).astype(jnp.int32)
    total = jnp.full((1,), n_tiles * MOE_TM, jnp.int32)
    gaps = jnp.stack([jnp.concatenate([offs + counts, ends[-1:]]),
                      jnp.concatenate([ends, total])], axis=1).reshape(-1).astype(jnp.int32)
    return pos, tile_expert, n_valid.reshape(1), gaps


def _moe(u2, ids, topw, wg, wu, wd, x1, g2, ln_w, ln_b, seq):
    N = u2.shape[0]
    n_tiles = -(-2 * N // MOE_TM) + N_EXPERTS
    pos, tile_expert, n_valid, gaps = _routing_plan(ids, n_tiles)
    u_sorted = _dispatch(u2, pos, gaps, n_tiles * MOE_TM)
    y_sorted = _moe_ffn(u_sorted, tile_expert, n_valid, wg, wu, wd)
    return _combine(y_sorted, pos, topw, x1, g2, ln_w, ln_b, seq)


def _split_w_in_kernel(wt_ref, gt_ref, main_ref, gate_ref):
    main_ref[...] = wt_ref[0].T.astype(BF16)

    @pl.when(pl.program_id(1) == 0)
    def _():
        g = gt_ref[...]
        g = jnp.concatenate([g, jnp.zeros((LANES - g.shape[0], g.shape[1]), F32)], axis=0)
        gate_ref[...] = g.T.astype(BF16)


def _split_w_in(w_in):
    L, D, C = w_in.shape
    w_t = jnp.swapaxes(w_in, 1, 2)
    g0 = 4 * GROUP_WIDTH
    ng = 2 * N_HEADS
    tc = 1024
    src_col = lambda j: (j * (tc // ng) + jnp.where(j * tc >= g0, 1, 0)) * ng
    return pl.pallas_call(
        _split_w_in_kernel,
        out_shape=(jax.ShapeDtypeStruct((L, D, C - ng), BF16),
                   jax.ShapeDtypeStruct((L, D, LANES), BF16)),
        grid=(L, (C - ng) // tc),
        in_specs=[pl.BlockSpec((pl.Element(1), pl.Element(tc), pl.Element(D)),
                               lambda l, j: (l, src_col(j), 0)),
                  pl.BlockSpec((None, ng, D), lambda l, j: (l, g0 // ng, 0))],
        out_specs=(pl.BlockSpec((None, D, tc), lambda l, j: (l, 0, j)),
                   pl.BlockSpec((None, D, LANES), lambda l, j: (l, 0, 0))),
        compiler_params=_params(("parallel", "arbitrary")),
        name="split_w_in",
    )(w_t, w_t)


def kernel(x, c, w_in, mlstm_conv_w, mlstm_conv_b, mlstm_gate_b, mlstm_norm_w, pool_w, pool_scale, conv_dw_w, conv_dw_b, conv_ln_w, conv_ln_b, conv_pw_w, conv_pw_b, group_norm_w, w_out, ada_w, ada_b, ln1_w, ln1_b, ln2_w, ln2_b, ffn_w_gate, ffn_w_up, ffn_w_down, moe_router_w, moe_router_b, moe_w_gate, moe_w_up, moe_w_down):
    B, S, D = x.shape
    GW = GROUP_WIDTH
    ada = _ada_all(c, ada_w, ada_b)
    x2d = x.reshape(B * S, D)
    w_main, w_gate = _split_w_in(w_in)
    for l in range(DEPTH):
        mod = [ada[l, :, i * D:(i + 1) * D].reshape(B, 1, D) for i in range(6)]
        sh1, sc1, g1, sh2, sc2, g2 = mod
        proj, gates = _in_proj(x2d, sc1, sh1, w_main, w_gate, l, S)
        proj = proj.reshape(B, S, -1)
        gates = gates.reshape(B, S, LANES)
        gn_b, gn_c, gn_d = (group_norm_w[l, i * GW:(i + 1) * GW] for i in range(3))
        ya = _mlstm_mixer(proj, gates, mlstm_conv_w[l], mlstm_conv_b[l], mlstm_gate_b[l],
                          mlstm_norm_w[l], B, S)
        yb = _pool_mixer(proj, pool_w[l], pool_scale[l], gn_b, B, S)
        yc = _sb_mixer(proj, B, S)
        yd = _conv_mixer(proj, conv_dw_w[l], conv_dw_b[l], conv_ln_w[l], conv_ln_b[l],
                         conv_pw_w[l], conv_pw_b[l], gn_d, B, S)
        flat = lambda t: t.reshape(B * S, GW)
        j = l // 2
        router = None
        if l % 2 == 1:
            rw = jnp.zeros((D, LANES), F32).at[:, :N_EXPERTS].set(moe_router_w[j])
            rw_hi = rw.astype(BF16)
            rw_lo = (rw - rw_hi.astype(F32)).astype(BF16)
            rb = jnp.zeros((1, LANES), F32).at[0, :N_EXPERTS].set(moe_router_b[j])
            router = (jnp.concatenate([rw_hi, rw_lo], axis=1), rb)
        outs = _out_proj(flat(ya), flat(yb), flat(yc), flat(yd), gn_c, w_out[l].astype(BF16), x2d,
                         g1, ln1_w[l], ln1_b[l], sc2, sh2, S, router)
        if l % 2 == 0:
            x1, u2 = outs
            x2d = _ffn(u2, ffn_w_gate[j].astype(BF16), ffn_w_up[j].astype(BF16),
                       ffn_w_down[j].astype(BF16), x1, g2, ln2_w[l], ln2_b[l], S)
        else:
            x1, u2, ids, topw = outs
            x2d = _moe(u2, ids, topw, moe_w_gate[j], moe_w_up[j], moe_w_down[j], x1, g2,
                       ln2_w[l], ln2_b[l], S)
    return x2d.reshape(B, S, D)
```

```python
import functools

import jax
import jax.numpy as jnp
from jax import lax
from jax.experimental import pallas as pl
from jax.experimental.pallas import tpu as pltpu

F32 = jnp.float32
BF16 = jnp.bfloat16

DEPTH = 2
CHUNK = 256
GROUP_WIDTH = 512
N_HEADS = 4
HEAD_DIM = 128
MLSTM_CONV = 4
POOL_WINDOWS = (2, 4, 8, 16)
CONV_WIDTH = 31
SB_KEY_BLOCK = 256
N_EXPERTS = 8
ALPHA = (2.0 * DEPTH) ** 0.25
EPS = 1e-5
LOG2E = 1.4426950408889634
LANES = 128
VMEM_LIMIT = 56 * 1024 * 1024


def _params(sem, vmem=VMEM_LIMIT):
    return pltpu.CompilerParams(dimension_semantics=sem, vmem_limit_bytes=vmem)


def _silu(x):
    return x * (1.0 / (1.0 + jnp.exp(-x)))


def _sigmoid(x):
    return 1.0 / (1.0 + jnp.exp(-x))


def _neg_softplus(x):
    return -(jnp.maximum(x, 0.0) + jnp.log(1.0 + jnp.exp(-jnp.abs(x))))


def _layer_norm_rows(r, w, b):
    mu = jnp.mean(r, axis=-1, keepdims=True)
    d = r - mu
    var = jnp.mean(d * d, axis=-1, keepdims=True)
    return d * lax.rsqrt(var + EPS) * w + b


def _rms_rows(y, w):
    return y * lax.rsqrt(jnp.mean(y * y, axis=-1, keepdims=True) + EPS) * w


def _bdot(a, b):
    return jnp.dot(a.astype(BF16), b.astype(BF16), preferred_element_type=F32)


def _ada_kernel(c_ref, w_ref, b_ref, out_ref):
    out_ref[...] = _bdot(_silu(c_ref[...]), w_ref[...]) + b_ref[...]


def _ada_all(c, ada_w, ada_b):
    L, D, D6 = ada_w.shape
    Bn = c.shape[0]
    tn = 2048
    return pl.pallas_call(
        _ada_kernel,
        out_shape=jax.ShapeDtypeStruct((L, Bn, D6), F32),
        grid=(L, D6 // tn),
        in_specs=[pl.BlockSpec((Bn, D), lambda l, n: (0, 0)),
                  pl.BlockSpec((None, D, tn), lambda l, n: (l, 0, n)),
                  pl.BlockSpec((None, 1, tn), lambda l, n: (l, 0, n))],
        out_specs=pl.BlockSpec((None, Bn, tn), lambda l, n: (l, 0, n)),
        compiler_params=_params(("parallel", "parallel")),
        name="ada_mod",
    )(c, ada_w, ada_b.reshape(L, 1, D6))


def _inproj_kernel(x_ref, sc_ref, sh_ref, w_ref, wg_ref, proj_ref, gates_ref, u_ref):
    @pl.when(pl.program_id(1) == 0)
    def _():
        u = (x_ref[...] * (1.0 + sc_ref[...]) + sh_ref[...]).astype(BF16)
        u_ref[...] = u
        gates_ref[...] = jnp.dot(u, wg_ref[...], preferred_element_type=F32)

    proj_ref[...] = jnp.dot(u_ref[...], w_ref[...], preferred_element_type=F32)


def _in_proj(x2d, sc, sh, w_main, w_gate, layer, seq):
    N, D = x2d.shape
    NC = w_main.shape[2]
    tm, tn = 1024, 1280
    tpb = seq // tm
    return pl.pallas_call(
        _inproj_kernel,
        out_shape=(jax.ShapeDtypeStruct((N, NC), F32), jax.ShapeDtypeStruct((N, LANES), F32)),
        grid=(N // tm, NC // tn),
        in_specs=[pl.BlockSpec((tm, D), lambda m, n: (m, 0)),
                  pl.BlockSpec((None, 1, D), lambda m, n: (m // tpb, 0, 0)),
                  pl.BlockSpec((None, 1, D), lambda m, n: (m // tpb, 0, 0)),
                  pl.BlockSpec((None, D, tn), lambda m, n: (layer, 0, n)),
                  pl.BlockSpec((None, D, LANES), lambda m, n: (layer, 0, 0))],
        out_specs=(pl.BlockSpec((tm, tn), lambda m, n: (m, n)),
                   pl.BlockSpec((tm, LANES), lambda m, n: (m, 0))),
        scratch_shapes=[pltpu.VMEM((tm, D), BF16)],
        compiler_params=_params(("parallel", "arbitrary")),
        name="in_proj",
    )(x2d, sc, sh, w_main, w_gate)


def _cumsum_rows(x):
    n = x.shape[0]
    row = lax.broadcasted_iota(jnp.int32, x.shape, 0)
    k = 1
    while k < n:
        x = x + jnp.where(row >= k, pltpu.roll(x, k, axis=0), 0.0)
        k *= 2
    return x


def _mlstm_kernel(qk_ref, v_ref, o_ref, g_ref, cw_ref, cb_ref, gb_ref, nw_ref, out_ref,
                  c_state, m_state, tail_ref):
    @pl.when(pl.program_id(1) == 0)
    def _():
        c_state[...] = jnp.zeros_like(c_state)
        m_state[...] = jnp.zeros_like(m_state)
        tail_ref[...] = jnp.zeros_like(tail_ref)

    for bb in range(qk_ref.shape[0]):
        _mlstm_block(qk_ref.at[bb], v_ref.at[bb], o_ref.at[bb], g_ref.at[bb], cw_ref, cb_ref, gb_ref,
                     nw_ref, out_ref.at[bb], c_state.at[bb], m_state.at[bb], tail_ref.at[bb])


def _mlstm_block(qk_ref, v_ref, o_ref, g_ref, cw_ref, cb_ref, gb_ref, nw_ref, out_ref,
                 c_state, m_state, tail_ref):
    T = qk_ref.shape[0]
    GW = GROUP_WIDTH

    xx = jnp.concatenate([tail_ref[...], qk_ref[...]], axis=0)
    tail_ref[...] = qk_ref[T - 8:T, :]
    conv = cb_ref[...]
    for j in range(MLSTM_CONV):
        off = 8 - (MLSTM_CONV - 1) + j
        tap = xx[off:off + T, :] if off % 8 == 0 else pltpu.roll(xx, T + 8 - off, axis=0)[0:T, :]
        conv = conv + cw_ref[j:j + 1, :] * tap
    qk = _silu(conv)

    gates = g_ref[...] + gb_ref[...]
    logf = _neg_softplus(-gates)
    causal = (lax.broadcasted_iota(jnp.int32, (CHUNK, CHUNK), 1)
              <= lax.broadcasted_iota(jnp.int32, (CHUNK, CHUNK), 0))

    for c in range(T // CHUNK):
        rows = slice(c * CHUNK, (c + 1) * CHUNK)
        gi = gates[rows, :]
        bcum = _cumsum_rows(logf[rows, :])
        gi_t = gi.T
        bcum_t = bcum.T
        for h in range(N_HEADS):
            cols = slice(h * HEAD_DIM, (h + 1) * HEAD_DIM)
            q = qk[rows, cols] * (HEAD_DIM ** -0.5)
            k = qk[rows, GW + h * HEAD_DIM:GW + (h + 1) * HEAD_DIM]
            v = v_ref[rows, cols]
            b_col = bcum[:, N_HEADS + h:N_HEADS + h + 1]
            b_row = bcum_t[N_HEADS + h:N_HEADS + h + 1, :]
            i_col = gi[:, h:h + 1]
            i_row = gi_t[h:h + 1, :]
            g_tot = b_col[CHUNK - 1:CHUNK, :]
            m_prev = m_state[h:h + 1, 0:1]
            c_prev = c_state[h]
            v1 = jnp.concatenate([v, jnp.ones_like(v)], axis=1)

            log_d = jnp.where(causal, b_col - b_row + i_row, -jnp.inf)
            m_inter = b_col + m_prev
            m_t = jnp.maximum(m_inter, jnp.max(log_d, axis=-1, keepdims=True))
            s = lax.dot_general(q.astype(BF16), k.astype(BF16), (((1,), (1,)), ((), ())),
                                preferred_element_type=F32)
            w = jnp.exp(log_d - m_t) * s
            inter = jnp.exp(m_inter - m_t)
            both = inter * _bdot(q, c_prev) + _bdot(w, v1)
            num, den = both[:, :HEAD_DIM], both[:, HEAD_DIM:]
            hcur = num / jnp.maximum(jnp.abs(den), jnp.exp(-m_t))

            a_col = g_tot - b_col + i_col
            m_new = jnp.maximum(g_tot + m_prev, jnp.max(a_col, axis=0, keepdims=True))
            decay = jnp.exp(g_tot + m_prev - m_new)
            kw = k * jnp.exp(a_col - m_new)
            c_state[h] = decay * c_prev + _bdot(kw.T, v1)
            m_state[h:h + 1, :] = jnp.broadcast_to(m_new, (1, LANES))

            mu = jnp.mean(hcur, axis=-1, keepdims=True)
            dlt = hcur - mu
            var = jnp.mean(dlt * dlt, axis=-1, keepdims=True)
            hn = dlt * lax.rsqrt(var + EPS) * nw_ref[:, cols]
            out_ref[rows, cols] = (_sigmoid(o_ref[rows, cols]) * hn).astype(out_ref.dtype)


def _mlstm_mixer(proj, gates, conv_w, conv_b, gate_b, norm_w, B, S):
    T = 256
    NB = 2
    GW = GROUP_WIDTH
    gb = jnp.zeros((1, LANES), F32).at[0, :2 * N_HEADS].set(gate_b)
    return pl.pallas_call(
        _mlstm_kernel,
        out_shape=jax.ShapeDtypeStruct((B, S, GW), BF16),
        grid=(B // NB, S // T),
        in_specs=[pl.BlockSpec((NB, T, 2 * GW), lambda b, t: (b, t, 0)),
                  pl.BlockSpec((NB, T, GW), lambda b, t: (b, t, 2)),
                  pl.BlockSpec((NB, T, GW), lambda b, t: (b, t, 3)),
                  pl.BlockSpec((NB, T, LANES), lambda b, t: (b, t, 0)),
                  pl.BlockSpec((MLSTM_CONV, 2 * GW), lambda b, t: (0, 0)),
                  pl.BlockSpec((1, 2 * GW), lambda b, t: (0, 0)),
                  pl.BlockSpec((1, LANES), lambda b, t: (0, 0)),
                  pl.BlockSpec((1, GW), lambda b, t: (0, 0))],
        out_specs=pl.BlockSpec((NB, T, GW), lambda b, t: (b, t, 0)),
        scratch_shapes=[pltpu.VMEM((NB, N_HEADS, HEAD_DIM, 2 * HEAD_DIM), F32),
                        pltpu.VMEM((NB, 8, LANES), F32),
                        pltpu.VMEM((NB, 8, 2 * GW), F32)],
        compiler_params=_params(("parallel", "arbitrary")),
        name="mlstm_mixer",
    )(proj, proj, proj, gates, conv_w, conv_b.reshape(1, -1), gb, norm_w.reshape(1, -1))


def _pool_kernel(x_ref, w_ref, scale_ref, gn_ref, out_ref, tail_ref):
    T = x_ref.shape[0]
    HALO = 16
    t_blk = pl.program_id(1)

    @pl.when(t_blk == 0)
    def _():
        tail_ref[...] = jnp.zeros_like(tail_ref)

    x = x_ref[...]
    xx = jnp.concatenate([tail_ref[...], x], axis=0)
    tail_ref[...] = x[T - HALO:T, :]
    pos = t_blk * T + lax.broadcasted_iota(jnp.int32, (T, 1), 0) + 1
    outs = []
    for g, win in enumerate(POOL_WINDOWS):
        cols = slice(g * LANES, (g + 1) * LANES)
        s = xx[:, cols]
        k = 1
        while k < win:
            s = s + pltpu.roll(s, k, axis=0)
            k *= 2
        cnt = jnp.minimum(pos, win).astype(F32)
        yg = s[HALO:, :] / cnt - x[:, cols]
        outs.append(_bdot(yg, w_ref[g]))
    y = jnp.concatenate(outs, axis=-1) * scale_ref[...]
    out_ref[...] = _rms_rows(y, gn_ref[...]).astype(out_ref.dtype)


def _pool_mixer(proj, pool_w, pool_scale, gn, B, S):
    T = 512
    GW = GROUP_WIDTH
    return pl.pallas_call(
        _pool_kernel,
        out_shape=jax.ShapeDtypeStruct((B, S, GW), BF16),
        grid=(B, S // T),
        in_specs=[pl.BlockSpec((None, T, GW), lambda b, t: (b, t, 4)),
                  pl.BlockSpec((len(POOL_WINDOWS), LANES, LANES), lambda b, t: (0, 0, 0)),
                  pl.BlockSpec((1, GW), lambda b, t: (0, 0)),
                  pl.BlockSpec((1, GW), lambda b, t: (0, 0))],
        out_specs=pl.BlockSpec((None, T, GW), lambda b, t: (b, t, 0)),
        scratch_shapes=[pltpu.VMEM((16, GW), F32)],
        compiler_params=_params(("parallel", "arbitrary")),
        name="pool_mixer",
    )(proj, pool_w.astype(BF16), pool_scale.reshape(1, -1), gn.reshape(1, -1))


def _sb_kernel(q_ref, k_ref, v_ref, out_ref, kb_ref, vb_ref):
    LQ = q_ref.shape[0]
    LK = SB_KEY_BLOCK
    ratio = LQ // LK
    qi = pl.program_id(1)

    @pl.when(qi == 0)
    def _():
        kb_ref[...] = k_ref[...].astype(BF16)
        vb_ref[...] = v_ref[...].astype(BF16)

    scale2 = HEAD_DIM ** -0.5 * LOG2E
    tri = (lax.broadcasted_iota(jnp.int32, (LK, LK), 0)
           > lax.broadcasted_iota(jnp.int32, (LK, LK), 1)).astype(BF16)
    row = lax.broadcasted_iota(jnp.int32, (LQ, LK), 0)
    col = lax.broadcasted_iota(jnp.int32, (LQ, LK), 1)
    heads = [slice(h * HEAD_DIM, (h + 1) * HEAD_DIM) for h in range(N_HEADS)]
    qs = [(q_ref[:, hs] * scale2).astype(BF16) for hs in heads]

    def sweep(kb, carry, strict):
        start = pl.multiple_of(kb * LK, LK)
        zs, lks = [], []
        for h, hs in enumerate(heads):
            k_blk = kb_ref[pl.ds(start, LK), hs]
            z = lax.dot_general(qs[h], k_blk, (((1,), (1,)), ((), ())), preferred_element_type=F32)
            lk = -(jnp.maximum(z, 0.0) + jnp.log2(1.0 + jnp.exp2(-jnp.abs(z))))
            if strict is not None:
                lk = jnp.where(strict, lk, 0.0)
            zs.append(z)
            lks.append(lk)
        inner = jnp.dot(jnp.concatenate([lk.astype(BF16) for lk in lks], axis=0), tri,
                        preferred_element_type=F32)
        new = []
        for h, hs in enumerate(heads):
            after, acc = carry[h]
            v_blk = vb_ref[pl.ds(start, LK), hs]
            a = jnp.exp2(zs[h] + lks[h] + inner[h * LQ:(h + 1) * LQ, :] + after)
            if strict is not None:
                a = jnp.where(strict, a, 0.0)
            acc = acc + jnp.dot(a.astype(BF16), v_blk, preferred_element_type=F32)
            after = after + jnp.sum(lks[h], axis=-1, keepdims=True)
            new.append((after, acc))
        return tuple(new)

    init = tuple((jnp.zeros((LQ, 1), F32), jnp.zeros((LQ, HEAD_DIM), F32)) for _ in heads)
    carry = init
    for j in reversed(range(ratio)):
        carry = sweep(ratio * qi + j, carry, (j * LK + col) < row)
    def sweep_group(it, c):
        for j in range(ratio):
            c = sweep(ratio * (qi - it) - 1 - j, c, None)
        return c

    carry = lax.fori_loop(0, qi, sweep_group, carry)
    for h, hs in enumerate(heads):
        out_ref[:, hs] = carry[h][1]


def _sb_mixer(proj, B, S):
    LQ = 512
    GW = GROUP_WIDTH
    return pl.pallas_call(
        _sb_kernel,
        out_shape=jax.ShapeDtypeStruct((B, S, GW), F32),
        grid=(B, S // LQ),
        in_specs=[pl.BlockSpec((None, LQ, GW), lambda b, i: (b, i, 5)),
                  pl.BlockSpec((None, S, GW), lambda b, i: (b, 0, 6)),
                  pl.BlockSpec((None, S, GW), lambda b, i: (b, 0, 7))],
        out_specs=pl.BlockSpec((None, LQ, GW), lambda b, i: (b, i, 0)),
        scratch_shapes=[pltpu.VMEM((S, GW), BF16), pltpu.VMEM((S, GW), BF16)],
        compiler_params=_params(("parallel", "arbitrary")),
        name="sb_attention",
    )(proj, proj, proj)


def _conv_kernel(x_ref, dw_ref, dwb_ref, lnw_ref, lnb_ref, pw_ref, pwb_ref, gn_ref, out_ref,
                 hbuf, ybuf):
    T = x_ref.shape[0]
    GW = GROUP_WIDTH
    HALO = 32
    SUB = 64

    @pl.when(pl.program_id(1) == 0)
    def _():
        hbuf[0:HALO, :] = jnp.zeros((HALO, GW), F32)

    @pl.when(pl.program_id(1) > 0)
    def _():
        hbuf[0:HALO, :] = hbuf[T:T + HALO, :]

    hbuf[HALO:HALO + T, :] = x_ref[:, 0:GW] * _sigmoid(x_ref[:, GW:2 * GW])

    def sub(i, carry):
        r0 = pl.multiple_of(i * SUB, SUB)
        acc = jnp.broadcast_to(dwb_ref[...], (SUB, GW))
        win = hbuf[pl.ds(r0, SUB + HALO), :]
        for res in range(8):
            shifted = win if res == 0 else pltpu.roll(win, SUB + HALO - res, axis=0)
            for j in range(CONV_WIDTH):
                off = HALO - CONV_WIDTH + 1 + j
                if off % 8 == res:
                    base = off - res
                    acc = acc + dw_ref[j:j + 1, :] * shifted[base:base + SUB, :]
        ybuf[pl.ds(r0, SUB), :] = acc
        return carry

    lax.fori_loop(0, T // SUB, sub, 0)
    hn = _silu(_layer_norm_rows(ybuf[...], lnw_ref[...], lnb_ref[...]))
    y = _bdot(hn, pw_ref[...]) + pwb_ref[...]
    out_ref[...] = _rms_rows(y, gn_ref[...]).astype(out_ref.dtype)


def _conv_mixer(proj, dw_w, dw_b, ln_w, ln_b, pw_w, pw_b, gn, B, S):
    T = 512
    GW = GROUP_WIDTH
    dw_pad = jnp.zeros((32, GW), F32).at[:CONV_WIDTH].set(dw_w)
    r = lambda a: a.reshape(1, -1)
    return pl.pallas_call(
        _conv_kernel,
        out_shape=jax.ShapeDtypeStruct((B, S, GW), BF16),
        grid=(B, S // T),
        in_specs=[pl.BlockSpec((None, T, 2 * GW), lambda b, t: (b, t, 4)),
                  pl.BlockSpec((32, GW), lambda b, t: (0, 0)),
                  pl.BlockSpec((1, GW), lambda b, t: (0, 0)),
                  pl.BlockSpec((1, GW), lambda b, t: (0, 0)),
                  pl.BlockSpec((1, GW), lambda b, t: (0, 0)),
                  pl.BlockSpec((GW, GW), lambda b, t: (0, 0)),
                  pl.BlockSpec((1, GW), lambda b, t: (0, 0)),
                  pl.BlockSpec((1, GW), lambda b, t: (0, 0))],
        out_specs=pl.BlockSpec((None, T, GW), lambda b, t: (b, t, 0)),
        scratch_shapes=[pltpu.VMEM((T + 32, GW), F32), pltpu.VMEM((T, GW), F32)],
        compiler_params=_params(("parallel", "arbitrary")),
        name="conv_mixer",
    )(proj, dw_pad, r(dw_b), r(ln_w), r(ln_b), pw_w.astype(BF16), r(pw_b), r(gn))


def _outproj_kernel(with_router, ya_ref, yb_ref, yc_ref, yd_ref, gnc_ref, w_ref, x_ref, g1_ref,
                    lnw_ref, lnb_ref, sc_ref, sh_ref, *rest):
    GW = GROUP_WIDTH
    if with_router:
        rw_ref, rb_ref, x1_ref, u2_ref, ids_ref, topw_ref = rest
    else:
        x1_ref, u2_ref = rest
    yc = _rms_rows(yc_ref[...], gnc_ref[...]).astype(BF16)
    acc = jnp.dot(ya_ref[...], w_ref[0:GW, :], preferred_element_type=F32)
    acc = acc + jnp.dot(yb_ref[...], w_ref[GW:2 * GW, :], preferred_element_type=F32)
    acc = acc + jnp.dot(yc, w_ref[2 * GW:3 * GW, :], preferred_element_type=F32)
    acc = acc + jnp.dot(yd_ref[...], w_ref[3 * GW:4 * GW, :], preferred_element_type=F32)
    r = ALPHA * x_ref[...] + (1.0 + g1_ref[...]) * acc
    x1 = _layer_norm_rows(r, lnw_ref[...], lnb_ref[...])
    x1_ref[...] = x1
    u2 = x1 * (1.0 + sc_ref[...]) + sh_ref[...]
    u2_ref[...] = u2.astype(u2_ref.dtype)
    if with_router:
        u_hi = u2.astype(BF16)
        u_lo = (u2 - u_hi.astype(F32)).astype(BF16)
        p = jnp.dot(u_hi, rw_ref[...], preferred_element_type=F32)
        logits = (p[:, :LANES] + p[:, LANES:]
                  + jnp.dot(u_lo, rw_ref[:, :LANES], preferred_element_type=F32) + rb_ref[...])
        lane = lax.broadcasted_iota(jnp.int32, logits.shape, 1)
        lg = jnp.where(lane < N_EXPERTS, logits, -jnp.inf)
        m1 = jnp.max(lg, axis=-1, keepdims=True)
        i1 = jnp.min(jnp.where(lg == m1, lane, LANES), axis=-1, keepdims=True)
        lg2 = jnp.where(lane == i1, -jnp.inf, lg)
        m2 = jnp.max(lg2, axis=-1, keepdims=True)
        i2 = jnp.min(jnp.where(lg2 == m2, lane, LANES), axis=-1, keepdims=True)
        e2 = jnp.exp(m2 - m1)
        w1 = 1.0 / (1.0 + e2)
        w2 = e2 / (1.0 + e2)
        ids_ref[...] = jnp.where(lane == 0, i1, jnp.where(lane == 1, i2, 0))
        topw_ref[...] = jnp.where(lane == 0, w1, jnp.where(lane == 1, w2, 0.0))


def _out_proj(ya, yb, yc, yd, gnc, w_out, x2d, g1, ln_w, ln_b, sc2, sh2, seq, router=None):
    N, D = x2d.shape
    GW = GROUP_WIDTH
    tm = 512
    tpb = seq // tm
    row = lambda m: (m, 0)
    const = lambda m: (0, 0)
    perb = lambda m: (m // tpb, 0, 0)
    in_specs = [pl.BlockSpec((tm, GW), row), pl.BlockSpec((tm, GW), row),
                pl.BlockSpec((tm, GW), row), pl.BlockSpec((tm, GW), row),
                pl.BlockSpec((1, GW), const),
                pl.BlockSpec((D, D), const),
                pl.BlockSpec((tm, D), row),
                pl.BlockSpec((None, 1, D), perb),
                pl.BlockSpec((1, D), const), pl.BlockSpec((1, D), const),
                pl.BlockSpec((None, 1, D), perb), pl.BlockSpec((None, 1, D), perb)]
    args = [ya, yb, yc, yd, gnc.reshape(1, -1), w_out, x2d, g1, ln_w.reshape(1, -1),
            ln_b.reshape(1, -1), sc2, sh2]
    out_shape = [jax.ShapeDtypeStruct((N, D), F32),
                 jax.ShapeDtypeStruct((N, D), BF16 if router is None else F32)]
    out_specs = [pl.BlockSpec((tm, D), row), pl.BlockSpec((tm, D), row)]
    if router is not None:
        rw, rb = router
        in_specs += [pl.BlockSpec((D, 2 * LANES), const), pl.BlockSpec((1, LANES), const)]
        args += [rw, rb]
        out_shape += [jax.ShapeDtypeStruct((N, LANES), jnp.int32), jax.ShapeDtypeStruct((N, LANES), F32)]
        out_specs += [pl.BlockSpec((tm, LANES), row), pl.BlockSpec((tm, LANES), row)]
    return pl.pallas_call(
        functools.partial(_outproj_kernel, router is not None),
        out_shape=tuple(out_shape),
        grid=(N // tm,),
        in_specs=in_specs,
        out_specs=tuple(out_specs),
        compiler_params=_params(("parallel",)),
        name="out_proj_ln",
    )(*args)


FFN_CW = 512


def _ffn_kernel(u_ref, wg_hbm, wu_hbm, wd_hbm, x_ref, g2_ref, lnw_ref, lnb_ref, out_ref,
                acc_ref, wgbuf, wubuf, wdbuf, sem):
    m = pl.program_id(0)
    n_chunks = wg_hbm.shape[1] // FFN_CW
    total = pl.num_programs(0) * n_chunks
    n_slots = wgbuf.shape[0]
    ahead = n_slots - 1

    def chunk_copies(g):
        slot = g % n_slots
        cols = pl.ds(pl.multiple_of((g % n_chunks) * FFN_CW, FFN_CW), FFN_CW)
        return (pltpu.make_async_copy(wg_hbm.at[:, cols], wgbuf.at[slot], sem.at[slot]),
                pltpu.make_async_copy(wu_hbm.at[:, cols], wubuf.at[slot], sem.at[slot]),
                pltpu.make_async_copy(wd_hbm.at[cols, :], wdbuf.at[slot], sem.at[slot]))

    @pl.when(m == 0)
    def _():
        for g in range(ahead):
            for cp in chunk_copies(g):
                cp.start()

    u = u_ref[...]
    for ci in range(n_chunks):
        g = m * n_chunks + ci
        slot = g % n_slots
        for cp in chunk_copies(g):
            cp.wait()

        @pl.when(g + ahead < total)
        def _():
            for cp in chunk_copies(g + ahead):
                cp.start()

        hg = jnp.dot(u, wgbuf[slot], preferred_element_type=F32)
        hu = jnp.dot(u, wubuf[slot], preferred_element_type=F32)
        h = (_silu(hg) * hu).astype(BF16)
        part = jnp.dot(h, wdbuf[slot], preferred_element_type=F32)
        if ci == 0:
            acc_ref[...] = part
        else:
            acc_ref[...] += part

    r = ALPHA * x_ref[...] + (1.0 + g2_ref[...]) * acc_ref[...]
    out_ref[...] = _layer_norm_rows(r, lnw_ref[...], lnb_ref[...])


def _ffn(u2, wg, wu, wd, x1, g2, ln_w, ln_b, seq):
    N, D = u2.shape
    tm = 512
    n_slots = 3
    tpb = seq // tm
    row = lambda m: (m, 0)
    const = lambda m: (0, 0)
    hbm = pl.BlockSpec(memory_space=pl.ANY)
    return pl.pallas_call(
        _ffn_kernel,
        out_shape=jax.ShapeDtypeStruct((N, D), F32),
        grid=(N // tm,),
        in_specs=[pl.BlockSpec((tm, D), row), hbm, hbm, hbm,
                  pl.BlockSpec((tm, D), row),
                  pl.BlockSpec((None, 1, D), lambda m: (m // tpb, 0, 0)),
                  pl.BlockSpec((1, D), const), pl.BlockSpec((1, D), const)],
        out_specs=pl.BlockSpec((tm, D), row),
        scratch_shapes=[pltpu.VMEM((tm, D), F32),
                        pltpu.VMEM((n_slots, D, FFN_CW), BF16), pltpu.VMEM((n_slots, D, FFN_CW), BF16),
                        pltpu.VMEM((n_slots, FFN_CW, D), BF16),
                        pltpu.SemaphoreType.DMA((n_slots,))],
        compiler_params=_params(("arbitrary",)),
        name="ffn_ln",
    )(u2, wg, wu, wd, x1, g2, ln_w.reshape(1, -1), ln_b.reshape(1, -1))


MOE_TM = 768
MOE_CW = 256


def _row_copy(src, src_row, dst, dst_row, sem):
    return pltpu.make_async_copy(src.at[pl.ds(src_row, 1), :], dst.at[pl.ds(dst_row, 1), :], sem)


def _start_rows(n_rows, make_copy):
    def body(i, carry):
        base = pl.multiple_of(i * 8, 8)
        for j in range(8):
            make_copy(base, j).start(priority=j % 2)
        return carry
    lax.fori_loop(0, n_rows // 8, body, 0)


def _wait_rows(n_rows, one_copy):
    def body(i, carry):
        one_copy.wait()
        return carry
    lax.fori_loop(0, n_rows, body, 0, unroll=8)


def _dispatch_kernel(pos_ref, gap_ref, u_ref, out_ref, zbuf, sem, zsem):
    tm = u_ref.shape[0]
    first = pl.program_id(0) * tm

    def start(r, carry):
        for k in range(2):
            _row_copy(u_ref, r, out_ref, pos_ref[2 * (first + r) + k], sem).start(priority=k)
        return carry

    lax.fori_loop(0, tm, start, 0, unroll=8)

    @pl.when(pl.program_id(0) == pl.num_programs(0) - 1)
    def _():
        zbuf[...] = jnp.zeros_like(zbuf)
        n_gaps = gap_ref.shape[0] // 2

        def block_copy(i):
            return pltpu.make_async_copy(zbuf, out_ref.at[pl.ds(pl.multiple_of(i * 8, 8), 8), :], zsem.at[1])

        for g in range(n_gaps):
            lo, hi = gap_ref[2 * g], gap_ref[2 * g + 1]
            mid = jnp.minimum((lo + 7) // 8 * 8, hi)
            lax.fori_loop(lo, mid, lambda r, c: (_row_copy(zbuf, 0, out_ref, r, zsem.at[0]).start(), c)[1], 0)
            lax.fori_loop(mid // 8, hi // 8, lambda i, c: (block_copy(i).start(), c)[1], 0)
        for g in range(n_gaps):
            lo, hi = gap_ref[2 * g], gap_ref[2 * g + 1]
            mid = jnp.minimum((lo + 7) // 8 * 8, hi)
            lax.fori_loop(lo, mid, lambda r, c: (_row_copy(zbuf, 0, out_ref, 0, zsem.at[0]).wait(), c)[1], 0)
            lax.fori_loop(mid // 8, hi // 8, lambda i, c: (block_copy(0).wait(), c)[1], 0)

    _wait_rows(2 * tm, _row_copy(u_ref, 0, out_ref, 0, sem))


def _dispatch(u2, pos, gaps, n_rows):
    N, D = u2.shape
    tm = 512
    return pl.pallas_call(
        _dispatch_kernel,
        out_shape=jax.ShapeDtypeStruct((n_rows, D), u2.dtype),
        grid_spec=pltpu.PrefetchScalarGridSpec(
            num_scalar_prefetch=2,
            grid=(N // tm,),
            in_specs=[pl.BlockSpec((tm, D), lambda m, pos, gaps: (m, 0))],
            out_specs=pl.BlockSpec(memory_space=pl.ANY),
            scratch_shapes=[pltpu.VMEM((8, D), u2.dtype), pltpu.SemaphoreType.DMA(()),
                            pltpu.SemaphoreType.DMA((2,))]),
        compiler_params=_params(("arbitrary",)),
        name="moe_dispatch",
    )(pos, gaps, u2)


def _moe_ffn_kernel(te_ref, nv_ref, u_ref, wg_hbm, wu_hbm, wd_hbm, out_ref, wgbuf, wubuf, wdbuf, sem):
    t = pl.program_id(0)
    nv = nv_ref[0]
    F = wg_hbm.shape[2]
    n_chunks = F // MOE_CW
    n_slots = wgbuf.shape[0]
    ahead = n_slots - 1

    def chunk_copies(g):
        tile, ci = g // n_chunks, g % n_chunks
        e = te_ref[tile]
        slot = g % n_slots
        cols = pl.ds(pl.multiple_of(ci * MOE_CW, MOE_CW), MOE_CW)
        return (pltpu.make_async_copy(wg_hbm.at[e, :, cols], wgbuf.at[slot], sem.at[slot]),
                pltpu.make_async_copy(wu_hbm.at[e, :, cols], wubuf.at[slot], sem.at[slot]),
                pltpu.make_async_copy(wd_hbm.at[e, cols, :], wdbuf.at[slot], sem.at[slot]))

    @pl.when(t == 0)
    def _():
        for g in range(ahead):
            for cp in chunk_copies(g):
                cp.start()

    @pl.when(t < nv)
    def _():
        u = u_ref[...].astype(BF16)
        for ci in range(n_chunks):
            g = t * n_chunks + ci
            slot = g % n_slots
            for cp in chunk_copies(g):
                cp.wait()

            @pl.when(g + ahead < nv * n_chunks)
            def _():
                for cp in chunk_copies(g + ahead):
                    cp.start()

            hg = jnp.dot(u, wgbuf[slot].astype(BF16), preferred_element_type=F32)
            hu = jnp.dot(u, wubuf[slot].astype(BF16), preferred_element_type=F32)
            h = (_silu(hg) * hu).astype(BF16)
            part = jnp.dot(h, wdbuf[slot].astype(BF16), preferred_element_type=F32)
            if ci == 0:
                out_ref[...] = part
            else:
                out_ref[...] += part

    @pl.when(t >= nv)
    def _():
        out_ref[...] = jnp.zeros_like(out_ref)


def _moe_ffn(u_sorted, tile_expert, n_valid, wg, wu, wd):
    R, D = u_sorted.shape
    tm = MOE_TM
    n_slots = 3
    tile = lambda t, nv: jnp.minimum(t, nv[0] - 1)
    return pl.pallas_call(
        _moe_ffn_kernel,
        out_shape=jax.ShapeDtypeStruct((R, D), F32),
        grid_spec=pltpu.PrefetchScalarGridSpec(
            num_scalar_prefetch=2,
            grid=(R // tm,),
            in_specs=[pl.BlockSpec((tm, D), lambda t, te, nv: (tile(t, nv), 0)),
                      pl.BlockSpec(memory_space=pl.ANY),
                      pl.BlockSpec(memory_space=pl.ANY),
                      pl.BlockSpec(memory_space=pl.ANY)],
            out_specs=pl.BlockSpec((tm, D), lambda t, te, nv: (t, 0)),
            scratch_shapes=[pltpu.VMEM((n_slots, D, MOE_CW), F32), pltpu.VMEM((n_slots, D, MOE_CW), F32),
                            pltpu.VMEM((n_slots, MOE_CW, D), F32),
                            pltpu.SemaphoreType.DMA((n_slots,))]),
        compiler_params=_params(("arbitrary",), vmem=60 * 1024 * 1024),
        name="moe_ffn",
    )(tile_expert, n_valid, u_sorted, wg, wu, wd)


def _combine_kernel(pos_ref, y_ref, topw_ref, x_ref, g2_ref, lnw_ref, lnb_ref, out_ref, ybuf, sem):
    tm = x_ref.shape[0]
    m = pl.program_id(0)
    slot = m % 2

    def gather(step, buf):
        for k in range(2):
            _start_rows(tm, lambda base, j: _row_copy(y_ref, pos_ref[2 * (step * tm + base + j) + k],
                                                      ybuf.at[buf, k], base + j, sem.at[buf]))

    @pl.when(m == 0)
    def _():
        gather(0, 0)

    @pl.when(m + 1 < pl.num_programs(0))
    def _():
        gather(m + 1, 1 - slot)

    _wait_rows(2 * tm, _row_copy(y_ref, 0, ybuf.at[slot, 0], 0, sem.at[slot]))
    tw = topw_ref[...]
    y = tw[:, 0:1] * ybuf[slot, 0] + tw[:, 1:2] * ybuf[slot, 1]
    r = ALPHA * x_ref[...] + (1.0 + g2_ref[...]) * y
    out_ref[...] = _layer_norm_rows(r, lnw_ref[...], lnb_ref[...])


def _combine(y_sorted, pos, topw, x1, g2, ln_w, ln_b, seq):
    N, D = x1.shape
    tm = 512
    tpb = seq // tm
    row = lambda m, pos: (m, 0)
    const = lambda m, pos: (0, 0)
    return pl.pallas_call(
        _combine_kernel,
        out_shape=jax.ShapeDtypeStruct((N, D), F32),
        grid_spec=pltpu.PrefetchScalarGridSpec(
            num_scalar_prefetch=1,
            grid=(N // tm,),
            in_specs=[pl.BlockSpec(memory_space=pl.ANY),
                      pl.BlockSpec((tm, LANES), row),
                      pl.BlockSpec((tm, D), row),
                      pl.BlockSpec((None, 1, D), lambda m, pos: (m // tpb, 0, 0)),
                      pl.BlockSpec((1, D), const), pl.BlockSpec((1, D), const)],
            out_specs=pl.BlockSpec((tm, D), row),
            scratch_shapes=[pltpu.VMEM((2, 2, tm, D), F32), pltpu.SemaphoreType.DMA((2,))]),
        compiler_params=_params(("arbitrary",)),
        name="moe_combine_ln",
    )(pos, y_sorted, topw, x1, g2, ln_w.reshape(1, -1), ln_b.reshape(1, -1))


def _routing_plan(ids, n_tiles):
    e_flat = ids[:, :2].reshape(-1)
    onehot = (e_flat[:, None] == jnp.arange(N_EXPERTS, dtype=jnp.int32)[None, :]).astype(jnp.int32)
    csum = jnp.cumsum(onehot, axis=0)
    rank = jnp.sum((csum - onehot) * onehot, axis=1)
    counts = csum[-1]
    padded = ((counts + MOE_TM - 1) // MOE_TM) * MOE_TM
    ends = jnp.cumsum(padded)
    offs = ends - padded
    pos = (jnp.sum(onehot * offs[None, :], axis=1) + rank).astype(jnp.int32)
    tile_start = jnp.arange(n_tiles, dtype=jnp.int32) * MOE_TM
    n_valid = (ends[-1] // MOE_TM).astype(jnp.int32)
    tile_start = jnp.minimum(tile_start, (n_valid - 1) * MOE_TM)
    tile_expert = jnp.sum((tile_start[:, None] >= ends[None, :]).astype(jnp.int32), axis=1).astype(jnp.int32)
    total = jnp.full((1,), n_tiles * MOE_TM, jnp.int32)
    gaps = jnp.stack([jnp.concatenate([offs + counts, ends[-1:]]),
                      jnp.concatenate([ends, total])], axis=1).reshape(-1).astype(jnp.int32)
    return pos, tile_expert, n_valid.reshape(1), gaps


def _moe(u2, ids, topw, wg, wu, wd, x1, g2, ln_w, ln_b, seq):
    N = u2.shape[0]
    n_tiles = -(-2 * N // MOE_TM) + N_EXPERTS
    pos, tile_expert, n_valid, gaps = _routing_plan(ids, n_tiles)
    u_sorted = _dispatch(u2, pos, gaps, n_tiles * MOE_TM)
    y_sorted = _moe_ffn(u_sorted, tile_expert, n_valid, wg, wu, wd)
    return _combine(y_sorted, pos, topw, x1, g2, ln_w, ln_b, seq)


def _split_w_in_kernel(wt_ref, gt_ref, main_ref, gate_ref):
    main_ref[...] = wt_ref[0].T.astype(BF16)

    @pl.when(pl.program_id(1) == 0)
    def _():
        g = gt_ref[...]
        g = jnp.concatenate([g, jnp.zeros((LANES - g.shape[0], g.shape[1]), F32)], axis=0)
        gate_ref[...] = g.T.astype(BF16)


def _split_w_in(w_in):
    L, D, C = w_in.shape
    w_t = jnp.swapaxes(w_in, 1, 2)
    g0 = 4 * GROUP_WIDTH
    ng = 2 * N_HEADS
    tc = 1024
    src_col = lambda j: (j * (tc // ng) + jnp.where(j * tc >= g0, 1, 0)) * ng
    return pl.pallas_call(
        _split_w_in_kernel,
        out_shape=(jax.ShapeDtypeStruct((L, D, C - ng), BF16),
                   jax.ShapeDtypeStruct((L, D, LANES), BF16)),
        grid=(L, (C - ng) // tc),
        in_specs=[pl.BlockSpec((pl.Element(1), pl.Element(tc), pl.Element(D)),
                               lambda l, j: (l, src_col(j), 0)),
                  pl.BlockSpec((None, ng, D), lambda l, j: (l, g0 // ng, 0))],
        out_specs=(pl.BlockSpec((None, D, tc), lambda l, j: (l, 0, j)),
                   pl.BlockSpec((None, D, LANES), lambda l, j: (l, 0, 0))),
        compiler_params=_params(("parallel", "arbitrary")),
        name="split_w_in",
    )(w_t, w_t)


def kernel(x, c, w_in, mlstm_conv_w, mlstm_conv_b, mlstm_gate_b, mlstm_norm_w, pool_w, pool_scale, conv_dw_w, conv_dw_b, conv_ln_w, conv_ln_b, conv_pw_w, conv_pw_b, group_norm_w, w_out, ada_w, ada_b, ln1_w, ln1_b, ln2_w, ln2_b, ffn_w_gate, ffn_w_up, ffn_w_down, moe_router_w, moe_router_b, moe_w_gate, moe_w_up, moe_w_down):
    B, S, D = x.shape
    GW = GROUP_WIDTH
    ada = _ada_all(c, ada_w, ada_b)
    x2d = x.reshape(B * S, D)
    w_main, w_gate = _split_w_in(w_in)
    for l in range(DEPTH):
        mod = [ada[l, :, i * D:(i + 1) * D].reshape(B, 1, D) for i in range(6)]
        sh1, sc1, g1, sh2, sc2, g2 = mod
        proj, gates = _in_proj(x2d, sc1, sh1, w_main, w_gate, l, S)
        proj = proj.reshape(B, S, -1)
        gates = gates.reshape(B, S, LANES)
        gn_b, gn_c, gn_d = (group_norm_w[l, i * GW:(i + 1) * GW] for i in range(3))
        ya = _mlstm_mixer(proj, gates, mlstm_conv_w[l], mlstm_conv_b[l], mlstm_gate_b[l],
                          mlstm_norm_w[l], B, S)
        yb = _pool_mixer(proj, pool_w[l], pool_scale[l], gn_b, B, S)
        yc = _sb_mixer(proj, B, S)
        yd = _conv_mixer(proj, conv_dw_w[l], conv_dw_b[l], conv_ln_w[l], conv_ln_b[l],
                         conv_pw_w[l], conv_pw_b[l], gn_d, B, S)
        flat = lambda t: t.reshape(B * S, GW)
        j = l // 2
        router = None
        if l % 2 == 1:
            rw = jnp.zeros((D, LANES), F32).at[:, :N_EXPERTS].set(moe_router_w[j])
            rw_hi = rw.astype(BF16)
            rw_lo = (rw - rw_hi.astype(F32)).astype(BF16)
            rb = jnp.zeros((1, LANES), F32).at[0, :N_EXPERTS].set(moe_router_b[j])
            router = (jnp.concatenate([rw_hi, rw_lo], axis=1), rb)
        outs = _out_proj(flat(ya), flat(yb), flat(yc), flat(yd), gn_c, w_out[l].astype(BF16), x2d,
                         g1, ln1_w[l], ln1_b[l], sc2, sh2, S, router)
        if l % 2 == 0:
            x1, u2 = outs
            x2d = _ffn(u2, ffn_w_gate[j].astype(BF16), ffn_w_up[j].astype(BF16),
                       ffn_w_down[j].astype(BF16), x1, g2, ln2_w[l], ln2_b[l], S)
        else:
            x1, u2, ids, topw = outs
            x2d = _moe(u2, ids, topw, moe_w_gate[j], moe_w_up[j], moe_w_down[j], x1, g2,
                       ln2_w[l], ln2_b[l], S)
    return x2d.reshape(B, S, D)
```

```python
import functools

import jax
import jax.numpy as jnp
from jax import lax
from jax.experimental import pallas as pl
from jax.experimental.pallas import tpu as pltpu

F32 = jnp.float32
BF16 = jnp.bfloat16

DEPTH = 2
CHUNK = 256
GROUP_WIDTH = 512
N_HEADS = 4
HEAD_DIM = 128
MLSTM_CONV = 4
POOL_WINDOWS = (2, 4, 8, 16)
CONV_WIDTH = 31
SB_KEY_BLOCK = 256
N_EXPERTS = 8
ALPHA = (2.0 * DEPTH) ** 0.25
EPS = 1e-5
LOG2E = 1.4426950408889634
LANES = 128
VMEM_LIMIT = 56 * 1024 * 1024


def _params(sem, vmem=VMEM_LIMIT):
    return pltpu.CompilerParams(dimension_semantics=sem, vmem_limit_bytes=vmem)


def _silu(x):
    return x * (1.0 / (1.0 + jnp.exp(-x)))


def _sigmoid(x):
    return 1.0 / (1.0 + jnp.exp(-x))


def _neg_softplus(x):
    return -(jnp.maximum(x, 0.0) + jnp.log(1.0 + jnp.exp(-jnp.abs(x))))


def _layer_norm_rows(r, w, b):
    mu = jnp.mean(r, axis=-1, keepdims=True)
    d = r - mu
    var = jnp.mean(d * d, axis=-1, keepdims=True)
    return d * lax.rsqrt(var + EPS) * w + b


def _rms_rows(y, w):
    return y * lax.rsqrt(jnp.mean(y * y, axis=-1, keepdims=True) + EPS) * w


def _bdot(a, b):
    return jnp.dot(a.astype(BF16), b.astype(BF16), preferred_element_type=F32)


def _ada_kernel(c_ref, w_ref, b_ref, out_ref):
    out_ref[...] = _bdot(_silu(c_ref[...]), w_ref[...]) + b_ref[...]


def _ada_all(c, ada_w, ada_b):
    L, D, D6 = ada_w.shape
    Bn = c.shape[0]
    tn = 2048
    return pl.pallas_call(
        _ada_kernel,
        out_shape=jax.ShapeDtypeStruct((L, Bn, D6), F32),
        grid=(L, D6 // tn),
        in_specs=[pl.BlockSpec((Bn, D), lambda l, n: (0, 0)),
                  pl.BlockSpec((None, D, tn), lambda l, n: (l, 0, n)),
                  pl.BlockSpec((None, 1, tn), lambda l, n: (l, 0, n))],
        out_specs=pl.BlockSpec((None, Bn, tn), lambda l, n: (l, 0, n)),
        compiler_params=_params(("parallel", "parallel")),
        name="ada_mod",
    )(c, ada_w, ada_b.reshape(L, 1, D6))


def _inproj_kernel(x_ref, sc_ref, sh_ref, w_ref, wg_ref, proj_ref, gates_ref, u_ref):
    @pl.when(pl.program_id(1) == 0)
    def _():
        u = (x_ref[...] * (1.0 + sc_ref[...]) + sh_ref[...]).astype(BF16)
        u_ref[...] = u
        gates_ref[...] = jnp.dot(u, wg_ref[...], preferred_element_type=F32)

    proj_ref[...] = jnp.dot(u_ref[...], w_ref[...], preferred_element_type=F32)


def _in_proj(x2d, sc, sh, w_main, w_gate, layer, seq):
    N, D = x2d.shape
    NC = w_main.shape[2]
    tm, tn = 1024, 1280
    tpb = seq // tm
    return pl.pallas_call(
        _inproj_kernel,
        out_shape=(jax.ShapeDtypeStruct((N, NC), F32), jax.ShapeDtypeStruct((N, LANES), F32)),
        grid=(N // tm, NC // tn),
        in_specs=[pl.BlockSpec((tm, D), lambda m, n: (m, 0)),
                  pl.BlockSpec((None, 1, D), lambda m, n: (m // tpb, 0, 0)),
                  pl.BlockSpec((None, 1, D), lambda m, n: (m // tpb, 0, 0)),
                  pl.BlockSpec((None, D, tn), lambda m, n: (layer, 0, n)),
                  pl.BlockSpec((None, D, LANES), lambda m, n: (layer, 0, 0))],
        out_specs=(pl.BlockSpec((tm, tn), lambda m, n: (m, n)),
                   pl.BlockSpec((tm, LANES), lambda m, n: (m, 0))),
        scratch_shapes=[pltpu.VMEM((tm, D), BF16)],
        compiler_params=_params(("parallel", "arbitrary")),
        name="in_proj",
    )(x2d, sc, sh, w_main, w_gate)


def _cumsum_rows(x):
    n = x.shape[0]
    row = lax.broadcasted_iota(jnp.int32, x.shape, 0)
    k = 1
    while k < n:
        x = x + jnp.where(row >= k, pltpu.roll(x, k, axis=0), 0.0)
        k *= 2
    return x


def _mlstm_kernel(qk_ref, v_ref, o_ref, g_ref, cw_ref, cb_ref, gb_ref, nw_ref, out_ref,
                  c_state, m_state, tail_ref):
    @pl.when(pl.program_id(1) == 0)
    def _():
        c_state[...] = jnp.zeros_like(c_state)
        m_state[...] = jnp.zeros_like(m_state)
        tail_ref[...] = jnp.zeros_like(tail_ref)

    for bb in range(qk_ref.shape[0]):
        _mlstm_block(qk_ref.at[bb], v_ref.at[bb], o_ref.at[bb], g_ref.at[bb], cw_ref, cb_ref, gb_ref,
                     nw_ref, out_ref.at[bb], c_state.at[bb], m_state.at[bb], tail_ref.at[bb])


def _mlstm_block(qk_ref, v_ref, o_ref, g_ref, cw_ref, cb_ref, gb_ref, nw_ref, out_ref,
                 c_state, m_state, tail_ref):
    T = qk_ref.shape[0]
    GW = GROUP_WIDTH

    xx = jnp.concatenate([tail_ref[...], qk_ref[...]], axis=0)
    tail_ref[...] = qk_ref[T - 8:T, :]
    conv = cb_ref[...]
    for j in range(MLSTM_CONV):
        off = 8 - (MLSTM_CONV - 1) + j
        tap = xx[off:off + T, :] if off % 8 == 0 else pltpu.roll(xx, T + 8 - off, axis=0)[0:T, :]
        conv = conv + cw_ref[j:j + 1, :] * tap
    qk = _silu(conv)

    gates = g_ref[...] + gb_ref[...]
    logf = _neg_softplus(-gates)
    causal = (lax.broadcasted_iota(jnp.int32, (CHUNK, CHUNK), 1)
              <= lax.broadcasted_iota(jnp.int32, (CHUNK, CHUNK), 0))

    for c in range(T // CHUNK):
        rows = slice(c * CHUNK, (c + 1) * CHUNK)
        gi = gates[rows, :]
        bcum = _cumsum_rows(logf[rows, :])
        gi_t = gi.T
        bcum_t = bcum.T
        for h in range(N_HEADS):
            cols = slice(h * HEAD_DIM, (h + 1) * HEAD_DIM)
            q = qk[rows, cols] * (HEAD_DIM ** -0.5)
            k = qk[rows, GW + h * HEAD_DIM:GW + (h + 1) * HEAD_DIM]
            v = v_ref[rows, cols]
            b_col = bcum[:, N_HEADS + h:N_HEADS + h + 1]
            b_row = bcum_t[N_HEADS + h:N_HEADS + h + 1, :]
            i_col = gi[:, h:h + 1]
            i_row = gi_t[h:h + 1, :]
            g_tot = b_col[CHUNK - 1:CHUNK, :]
            m_prev = m_state[h:h + 1, 0:1]
            c_prev = c_state[h]
            v1 = jnp.concatenate([v, jnp.ones_like(v)], axis=1)

            log_d = jnp.where(causal, b_col - b_row + i_row, -jnp.inf)
            m_inter = b_col + m_prev
            m_t = jnp.maximum(m_inter, jnp.max(log_d, axis=-1, keepdims=True))
            s = lax.dot_general(q.astype(BF16), k.astype(BF16), (((1,), (1,)), ((), ())),
                                preferred_element_type=F32)
            w = jnp.exp(log_d - m_t) * s
            inter = jnp.exp(m_inter - m_t)
            both = inter * _bdot(q, c_prev) + _bdot(w, v1)
            num, den = both[:, :HEAD_DIM], both[:, HEAD_DIM:]
            hcur = num / jnp.maximum(jnp.abs(den), jnp.exp(-m_t))

            a_col = g_tot - b_col + i_col
            m_new = jnp.maximum(g_tot + m_prev, jnp.max(a_col, axis=0, keepdims=True))
            decay = jnp.exp(g_tot + m_prev - m_new)
            kw = k * jnp.exp(a_col - m_new)
            c_state[h] = decay * c_prev + _bdot(kw.T, v1)
            m_state[h:h + 1, :] = jnp.broadcast_to(m_new, (1, LANES))

            mu = jnp.mean(hcur, axis=-1, keepdims=True)
            dlt = hcur - mu
            var = jnp.mean(dlt * dlt, axis=-1, keepdims=True)
            hn = dlt * lax.rsqrt(var + EPS) * nw_ref[:, cols]
            out_ref[rows, cols] = (_sigmoid(o_ref[rows, cols]) * hn).astype(out_ref.dtype)


def _mlstm_mixer(proj, gates, conv_w, conv_b, gate_b, norm_w, B, S):
    T = 256
    NB = 2
    GW = GROUP_WIDTH
    gb = jnp.zeros((1, LANES), F32).at[0, :2 * N_HEADS].set(gate_b)
    return pl.pallas_call(
        _mlstm_kernel,
        out_shape=jax.ShapeDtypeStruct((B, S, GW), BF16),
        grid=(B // NB, S // T),
        in_specs=[pl.BlockSpec((NB, T, 2 * GW), lambda b, t: (b, t, 0)),
                  pl.BlockSpec((NB, T, GW), lambda b, t: (b, t, 2)),
                  pl.BlockSpec((NB, T, GW), lambda b, t: (b, t, 3)),
                  pl.BlockSpec((NB, T, LANES), lambda b, t: (b, t, 0)),
                  pl.BlockSpec((MLSTM_CONV, 2 * GW), lambda b, t: (0, 0)),
                  pl.BlockSpec((1, 2 * GW), lambda b, t: (0, 0)),
                  pl.BlockSpec((1, LANES), lambda b, t: (0, 0)),
                  pl.BlockSpec((1, GW), lambda b, t: (0, 0))],
        out_specs=pl.BlockSpec((NB, T, GW), lambda b, t: (b, t, 0)),
        scratch_shapes=[pltpu.VMEM((NB, N_HEADS, HEAD_DIM, 2 * HEAD_DIM), F32),
                        pltpu.VMEM((NB, 8, LANES), F32),
                        pltpu.VMEM((NB, 8, 2 * GW), F32)],
        compiler_params=_params(("parallel", "arbitrary")),
        name="mlstm_mixer",
    )(proj, proj, proj, gates, conv_w, conv_b.reshape(1, -1), gb, norm_w.reshape(1, -1))


def _pool_kernel(x_ref, w_ref, scale_ref, gn_ref, out_ref, tail_ref):
    T = x_ref.shape[0]
    HALO = 16
    t_blk = pl.program_id(1)

    @pl.when(t_blk == 0)
    def _():
        tail_ref[...] = jnp.zeros_like(tail_ref)

    x = x_ref[...]
    xx = jnp.concatenate([tail_ref[...], x], axis=0)
    tail_ref[...] = x[T - HALO:T, :]
    pos = t_blk * T + lax.broadcasted_iota(jnp.int32, (T, 1), 0) + 1
    outs = []
    for g, win in enumerate(POOL_WINDOWS):
        cols = slice(g * LANES, (g + 1) * LANES)
        s = xx[:, cols]
        k = 1
        while k < win:
            s = s + pltpu.roll(s, k, axis=0)
            k *= 2
        cnt = jnp.minimum(pos, win).astype(F32)
        yg = s[HALO:, :] / cnt - x[:, cols]
        outs.append(_bdot(yg, w_ref[g]))
    y = jnp.concatenate(outs, axis=-1) * scale_ref[...]
    out_ref[...] = _rms_rows(y, gn_ref[...]).astype(out_ref.dtype)


def _pool_mixer(proj, pool_w, pool_scale, gn, B, S):
    T = 512
    GW = GROUP_WIDTH
    return pl.pallas_call(
        _pool_kernel,
        out_shape=jax.ShapeDtypeStruct((B, S, GW), BF16),
        grid=(B, S // T),
        in_specs=[pl.BlockSpec((None, T, GW), lambda b, t: (b, t, 4)),
                  pl.BlockSpec((len(POOL_WINDOWS), LANES, LANES), lambda b, t: (0, 0, 0)),
                  pl.BlockSpec((1, GW), lambda b, t: (0, 0)),
                  pl.BlockSpec((1, GW), lambda b, t: (0, 0))],
        out_specs=pl.BlockSpec((None, T, GW), lambda b, t: (b, t, 0)),
        scratch_shapes=[pltpu.VMEM((16, GW), F32)],
        compiler_params=_params(("parallel", "arbitrary")),
        name="pool_mixer",
    )(proj, pool_w.astype(BF16), pool_scale.reshape(1, -1), gn.reshape(1, -1))


def _sb_kernel(q_ref, k_ref, v_ref, out_ref, kb_ref, vb_ref):
    LQ = q_ref.shape[0]
    LK = SB_KEY_BLOCK
    ratio = LQ // LK
    qi = pl.program_id(1)

    @pl.when(qi == 0)
    def _():
        kb_ref[...] = k_ref[...].astype(BF16)
        vb_ref[...] = v_ref[...].astype(BF16)

    scale2 = HEAD_DIM ** -0.5 * LOG2E
    tri = (lax.broadcasted_iota(jnp.int32, (LK, LK), 0)
           > lax.broadcasted_iota(jnp.int32, (LK, LK), 1)).astype(BF16)
    row = lax.broadcasted_iota(jnp.int32, (LQ, LK), 0)
    col = lax.broadcasted_iota(jnp.int32, (LQ, LK), 1)
    heads = [slice(h * HEAD_DIM, (h + 1) * HEAD_DIM) for h in range(N_HEADS)]
    qs = [(q_ref[:, hs] * scale2).astype(BF16) for hs in heads]

    def sweep(kb, carry, strict):
        start = pl.multiple_of(kb * LK, LK)
        zs, lks = [], []
        for h, hs in enumerate(heads):
            k_blk = kb_ref[pl.ds(start, LK), hs]
            z = lax.dot_general(qs[h], k_blk, (((1,), (1,)), ((), ())), preferred_element_type=F32)
            lk = -(jnp.maximum(z, 0.0) + jnp.log2(1.0 + jnp.exp2(-jnp.abs(z))))
            if strict is not None:
                lk = jnp.where(strict, lk, 0.0)
            zs.append(z)
            lks.append(lk)
        inner = jnp.dot(jnp.concatenate([lk.astype(BF16) for lk in lks], axis=0), tri,
                        preferred_element_type=F32)
        new = []
        for h, hs in enumerate(heads):
            after, acc = carry[h]
            v_blk = vb_ref[pl.ds(start, LK), hs]
            a = jnp.exp2(zs[h] + lks[h] + inner[h * LQ:(h + 1) * LQ, :] + after)
            if strict is not None:
                a = jnp.where(strict, a, 0.0)
            acc = acc + jnp.dot(a.astype(BF16), v_blk, preferred_element_type=F32)
            after = after + (inner[h * LQ:(h + 1) * LQ, 0:1] + lks[h][:, 0:1])
            new.append((after, acc))
        return tuple(new)

    init = tuple((jnp.zeros((LQ, 1), F32), jnp.zeros((LQ, HEAD_DIM), F32)) for _ in heads)
    carry = init
    for j in reversed(range(ratio)):
        carry = sweep(ratio * qi + j, carry, (j * LK + col) < row)
    def sweep_group(it, c):
        for j in range(ratio):
            c = sweep(ratio * (qi - it) - 1 - j, c, None)
        return c

    carry = lax.fori_loop(0, qi, sweep_group, carry)
    for h, hs in enumerate(heads):
        out_ref[:, hs] = carry[h][1]


def _sb_mixer(proj, B, S):
    LQ = 512
    GW = GROUP_WIDTH
    return pl.pallas_call(
        _sb_kernel,
        out_shape=jax.ShapeDtypeStruct((B, S, GW), F32),
        grid=(B, S // LQ),
        in_specs=[pl.BlockSpec((None, LQ, GW), lambda b, i: (b, i, 5)),
                  pl.BlockSpec((None, S, GW), lambda b, i: (b, 0, 6)),
                  pl.BlockSpec((None, S, GW), lambda b, i: (b, 0, 7))],
        out_specs=pl.BlockSpec((None, LQ, GW), lambda b, i: (b, i, 0)),
        scratch_shapes=[pltpu.VMEM((S, GW), BF16), pltpu.VMEM((S, GW), BF16)],
        compiler_params=_params(("parallel", "arbitrary")),
        name="sb_attention",
    )(proj, proj, proj)


def _conv_kernel(x_ref, dw_ref, dwb_ref, lnw_ref, lnb_ref, pw_ref, pwb_ref, gn_ref, out_ref,
                 hbuf, ybuf):
    T = x_ref.shape[0]
    GW = GROUP_WIDTH
    HALO = 32
    SUB = 64

    @pl.when(pl.program_id(1) == 0)
    def _():
        hbuf[0:HALO, :] = jnp.zeros((HALO, GW), F32)

    @pl.when(pl.program_id(1) > 0)
    def _():
        hbuf[0:HALO, :] = hbuf[T:T + HALO, :]

    hbuf[HALO:HALO + T, :] = x_ref[:, 0:GW] * _sigmoid(x_ref[:, GW:2 * GW])

    def sub(i, carry):
        r0 = pl.multiple_of(i * SUB, SUB)
        acc = jnp.broadcast_to(dwb_ref[...], (SUB, GW))
        win = hbuf[pl.ds(r0, SUB + HALO), :]
        for res in range(8):
            shifted = win if res == 0 else pltpu.roll(win, SUB + HALO - res, axis=0)
            for j in range(CONV_WIDTH):
                off = HALO - CONV_WIDTH + 1 + j
                if off % 8 == res:
                    base = off - res
                    acc = acc + dw_ref[j:j + 1, :] * shifted[base:base + SUB, :]
        ybuf[pl.ds(r0, SUB), :] = acc
        return carry

    lax.fori_loop(0, T // SUB, sub, 0)
    hn = _silu(_layer_norm_rows(ybuf[...], lnw_ref[...], lnb_ref[...]))
    y = _bdot(hn, pw_ref[...]) + pwb_ref[...]
    out_ref[...] = _rms_rows(y, gn_ref[...]).astype(out_ref.dtype)


def _conv_mixer(proj, dw_w, dw_b, ln_w, ln_b, pw_w, pw_b, gn, B, S):
    T = 512
    GW = GROUP_WIDTH
    dw_pad = jnp.zeros((32, GW), F32).at[:CONV_WIDTH].set(dw_w)
    r = lambda a: a.reshape(1, -1)
    return pl.pallas_call(
        _conv_kernel,
        out_shape=jax.ShapeDtypeStruct((B, S, GW), BF16),
        grid=(B, S // T),
        in_specs=[pl.BlockSpec((None, T, 2 * GW), lambda b, t: (b, t, 4)),
                  pl.BlockSpec((32, GW), lambda b, t: (0, 0)),
                  pl.BlockSpec((1, GW), lambda b, t: (0, 0)),
                  pl.BlockSpec((1, GW), lambda b, t: (0, 0)),
                  pl.BlockSpec((1, GW), lambda b, t: (0, 0)),
                  pl.BlockSpec((GW, GW), lambda b, t: (0, 0)),
                  pl.BlockSpec((1, GW), lambda b, t: (0, 0)),
                  pl.BlockSpec((1, GW), lambda b, t: (0, 0))],
        out_specs=pl.BlockSpec((None, T, GW), lambda b, t: (b, t, 0)),
        scratch_shapes=[pltpu.VMEM((T + 32, GW), F32), pltpu.VMEM((T, GW), F32)],
        compiler_params=_params(("parallel", "arbitrary")),
        name="conv_mixer",
    )(proj, dw_pad, r(dw_b), r(ln_w), r(ln_b), pw_w.astype(BF16), r(pw_b), r(gn))


def _outproj_kernel(with_router, ya_ref, yb_ref, yc_ref, yd_ref, gnc_ref, w_ref, x_ref, g1_ref,
                    lnw_ref, lnb_ref, sc_ref, sh_ref, *rest):
    GW = GROUP_WIDTH
    if with_router:
        rw_ref, rb_ref, x1_ref, u2_ref, ids_ref, topw_ref = rest
    else:
        x1_ref, u2_ref = rest
    yc = _rms_rows(yc_ref[...], gnc_ref[...]).astype(BF16)
    acc = jnp.dot(ya_ref[...], w_ref[0:GW, :], preferred_element_type=F32)
    acc = acc + jnp.dot(yb_ref[...], w_ref[GW:2 * GW, :], preferred_element_type=F32)
    acc = acc + jnp.dot(yc, w_ref[2 * GW:3 * GW, :], preferred_element_type=F32)
    acc = acc + jnp.dot(yd_ref[...], w_ref[3 * GW:4 * GW, :], preferred_element_type=F32)
    r = ALPHA * x_ref[...] + (1.0 + g1_ref[...]) * acc
    x1 = _layer_norm_rows(r, lnw_ref[...], lnb_ref[...])
    x1_ref[...] = x1
    u2 = x1 * (1.0 + sc_ref[...]) + sh_ref[...]
    u2_ref[...] = u2.astype(u2_ref.dtype)
    if with_router:
        u_hi = u2.astype(BF16)
        u_lo = (u2 - u_hi.astype(F32)).astype(BF16)
        p = jnp.dot(u_hi, rw_ref[...], preferred_element_type=F32)
        logits = (p[:, :LANES] + p[:, LANES:]
                  + jnp.dot(u_lo, rw_ref[:, :LANES], preferred_element_type=F32) + rb_ref[...])
        lane = lax.broadcasted_iota(jnp.int32, logits.shape, 1)
        lg = jnp.where(lane < N_EXPERTS, logits, -jnp.inf)
        m1 = jnp.max(lg, axis=-1, keepdims=True)
        i1 = jnp.min(jnp.where(lg == m1, lane, LANES), axis=-1, keepdims=True)
        lg2 = jnp.where(lane == i1, -jnp.inf, lg)
        m2 = jnp.max(lg2, axis=-1, keepdims=True)
        i2 = jnp.min(jnp.where(lg2 == m2, lane, LANES), axis=-1, keepdims=True)
        e2 = jnp.exp(m2 - m1)
        w1 = 1.0 / (1.0 + e2)
        w2 = e2 / (1.0 + e2)
        ids_ref[...] = jnp.where(lane == 0, i1, jnp.where(lane == 1, i2, 0))
        topw_ref[...] = jnp.where(lane == 0, w1, jnp.where(lane == 1, w2, 0.0))


def _out_proj(ya, yb, yc, yd, gnc, w_out, x2d, g1, ln_w, ln_b, sc2, sh2, seq, router=None):
    N, D = x2d.shape
    GW = GROUP_WIDTH
    tm = 512
    tpb = seq // tm
    row = lambda m: (m, 0)
    const = lambda m: (0, 0)
    perb = lambda m: (m // tpb, 0, 0)
    in_specs = [pl.BlockSpec((tm, GW), row), pl.BlockSpec((tm, GW), row),
                pl.BlockSpec((tm, GW), row), pl.BlockSpec((tm, GW), row),
                pl.BlockSpec((1, GW), const),
                pl.BlockSpec((D, D), const),
                pl.BlockSpec((tm, D), row),
                pl.BlockSpec((None, 1, D), perb),
                pl.BlockSpec((1, D), const), pl.BlockSpec((1, D), const),
                pl.BlockSpec((None, 1, D), perb), pl.BlockSpec((None, 1, D), perb)]
    args = [ya, yb, yc, yd, gnc.reshape(1, -1), w_out, x2d, g1, ln_w.reshape(1, -1),
            ln_b.reshape(1, -1), sc2, sh2]
    out_shape = [jax.ShapeDtypeStruct((N, D), F32),
                 jax.ShapeDtypeStruct((N, D), BF16 if router is None else F32)]
    out_specs = [pl.BlockSpec((tm, D), row), pl.BlockSpec((tm, D), row)]
    if router is not None:
        rw, rb = router
        in_specs += [pl.BlockSpec((D, 2 * LANES), const), pl.BlockSpec((1, LANES), const)]
        args += [rw, rb]
        out_shape += [jax.ShapeDtypeStruct((N, LANES), jnp.int32), jax.ShapeDtypeStruct((N, LANES), F32)]
        out_specs += [pl.BlockSpec((tm, LANES), row), pl.BlockSpec((tm, LANES), row)]
    return pl.pallas_call(
        functools.partial(_outproj_kernel, router is not None),
        out_shape=tuple(out_shape),
        grid=(N // tm,),
        in_specs=in_specs,
        out_specs=tuple(out_specs),
        compiler_params=_params(("parallel",)),
        name="out_proj_ln",
    )(*args)


FFN_CW = 512


def _ffn_kernel(u_ref, wg_hbm, wu_hbm, wd_hbm, x_ref, g2_ref, lnw_ref, lnb_ref, out_ref,
                acc_ref, wgbuf, wubuf, wdbuf, sem):
    m = pl.program_id(0)
    n_chunks = wg_hbm.shape[1] // FFN_CW
    total = pl.num_programs(0) * n_chunks
    n_slots = wgbuf.shape[0]
    ahead = n_slots - 1

    def chunk_copies(g):
        slot = g % n_slots
        cols = pl.ds(pl.multiple_of((g % n_chunks) * FFN_CW, FFN_CW), FFN_CW)
        return (pltpu.make_async_copy(wg_hbm.at[:, cols], wgbuf.at[slot], sem.at[slot]),
                pltpu.make_async_copy(wu_hbm.at[:, cols], wubuf.at[slot], sem.at[slot]),
                pltpu.make_async_copy(wd_hbm.at[cols, :], wdbuf.at[slot], sem.at[slot]))

    @pl.when(m == 0)
    def _():
        for g in range(ahead):
            for cp in chunk_copies(g):
                cp.start()

    u = u_ref[...]
    for ci in range(n_chunks):
        g = m * n_chunks + ci
        slot = g % n_slots
        for cp in chunk_copies(g):
            cp.wait()

        @pl.when(g + ahead < total)
        def _():
            for cp in chunk_copies(g + ahead):
                cp.start()

        hg = jnp.dot(u, wgbuf[slot], preferred_element_type=F32)
        hu = jnp.dot(u, wubuf[slot], preferred_element_type=F32)
        h = (_silu(hg) * hu).astype(BF16)
        part = jnp.dot(h, wdbuf[slot], preferred_element_type=F32)
        if ci == 0:
            acc_ref[...] = part
        else:
            acc_ref[...] += part

    r = ALPHA * x_ref[...] + (1.0 + g2_ref[...]) * acc_ref[...]
    out_ref[...] = _layer_norm_rows(r, lnw_ref[...], lnb_ref[...])


def _ffn(u2, wg, wu, wd, x1, g2, ln_w, ln_b, seq):
    N, D = u2.shape
    tm = 512
    n_slots = 3
    tpb = seq // tm
    row = lambda m: (m, 0)
    const = lambda m: (0, 0)
    hbm = pl.BlockSpec(memory_space=pl.ANY)
    return pl.pallas_call(
        _ffn_kernel,
        out_shape=jax.ShapeDtypeStruct((N, D), F32),
        grid=(N // tm,),
        in_specs=[pl.BlockSpec((tm, D), row), hbm, hbm, hbm,
                  pl.BlockSpec((tm, D), row),
                  pl.BlockSpec((None, 1, D), lambda m: (m // tpb, 0, 0)),
                  pl.BlockSpec((1, D), const), pl.BlockSpec((1, D), const)],
        out_specs=pl.BlockSpec((tm, D), row),
        scratch_shapes=[pltpu.VMEM((tm, D), F32),
                        pltpu.VMEM((n_slots, D, FFN_CW), BF16), pltpu.VMEM((n_slots, D, FFN_CW), BF16),
                        pltpu.VMEM((n_slots, FFN_CW, D), BF16),
                        pltpu.SemaphoreType.DMA((n_slots,))],
        compiler_params=_params(("arbitrary",)),
        name="ffn_ln",
    )(u2, wg, wu, wd, x1, g2, ln_w.reshape(1, -1), ln_b.reshape(1, -1))


MOE_TM = 768
MOE_CW = 256


def _row_copy(src, src_row, dst, dst_row, sem):
    return pltpu.make_async_copy(src.at[pl.ds(src_row, 1), :], dst.at[pl.ds(dst_row, 1), :], sem)


def _start_rows(n_rows, make_copy):
    def body(i, carry):
        base = pl.multiple_of(i * 8, 8)
        for j in range(8):
            make_copy(base, j).start(priority=j % 2)
        return carry
    lax.fori_loop(0, n_rows // 8, body, 0)


def _wait_rows(n_rows, one_copy):
    def body(i, carry):
        one_copy.wait()
        return carry
    lax.fori_loop(0, n_rows, body, 0, unroll=8)


def _dispatch_kernel(pos_ref, gap_ref, u_ref, out_ref, zbuf, sem, zsem):
    tm = u_ref.shape[0]
    first = pl.program_id(0) * tm

    def start(r, carry):
        for k in range(2):
            _row_copy(u_ref, r, out_ref, pos_ref[2 * (first + r) + k], sem).start(priority=k)
        return carry

    lax.fori_loop(0, tm, start, 0, unroll=8)

    @pl.when(pl.program_id(0) == pl.num_programs(0) - 1)
    def _():
        zbuf[...] = jnp.zeros_like(zbuf)
        n_gaps = gap_ref.shape[0] // 2

        def block_copy(i):
            return pltpu.make_async_copy(zbuf, out_ref.at[pl.ds(pl.multiple_of(i * 8, 8), 8), :], zsem.at[1])

        for g in range(n_gaps):
            lo, hi = gap_ref[2 * g], gap_ref[2 * g + 1]
            mid = jnp.minimum((lo + 7) // 8 * 8, hi)
            lax.fori_loop(lo, mid, lambda r, c: (_row_copy(zbuf, 0, out_ref, r, zsem.at[0]).start(), c)[1], 0)
            lax.fori_loop(mid // 8, hi // 8, lambda i, c: (block_copy(i).start(), c)[1], 0)
        for g in range(n_gaps):
            lo, hi = gap_ref[2 * g], gap_ref[2 * g + 1]
            mid = jnp.minimum((lo + 7) // 8 * 8, hi)
            lax.fori_loop(lo, mid, lambda r, c: (_row_copy(zbuf, 0, out_ref, 0, zsem.at[0]).wait(), c)[1], 0)
            lax.fori_loop(mid // 8, hi // 8, lambda i, c: (block_copy(0).wait(), c)[1], 0)

    _wait_rows(2 * tm, _row_copy(u_ref, 0, out_ref, 0, sem))


def _dispatch(u2, pos, gaps, n_rows):
    N, D = u2.shape
    tm = 512
    return pl.pallas_call(
        _dispatch_kernel,
        out_shape=jax.ShapeDtypeStruct((n_rows, D), u2.dtype),
        grid_spec=pltpu.PrefetchScalarGridSpec(
            num_scalar_prefetch=2,
            grid=(N // tm,),
            in_specs=[pl.BlockSpec((tm, D), lambda m, pos, gaps: (m, 0))],
            out_specs=pl.BlockSpec(memory_space=pl.ANY),
            scratch_shapes=[pltpu.VMEM((8, D), u2.dtype), pltpu.SemaphoreType.DMA(()),
                            pltpu.SemaphoreType.DMA((2,))]),
        compiler_params=_params(("arbitrary",)),
        name="moe_dispatch",
    )(pos, gaps, u2)


def _moe_ffn_kernel(te_ref, nv_ref, u_ref, wg_hbm, wu_hbm, wd_hbm, out_ref, wgbuf, wubuf, wdbuf, sem):
    t = pl.program_id(0)
    nv = nv_ref[0]
    F = wg_hbm.shape[2]
    n_chunks = F // MOE_CW
    n_slots = wgbuf.shape[0]
    ahead = n_slots - 1

    def chunk_copies(g):
        tile, ci = g // n_chunks, g % n_chunks
        e = te_ref[tile]
        slot = g % n_slots
        cols = pl.ds(pl.multiple_of(ci * MOE_CW, MOE_CW), MOE_CW)
        return (pltpu.make_async_copy(wg_hbm.at[e, :, cols], wgbuf.at[slot], sem.at[slot]),
                pltpu.make_async_copy(wu_hbm.at[e, :, cols], wubuf.at[slot], sem.at[slot]),
                pltpu.make_async_copy(wd_hbm.at[e, cols, :], wdbuf.at[slot], sem.at[slot]))

    @pl.when(t == 0)
    def _():
        for g in range(ahead):
            for cp in chunk_copies(g):
                cp.start()

    @pl.when(t < nv)
    def _():
        u = u_ref[...].astype(BF16)
        for ci in range(n_chunks):
            g = t * n_chunks + ci
            slot = g % n_slots
            for cp in chunk_copies(g):
                cp.wait()

            @pl.when(g + ahead < nv * n_chunks)
            def _():
                for cp in chunk_copies(g + ahead):
                    cp.start()

            hg = jnp.dot(u, wgbuf[slot].astype(BF16), preferred_element_type=F32)
            hu = jnp.dot(u, wubuf[slot].astype(BF16), preferred_element_type=F32)
            h = (_silu(hg) * hu).astype(BF16)
            part = jnp.dot(h, wdbuf[slot].astype(BF16), preferred_element_type=F32)
            if ci == 0:
                out_ref[...] = part
            else:
                out_ref[...] += part

    @pl.when(t >= nv)
    def _():
        out_ref[...] = jnp.zeros_like(out_ref)


def _moe_ffn(u_sorted, tile_expert, n_valid, wg, wu, wd):
    R, D = u_sorted.shape
    tm = MOE_TM
    n_slots = 3
    tile = lambda t, nv: jnp.minimum(t, nv[0] - 1)
    return pl.pallas_call(
        _moe_ffn_kernel,
        out_shape=jax.ShapeDtypeStruct((R, D), F32),
        grid_spec=pltpu.PrefetchScalarGridSpec(
            num_scalar_prefetch=2,
            grid=(R // tm,),
            in_specs=[pl.BlockSpec((tm, D), lambda t, te, nv: (tile(t, nv), 0)),
                      pl.BlockSpec(memory_space=pl.ANY),
                      pl.BlockSpec(memory_space=pl.ANY),
                      pl.BlockSpec(memory_space=pl.ANY)],
            out_specs=pl.BlockSpec((tm, D), lambda t, te, nv: (t, 0)),
            scratch_shapes=[pltpu.VMEM((n_slots, D, MOE_CW), F32), pltpu.VMEM((n_slots, D, MOE_CW), F32),
                            pltpu.VMEM((n_slots, MOE_CW, D), F32),
                            pltpu.SemaphoreType.DMA((n_slots,))]),
        compiler_params=_params(("arbitrary",), vmem=60 * 1024 * 1024),
        name="moe_ffn",
    )(tile_expert, n_valid, u_sorted, wg, wu, wd)


def _combine_kernel(pos_ref, y_ref, topw_ref, x_ref, g2_ref, lnw_ref, lnb_ref, out_ref, ybuf, sem):
    tm = x_ref.shape[0]
    m = pl.program_id(0)
    slot = m % 2

    def gather(step, buf):
        for k in range(2):
            _start_rows(tm, lambda base, j: _row_copy(y_ref, pos_ref[2 * (step * tm + base + j) + k],
                                                      ybuf.at[buf, k], base + j, sem.at[buf]))

    @pl.when(m == 0)
    def _():
        gather(0, 0)

    @pl.when(m + 1 < pl.num_programs(0))
    def _():
        gather(m + 1, 1 - slot)

    _wait_rows(2 * tm, _row_copy(y_ref, 0, ybuf.at[slot, 0], 0, sem.at[slot]))
    tw = topw_ref[...]
    y = tw[:, 0:1] * ybuf[slot, 0] + tw[:, 1:2] * ybuf[slot, 1]
    r = ALPHA * x_ref[...] + (1.0 + g2_ref[...]) * y
    out_ref[...] = _layer_norm_rows(r, lnw_ref[...], lnb_ref[...])


def _combine(y_sorted, pos, topw, x1, g2, ln_w, ln_b, seq):
    N, D = x1.shape
    tm = 512
    tpb = seq // tm
    row = lambda m, pos: (m, 0)
    const = lambda m, pos: (0, 0)
    return pl.pallas_call(
        _combine_kernel,
        out_shape=jax.ShapeDtypeStruct((N, D), F32),
        grid_spec=pltpu.PrefetchScalarGridSpec(
            num_scalar_prefetch=1,
            grid=(N // tm,),
            in_specs=[pl.BlockSpec(memory_space=pl.ANY),
                      pl.BlockSpec((tm, LANES), row),
                      pl.BlockSpec((tm, D), row),
                      pl.BlockSpec((None, 1, D), lambda m, pos: (m // tpb, 0, 0)),
                      pl.BlockSpec((1, D), const), pl.BlockSpec((1, D), const)],
            out_specs=pl.BlockSpec((tm, D), row),
            scratch_shapes=[pltpu.VMEM((2, 2, tm, D), F32), pltpu.SemaphoreType.DMA((2,))]),
        compiler_params=_params(("arbitrary",)),
        name="moe_combine_ln",
    )(pos, y_sorted, topw, x1, g2, ln_w.reshape(1, -1), ln_b.reshape(1, -1))


def _routing_plan(ids, n_tiles):
    e_flat = ids[:, :2].reshape(-1)
    onehot = (e_flat[:, None] == jnp.arange(N_EXPERTS, dtype=jnp.int32)[None, :]).astype(jnp.int32)
    csum = jnp.cumsum(onehot, axis=0)
    rank = jnp.sum((csum - onehot) * onehot, axis=1)
    counts = csum[-1]
    padded = ((counts + MOE_TM - 1) // MOE_TM) * MOE_TM
    ends = jnp.cumsum(padded)
    offs = ends - padded
    pos = (jnp.sum(onehot * offs[None, :], axis=1) + rank).astype(jnp.int32)
    tile_start = jnp.arange(n_tiles, dtype=jnp.int32) * MOE_TM
    n_valid = (ends[-1] // MOE_TM).astype(jnp.int32)
    tile_start = jnp.minimum(tile_start, (n_valid - 1) * MOE_TM)
    tile_expert = jnp.sum((tile_start[:, None] >= ends[None, :]).astype(jnp.int32), axis=1).astype(jnp.int32)
    total = jnp.full((1,), n_tiles * MOE_TM, jnp.int32)
    gaps = jnp.stack([jnp.concatenate([offs + counts, ends[-1:]]),
                      jnp.concatenate([ends, total])], axis=1).reshape(-1).astype(jnp.int32)
    return pos, tile_expert, n_valid.reshape(1), gaps


def _moe(u2, ids, topw, wg, wu, wd, x1, g2, ln_w, ln_b, seq):
    N = u2.shape[0]
    n_tiles = -(-2 * N // MOE_TM) + N_EXPERTS
    pos, tile_expert, n_valid, gaps = _routing_plan(ids, n_tiles)
    u_sorted = _dispatch(u2, pos, gaps, n_tiles * MOE_TM)
    y_sorted = _moe_ffn(u_sorted, tile_expert, n_valid, wg, wu, wd)
    return _combine(y_sorted, pos, topw, x1, g2, ln_w, ln_b, seq)


def _split_w_in_kernel(wt_ref, gt_ref, main_ref, gate_ref):
    main_ref[...] = wt_ref[0].T.astype(BF16)

    @pl.when(pl.program_id(1) == 0)
    def _():
        g = gt_ref[...]
        g = jnp.concatenate([g, jnp.zeros((LANES - g.shape[0], g.shape[1]), F32)], axis=0)
        gate_ref[...] = g.T.astype(BF16)


def _split_w_in(w_in):
    L, D, C = w_in.shape
    w_t = jnp.swapaxes(w_in, 1, 2)
    g0 = 4 * GROUP_WIDTH
    ng = 2 * N_HEADS
    tc = 1024
    src_col = lambda j: (j * (tc // ng) + jnp.where(j * tc >= g0, 1, 0)) * ng
    return pl.pallas_call(
        _split_w_in_kernel,
        out_shape=(jax.ShapeDtypeStruct((L, D, C - ng), BF16),
                   jax.ShapeDtypeStruct((L, D, LANES), BF16)),
        grid=(L, (C - ng) // tc),
        in_specs=[pl.BlockSpec((pl.Element(1), pl.Element(tc), pl.Element(D)),
                               lambda l, j: (l, src_col(j), 0)),
                  pl.BlockSpec((None, ng, D), lambda l, j: (l, g0 // ng, 0))],
        out_specs=(pl.BlockSpec((None, D, tc), lambda l, j: (l, 0, j)),
                   pl.BlockSpec((None, D, LANES), lambda l, j: (l, 0, 0))),
        compiler_params=_params(("parallel", "arbitrary")),
        name="split_w_in",
    )(w_t, w_t)


def kernel(x, c, w_in, mlstm_conv_w, mlstm_conv_b, mlstm_gate_b, mlstm_norm_w, pool_w, pool_scale, conv_dw_w, conv_dw_b, conv_ln_w, conv_ln_b, conv_pw_w, conv_pw_b, group_norm_w, w_out, ada_w, ada_b, ln1_w, ln1_b, ln2_w, ln2_b, ffn_w_gate, ffn_w_up, ffn_w_down, moe_router_w, moe_router_b, moe_w_gate, moe_w_up, moe_w_down):
    B, S, D = x.shape
    GW = GROUP_WIDTH
    ada = _ada_all(c, ada_w, ada_b)
    x2d = x.reshape(B * S, D)
    w_main, w_gate = _split_w_in(w_in)
    for l in range(DEPTH):
        mod = [ada[l, :, i * D:(i + 1) * D].reshape(B, 1, D) for i in range(6)]
        sh1, sc1, g1, sh2, sc2, g2 = mod
        proj, gates = _in_proj(x2d, sc1, sh1, w_main, w_gate, l, S)
        proj = proj.reshape(B, S, -1)
        gates = gates.reshape(B, S, LANES)
        gn_b, gn_c, gn_d = (group_norm_w[l, i * GW:(i + 1) * GW] for i in range(3))
        ya = _mlstm_mixer(proj, gates, mlstm_conv_w[l], mlstm_conv_b[l], mlstm_gate_b[l],
                          mlstm_norm_w[l], B, S)
        yb = _pool_mixer(proj, pool_w[l], pool_scale[l], gn_b, B, S)
        yc = _sb_mixer(proj, B, S)
        yd = _conv_mixer(proj, conv_dw_w[l], conv_dw_b[l], conv_ln_w[l], conv_ln_b[l],
                         conv_pw_w[l], conv_pw_b[l], gn_d, B, S)
        flat = lambda t: t.reshape(B * S, GW)
        j = l // 2
        router = None
        if l % 2 == 1:
            rw = jnp.zeros((D, LANES), F32).at[:, :N_EXPERTS].set(moe_router_w[j])
            rw_hi = rw.astype(BF16)
            rw_lo = (rw - rw_hi.astype(F32)).astype(BF16)
            rb = jnp.zeros((1, LANES), F32).at[0, :N_EXPERTS].set(moe_router_b[j])
            router = (jnp.concatenate([rw_hi, rw_lo], axis=1), rb)
        outs = _out_proj(flat(ya), flat(yb), flat(yc), flat(yd), gn_c, w_out[l].astype(BF16), x2d,
                         g1, ln1_w[l], ln1_b[l], sc2, sh2, S, router)
        if l % 2 == 0:
            x1, u2 = outs
            x2d = _ffn(u2, ffn_w_gate[j].astype(BF16), ffn_w_up[j].astype(BF16),
                       ffn_w_down[j].astype(BF16), x1, g2, ln2_w[l], ln2_b[l], S)
        else:
            x1, u2, ids, topw = outs
            x2d = _moe(u2, ids, topw, moe_w_gate[j], moe_w_up[j], moe_w_down[j], x1, g2,
                       ln2_w[l], ln2_b[l], S)
    return x2d.reshape(B, S, D)
```
